```python
import math
import jax, jax.numpy as jnp
from jax import lax
import numpy as np

D_MODEL = 2048
BATCH = 4
SEQ = 2048
DEPTH = 2
DEC_BATCH = 128
DEC_SEQ = 1
PAST_LEN = 16384
PAGE_SIZE = 128

PLE_DIM = 256
D_A = D_MODEL // 2
N_HEADS_A = 4
HEAD_DIM_A = D_A // N_HEADS_A
CONV_W = 4
MLSTM_CHUNK = 64
D_B = D_MODEL // 2
S5_GROUP = 16
N_GROUPS = D_B // S5_GROUP
S5_STATE = 64
D_FF = 5504
N_EXPERTS = 8
TOP_K = 2
N_DENSE = (DEPTH + 1) // 2
N_MOE = DEPTH // 2
EPS = 1e-6
PROJ_SIZES = (D_A, D_A, 2 * N_HEADS_A, D_B, D_MODEL, D_MODEL)
D_IN_PROJ = sum(PROJ_SIZES)

kernel_name = 'hybrid_mlstm_s5_gated_decode_step'

F32 = jnp.float32


def rmsnorm(x, g):
    x = x.astype(F32)
    return x * lax.rsqrt(jnp.mean(x * x, axis=-1, keepdims=True) + EPS) * g.astype(F32)


def split_proj(proj):
    cuts = [int(c) for c in np.cumsum(PROJ_SIZES)[:-1]]
    return jnp.split(proj, cuts, axis=-1)


def mlstm_chunkwise(q, k, v, log_i, log_f, c0, n0, m0):
    bsz, nh, seq, dh = q.shape
    chunk = math.gcd(seq, MLSTM_CHUNK)
    nc = seq // chunk

    def to_chunks(t):
        return jnp.moveaxis(t.reshape(t.shape[:2] + (nc, chunk) + t.shape[3:]), 2, 0)

    causal = jnp.tril(jnp.ones((chunk, chunk), dtype=bool))

    def step(carry, inp):
        c, n, m = carry
        qc, kc, vc, lic, lfc = inp
        b = jnp.cumsum(lfc, axis=-1)
        d = jnp.where(causal, b[..., :, None] - b[..., None, :] + lic[..., None, :], -jnp.inf)
        inter = b + m[..., None]
        m_t = jnp.maximum(inter, jnp.max(d, axis=-1))
        w_inter = jnp.exp(inter - m_t)
        s = jnp.einsum('bhtd,bhsd->bhts', qc, kc) * jnp.exp(d - m_t[..., None])
        num = w_inter[..., None] * jnp.einsum('bhtd,bhde->bhte', qc, c) + jnp.einsum('bhts,bhse->bhte', s, vc)
        den = w_inter * jnp.einsum('bhtd,bhd->bht', qc, n) + jnp.sum(s, axis=-1)
        h = num / jnp.maximum(jnp.abs(den), jnp.exp(-m_t))[..., None]
        b_last = b[..., -1]
        g = b_last[..., None] - b + lic
        m_new = jnp.maximum(b_last + m, jnp.max(g, axis=-1))
        decay = jnp.exp(b_last + m - m_new)
        wg = jnp.exp(g - m_new[..., None])
        c_new = decay[..., None, None] * c + jnp.einsum('bhs,bhsd,bhse->bhde', wg, kc, vc)
        n_new = decay[..., None] * n + jnp.einsum('bhs,bhsd->bhd', wg, kc)
        return (c_new, n_new, m_new), h

    (c, n, m), hs = lax.scan(step, (c0, n0, m0), tuple(to_chunks(t) for t in (q, k, v, log_i, log_f)))
    h = jnp.moveaxis(hs, 0, 2).reshape(bsz, nh, seq, dh)
    return h, (c, n, m)


def mlstm_branch(u, o_pre, if_pre, conv0, c0, n0, m0, conv_w, conv_b, w_q, w_k, w_v, b_i, b_f, g_head, skip, w_proj):
    bsz, seq, _ = u.shape
    upad = jnp.concatenate([conv0.astype(F32), u], axis=1)
    conv = conv_b.astype(F32) + sum(upad[:, j:j + seq] * conv_w[j] for j in range(CONV_W))
    c = jax.nn.silu(conv)
    new_conv = upad[:, upad.shape[1] - (CONV_W - 1):]
    ch = c.reshape(bsz, seq, N_HEADS_A, HEAD_DIM_A)
    uh = u.reshape(bsz, seq, N_HEADS_A, HEAD_DIM_A)
    q = jnp.einsum('bshd,hde->bhse', ch, w_q)
    k = jnp.einsum('bshd,hde->bhse', ch, w_k) * (HEAD_DIM_A ** -0.5)
    v = jnp.einsum('bshd,hde->bhse', uh, w_v)
    log_i = jnp.moveaxis(if_pre[..., :N_HEADS_A] + b_i, -1, 1)
    log_f = jax.nn.log_sigmoid(jnp.moveaxis(if_pre[..., N_HEADS_A:] + b_f, -1, 1))
    h, (c_new, n_new, m_new) = mlstm_chunkwise(q, k, v, log_i, log_f, c0, n0, m0)
    h = rmsnorm(jnp.moveaxis(h, 1, 2), g_head).reshape(bsz, seq, D_A)
    h = (h + skip * c) * jax.nn.sigmoid(o_pre)
    return h @ w_proj, (c_new, n_new, m_new, new_conv)


def s5_combine(e1, e2):
    ar1, ai1, br1, bi1 = e1
    ar2, ai2, br2, bi2 = e2
    return (ar2 * ar1 - ai2 * ai1, ar2 * ai1 + ai2 * ar1,
            ar2 * br1 - ai2 * bi1 + br2, ar2 * bi1 + ai2 * br1 + bi2)


def s5_branch(u, x0_re, x0_im, log_dt, a_re, a_im, b_re, b_im, c_re, c_im, d_skip, w_glu):
    bsz, seq, _ = u.shape
    ug = u.reshape(bsz, seq, N_GROUPS, S5_GROUP)
    dt = jnp.exp(log_dt.astype(F32))[:, None]
    a_re = a_re.astype(F32)
    a_im = a_im.astype(F32)
    mag = jnp.exp(a_re * dt)
    ang = a_im * dt
    ab_re = mag * jnp.cos(ang)
    ab_im = mag * jnp.sin(ang)
    den = a_re * a_re + a_im * a_im
    nr = ab_re - 1.0
    ni = ab_im
    k_re = (nr * a_re + ni * a_im) / den
    k_im = (ni * a_re - nr * a_im) / den
    b_re = b_re.astype(F32)
    b_im = b_im.astype(F32)
    bb_re = k_re[..., None] * b_re - k_im[..., None] * b_im
    bb_im = k_re[..., None] * b_im + k_im[..., None] * b_re
    bu_re = jnp.einsum('bsgc,gpc->bsgp', ug, bb_re)
    bu_im = jnp.einsum('bsgc,gpc->bsgp', ug, bb_im)
    bu_re = bu_re.at[:, 0].add(ab_re * x0_re - ab_im * x0_im)
    bu_im = bu_im.at[:, 0].add(ab_re * x0_im + ab_im * x0_re)
    ar = jnp.broadcast_to(ab_re, bu_re.shape)
    ai = jnp.broadcast_to(ab_im, bu_im.shape)
    _, _, xr, xi = lax.associative_scan(s5_combine, (ar, ai, bu_re, bu_im), axis=1)
    y = (jnp.einsum('bsgp,gcp->bsgc', xr, c_re.astype(F32)) - jnp.einsum('bsgp,gcp->bsgc', xi, c_im.astype(F32))
         + d_skip.astype(F32).reshape(N_GROUPS, S5_GROUP) * ug)
    y = jax.nn.gelu(y.reshape(bsz, seq, D_B))
    val, gate = jnp.split(y @ w_glu, 2, axis=-1)
    return val * jax.nn.sigmoid(gate), (xr[:, -1], xi[:, -1])


def swiglu(x, w_gate, w_up, w_down):
    return (jax.nn.silu(x @ w_gate) * (x @ w_up)) @ w_down


def moe_swiglu(x, w_router, b_router, w_gate, w_up, w_down):
    bsz, seq, d = x.shape
    xt = x.reshape(bsz * seq, d)
    logits = (xt @ w_router).astype(F32) + b_router.astype(F32)
    top_v, top_e = lax.top_k(logits, TOP_K)
    gates = jax.nn.softmax(top_v, axis=-1)
    flat_e = top_e.reshape(-1)
    order = jnp.argsort(flat_e)
    tok = order // TOP_K
    sizes = jnp.bincount(flat_e, length=N_EXPERTS).astype(jnp.int32)
    xs = xt[tok].astype(w_gate.dtype)
    hid = jax.nn.silu(lax.ragged_dot(xs, w_gate, sizes)) * lax.ragged_dot(xs, w_up, sizes)
    ys = lax.ragged_dot(hid.astype(w_down.dtype), w_down, sizes).astype(F32)
    ys = ys * gates.reshape(-1)[order][:, None]
    return jnp.zeros((bsz * seq, d), F32).at[tok].add(ys).reshape(bsz, seq, d)


def trunk(x, p, c0, n0, m0, conv0, sre0, sim0, w):
    h = x.astype(F32)
    outs = [[] for _ in range(6)]
    for i in range(DEPTH):
        xn = rmsnorm(h, w['g_mix'][i])
        u_a, o_a, if_a, u_b, gate_a, gate_b = split_proj(xn @ w['w_in'][i])
        a_out, st_a = mlstm_branch(u_a, o_a, if_a, conv0[i], c0[i].astype(F32), n0[i].astype(F32), m0[i].astype(F32),
                                   w['conv_w'][i], w['conv_b'][i], w['w_q'][i], w['w_k'][i], w['w_v'][i],
                                   w['b_i'][i], w['b_f'][i], w['g_head'][i], w['skip_a'][i], w['w_proj_a'][i])
        b_out, st_b = s5_branch(u_b, sre0[i].astype(F32), sim0[i].astype(F32), w['s5_log_dt'][i], w['s5_A_re'][i],
                                w['s5_A_im'][i], w['s5_B_re'][i], w['s5_B_im'][i], w['s5_C_re'][i], w['s5_C_im'][i],
                                w['s5_D'][i], w['w_glu_b'][i])
        h = h + (jax.nn.sigmoid(gate_a) * a_out + jax.nn.sigmoid(gate_b) * b_out) @ w['w_out'][i]
        hn = rmsnorm(h, w['g_ffn'][i])
        j = i // 2
        if i % 2 == 0:
            h = h + swiglu(hn, w['w_ff_gate'][j], w['w_ff_up'][j], w['w_ff_down'][j])
        else:
            h = h + moe_swiglu(hn, w['w_router'][j], w['b_router'][j], w['w_moe_gate'][j], w['w_moe_up'][j], w['w_moe_down'][j])
        h = h + (p[i].astype(F32) @ w['w_ple'][i]) * jax.nn.sigmoid(rmsnorm(h, w['g_ple'][i]) @ w['w_pg'][i])
        for lst, s in zip(outs, st_a + st_b):
            lst.append(s)
    y = rmsnorm(h, w['g_final'])
    return y, [jnp.stack(lst) for lst in outs]


def setup_inputs(seed: int = 0) -> dict:
    key = jax.random.key(seed)
    ks = iter(jax.random.split(key, 64))

    def nrm(shape, scale):
        return jax.random.normal(next(ks), shape, F32) * scale

    def gain(shape):
        return 1.0 + nrm(shape, 0.02)

    d = D_MODEL
    out = {
        'x_prompt': nrm((BATCH, SEQ, d), 1.0),
        'x_sample': nrm((DEC_BATCH, DEC_SEQ, d), 1.0),
        'p_prompt': nrm((DEPTH, BATCH, SEQ, PLE_DIM), 1.0),
        'p_sample': nrm((DEPTH, DEC_BATCH, DEC_SEQ, PLE_DIM), 1.0),
        'state_mlstm_C': nrm((DEPTH, DEC_BATCH, N_HEADS_A, HEAD_DIM_A, HEAD_DIM_A), 0.1),
        'state_mlstm_n': nrm((DEPTH, DEC_BATCH, N_HEADS_A, HEAD_DIM_A), 0.1),
        'state_mlstm_m': nrm((DEPTH, DEC_BATCH, N_HEADS_A), 0.5),
        'state_mlstm_conv': nrm((DEPTH, DEC_BATCH, CONV_W - 1, D_A), 1.0),
        'state_s5_re': nrm((DEPTH, DEC_BATCH, N_GROUPS, S5_STATE), 0.1),
        'state_s5_im': nrm((DEPTH, DEC_BATCH, N_GROUPS, S5_STATE), 0.1),
        'g_mix': gain((DEPTH, d)),
        'w_in': nrm((DEPTH, d, D_IN_PROJ), d ** -0.5),
        'conv_w': nrm((DEPTH, CONV_W, D_A), CONV_W ** -0.5),
        'conv_b': nrm((DEPTH, D_A), 0.01),
        'w_q': nrm((DEPTH, N_HEADS_A, HEAD_DIM_A, HEAD_DIM_A), HEAD_DIM_A ** -0.5),
        'w_k': nrm((DEPTH, N_HEADS_A, HEAD_DIM_A, HEAD_DIM_A), HEAD_DIM_A ** -0.5),
        'w_v': nrm((DEPTH, N_HEADS_A, HEAD_DIM_A, HEAD_DIM_A), HEAD_DIM_A ** -0.5),
        'b_i': nrm((DEPTH, N_HEADS_A), 0.1),
        'b_f': jnp.linspace(3.0, 6.0, N_HEADS_A, dtype=F32)[None, :] + nrm((DEPTH, N_HEADS_A), 0.1),
        'g_head': gain((DEPTH, N_HEADS_A, HEAD_DIM_A)),
        'skip_a': gain((DEPTH, D_A)),
        'w_proj_a': nrm((DEPTH, D_A, d), D_A ** -0.5),
        's5_log_dt': jax.random.uniform(next(ks), (DEPTH, N_GROUPS), F32, math.log(1e-3), math.log(1e-1)),
        's5_A_re': -0.5 + nrm((DEPTH, N_GROUPS, S5_STATE), 0.01),
        's5_A_im': math.pi * jnp.arange(S5_STATE, dtype=F32) + nrm((DEPTH, N_GROUPS, S5_STATE), 0.01),
        's5_B_re': nrm((DEPTH, N_GROUPS, S5_STATE, S5_GROUP), (2 * S5_GROUP) ** -0.5),
        's5_B_im': nrm((DEPTH, N_GROUPS, S5_STATE, S5_GROUP), (2 * S5_GROUP) ** -0.5),
        's5_C_re': nrm((DEPTH, N_GROUPS, S5_GROUP, S5_STATE), S5_STATE ** -0.5),
        's5_C_im': nrm((DEPTH, N_GROUPS, S5_GROUP, S5_STATE), S5_STATE ** -0.5),
        's5_D': nrm((DEPTH, D_B), 1.0),
        'w_glu_b': nrm((DEPTH, D_B, 2 * d), D_B ** -0.5),
        'w_out': nrm((DEPTH, d, d), d ** -0.5),
        'g_ffn': gain((DEPTH, d)),
        'w_ff_gate': nrm((N_DENSE, d, D_FF), d ** -0.5),
        'w_ff_up': nrm((N_DENSE, d, D_FF), d ** -0.5),
        'w_ff_down': nrm((N_DENSE, D_FF, d), D_FF ** -0.5),
        'w_router': nrm((N_MOE, d, N_EXPERTS), d ** -0.5),
        'b_router': nrm((N_MOE, N_EXPERTS), 0.01),
        'w_moe_gate': nrm((N_MOE, N_EXPERTS, d, D_FF), d ** -0.5),
        'w_moe_up': nrm((N_MOE, N_EXPERTS, d, D_FF), d ** -0.5),
        'w_moe_down': nrm((N_MOE, N_EXPERTS, D_FF, d), D_FF ** -0.5),
        'g_ple': gain((DEPTH, d)),
        'w_ple': nrm((DEPTH, PLE_DIM, d), PLE_DIM ** -0.5),
        'w_pg': nrm((DEPTH, d, d), d ** -0.5),
        'g_final': gain((d,)),
    }
    return out


def reference(x_prompt, x_sample, p_prompt, p_sample, state_mlstm_C, state_mlstm_n, state_mlstm_m, state_mlstm_conv,
              state_s5_re, state_s5_im, g_mix, w_in, conv_w, conv_b, w_q, w_k, w_v, b_i, b_f, g_head, skip_a, w_proj_a,
              s5_log_dt, s5_A_re, s5_A_im, s5_B_re, s5_B_im, s5_C_re, s5_C_im, s5_D, w_glu_b, w_out, g_ffn,
              w_ff_gate, w_ff_up, w_ff_down, w_router, b_router, w_moe_gate, w_moe_up, w_moe_down,
              g_ple, w_ple, w_pg, g_final):
    w = dict(g_mix=g_mix, w_in=w_in, conv_w=conv_w, conv_b=conv_b, w_q=w_q, w_k=w_k, w_v=w_v, b_i=b_i, b_f=b_f,
             g_head=g_head, skip_a=skip_a, w_proj_a=w_proj_a, s5_log_dt=s5_log_dt, s5_A_re=s5_A_re, s5_A_im=s5_A_im,
             s5_B_re=s5_B_re, s5_B_im=s5_B_im, s5_C_re=s5_C_re, s5_C_im=s5_C_im, s5_D=s5_D, w_glu_b=w_glu_b,
             w_out=w_out, g_ffn=g_ffn, w_ff_gate=w_ff_gate, w_ff_up=w_ff_up, w_ff_down=w_ff_down,
             w_router=w_router, b_router=b_router, w_moe_gate=w_moe_gate, w_moe_up=w_moe_up, w_moe_down=w_moe_down,
             g_ple=g_ple, w_ple=w_ple, w_pg=w_pg, g_final=g_final)
    zc = jnp.zeros((DEPTH, BATCH, N_HEADS_A, HEAD_DIM_A, HEAD_DIM_A), F32)
    zn = jnp.zeros((DEPTH, BATCH, N_HEADS_A, HEAD_DIM_A), F32)
    zm = jnp.zeros((DEPTH, BATCH, N_HEADS_A), F32)
    zconv = jnp.zeros((DEPTH, BATCH, CONV_W - 1, D_A), F32)
    zs = jnp.zeros((DEPTH, BATCH, N_GROUPS, S5_STATE), F32)
    y_prompt, st_p = trunk(x_prompt, p_prompt, zc, zn, zm, zconv, zs, zs, w)
    c_p, n_p, m_p, conv_p, s5re_p, s5im_p = st_p
    y_sample, st_s = trunk(x_sample, p_sample, state_mlstm_C, state_mlstm_n, state_mlstm_m, state_mlstm_conv,
                           state_s5_re, state_s5_im, w)
    c_s, n_s, m_s, conv_s, s5re_s, s5im_s = st_s
    return (y_prompt, y_sample, c_p, n_p, m_p, conv_p, s5re_p, s5im_p, c_s, n_s, m_s, conv_s, s5re_s, s5im_s)
```

```python
import functools
import math

import numpy as np
import jax
import jax.numpy as jnp
from jax import lax
from jax.experimental import pallas as pl
from jax.experimental.pallas import tpu as pltpu

F32 = jnp.float32
BF16 = jnp.bfloat16
EPS = 1e-6

V7X_VMEM_BYTES = 64 * 1024 * 1024
V7X_LANES = 128
V7X_SUBLANES = 8
VMEM_LIMIT = 56 * 1024 * 1024

N_HEADS = 4
CONV_W = 4
S5_GROUP = 16
S5_STATE = 64
N_EXPERTS = 8
TOP_K = 2


def _cparams(n_axes, vmem=VMEM_LIMIT):
    return pltpu.CompilerParams(dimension_semantics=("arbitrary",) * n_axes, vmem_limit_bytes=vmem)


def _sigmoid(x):
    return 1.0 / (1.0 + jnp.exp(-x))


def _silu(x):
    return x * _sigmoid(x)


def _gelu_tanh(x):
    return 0.5 * x * (1.0 + jnp.tanh(math.sqrt(2.0 / math.pi) * (x + 0.044715 * (x * x * x))))


def _log_sigmoid(x):
    return jnp.minimum(x, 0.0) - jnp.log(1.0 + jnp.exp(-jnp.abs(x)))


def _dot(a, b):
    return jnp.dot(a, b, preferred_element_type=F32)


def _dot_nt(a, b):
    return lax.dot_general(a, b, (((1,), (1,)), ((), ())), preferred_element_type=F32)


def _dot_tn(a, b):
    return lax.dot_general(a, b, (((0,), (0,)), ((), ())), preferred_element_type=F32)


def _rmsnorm_rows(x, g):
    ms = jnp.mean(x * x, axis=-1, keepdims=True)
    return x * lax.rsqrt(ms + EPS) * g


def _row_chunks(tm):
    for rc in (256, 208, 128, 104, 64, 32, 16, 8):
        if tm % rc == 0:
            return rc
    return tm


def _norm_to_scratch(x_ref, g_ref, xn_ref, tm):
    rc = _row_chunks(tm)

    def body(r, c):
        rows = pl.ds(pl.multiple_of(r * rc, rc), rc)
        xn_ref[rows, :] = _rmsnorm_rows(x_ref[rows, :], g_ref[...]).astype(BF16)
        return c

    lax.fori_loop(0, tm // rc, body, 0)


def _k_norm_mm(x_ref, g_ref, w_ref, o_ref, xn_ref, *, tm):
    @pl.when(pl.program_id(1) == 0)
    def _():
        _norm_to_scratch(x_ref, g_ref, xn_ref, tm)

    o_ref[...] = _dot(xn_ref[...], w_ref[...].astype(BF16)).astype(o_ref.dtype)


def norm_matmul(x, g, w, *, tm, tn, out_dtype=F32, name="norm_mm"):
    t, k = x.shape
    n = w.shape[1]
    return pl.pallas_call(
        functools.partial(_k_norm_mm, tm=tm),
        grid=(t // tm, pl.cdiv(n, tn)),
        in_specs=[pl.BlockSpec((tm, k), lambda i, j: (i, 0)),
                  pl.BlockSpec((1, k), lambda i, j: (0, 0)),
                  pl.BlockSpec((k, tn), lambda i, j: (0, j))],
        out_specs=pl.BlockSpec((tm, tn), lambda i, j: (i, j)),
        out_shape=jax.ShapeDtypeStruct((t, n), out_dtype),
        scratch_shapes=[pltpu.VMEM((tm, k), BF16)],
        compiler_params=_cparams(2), name=name,
    )(x, g.reshape(1, k), w)


def _k_mm_res(x_ref, w_ref, r_ref, o_ref):
    o_ref[...] = r_ref[...] + _dot(x_ref[...], w_ref[...].astype(BF16))


def _k_mm(x_ref, w_ref, o_ref):
    o_ref[...] = _dot(x_ref[...], w_ref[...].astype(BF16)).astype(o_ref.dtype)


def matmul(x, w, res=None, *, tm, tn, out_dtype=F32, name="mm"):
    t, k = x.shape
    n = w.shape[1]
    in_specs = [pl.BlockSpec((tm, k), lambda i, j: (i, 0)),
                pl.BlockSpec((k, tn), lambda i, j: (0, j))]
    args = [x, w]
    body = _k_mm
    if res is not None:
        in_specs.append(pl.BlockSpec((tm, tn), lambda i, j: (i, j)))
        args.append(res)
        body = _k_mm_res
    return pl.pallas_call(
        body, grid=(t // tm, pl.cdiv(n, tn)), in_specs=in_specs,
        out_specs=pl.BlockSpec((tm, tn), lambda i, j: (i, j)),
        out_shape=jax.ShapeDtypeStruct((t, n), out_dtype),
        compiler_params=_cparams(2), name=name,
    )(*args)


def _k_glu_gated(x_ref, wv_ref, wg_ref, gb_ref, o_ref):
    x = x_ref[...]
    val = _dot(x, wv_ref[...].astype(BF16))
    gate = _dot(x, wg_ref[...].astype(BF16))
    o_ref[...] = _sigmoid(gb_ref[...]) * (val * _sigmoid(gate))


def glu_gated_matmul(x, w, proj, gate_col0, *, tm, tn, name="glu_b"):
    t, k = x.shape
    n = w.shape[1] // 2
    nj = n // tn
    return pl.pallas_call(
        _k_glu_gated, grid=(t // tm, nj),
        in_specs=[pl.BlockSpec((tm, k), lambda i, j: (i, 0)),
                  pl.BlockSpec((k, tn), lambda i, j: (0, j)),
                  pl.BlockSpec((k, tn), lambda i, j: (0, j + nj)),
                  pl.BlockSpec((tm, tn), lambda i, j: (i, gate_col0 + j))],
        out_specs=pl.BlockSpec((tm, tn), lambda i, j: (i, j)),
        out_shape=jax.ShapeDtypeStruct((t, n), F32),
        compiler_params=_cparams(2), name=name,
    )(x, w, w, proj)


def _k_mm_gated_add(x_ref, w_ref, ga_ref, b_ref, o_ref):
    acc = _dot(x_ref[...], w_ref[...].astype(BF16))
    o_ref[...] = (_sigmoid(ga_ref[...]) * acc + b_ref[...]).astype(o_ref.dtype)


def gated_add_matmul(x, w, proj, gate_col0, b, *, tm, tn, name="proj_a"):
    t, k = x.shape
    n = w.shape[1]
    return pl.pallas_call(
        _k_mm_gated_add, grid=(t // tm, n // tn),
        in_specs=[pl.BlockSpec((tm, k), lambda i, j: (i, 0)),
                  pl.BlockSpec((k, tn), lambda i, j: (0, j)),
                  pl.BlockSpec((tm, tn), lambda i, j: (i, gate_col0 + j)),
                  pl.BlockSpec((tm, tn), lambda i, j: (i, j))],
        out_specs=pl.BlockSpec((tm, tn), lambda i, j: (i, j)),
        out_shape=jax.ShapeDtypeStruct((t, n), BF16),
        compiler_params=_cparams(2), name=name,
    )(x, w, proj, b)


def _k_swiglu_up(x_ref, g_ref, wg_ref, wu_ref, o_ref, xn_ref, *, tm):
    @pl.when(pl.program_id(1) == 0)
    def _():
        _norm_to_scratch(x_ref, g_ref, xn_ref, tm)

    xn = xn_ref[...]
    gate = _dot(xn, wg_ref[...].astype(BF16))
    up = _dot(xn, wu_ref[...].astype(BF16))
    o_ref[...] = (_silu(gate) * up).astype(o_ref.dtype)


def norm_swiglu_up(x, g, w_gate, w_up, *, tm, tn, name="ffn_up"):
    t, k = x.shape
    n = w_gate.shape[1]
    return pl.pallas_call(
        functools.partial(_k_swiglu_up, tm=tm),
        grid=(t // tm, pl.cdiv(n, tn)),
        in_specs=[pl.BlockSpec((tm, k), lambda i, j: (i, 0)),
                  pl.BlockSpec((1, k), lambda i, j: (0, 0)),
                  pl.BlockSpec((k, tn), lambda i, j: (0, j)),
                  pl.BlockSpec((k, tn), lambda i, j: (0, j))],
        out_specs=pl.BlockSpec((tm, tn), lambda i, j: (i, j)),
        out_shape=jax.ShapeDtypeStruct((t, n), BF16),
        scratch_shapes=[pltpu.VMEM((tm, k), BF16)],
        compiler_params=_cparams(2), name=name,
    )(x, g.reshape(1, k), w_gate, w_up)


def _k_ple(x_ref, g_ref, p_ref, wpg_ref, wple_ref, r_ref, o_ref, xn_ref, pb_ref, *, tm):
    @pl.when(pl.program_id(1) == 0)
    def _():
        _norm_to_scratch(x_ref, g_ref, xn_ref, tm)
        pb_ref[...] = p_ref[...].astype(BF16)

    gate = _dot(xn_ref[...], wpg_ref[...].astype(BF16))
    emb = _dot(pb_ref[...], wple_ref[...].astype(BF16))
    o_ref[...] = r_ref[...] + emb * _sigmoid(gate)


def ple_matmul(x, g, p, w_pg, w_ple, *, tm, tn, name="ple"):
    t, k = x.shape
    kp = p.shape[1]
    return pl.pallas_call(
        functools.partial(_k_ple, tm=tm),
        grid=(t // tm, k // tn),
        in_specs=[pl.BlockSpec((tm, k), lambda i, j: (i, 0)),
                  pl.BlockSpec((1, k), lambda i, j: (0, 0)),
                  pl.BlockSpec((tm, kp), lambda i, j: (i, 0)),
                  pl.BlockSpec((k, tn), lambda i, j: (0, j)),
                  pl.BlockSpec((kp, tn), lambda i, j: (0, j)),
                  pl.BlockSpec((tm, tn), lambda i, j: (i, j))],
        out_specs=pl.BlockSpec((tm, tn), lambda i, j: (i, j)),
        out_shape=jax.ShapeDtypeStruct((t, k), F32),
        scratch_shapes=[pltpu.VMEM((tm, k), BF16), pltpu.VMEM((tm, kp), BF16)],
        compiler_params=_cparams(2), name=name,
    )(x, g.reshape(1, k), p, w_pg, w_ple, x)


def _k_norm(x_ref, g_ref, o_ref):
    o_ref[...] = _rmsnorm_rows(x_ref[...], g_ref[...])


def rmsnorm_rows(x, g, *, tm, name="final_norm"):
    t, k = x.shape
    return pl.pallas_call(
        _k_norm, grid=(t // tm,),
        in_specs=[pl.BlockSpec((tm, k), lambda i: (i, 0)), pl.BlockSpec((1, k), lambda i: (0, 0))],
        out_specs=pl.BlockSpec((tm, k), lambda i: (i, 0)),
        out_shape=jax.ShapeDtypeStruct((t, k), F32),
        compiler_params=_cparams(1), name=name,
    )(x, g.reshape(1, k))


def _split_hi_lo(x):
    hi = x.astype(BF16)
    lo = (x - hi.astype(F32)).astype(BF16)
    return hi, lo


def _k_mlstm_prompt(u_ref, o_ref_in, if_ref, cw_ref, cb_ref, wq_ref, wk_ref, wv_ref, bif_ref, gh_ref, sk_ref,
                    tri_ref, a_init, a_ref, c_out, n_out, m_out, conv_out, upad_ref, *, L, dh):
    del a_init
    c_idx = pl.program_id(1)
    nh = N_HEADS

    @pl.when(c_idx == 0)
    def _():
        c_out[...] = jnp.zeros_like(c_out)
        n_out[...] = jnp.zeros_like(n_out)
        m_out[...] = jnp.zeros_like(m_out)
        upad_ref[pl.ds(0, 8), :] = jnp.zeros((8, nh * dh), F32)

    @pl.when(c_idx > 0)
    def _():
        upad_ref[pl.ds(0, 8), :] = upad_ref[pl.ds(L, 8), :]

    u = u_ref[...]
    upad_ref[pl.ds(8, L), :] = u
    conv = cb_ref[...] + u * cw_ref[CONV_W - 1:CONV_W, :]
    for j in range(CONV_W - 1):
        conv = conv + upad_ref[pl.ds(8 - (CONV_W - 1) + j, L), :] * cw_ref[j:j + 1, :]
    cact = _silu(conv)
    conv_out[0] = upad_ref[pl.ds(L, 8), :]

    pre = if_ref[...]
    li = pre[:, :V7X_LANES] + bif_ref[:, :V7X_LANES]
    lf = _log_sigmoid(pre[:, V7X_LANES:] + bif_ref[:, V7X_LANES:])
    tri = tri_ref[...]
    lf_hi, lf_mid = _split_hi_lo(lf)
    lf_lo = (lf - lf_hi.astype(F32) - lf_mid.astype(F32)).astype(BF16)
    bcum = _dot(tri, lf_hi) + _dot(tri, lf_mid) + _dot(tri, lf_lo)
    li_t = li.T
    b_t = bcum.T
    row_id = lax.broadcasted_iota(jnp.int32, (L, L), 0)
    col_id = lax.broadcasted_iota(jnp.int32, (L, L), 1)
    causal = col_id <= row_id
    lane = lax.broadcasted_iota(jnp.int32, (1, V7X_LANES), 1)
    m_row = m_out[0]
    m_new_row = m_row

    for h in range(nh):
        hs = slice(h * dh, (h + 1) * dh)
        ch = cact[:, hs].astype(BF16)
        uh = u[:, hs].astype(BF16)
        q = _dot(ch, wq_ref[h].astype(BF16))
        k = _dot(ch, wk_ref[h].astype(BF16)) * (dh ** -0.5)
        v = _dot(uh, wv_ref[h].astype(BF16))
        qb, kb, vb = q.astype(BF16), k.astype(BF16), v.astype(BF16)

        b_col = bcum[:, h:h + 1]
        li_col = li[:, h:h + 1]
        r_row = li_t[h:h + 1, :] - b_t[h:h + 1, :]
        m_prev = m_row[:, h:h + 1]
        d = jnp.where(causal, b_col + r_row, -jnp.inf)
        inter = b_col + m_prev
        m_t = jnp.maximum(inter, jnp.max(d, axis=-1, keepdims=True))
        w_inter = jnp.exp(inter - m_t)
        s = _dot_nt(qb, kb) * jnp.exp(d - m_t)
        c_prev = c_out[0, h]
        n_prev = n_out[0, h:h + 1, :]
        num = w_inter * _dot(qb, c_prev.astype(BF16)) + _dot(s.astype(BF16), vb)
        den = w_inter * jnp.sum(q * n_prev, axis=-1, keepdims=True) + jnp.sum(s, axis=-1, keepdims=True)
        hh = num / jnp.maximum(jnp.abs(den), jnp.exp(-m_t))

        b_last = b_col[L - 1:L, :]
        g_col = b_last - b_col + li_col
        m_new = jnp.maximum(b_last + m_prev, jnp.max(g_col, axis=0, keepdims=True))
        decay = jnp.exp(b_last + m_prev - m_new)
        wk_ = jnp.exp(g_col - m_new) * k
        c_out[0, h] = decay * c_prev + _dot_tn(wk_.astype(BF16), vb)
        n_out[0, h:h + 1, :] = decay * n_prev + jnp.sum(wk_, axis=0, keepdims=True)
        m_new_row = jnp.where(lane == h, m_new, m_new_row)

        hn = _rmsnorm_rows(hh, gh_ref[:, hs])
        gated = (hn + sk_ref[:, hs] * cact[:, hs]) * _sigmoid(o_ref_in[:, hs])
        a_ref[:, hs] = gated.astype(a_ref.dtype)

    m_out[0] = m_new_row


def mlstm_prompt(proj, col_u, col_o, col_if, n_rows, bsz, seq, lw, *, L):
    dh = lw["w_q"].shape[-1]
    da = N_HEADS * dh
    nc = seq // L
    tri = jnp.asarray(np.tril(np.ones((L, L), np.float32)), BF16)
    row_blk = lambda b, c: b * nc + c
    full = lambda *shape: pl.BlockSpec(shape, lambda b, c: (0,) * len(shape))
    outs = pl.pallas_call(
        functools.partial(_k_mlstm_prompt, L=L, dh=dh),
        grid=(bsz, nc),
        in_specs=[pl.BlockSpec((L, da), lambda b, c: (row_blk(b, c), col_u)),
                  pl.BlockSpec((L, da), lambda b, c: (row_blk(b, c), col_o)),
                  pl.BlockSpec((L, 2 * V7X_LANES), lambda b, c: (row_blk(b, c), col_if)),
                  full(CONV_W, da), full(1, da), full(N_HEADS, dh, dh), full(N_HEADS, dh, dh),
                  full(N_HEADS, dh, dh), full(1, 2 * V7X_LANES), full(1, da), full(1, da), full(L, L),
                  pl.BlockSpec(memory_space=pl.ANY)],
        out_specs=[pl.BlockSpec((L, da), lambda b, c: (row_blk(b, c), 0)),
                   pl.BlockSpec((1, N_HEADS, dh, dh), lambda b, c: (b, 0, 0, 0)),
                   pl.BlockSpec((1, N_HEADS, dh), lambda b, c: (b, 0, 0)),
                   pl.BlockSpec((1, 1, V7X_LANES), lambda b, c: (b, 0, 0)),
                   pl.BlockSpec((1, 8, da), lambda b, c: (b, 0, 0))],
        out_shape=[jax.ShapeDtypeStruct((n_rows, da), BF16),
                   jax.ShapeDtypeStruct((bsz, N_HEADS, dh, dh), F32),
                   jax.ShapeDtypeStruct((bsz, N_HEADS, dh), F32),
                   jax.ShapeDtypeStruct((bsz, 1, V7X_LANES), F32),
                   jax.ShapeDtypeStruct((bsz, 8, da), F32)],
        scratch_shapes=[pltpu.VMEM((L + 8, da), F32)],
        input_output_aliases={12: 0},
        compiler_params=_cparams(2), name="mlstm_prompt",
    )(proj, proj, proj, lw["conv_w"], lw["conv_b"], lw["w_q"], lw["w_k"], lw["w_v"], lw["b_if"],
      lw["g_head"], lw["skip"], tri, jnp.zeros((n_rows, da), BF16))
    a_pre, c_p, n_p, m_p, conv_p = outs
    return a_pre, c_p, n_p, m_p[:, 0, :N_HEADS], conv_p[:, 8 - (CONV_W - 1):, :]


def _k_mlstm_sample_pre(u_ref, if_ref, conv_ref, m_ref, cw_ref, cb_ref, wq_ref, wk_ref, wv_ref, bif_ref,
                        q_out, k_out, v_out, c_out, gates_out, conv_out, *, dh):
    nh = N_HEADS
    u = u_ref[...]
    conv = cb_ref[...] + u * cw_ref[CONV_W - 1:CONV_W, :]
    for j in range(CONV_W - 1):
        conv = conv + conv_ref[j] * cw_ref[j:j + 1, :]
        if j > 0:
            conv_out[j - 1] = conv_ref[j]
    conv_out[CONV_W - 2] = u
    cact = _silu(conv)
    c_out[...] = cact
    for h in range(nh):
        hs = slice(h * dh, (h + 1) * dh)
        ch = cact[:, hs].astype(BF16)
        q_out[:, hs] = _dot(ch, wq_ref[h].astype(BF16))
        k_out[:, hs] = _dot(ch, wk_ref[h].astype(BF16)) * (dh ** -0.5)
        v_out[:, hs] = _dot(u[:, hs].astype(BF16), wv_ref[h].astype(BF16))
    pre = if_ref[...]
    li = pre[:, :V7X_LANES] + bif_ref[:, :V7X_LANES]
    lf = _log_sigmoid(pre[:, V7X_LANES:] + bif_ref[:, V7X_LANES:])
    m_prev = m_ref[...]
    inter = lf + m_prev
    m_t = jnp.maximum(inter, li)
    gates_out[0] = jnp.exp(inter - m_t)
    gates_out[1] = jnp.exp(li - m_t)
    gates_out[2] = jnp.exp(-m_t)
    gates_out[3] = m_t


def _k_mlstm_sample_step(q_ref, k_ref, v_ref, gates_ref, c_ref, n_ref, cact_ref, o_ref_in, gh_ref, sk_ref,
                         a_any, c_out, n_out, a_out, hh_ref, *, bt, dh):
    del a_any
    i = pl.program_id(0)
    nh = N_HEADS
    q = q_ref[...]
    k = k_ref[...]
    v = v_ref[...]
    w_inter = gates_ref[0]
    w_new = gates_ref[1]
    e_neg_m = gates_ref[2]
    n_prev = n_ref[...]
    rows = pl.ds(pl.multiple_of(i * bt, bt), bt)
    for h in range(nh):
        hs = slice(h * dh, (h + 1) * dh)
        qh, kh, vh, nh_prev = q[:, hs], k[:, hs], v[:, hs], n_prev[:, hs]
        q_t = qh.T
        k_t = kh.T
        wi = w_inter[:, h:h + 1]
        wn = w_new[:, h:h + 1]
        s = jnp.sum(qh * kh, axis=-1, keepdims=True) * wn
        den = wi * jnp.sum(qh * nh_prev, axis=-1, keepdims=True) + s
        wv = wn * vh
        qc_rows = []
        for bl in range(bt):
            c_prev = c_ref[bl, h]
            qc_rows.append(jnp.sum(q_t[:, bl:bl + 1] * c_prev, axis=0, keepdims=True))
            c_out[bl, h] = wi[bl:bl + 1, :] * c_prev + k_t[:, bl:bl + 1] * wv[bl:bl + 1, :]
        qc = jnp.concatenate(qc_rows, axis=0)
        num = wi * qc + s * vh
        hh = num / jnp.maximum(jnp.abs(den), e_neg_m[:, h:h + 1])
        hh_ref[rows, hs] = hh
        n_out[:, hs] = wi * nh_prev + wn * kh

    @pl.when(i == pl.num_programs(0) - 1)
    def _():
        for h in range(nh):
            hs = slice(h * dh, (h + 1) * dh)
            hn = _rmsnorm_rows(hh_ref[:, hs], gh_ref[:, hs])
            a_out[:, hs] = ((hn + sk_ref[:, hs] * cact_ref[:, hs]) * _sigmoid(o_ref_in[:, hs])).astype(a_out.dtype)


def mlstm_sample(proj, col_u, col_o, col_if, row0, a_pre_all, c0, n0, m0, conv0, lw, *, bt=8):
    bs, nh, dh, _ = c0.shape
    da = nh * dh
    rb = row0 // bs
    m_pad = jnp.pad(m0, ((0, 0), (0, V7X_LANES - nh)))
    conv_t = jnp.transpose(conv0, (1, 0, 2))
    full = lambda *shape: pl.BlockSpec(shape, lambda i: (0,) * len(shape))
    q, k, v, cact, gates, conv_new = pl.pallas_call(
        functools.partial(_k_mlstm_sample_pre, dh=dh),
        grid=(1,),
        in_specs=[pl.BlockSpec((bs, da), lambda i: (rb, col_u)),
                  pl.BlockSpec((bs, 2 * V7X_LANES), lambda i: (rb, col_if)),
                  full(CONV_W - 1, bs, da), full(bs, V7X_LANES), full(CONV_W, da), full(1, da),
                  full(nh, dh, dh), full(nh, dh, dh), full(nh, dh, dh), full(1, 2 * V7X_LANES)],
        out_specs=[full(bs, da), full(bs, da), full(bs, da), full(bs, da), full(4, bs, V7X_LANES),
                   full(CONV_W - 1, bs, da)],
        out_shape=[jax.ShapeDtypeStruct((bs, da), F32)] * 4
                  + [jax.ShapeDtypeStruct((4, bs, V7X_LANES), F32),
                     jax.ShapeDtypeStruct((CONV_W - 1, bs, da), F32)],
        compiler_params=_cparams(1), name="mlstm_sample_pre",
    )(proj, proj, conv_t, m_pad, lw["conv_w"], lw["conv_b"], lw["w_q"], lw["w_k"], lw["w_v"], lw["b_if"])

    blk = lambda *shape: pl.BlockSpec(shape, lambda i: (i,) + (0,) * (len(shape) - 1))
    cst = lambda *shape: pl.BlockSpec(shape, lambda i: (0,) * len(shape))
    c_new, n_new, a_pre_all = pl.pallas_call(
        functools.partial(_k_mlstm_sample_step, bt=bt, dh=dh),
        grid=(bs // bt,),
        in_specs=[blk(bt, da), blk(bt, da), blk(bt, da),
                  pl.BlockSpec((4, bt, V7X_LANES), lambda i: (0, i, 0)),
                  blk(bt, nh, dh, dh), blk(bt, da), cst(bs, da),
                  pl.BlockSpec((bs, da), lambda i: (rb, col_o)), cst(1, da), cst(1, da),
                  pl.BlockSpec(memory_space=pl.ANY)],
        out_specs=[blk(bt, nh, dh, dh), blk(bt, da), pl.BlockSpec((bs, da), lambda i: (rb, 0))],
        out_shape=[jax.ShapeDtypeStruct((bs, nh, dh, dh), F32), jax.ShapeDtypeStruct((bs, da), F32),
                   jax.ShapeDtypeStruct(a_pre_all.shape, a_pre_all.dtype)],
        scratch_shapes=[pltpu.VMEM((bs, da), F32)],
        input_output_aliases={10: 2},
        compiler_params=_cparams(1), name="mlstm_sample_step",
    )(q, k, v, gates, c0, n0.reshape(bs, da), cact, proj, lw["g_head"], lw["skip"], a_pre_all)
    m_new = gates[3][:, :nh]
    return a_pre_all, c_new, n_new.reshape(bs, nh, dh), m_new, jnp.transpose(conv_new, (1, 0, 2))


S5_SLAB_GROUPS = V7X_LANES // S5_GROUP
S5_SLAB_STATES = S5_SLAB_GROUPS * S5_STATE


def _s5_params(lp):
    g, p = lp["a_re"].shape
    dt = jnp.exp(lp["log_dt"].astype(F32))[:, None]
    a_re = lp["a_re"].astype(F32)
    a_im = lp["a_im"].astype(F32)
    lam_re = a_re * dt
    lam_im = a_im * dt
    mag = jnp.exp(lam_re)
    ab_re = mag * jnp.cos(lam_im)
    ab_im = mag * jnp.sin(lam_im)
    den = a_re * a_re + a_im * a_im
    nr = ab_re - 1.0
    ni = ab_im
    k_re = (nr * a_re + ni * a_im) / den
    k_im = (ni * a_re - nr * a_im) / den
    b_re = lp["b_re"].astype(F32)
    b_im = lp["b_im"].astype(F32)
    bb_re = k_re[..., None] * b_re - k_im[..., None] * b_im
    bb_im = k_re[..., None] * b_im + k_im[..., None] * b_re
    ns = g // S5_SLAB_GROUPS
    eye = jnp.eye(S5_SLAB_GROUPS, dtype=F32)

    def in_blockdiag(bb):
        bs = bb.reshape(ns, S5_SLAB_GROUPS, p, S5_GROUP)
        w = jnp.einsum("ab,sapc->sacbp", eye, bs)
        return w.reshape(ns, V7X_LANES, S5_SLAB_STATES).astype(BF16)

    def out_blockdiag(cc):
        cs = cc.astype(F32).reshape(ns, S5_SLAB_GROUPS, S5_GROUP, p)
        w = jnp.einsum("ab,sacp->sapbc", eye, cs)
        return w.reshape(ns, S5_SLAB_STATES, V7X_LANES)

    wc = jnp.concatenate([out_blockdiag(lp["c_re"]), -out_blockdiag(lp["c_im"])], axis=1).astype(BF16)
    return dict(lam_re=lam_re.reshape(1, g * p), lam_im=lam_im.reshape(1, g * p),
                ab_re=ab_re.reshape(1, g * p), ab_im=ab_im.reshape(1, g * p),
                wb_re=in_blockdiag(bb_re), wb_im=in_blockdiag(bb_im), wc=wc,
                d_skip=lp["d_skip"].astype(F32).reshape(1, g * S5_GROUP))


def _s5_powers(sp, ks):
    kk = jnp.asarray(ks, F32)[:, None]
    mag = jnp.exp(kk * sp["lam_re"])
    return mag * jnp.cos(kk * sp["lam_im"]), mag * jnp.sin(kk * sp["lam_im"])


def _s5_input_proj(u_bf, wbr_ref, wbi_ref, bur_ref, bui_ref):
    ns = wbr_ref.shape[0]
    for s in range(ns):
        us = u_bf[:, s * V7X_LANES:(s + 1) * V7X_LANES]
        cols = slice(s * S5_SLAB_STATES, (s + 1) * S5_SLAB_STATES)
        bur_ref[:, cols] = _dot(us, wbr_ref[s])
        bui_ref[:, cols] = _dot(us, wbi_ref[s])


def _s5_output_proj(xr_ref, xi_ref, wc_ref, skip):
    ns = wc_ref.shape[0]
    outs = []
    for s in range(ns):
        cols = slice(s * S5_SLAB_STATES, (s + 1) * S5_SLAB_STATES)
        xcat = jnp.concatenate([xr_ref[:, cols].astype(BF16), xi_ref[:, cols].astype(BF16)], axis=1)
        y = _dot(xcat, wc_ref[s]) + skip[:, s * V7X_LANES:(s + 1) * V7X_LANES]
        outs.append(_gelu_tanh(y).astype(BF16))
    return jnp.concatenate(outs, axis=1)


def _k_s5_prompt(u_ref, wbr_ref, wbi_ref, wc_ref, d_ref, apr_ref, api_ref, ajr_ref, aji_ref, perm_ref,
                 permt_ref, gy_init, gy_ref, sre_out, sim_out, bur_ref, bui_ref, *, tc, lane_blk):
    del gy_init
    c_idx = pl.program_id(1)
    nsteps = tc // V7X_SUBLANES
    nch = bur_ref.shape[1]

    @pl.when(c_idx == 0)
    def _():
        sre_out[...] = jnp.zeros_like(sre_out)
        sim_out[...] = jnp.zeros_like(sim_out)

    u = u_ref[...]
    u_hi, u_lo = _split_hi_lo(u)
    perm = perm_ref[...]
    up_hi = _dot(perm, u_hi)
    up = up_hi + _dot(perm, u_lo)
    _s5_input_proj(up_hi.astype(BF16), wbr_ref, wbi_ref, bur_ref, bui_ref)

    sub = lax.broadcasted_iota(jnp.int32, (V7X_SUBLANES, lane_blk), 0)
    for lb in range(nch // lane_blk):
        lanes = slice(lb * lane_blk, (lb + 1) * lane_blk)
        ar = jnp.broadcast_to(apr_ref[0:1, lanes], (V7X_SUBLANES, lane_blk))
        ai = jnp.broadcast_to(api_ref[0:1, lanes], (V7X_SUBLANES, lane_blk))

        def local_scan(i, carry):
            xr, xi = carry
            rows = pl.ds(pl.multiple_of(i * V7X_SUBLANES, V7X_SUBLANES), V7X_SUBLANES)
            nxr = ar * xr - ai * xi + bur_ref[rows, lanes]
            nxi = ar * xi + ai * xr + bui_ref[rows, lanes]
            bur_ref[rows, lanes] = nxr
            bui_ref[rows, lanes] = nxi
            return nxr, nxi

        zero = jnp.zeros((V7X_SUBLANES, lane_blk), F32)
        er, ei = lax.fori_loop(0, nsteps, local_scan, (zero, zero), unroll=4)

        for d, row in ((1, 0), (2, 1), (4, 3)):
            pr = ajr_ref[row:row + 1, lanes]
            pi = aji_ref[row:row + 1, lanes]
            sr = pltpu.roll(er, d, 0)
            si = pltpu.roll(ei, d, 0)
            keep = sub >= d
            er, ei = (er + jnp.where(keep, pr * sr - pi * si, 0.0),
                      ei + jnp.where(keep, pr * si + pi * sr, 0.0))
        c0r = jnp.broadcast_to(sre_out[0, :, lanes], (V7X_SUBLANES, lane_blk))
        c0i = jnp.broadcast_to(sim_out[0, :, lanes], (V7X_SUBLANES, lane_blk))
        ajr = ajr_ref[:, lanes]
        aji = aji_ref[:, lanes]
        fr = ajr * c0r - aji * c0i + er
        fi = ajr * c0i + aji * c0r + ei
        cin_r = jnp.where(sub >= 1, pltpu.roll(fr, 1, 0), c0r)
        cin_i = jnp.where(sub >= 1, pltpu.roll(fi, 1, 0), c0i)
        sre_out[0, :, lanes] = fr[V7X_SUBLANES - 1:V7X_SUBLANES, :]
        sim_out[0, :, lanes] = fi[V7X_SUBLANES - 1:V7X_SUBLANES, :]

        def add_carry(i, c):
            rows = pl.ds(pl.multiple_of(i * V7X_SUBLANES, V7X_SUBLANES), V7X_SUBLANES)
            pr = apr_ref[pl.ds(i, 1), lanes]
            pi = api_ref[pl.ds(i, 1), lanes]
            bur_ref[rows, lanes] = bur_ref[rows, lanes] + (pr * cin_r - pi * cin_i)
            bui_ref[rows, lanes] = bui_ref[rows, lanes] + (pr * cin_i + pi * cin_r)
            return c

        lax.fori_loop(0, nsteps, add_carry, 0, unroll=4)

    g_perm = _s5_output_proj(bur_ref, bui_ref, wc_ref, d_ref[...] * up)
    gy_ref[...] = _dot(permt_ref[...], g_perm).astype(gy_ref.dtype)


def s5_prompt(proj, col_u, n_rows, bsz, seq, sp, *, tc=256, lane_blk=512):
    db = sp["d_skip"].shape[1]
    nch = sp["ab_re"].shape[1]
    nc = seq // tc
    nsteps = tc // V7X_SUBLANES
    apr, api = _s5_powers(sp, np.arange(1, nsteps + 1))
    ajr, aji = _s5_powers(sp, nsteps * np.arange(1, V7X_SUBLANES + 1))
    perm = np.zeros((tc, tc), np.float32)
    r = np.arange(tc)
    perm[r, (r % V7X_SUBLANES) * nsteps + r // V7X_SUBLANES] = 1.0
    full = lambda a: pl.BlockSpec(a.shape, lambda b, c: (0,) * a.ndim)
    consts = [sp["wb_re"], sp["wb_im"], sp["wc"], sp["d_skip"], apr, api, ajr, aji,
              jnp.asarray(perm, BF16), jnp.asarray(perm.T, BF16)]
    gy, s_re, s_im = pl.pallas_call(
        functools.partial(_k_s5_prompt, tc=tc, lane_blk=lane_blk),
        grid=(bsz, nc),
        in_specs=[pl.BlockSpec((tc, db), lambda b, c: (b * nc + c, col_u))] + [full(a) for a in consts]
                 + [pl.BlockSpec(memory_space=pl.ANY)],
        out_specs=[pl.BlockSpec((tc, db), lambda b, c: (b * nc + c, 0)),
                   pl.BlockSpec((1, 1, nch), lambda b, c: (b, 0, 0)),
                   pl.BlockSpec((1, 1, nch), lambda b, c: (b, 0, 0))],
        out_shape=[jax.ShapeDtypeStruct((n_rows, db), BF16),
                   jax.ShapeDtypeStruct((bsz, 1, nch), F32), jax.ShapeDtypeStruct((bsz, 1, nch), F32)],
        scratch_shapes=[pltpu.VMEM((tc, nch), F32), pltpu.VMEM((tc, nch), F32)],
        input_output_aliases={1 + len(consts): 0},
        compiler_params=_cparams(2), name="s5_prompt",
    )(proj, *consts, jnp.zeros((n_rows, db), BF16))
    return gy, s_re, s_im


def _k_s5_sample(u_ref, x0r_ref, x0i_ref, wbr_ref, wbi_ref, wc_ref, d_ref, abr_ref, abi_ref, gy_any,
                 gy_ref, xr_out, xi_out):
    del gy_any
    u = u_ref[...]
    _s5_input_proj(u.astype(BF16), wbr_ref, wbi_ref, xr_out, xi_out)
    ar = abr_ref[...]
    ai = abi_ref[...]
    x0r = x0r_ref[...]
    x0i = x0i_ref[...]
    xr_out[...] = xr_out[...] + (ar * x0r - ai * x0i)
    xi_out[...] = xi_out[...] + (ar * x0i + ai * x0r)
    gy_ref[...] = _s5_output_proj(xr_out, xi_out, wc_ref, d_ref[...] * u).astype(gy_ref.dtype)


def s5_sample(proj, col_u, row0, gy_all, x0_re, x0_im, sp):
    bs = x0_re.shape[0]
    db = sp["d_skip"].shape[1]
    nch = sp["ab_re"].shape[1]
    rb = row0 // bs
    full = lambda a: pl.BlockSpec(a.shape, lambda i: (0,) * a.ndim)
    consts = [sp["wb_re"], sp["wb_im"], sp["wc"], sp["d_skip"], sp["ab_re"], sp["ab_im"]]
    x0r = x0_re.reshape(bs, nch)
    x0i = x0_im.reshape(bs, nch)
    gy_all, xr, xi = pl.pallas_call(
        _k_s5_sample,
        grid=(1,),
        in_specs=[pl.BlockSpec((bs, db), lambda i: (rb, col_u)), full(x0r), full(x0i)]
                 + [full(a) for a in consts] + [pl.BlockSpec(memory_space=pl.ANY)],
        out_specs=[pl.BlockSpec((bs, db), lambda i: (rb, 0)),
                   pl.BlockSpec((bs, nch), lambda i: (0, 0)), pl.BlockSpec((bs, nch), lambda i: (0, 0))],
        out_shape=[jax.ShapeDtypeStruct(gy_all.shape, gy_all.dtype),
                   jax.ShapeDtypeStruct((bs, nch), F32), jax.ShapeDtypeStruct((bs, nch), F32)],
        input_output_aliases={9: 0},
        compiler_params=_cparams(1), name="s5_sample",
    )(proj, x0r, x0i, *consts, gy_all)
    return gy_all, xr, xi


def _k_router(x_ref, g_ref, w_ref, b_ref, hn_ref, lg_ref):
    hn = _rmsnorm_rows(x_ref[...], g_ref[...])
    hn_ref[...] = hn
    x_hi, x_lo = _split_hi_lo(hn)
    w_hi, w_lo = _split_hi_lo(w_ref[...])
    lg_ref[...] = _dot(x_hi, w_hi) + (_dot(x_lo, w_hi) + _dot(x_hi, w_lo)) + b_ref[...]


def router(x, g, w_router, b_router, *, tm):
    t, k = x.shape
    ne = w_router.shape[1]
    w_pad = jnp.pad(w_router, ((0, 0), (0, V7X_LANES - ne)))
    b_pad = jnp.pad(b_router.astype(F32), (0, V7X_LANES - ne)).reshape(1, V7X_LANES)
    hn, lg = pl.pallas_call(
        _k_router, grid=(t // tm,),
        in_specs=[pl.BlockSpec((tm, k), lambda i: (i, 0)), pl.BlockSpec((1, k), lambda i: (0, 0)),
                  pl.BlockSpec((k, V7X_LANES), lambda i: (0, 0)), pl.BlockSpec((1, V7X_LANES), lambda i: (0, 0))],
        out_specs=[pl.BlockSpec((tm, k), lambda i: (i, 0)), pl.BlockSpec((tm, V7X_LANES), lambda i: (i, 0))],
        out_shape=[jax.ShapeDtypeStruct((t, k), F32), jax.ShapeDtypeStruct((t, V7X_LANES), F32)],
        compiler_params=_cparams(1), name="router",
    )(x, g.reshape(1, k), w_pad, b_pad)
    return hn, lg[:, :ne]


def _k_gather_rows(nused_ref, tok_ref, src_hbm, o_ref, buf_ref, sem, *, tr):
    t = pl.program_id(0)

    @pl.when(t < nused_ref[0])
    def _():
        base = t * tr

        def issue(r, c):
            tok = tok_ref[base + r]
            pltpu.make_async_copy(src_hbm.at[pl.ds(tok, 1), :], buf_ref.at[pl.ds(r, 1), :], sem).start()
            return c

        lax.fori_loop(0, tr, issue, 0)
        pltpu.make_async_copy(src_hbm.at[pl.ds(0, tr), :], buf_ref, sem).wait()
        o_ref[...] = buf_ref[...].astype(o_ref.dtype)

    @pl.when(t >= nused_ref[0])
    def _():
        o_ref[...] = jnp.zeros_like(o_ref)


def gather_rows(src, tok, n_used, *, tr):
    t, k = src.shape
    r_pad = tok.shape[0]
    nt = r_pad // tr
    return pl.pallas_call(
        functools.partial(_k_gather_rows, tr=tr),
        grid_spec=pltpu.PrefetchScalarGridSpec(
            num_scalar_prefetch=2, grid=(nt,),
            in_specs=[pl.BlockSpec(memory_space=pl.ANY)],
            out_specs=pl.BlockSpec((tr, k), lambda i, nu, tk: (i, 0)),
            scratch_shapes=[pltpu.VMEM((tr, k), src.dtype), pltpu.SemaphoreType.DMA(())]),
        out_shape=jax.ShapeDtypeStruct((r_pad, k), BF16),
        compiler_params=_cparams(1), name="moe_gather",
    )(n_used, tok, src)


def _new_group(te_ref, t):
    return jnp.logical_or(t == 0, te_ref[t] != te_ref[jnp.maximum(t - 1, 0)])


def _k_moe_up(nused_ref, te_ref, x_ref, wg_ref, wu_ref, o_ref, wgb_ref, wub_ref):
    t = pl.program_id(1)

    @pl.when(t < nused_ref[0])
    def _():
        @pl.when(_new_group(te_ref, t))
        def _():
            wgb_ref[...] = wg_ref[...].astype(BF16)
            wub_ref[...] = wu_ref[...].astype(BF16)

        x = x_ref[...]
        o_ref[...] = (_silu(_dot(x, wgb_ref[...])) * _dot(x, wub_ref[...])).astype(o_ref.dtype)

    @pl.when(t >= nused_ref[0])
    def _():
        o_ref[...] = jnp.zeros_like(o_ref)


def moe_up(xs, w_gate, w_up, tile_expert, n_used, *, tr, tn):
    r_pad, k = xs.shape
    ne, _, f = w_gate.shape
    nt = r_pad // tr
    row = lambda j, t, nu, te: (jnp.minimum(t, nu[0] - 1), 0)
    wmap = lambda j, t, nu, te: (te[t], 0, j)
    return pl.pallas_call(
        _k_moe_up,
        grid_spec=pltpu.PrefetchScalarGridSpec(
            num_scalar_prefetch=2, grid=(pl.cdiv(f, tn), nt),
            in_specs=[pl.BlockSpec((tr, k), row),
                      pl.BlockSpec((None, k, tn), wmap), pl.BlockSpec((None, k, tn), wmap)],
            out_specs=pl.BlockSpec((tr, tn), lambda j, t, nu, te: (t, j)),
            scratch_shapes=[pltpu.VMEM((k, tn), BF16), pltpu.VMEM((k, tn), BF16)]),
        out_shape=jax.ShapeDtypeStruct((r_pad, f), BF16),
        compiler_params=_cparams(2), name="moe_up",
    )(n_used, tile_expert, xs, w_gate, w_up)


def _k_moe_down(nused_ref, te_ref, x_ref, w_ref, o_ref, wb_ref):
    t = pl.program_id(1)

    @pl.when(t < nused_ref[0])
    def _():
        @pl.when(_new_group(te_ref, t))
        def _():
            wb_ref[...] = w_ref[...].astype(BF16)

        o_ref[...] = _dot(x_ref[...], wb_ref[...])

    @pl.when(t >= nused_ref[0])
    def _():
        o_ref[...] = jnp.zeros_like(o_ref)


def moe_down(hid, w_down, tile_expert, n_used, *, tr, tn):
    r_pad, f = hid.shape
    d = w_down.shape[2]
    nt = r_pad // tr
    return pl.pallas_call(
        _k_moe_down,
        grid_spec=pltpu.PrefetchScalarGridSpec(
            num_scalar_prefetch=2, grid=(d // tn, nt),
            in_specs=[pl.BlockSpec((tr, f), lambda j, t, nu, te: (jnp.minimum(t, nu[0] - 1), 0)),
                      pl.BlockSpec((None, f, tn), lambda j, t, nu, te: (te[t], 0, j))],
            out_specs=pl.BlockSpec((tr, tn), lambda j, t, nu, te: (t, j)),
            scratch_shapes=[pltpu.VMEM((f, tn), BF16)]),
        out_shape=jax.ShapeDtypeStruct((r_pad, d), F32),
        compiler_params=_cparams(2), name="moe_down",
    )(n_used, tile_expert, hid, w_down)


def _k_moe_combine(pos_ref, ys_hbm, gate_ref, res_ref, o_ref, buf0_ref, buf1_ref, sem, *, tr):
    base = pl.program_id(0) * tr
    n_tok = pl.num_programs(0) * tr

    def issue(r, c):
        p0 = pos_ref[base + r]
        p1 = pos_ref[n_tok + base + r]
        pltpu.make_async_copy(ys_hbm.at[pl.ds(p0, 1), :], buf0_ref.at[pl.ds(r, 1), :], sem).start()
        pltpu.make_async_copy(ys_hbm.at[pl.ds(p1, 1), :], buf1_ref.at[pl.ds(r, 1), :], sem).start()
        return c

    lax.fori_loop(0, tr, issue, 0)
    pltpu.make_async_copy(ys_hbm.at[pl.ds(0, tr), :], buf0_ref, sem).wait()
    pltpu.make_async_copy(ys_hbm.at[pl.ds(0, tr), :], buf1_ref, sem).wait()
    g = gate_ref[...]
    o_ref[...] = res_ref[...] + (g[:, 0:1] * buf0_ref[...] + g[:, 1:2] * buf1_ref[...])


def moe_combine(ys, pos, gates, res, *, tr):
    t, d = res.shape
    g_pad = jnp.pad(gates, ((0, 0), (0, V7X_LANES - gates.shape[1])))
    return pl.pallas_call(
        functools.partial(_k_moe_combine, tr=tr),
        grid_spec=pltpu.PrefetchScalarGridSpec(
            num_scalar_prefetch=1, grid=(t // tr,),
            in_specs=[pl.BlockSpec(memory_space=pl.ANY),
                      pl.BlockSpec((tr, V7X_LANES), lambda i, p: (i, 0)),
                      pl.BlockSpec((tr, d), lambda i, p: (i, 0))],
            out_specs=pl.BlockSpec((tr, d), lambda i, p: (i, 0)),
            scratch_shapes=[pltpu.VMEM((tr, d), F32), pltpu.VMEM((tr, d), F32), pltpu.SemaphoreType.DMA(())]),
        out_shape=jax.ShapeDtypeStruct((t, d), F32),
        compiler_params=_cparams(1), name="moe_combine",
    )(pos, ys, g_pad, res)


def moe_layer(h, g_ffn, w_router, b_router, w_gate, w_up, w_down, *, tm, tr, tn_up, tn_down, tr_gather,
              tr_combine):
    t, d = h.shape
    ne = w_gate.shape[0]
    hn, logits = router(h, g_ffn, w_router, b_router, tm=tm)
    top_v, top_e = lax.top_k(logits, TOP_K)
    gates = jax.nn.softmax(top_v, axis=-1)
    flat_e = top_e.reshape(-1)
    onehot = (flat_e[:, None] == jnp.arange(ne, dtype=flat_e.dtype)[None, :]).astype(jnp.int32)
    rank = jnp.sum((jnp.cumsum(onehot, axis=0) - onehot) * onehot, axis=1)
    sizes = jnp.sum(onehot, axis=0)
    tiles_per = (sizes + tr - 1) // tr
    tile_end = jnp.cumsum(tiles_per)
    tile_start = tile_end - tiles_per
    n_used = tile_end[-1:].astype(jnp.int32)
    nt = (t * TOP_K) // tr + ne
    r_pad = nt * tr
    pos = (tile_start[flat_e] * tr + rank).astype(jnp.int32)
    src_tok = jnp.zeros((r_pad,), jnp.int32).at[pos].set(jnp.arange(t * TOP_K, dtype=jnp.int32) // TOP_K)
    tile_ids = jnp.minimum(jnp.arange(nt, dtype=jnp.int32), n_used[0] - 1)
    tile_expert = jnp.sum((tile_ids[:, None] >= tile_end[None, :]).astype(jnp.int32), axis=1).astype(jnp.int32)
    xs = gather_rows(hn, src_tok, n_used * (tr // tr_gather), tr=tr_gather)
    hid = moe_up(xs, w_gate, w_up, tile_expert, n_used, tr=tr, tn=tn_up)
    ys = moe_down(hid, w_down, tile_expert, n_used, tr=tr, tn=tn_down)
    return moe_combine(ys, pos.reshape(t, TOP_K).T.reshape(-1), gates, h, tr=tr_combine)


def _layer_weights(i, conv_w, conv_b, w_q, w_k, w_v, b_i, b_f, g_head, skip_a):
    nh = b_i.shape[1]
    pad = jnp.zeros((V7X_LANES - nh,), F32)
    b_if = jnp.concatenate([b_i[i].astype(F32), pad, b_f[i].astype(F32), pad]).reshape(1, 2 * V7X_LANES)
    da = conv_w.shape[-1]
    return dict(conv_w=conv_w[i], conv_b=conv_b[i].reshape(1, da), w_q=w_q[i], w_k=w_k[i], w_v=w_v[i],
                b_if=b_if, g_head=g_head[i].reshape(1, da), skip=skip_a[i].reshape(1, da))


def _reorder_w_in(w, d_a, nh, d_b, d):
    o0 = d_a
    i0 = 2 * d_a
    ub0 = i0 + 2 * nh
    ga0 = ub0 + d_b
    gb0 = ga0 + d
    k = w.shape[0]
    zpad = jnp.zeros((k, V7X_LANES - nh), w.dtype)
    cols = [w[:, :d_a], w[:, ub0:ub0 + d_b], w[:, o0:o0 + d_a], w[:, ga0:ga0 + d], w[:, gb0:gb0 + d],
            w[:, i0:i0 + nh], zpad, w[:, i0 + nh:i0 + 2 * nh], zpad]
    return jnp.concatenate(cols, axis=1).astype(BF16)


def _tile_plan(n_rows, seq):
    del seq
    tm = next(c for c in (832, 640, 512, 256, 128, 64, 32, 16) if n_rows % c == 0)
    tc = next(c for c in (320, 256, 128, 64, 32, 16, 8) if n_rows % c == 0)
    return dict(tm=tm, tn=512, tn_down=256, mlstm_chunk=256, s5_chunk=256, s5_lane_blk=512,
                moe_tr=512, moe_tn_up=512, moe_tn_down=512, gather_tr=256, combine_tr=tc, norm_tm=tm // 2)


def kernel(x_prompt, x_sample, p_prompt, p_sample, state_mlstm_C, state_mlstm_n, state_mlstm_m, state_mlstm_conv,
           state_s5_re, state_s5_im, g_mix, w_in, conv_w, conv_b, w_q, w_k, w_v, b_i, b_f, g_head, skip_a, w_proj_a,
           s5_log_dt, s5_A_re, s5_A_im, s5_B_re, s5_B_im, s5_C_re, s5_C_im, s5_D, w_glu_b, w_out, g_ffn,
           w_ff_gate, w_ff_up, w_ff_down, w_router, b_router, w_moe_gate, w_moe_up, w_moe_down,
           g_ple, w_ple, w_pg, g_final):
    bsz, seq, d = x_prompt.shape
    bs = x_sample.shape[0]
    depth = g_mix.shape[0]
    nh = b_i.shape[1]
    d_a = conv_w.shape[-1]
    d_b = s5_D.shape[-1]
    n_p = bsz * seq
    t = n_p + bs
    tl = _tile_plan(t, seq)
    tm, tn = tl["tm"], tl["tn"]

    h = jnp.concatenate([x_prompt.reshape(n_p, d), x_sample.reshape(bs, d)], axis=0).astype(F32)
    p_all = jnp.concatenate([p_prompt.reshape(depth, n_p, -1), p_sample.reshape(depth, bs, -1)], axis=1)

    col_ua, col_ub, col_oa = 0, d_a // d_b, (d_a + d_b) // d_a
    col_ga = (2 * d_a + d_b) // tn
    col_gb = (2 * d_a + d_b + d) // tn
    col_if = (2 * d_a + d_b + 2 * d) // (2 * V7X_LANES)

    states = [[] for _ in range(12)]
    for i in range(depth):
        lw = _layer_weights(i, conv_w, conv_b, w_q, w_k, w_v, b_i, b_f, g_head, skip_a)
        sp = _s5_params(dict(log_dt=s5_log_dt[i], a_re=s5_A_re[i], a_im=s5_A_im[i], b_re=s5_B_re[i],
                             b_im=s5_B_im[i], c_re=s5_C_re[i], c_im=s5_C_im[i], d_skip=s5_D[i]))
        proj = norm_matmul(h, g_mix[i], _reorder_w_in(w_in[i], d_a, nh, d_b, d), tm=tm, tn=tn, name="in_proj")

        a_pre, c_p, n_pp, m_p, conv_p = mlstm_prompt(proj, col_ua, col_oa, col_if, t, bsz, seq, lw,
                                                     L=tl["mlstm_chunk"])
        a_pre, c_s, n_s, m_s, conv_s = mlstm_sample(proj, col_ua, col_oa, col_if, n_p, a_pre,
                                                    state_mlstm_C[i].astype(F32), state_mlstm_n[i].astype(F32),
                                                    state_mlstm_m[i].astype(F32), state_mlstm_conv[i].astype(F32), lw)
        gy, sre_p, sim_p = s5_prompt(proj, col_ub, t, bsz, seq, sp, tc=tl["s5_chunk"], lane_blk=tl["s5_lane_blk"])
        gy, sre_s, sim_s = s5_sample(proj, col_ub, n_p, gy, state_s5_re[i].astype(F32),
                                     state_s5_im[i].astype(F32), sp)
        b_gated = glu_gated_matmul(gy, w_glu_b[i], proj, col_gb, tm=tm, tn=tn)
        mix = gated_add_matmul(a_pre, w_proj_a[i], proj, col_ga, b_gated, tm=tm, tn=tn)
        h = matmul(mix, w_out[i], h, tm=tm, tn=tn, name="out_proj")

        j = i // 2
        if i % 2 == 0:
            hid = norm_swiglu_up(h, g_ffn[i], w_ff_gate[j], w_ff_up[j], tm=tm, tn=tn)
            h = matmul(hid, w_ff_down[j], h, tm=tm, tn=tl["tn_down"], name="ffn_down")
        else:
            h = moe_layer(h, g_ffn[i], w_router[j], b_router[j], w_moe_gate[j], w_moe_up[j], w_moe_down[j],
                          tm=tm, tr=tl["moe_tr"], tn_up=tl["moe_tn_up"], tn_down=tl["moe_tn_down"],
                          tr_gather=tl["gather_tr"], tr_combine=tl["combine_tr"])
        h = ple_matmul(h, g_ple[i], p_all[i], w_pg[i], w_ple[i], tm=tm, tn=tn)

        g_s, p_s = S5_STATE, sre_p.shape[-1] // S5_STATE
        new = [c_p, n_pp, m_p, conv_p, sre_p.reshape(bsz, p_s, g_s), sim_p.reshape(bsz, p_s, g_s),
               c_s, n_s, m_s, conv_s, sre_s.reshape(bs, p_s, g_s), sim_s.reshape(bs, p_s, g_s)]
        for lst, s in zip(states, new):
            lst.append(s)

    y = rmsnorm_rows(h, g_final, tm=tl["norm_tm"])
    y_prompt = y[:n_p].reshape(bsz, seq, d)
    y_sample = y[n_p:].reshape(bs, 1, d)
    return (y_prompt, y_sample) + tuple(jnp.stack(lst) for lst in states)
```

```python
import functools
import math

import numpy as np
import jax
import jax.numpy as jnp
from jax import lax
from jax.experimental import pallas as pl
from jax.experimental.pallas import tpu as pltpu

F32 = jnp.float32
BF16 = jnp.bfloat16
EPS = 1e-6

V7X_VMEM_BYTES = 64 * 1024 * 1024
V7X_LANES = 128
V7X_SUBLANES = 8
VMEM_LIMIT = 56 * 1024 * 1024

N_HEADS = 4
CONV_W = 4
S5_GROUP = 16
S5_STATE = 64
N_EXPERTS = 8
TOP_K = 2


def _cparams(n_axes, vmem=VMEM_LIMIT):
    return pltpu.CompilerParams(dimension_semantics=("arbitrary",) * n_axes, vmem_limit_bytes=vmem)


def _sigmoid(x):
    return 1.0 / (1.0 + jnp.exp(-x))


def _silu(x):
    return x * _sigmoid(x)


def _gelu_tanh(x):
    return 0.5 * x * (1.0 + jnp.tanh(math.sqrt(2.0 / math.pi) * (x + 0.044715 * (x * x * x))))


def _log_sigmoid(x):
    return jnp.minimum(x, 0.0) - jnp.log(1.0 + jnp.exp(-jnp.abs(x)))


def _dot(a, b):
    return jnp.dot(a, b, preferred_element_type=F32)


def _dot_nt(a, b):
    return lax.dot_general(a, b, (((1,), (1,)), ((), ())), preferred_element_type=F32)


def _dot_tn(a, b):
    return lax.dot_general(a, b, (((0,), (0,)), ((), ())), preferred_element_type=F32)


def _rmsnorm_rows(x, g):
    ms = jnp.mean(x * x, axis=-1, keepdims=True)
    return x * lax.rsqrt(ms + EPS) * g


def _row_chunks(tm):
    for rc in (256, 208, 128, 104, 64, 32, 16, 8):
        if tm % rc == 0:
            return rc
    return tm


def _norm_to_scratch(x_ref, g_ref, xn_ref, tm):
    rc = _row_chunks(tm)

    def body(r, c):
        rows = pl.ds(pl.multiple_of(r * rc, rc), rc)
        xn_ref[rows, :] = _rmsnorm_rows(x_ref[rows, :], g_ref[...]).astype(BF16)
        return c

    lax.fori_loop(0, tm // rc, body, 0)


def _k_norm_mm(x_ref, g_ref, w_ref, o_ref, xn_ref, *, tm):
    @pl.when(pl.program_id(1) == 0)
    def _():
        _norm_to_scratch(x_ref, g_ref, xn_ref, tm)

    o_ref[...] = _dot(xn_ref[...], w_ref[...].astype(BF16)).astype(o_ref.dtype)


def norm_matmul(x, g, w, *, tm, tn, out_dtype=F32, name="norm_mm"):
    t, k = x.shape
    n = w.shape[1]
    return pl.pallas_call(
        functools.partial(_k_norm_mm, tm=tm),
        grid=(t // tm, pl.cdiv(n, tn)),
        in_specs=[pl.BlockSpec((tm, k), lambda i, j: (i, 0)),
                  pl.BlockSpec((1, k), lambda i, j: (0, 0)),
                  pl.BlockSpec((k, tn), lambda i, j: (0, j))],
        out_specs=pl.BlockSpec((tm, tn), lambda i, j: (i, j)),
        out_shape=jax.ShapeDtypeStruct((t, n), out_dtype),
        scratch_shapes=[pltpu.VMEM((tm, k), BF16)],
        compiler_params=_cparams(2), name=name,
    )(x, g.reshape(1, k), w)


def _k_mm_res(x_ref, w_ref, r_ref, o_ref):
    o_ref[...] = r_ref[...] + _dot(x_ref[...], w_ref[...].astype(BF16))


def _k_mm(x_ref, w_ref, o_ref):
    o_ref[...] = _dot(x_ref[...], w_ref[...].astype(BF16)).astype(o_ref.dtype)


def matmul(x, w, res=None, *, tm, tn, out_dtype=F32, name="mm"):
    t, k = x.shape
    n = w.shape[1]
    in_specs = [pl.BlockSpec((tm, k), lambda i, j: (i, 0)),
                pl.BlockSpec((k, tn), lambda i, j: (0, j))]
    args = [x, w]
    body = _k_mm
    if res is not None:
        in_specs.append(pl.BlockSpec((tm, tn), lambda i, j: (i, j)))
        args.append(res)
        body = _k_mm_res
    return pl.pallas_call(
        body, grid=(t // tm, pl.cdiv(n, tn)), in_specs=in_specs,
        out_specs=pl.BlockSpec((tm, tn), lambda i, j: (i, j)),
        out_shape=jax.ShapeDtypeStruct((t, n), out_dtype),
        compiler_params=_cparams(2), name=name,
    )(*args)


def _k_glu_gated(x_ref, wv_ref, wg_ref, gb_ref, o_ref):
    x = x_ref[...]
    val = _dot(x, wv_ref[...].astype(BF16))
    gate = _dot(x, wg_ref[...].astype(BF16))
    o_ref[...] = _sigmoid(gb_ref[...]) * (val * _sigmoid(gate))


def glu_gated_matmul(x, w, proj, gate_col0, *, tm, tn, name="glu_b"):
    t, k = x.shape
    n = w.shape[1] // 2
    nj = n // tn
    return pl.pallas_call(
        _k_glu_gated, grid=(t // tm, nj),
        in_specs=[pl.BlockSpec((tm, k), lambda i, j: (i, 0)),
                  pl.BlockSpec((k, tn), lambda i, j: (0, j)),
                  pl.BlockSpec((k, tn), lambda i, j: (0, j + nj)),
                  pl.BlockSpec((tm, tn), lambda i, j: (i, gate_col0 + j))],
        out_specs=pl.BlockSpec((tm, tn), lambda i, j: (i, j)),
        out_shape=jax.ShapeDtypeStruct((t, n), F32),
        compiler_params=_cparams(2), name=name,
    )(x, w, w, proj)


def _k_mm_gated_add(x_ref, w_ref, ga_ref, b_ref, o_ref):
    acc = _dot(x_ref[...], w_ref[...].astype(BF16))
    o_ref[...] = (_sigmoid(ga_ref[...]) * acc + b_ref[...]).astype(o_ref.dtype)


def gated_add_matmul(x, w, proj, gate_col0, b, *, tm, tn, name="proj_a"):
    t, k = x.shape
    n = w.shape[1]
    return pl.pallas_call(
        _k_mm_gated_add, grid=(t // tm, n // tn),
        in_specs=[pl.BlockSpec((tm, k), lambda i, j: (i, 0)),
                  pl.BlockSpec((k, tn), lambda i, j: (0, j)),
                  pl.BlockSpec((tm, tn), lambda i, j: (i, gate_col0 + j)),
                  pl.BlockSpec((tm, tn), lambda i, j: (i, j))],
        out_specs=pl.BlockSpec((tm, tn), lambda i, j: (i, j)),
        out_shape=jax.ShapeDtypeStruct((t, n), BF16),
        compiler_params=_cparams(2), name=name,
    )(x, w, proj, b)


def _k_swiglu_up(x_ref, g_ref, wg_ref, wu_ref, o_ref, xn_ref, *, tm):
    @pl.when(pl.program_id(1) == 0)
    def _():
        _norm_to_scratch(x_ref, g_ref, xn_ref, tm)

    xn = xn_ref[...]
    gate = _dot(xn, wg_ref[...].astype(BF16))
    up = _dot(xn, wu_ref[...].astype(BF16))
    o_ref[...] = (_silu(gate) * up).astype(o_ref.dtype)


def norm_swiglu_up(x, g, w_gate, w_up, *, tm, tn, name="ffn_up"):
    t, k = x.shape
    n = w_gate.shape[1]
    return pl.pallas_call(
        functools.partial(_k_swiglu_up, tm=tm),
        grid=(t // tm, pl.cdiv(n, tn)),
        in_specs=[pl.BlockSpec((tm, k), lambda i, j: (i, 0)),
                  pl.BlockSpec((1, k), lambda i, j: (0, 0)),
                  pl.BlockSpec((k, tn), lambda i, j: (0, j)),
                  pl.BlockSpec((k, tn), lambda i, j: (0, j))],
        out_specs=pl.BlockSpec((tm, tn), lambda i, j: (i, j)),
        out_shape=jax.ShapeDtypeStruct((t, n), BF16),
        scratch_shapes=[pltpu.VMEM((tm, k), BF16)],
        compiler_params=_cparams(2), name=name,
    )(x, g.reshape(1, k), w_gate, w_up)


def _k_ple(x_ref, g_ref, p_ref, wpg_ref, wple_ref, r_ref, o_ref, xn_ref, pb_ref, *, tm):
    @pl.when(pl.program_id(1) == 0)
    def _():
        _norm_to_scratch(x_ref, g_ref, xn_ref, tm)
        pb_ref[...] = p_ref[...].astype(BF16)

    gate = _dot(xn_ref[...], wpg_ref[...].astype(BF16))
    emb = _dot(pb_ref[...], wple_ref[...].astype(BF16))
    o_ref[...] = r_ref[...] + emb * _sigmoid(gate)


def ple_matmul(x, g, p, w_pg, w_ple, *, tm, tn, name="ple"):
    t, k = x.shape
    kp = p.shape[1]
    return pl.pallas_call(
        functools.partial(_k_ple, tm=tm),
        grid=(t // tm, k // tn),
        in_specs=[pl.BlockSpec((tm, k), lambda i, j: (i, 0)),
                  pl.BlockSpec((1, k), lambda i, j: (0, 0)),
                  pl.BlockSpec((tm, kp), lambda i, j: (i, 0)),
                  pl.BlockSpec((k, tn), lambda i, j: (0, j)),
                  pl.BlockSpec((kp, tn), lambda i, j: (0, j)),
                  pl.BlockSpec((tm, tn), lambda i, j: (i, j))],
        out_specs=pl.BlockSpec((tm, tn), lambda i, j: (i, j)),
        out_shape=jax.ShapeDtypeStruct((t, k), F32),
        scratch_shapes=[pltpu.VMEM((tm, k), BF16), pltpu.VMEM((tm, kp), BF16)],
        compiler_params=_cparams(2), name=name,
    )(x, g.reshape(1, k), p, w_pg, w_ple, x)


def _k_final_norm(xp_ref, xs_ref, g_ref, op_ref, os_ref, *, n_prompt_tiles):
    i = pl.program_id(0)

    @pl.when(i < n_prompt_tiles)
    def _():
        op_ref[...] = _rmsnorm_rows(xp_ref[...], g_ref[...])

    @pl.when(i == n_prompt_tiles)
    def _():
        os_ref[...] = _rmsnorm_rows(xs_ref[...], g_ref[...])


def final_norm_split(x, g, n_prompt, *, tm):
    t, k = x.shape
    bs = t - n_prompt
    npt = n_prompt // tm
    last = npt - 1
    return pl.pallas_call(
        functools.partial(_k_final_norm, n_prompt_tiles=npt), grid=(npt + 1,),
        in_specs=[pl.BlockSpec((tm, k), lambda i: (jnp.minimum(i, last), 0)),
                  pl.BlockSpec((bs, k), lambda i: (n_prompt // bs, 0)),
                  pl.BlockSpec((1, k), lambda i: (0, 0))],
        out_specs=[pl.BlockSpec((tm, k), lambda i: (jnp.minimum(i, last), 0)),
                   pl.BlockSpec((bs, k), lambda i: (0, 0))],
        out_shape=[jax.ShapeDtypeStruct((n_prompt, k), F32), jax.ShapeDtypeStruct((bs, k), F32)],
        compiler_params=_cparams(1), name="final_norm",
    )(x, x, g.reshape(1, k))


def _split_hi_lo(x):
    hi = x.astype(BF16)
    lo = (x - hi.astype(F32)).astype(BF16)
    return hi, lo


def _k_mlstm_prompt(u_ref, o_ref_in, if_ref, cw_ref, cb_ref, wq_ref, wk_ref, wv_ref, bif_ref, gh_ref, sk_ref,
                    tri_ref, a_init, a_ref, c_out, n_out, m_out, conv_out, upad_ref, *, L, dh):
    del a_init
    c_idx = pl.program_id(1)
    nh = N_HEADS

    @pl.when(c_idx == 0)
    def _():
        c_out[...] = jnp.zeros_like(c_out)
        n_out[...] = jnp.zeros_like(n_out)
        m_out[...] = jnp.zeros_like(m_out)
        upad_ref[pl.ds(0, 8), :] = jnp.zeros((8, nh * dh), F32)

    @pl.when(c_idx > 0)
    def _():
        upad_ref[pl.ds(0, 8), :] = upad_ref[pl.ds(L, 8), :]

    u = u_ref[...]
    upad_ref[pl.ds(8, L), :] = u
    conv = cb_ref[...] + u * cw_ref[CONV_W - 1:CONV_W, :]
    for j in range(CONV_W - 1):
        conv = conv + upad_ref[pl.ds(8 - (CONV_W - 1) + j, L), :] * cw_ref[j:j + 1, :]
    cact = _silu(conv)
    conv_out[0] = upad_ref[pl.ds(L, 8), :]

    pre = if_ref[...]
    li = pre[:, :V7X_LANES] + bif_ref[:, :V7X_LANES]
    lf = _log_sigmoid(pre[:, V7X_LANES:] + bif_ref[:, V7X_LANES:])
    tri = tri_ref[...]
    lf_hi, lf_mid = _split_hi_lo(lf)
    lf_lo = (lf - lf_hi.astype(F32) - lf_mid.astype(F32)).astype(BF16)
    bcum = _dot(tri, lf_hi) + _dot(tri, lf_mid) + _dot(tri, lf_lo)
    li_t = li.T
    b_t = bcum.T
    row_id = lax.broadcasted_iota(jnp.int32, (L, L), 0)
    col_id = lax.broadcasted_iota(jnp.int32, (L, L), 1)
    causal = col_id <= row_id
    lane = lax.broadcasted_iota(jnp.int32, (1, V7X_LANES), 1)
    m_row = m_out[0]
    m_new_row = m_row

    for h in range(nh):
        hs = slice(h * dh, (h + 1) * dh)
        ch = cact[:, hs].astype(BF16)
        uh = u[:, hs].astype(BF16)
        q = _dot(ch, wq_ref[h].astype(BF16))
        k = _dot(ch, wk_ref[h].astype(BF16)) * (dh ** -0.5)
        v = _dot(uh, wv_ref[h].astype(BF16))
        qb, kb, vb = q.astype(BF16), k.astype(BF16), v.astype(BF16)

        b_col = bcum[:, h:h + 1]
        li_col = li[:, h:h + 1]
        r_row = li_t[h:h + 1, :] - b_t[h:h + 1, :]
        m_prev = m_row[:, h:h + 1]
        d = jnp.where(causal, b_col + r_row, -jnp.inf)
        inter = b_col + m_prev
        m_t = jnp.maximum(inter, jnp.max(d, axis=-1, keepdims=True))
        w_inter = jnp.exp(inter - m_t)
        s = _dot_nt(qb, kb) * jnp.exp(d - m_t)
        c_prev = c_out[0, h]
        n_prev = n_out[0, h:h + 1, :]
        num = w_inter * _dot(qb, c_prev.astype(BF16)) + _dot(s.astype(BF16), vb)
        den = w_inter * jnp.sum(q * n_prev, axis=-1, keepdims=True) + jnp.sum(s, axis=-1, keepdims=True)
        hh = num / jnp.maximum(jnp.abs(den), jnp.exp(-m_t))

        b_last = b_col[L - 1:L, :]
        g_col = b_last - b_col + li_col
        m_new = jnp.maximum(b_last + m_prev, jnp.max(g_col, axis=0, keepdims=True))
        decay = jnp.exp(b_last + m_prev - m_new)
        wk_ = jnp.exp(g_col - m_new) * k
        c_out[0, h] = decay * c_prev + _dot_tn(wk_.astype(BF16), vb)
        n_out[0, h:h + 1, :] = decay * n_prev + jnp.sum(wk_, axis=0, keepdims=True)
        m_new_row = jnp.where(lane == h, m_new, m_new_row)

        hn = _rmsnorm_rows(hh, gh_ref[:, hs])
        gated = (hn + sk_ref[:, hs] * cact[:, hs]) * _sigmoid(o_ref_in[:, hs])
        a_ref[:, hs] = gated.astype(a_ref.dtype)

    m_out[0] = m_new_row


def mlstm_prompt(proj, col_u, col_o, col_if, n_rows, bsz, seq, lw, *, L):
    dh = lw["w_q"].shape[-1]
    da = N_HEADS * dh
    nc = seq // L
    tri = jnp.asarray(np.tril(np.ones((L, L), np.float32)), BF16)
    row_blk = lambda b, c: b * nc + c
    full = lambda *shape: pl.BlockSpec(shape, lambda b, c: (0,) * len(shape))
    outs = pl.pallas_call(
        functools.partial(_k_mlstm_prompt, L=L, dh=dh),
        grid=(bsz, nc),
        in_specs=[pl.BlockSpec((L, da), lambda b, c: (row_blk(b, c), col_u)),
                  pl.BlockSpec((L, da), lambda b, c: (row_blk(b, c), col_o)),
                  pl.BlockSpec((L, 2 * V7X_LANES), lambda b, c: (row_blk(b, c), col_if)),
                  full(CONV_W, da), full(1, da), full(N_HEADS, dh, dh), full(N_HEADS, dh, dh),
                  full(N_HEADS, dh, dh), full(1, 2 * V7X_LANES), full(1, da), full(1, da), full(L, L),
                  pl.BlockSpec(memory_space=pl.ANY)],
        out_specs=[pl.BlockSpec((L, da), lambda b, c: (row_blk(b, c), 0)),
                   pl.BlockSpec((1, N_HEADS, dh, dh), lambda b, c: (b, 0, 0, 0)),
                   pl.BlockSpec((1, N_HEADS, dh), lambda b, c: (b, 0, 0)),
                   pl.BlockSpec((1, 1, V7X_LANES), lambda b, c: (b, 0, 0)),
                   pl.BlockSpec((1, 8, da), lambda b, c: (b, 0, 0))],
        out_shape=[jax.ShapeDtypeStruct((n_rows, da), BF16),
                   jax.ShapeDtypeStruct((bsz, N_HEADS, dh, dh), F32),
                   jax.ShapeDtypeStruct((bsz, N_HEADS, dh), F32),
                   jax.ShapeDtypeStruct((bsz, 1, V7X_LANES), F32),
                   jax.ShapeDtypeStruct((bsz, 8, da), F32)],
        scratch_shapes=[pltpu.VMEM((L + 8, da), F32)],
        input_output_aliases={12: 0},
        compiler_params=_cparams(2), name="mlstm_prompt",
    )(proj, proj, proj, lw["conv_w"], lw["conv_b"], lw["w_q"], lw["w_k"], lw["w_v"], lw["b_if"],
      lw["g_head"], lw["skip"], tri, jnp.zeros((n_rows, da), BF16))
    a_pre, c_p, n_p, m_p, conv_p = outs
    return a_pre, c_p, n_p, m_p[:, 0, :N_HEADS], conv_p[:, 8 - (CONV_W - 1):, :]


def _k_mlstm_sample_pre(u_ref, if_ref, conv_ref, m_ref, cw_ref, cb_ref, wq_ref, wk_ref, wv_ref, bif_ref,
                        q_out, k_out, v_out, c_out, gates_out, conv_out, *, dh):
    nh = N_HEADS
    u = u_ref[...]
    conv = cb_ref[...] + u * cw_ref[CONV_W - 1:CONV_W, :]
    for j in range(CONV_W - 1):
        conv = conv + conv_ref[j] * cw_ref[j:j + 1, :]
        if j > 0:
            conv_out[j - 1] = conv_ref[j]
    conv_out[CONV_W - 2] = u
    cact = _silu(conv)
    c_out[...] = cact
    for h in range(nh):
        hs = slice(h * dh, (h + 1) * dh)
        ch = cact[:, hs].astype(BF16)
        q_out[:, hs] = _dot(ch, wq_ref[h].astype(BF16))
        k_out[:, hs] = _dot(ch, wk_ref[h].astype(BF16)) * (dh ** -0.5)
        v_out[:, hs] = _dot(u[:, hs].astype(BF16), wv_ref[h].astype(BF16))
    pre = if_ref[...]
    li = pre[:, :V7X_LANES] + bif_ref[:, :V7X_LANES]
    lf = _log_sigmoid(pre[:, V7X_LANES:] + bif_ref[:, V7X_LANES:])
    m_prev = m_ref[...]
    inter = lf + m_prev
    m_t = jnp.maximum(inter, li)
    gates_out[0] = jnp.exp(inter - m_t)
    gates_out[1] = jnp.exp(li - m_t)
    gates_out[2] = jnp.exp(-m_t)
    gates_out[3] = m_t


def _k_mlstm_sample_step(q_ref, k_ref, v_ref, gates_ref, c_ref, n_ref, cact_ref, o_ref_in, gh_ref, sk_ref,
                         a_any, c_any, c_out, n_out, a_out, hh_ref, *, bt, dh):
    del a_any, c_any
    i = pl.program_id(0)
    nh = N_HEADS
    q = q_ref[...]
    k = k_ref[...]
    v = v_ref[...]
    w_inter = gates_ref[0]
    w_new = gates_ref[1]
    e_neg_m = gates_ref[2]
    n_prev = n_ref[...]
    rows = pl.ds(pl.multiple_of(i * bt, bt), bt)
    for h in range(nh):
        hs = slice(h * dh, (h + 1) * dh)
        qh, kh, vh, nh_prev = q[:, hs], k[:, hs], v[:, hs], n_prev[:, hs]
        q_t = qh.T
        k_t = kh.T
        wi = w_inter[:, h:h + 1]
        wn = w_new[:, h:h + 1]
        s = jnp.sum(qh * kh, axis=-1, keepdims=True) * wn
        den = wi * jnp.sum(qh * nh_prev, axis=-1, keepdims=True) + s
        wv = wn * vh
        qc_rows = []
        for bl in range(bt):
            c_prev = c_ref[bl, h]
            qc_rows.append(jnp.sum(q_t[:, bl:bl + 1] * c_prev, axis=0, keepdims=True))
            c_out[bl, h] = wi[bl:bl + 1, :] * c_prev + k_t[:, bl:bl + 1] * wv[bl:bl + 1, :]
        qc = jnp.concatenate(qc_rows, axis=0)
        num = wi * qc + s * vh
        hh = num / jnp.maximum(jnp.abs(den), e_neg_m[:, h:h + 1])
        hh_ref[rows, hs] = hh
        n_out[:, hs] = wi * nh_prev + wn * kh

    @pl.when(i == pl.num_programs(0) - 1)
    def _():
        for h in range(nh):
            hs = slice(h * dh, (h + 1) * dh)
            hn = _rmsnorm_rows(hh_ref[:, hs], gh_ref[:, hs])
            a_out[:, hs] = ((hn + sk_ref[:, hs] * cact_ref[:, hs]) * _sigmoid(o_ref_in[:, hs])).astype(a_out.dtype)


def mlstm_sample(proj, col_u, col_o, col_if, row0, a_pre_all, c_all, layer, c_new_all, n0, m0, conv0, lw, *,
                 bt=8):
    _, bs, nh, dh, _ = c_all.shape
    da = nh * dh
    rb = row0 // bs
    m_pad = jnp.pad(m0, ((0, 0), (0, V7X_LANES - nh)))
    conv_t = jnp.transpose(conv0, (1, 0, 2))
    full = lambda *shape: pl.BlockSpec(shape, lambda i: (0,) * len(shape))
    q, k, v, cact, gates, conv_new = pl.pallas_call(
        functools.partial(_k_mlstm_sample_pre, dh=dh),
        grid=(1,),
        in_specs=[pl.BlockSpec((bs, da), lambda i: (rb, col_u)),
                  pl.BlockSpec((bs, 2 * V7X_LANES), lambda i: (rb, col_if)),
                  full(CONV_W - 1, bs, da), full(bs, V7X_LANES), full(CONV_W, da), full(1, da),
                  full(nh, dh, dh), full(nh, dh, dh), full(nh, dh, dh), full(1, 2 * V7X_LANES)],
        out_specs=[full(bs, da), full(bs, da), full(bs, da), full(bs, da), full(4, bs, V7X_LANES),
                   full(CONV_W - 1, bs, da)],
        out_shape=[jax.ShapeDtypeStruct((bs, da), F32)] * 4
                  + [jax.ShapeDtypeStruct((4, bs, V7X_LANES), F32),
                     jax.ShapeDtypeStruct((CONV_W - 1, bs, da), F32)],
        compiler_params=_cparams(1), name="mlstm_sample_pre",
    )(proj, proj, conv_t, m_pad, lw["conv_w"], lw["conv_b"], lw["w_q"], lw["w_k"], lw["w_v"], lw["b_if"])

    blk = lambda *shape: pl.BlockSpec(shape, lambda i: (i,) + (0,) * (len(shape) - 1))
    cst = lambda *shape: pl.BlockSpec(shape, lambda i: (0,) * len(shape))
    c_blk = pl.BlockSpec((None, bt, nh, dh, dh), lambda i: (layer, i, 0, 0, 0))
    c_new_all, n_new, a_pre_all = pl.pallas_call(
        functools.partial(_k_mlstm_sample_step, bt=bt, dh=dh),
        grid=(bs // bt,),
        in_specs=[blk(bt, da), blk(bt, da), blk(bt, da),
                  pl.BlockSpec((4, bt, V7X_LANES), lambda i: (0, i, 0)),
                  c_blk, blk(bt, da), cst(bs, da),
                  pl.BlockSpec((bs, da), lambda i: (rb, col_o)), cst(1, da), cst(1, da),
                  pl.BlockSpec(memory_space=pl.ANY), pl.BlockSpec(memory_space=pl.ANY)],
        out_specs=[c_blk, blk(bt, da), pl.BlockSpec((bs, da), lambda i: (rb, 0))],
        out_shape=[jax.ShapeDtypeStruct(c_new_all.shape, F32), jax.ShapeDtypeStruct((bs, da), F32),
                   jax.ShapeDtypeStruct(a_pre_all.shape, a_pre_all.dtype)],
        scratch_shapes=[pltpu.VMEM((bs, da), F32)],
        input_output_aliases={10: 2, 11: 0},
        compiler_params=_cparams(1), name="mlstm_sample_step",
    )(q, k, v, gates, c_all, n0.reshape(bs, da), cact, proj, lw["g_head"], lw["skip"], a_pre_all, c_new_all)
    m_new = gates[3][:, :nh]
    return a_pre_all, c_new_all, n_new.reshape(bs, nh, dh), m_new, jnp.transpose(conv_new, (1, 0, 2))


S5_SLAB_GROUPS = V7X_LANES // S5_GROUP
S5_SLAB_STATES = S5_SLAB_GROUPS * S5_STATE


def _s5_params(lp):
    g, p = lp["a_re"].shape
    dt = jnp.exp(lp["log_dt"].astype(F32))[:, None]
    a_re = lp["a_re"].astype(F32)
    a_im = lp["a_im"].astype(F32)
    lam_re = a_re * dt
    lam_im = a_im * dt
    mag = jnp.exp(lam_re)
    ab_re = mag * jnp.cos(lam_im)
    ab_im = mag * jnp.sin(lam_im)
    den = a_re * a_re + a_im * a_im
    nr = ab_re - 1.0
    ni = ab_im
    k_re = (nr * a_re + ni * a_im) / den
    k_im = (ni * a_re - nr * a_im) / den
    b_re = lp["b_re"].astype(F32)
    b_im = lp["b_im"].astype(F32)
    bb_re = k_re[..., None] * b_re - k_im[..., None] * b_im
    bb_im = k_re[..., None] * b_im + k_im[..., None] * b_re
    ns = g // S5_SLAB_GROUPS
    eye = jnp.eye(S5_SLAB_GROUPS, dtype=F32)

    def in_blockdiag(bb):
        bs = bb.reshape(ns, S5_SLAB_GROUPS, p, S5_GROUP)
        w = jnp.einsum("ab,sapc->sacbp", eye, bs)
        return w.reshape(ns, V7X_LANES, S5_SLAB_STATES).astype(BF16)

    def out_blockdiag(cc):
        cs = cc.astype(F32).reshape(ns, S5_SLAB_GROUPS, S5_GROUP, p)
        w = jnp.einsum("ab,sacp->sapbc", eye, cs)
        return w.reshape(ns, S5_SLAB_STATES, V7X_LANES)

    wc = jnp.concatenate([out_blockdiag(lp["c_re"]), -out_blockdiag(lp["c_im"])], axis=1).astype(BF16)
    return dict(lam_re=lam_re.reshape(1, g * p), lam_im=lam_im.reshape(1, g * p),
                ab_re=ab_re.reshape(1, g * p), ab_im=ab_im.reshape(1, g * p),
                wb_re=in_blockdiag(bb_re), wb_im=in_blockdiag(bb_im), wc=wc,
                d_skip=lp["d_skip"].astype(F32).reshape(1, g * S5_GROUP))


def _s5_powers(sp, ks):
    kk = jnp.asarray(ks, F32)[:, None]
    mag = jnp.exp(kk * sp["lam_re"])
    return mag * jnp.cos(kk * sp["lam_im"]), mag * jnp.sin(kk * sp["lam_im"])


def _s5_input_proj(u_bf, wbr_ref, wbi_ref, bur_ref, bui_ref):
    ns = wbr_ref.shape[0]
    for s in range(ns):
        us = u_bf[:, s * V7X_LANES:(s + 1) * V7X_LANES]
        cols = slice(s * S5_SLAB_STATES, (s + 1) * S5_SLAB_STATES)
        bur_ref[:, cols] = _dot(us, wbr_ref[s])
        bui_ref[:, cols] = _dot(us, wbi_ref[s])


def _s5_output_proj(xr_ref, xi_ref, wc_ref, skip):
    ns = wc_ref.shape[0]
    outs = []
    for s in range(ns):
        cols = slice(s * S5_SLAB_STATES, (s + 1) * S5_SLAB_STATES)
        xcat = jnp.concatenate([xr_ref[:, cols].astype(BF16), xi_ref[:, cols].astype(BF16)], axis=1)
        y = _dot(xcat, wc_ref[s]) + skip[:, s * V7X_LANES:(s + 1) * V7X_LANES]
        outs.append(_gelu_tanh(y).astype(BF16))
    return jnp.concatenate(outs, axis=1)


def _k_s5_prompt(u_ref, wbr_ref, wbi_ref, wc_ref, d_ref, abr_ref, abi_ref, tpr_ref, tpi_ref, ajr_ref, aji_ref,
                 perm_ref, permt_ref, gy_init, gy_ref, sre_out, sim_out,
                 bur0, bur1, bui0, bui1, xb0, xb1, *, tc):
    del gy_init
    c_idx = pl.program_id(1)
    nsteps = tc // V7X_SUBLANES
    ns = wbr_ref.shape[0]
    lw = S5_SLAB_STATES
    pair = 2 * V7X_SUBLANES
    bur, bui, xb = (bur0, bur1), (bui0, bui1), (xb0, xb1)

    @pl.when(c_idx == 0)
    def _():
        sre_out[...] = jnp.zeros_like(sre_out)
        sim_out[...] = jnp.zeros_like(sim_out)

    u = u_ref[...]
    u_hi, u_lo = _split_hi_lo(u)
    perm = perm_ref[...]
    up_hi = _dot(perm, u_hi)
    skip = d_ref[...] * (up_hi + _dot(perm, u_lo))
    up_bf = up_hi.astype(BF16)

    sub = lax.broadcasted_iota(jnp.int32, (V7X_SUBLANES, lw), 0)

    def input_proj(s):
        us = up_bf[:, s * V7X_LANES:(s + 1) * V7X_LANES]
        bur[s % 2][...] = _dot(us, wbr_ref[s])
        bui[s % 2][...] = _dot(us, wbi_ref[s])

    def scan(s):
        br, bi, xo = bur[s % 2], bui[s % 2], xb[s % 2]
        lanes = slice(s * lw, (s + 1) * lw)
        ar = jnp.broadcast_to(abr_ref[:, lanes], (V7X_SUBLANES, lw))
        ai = jnp.broadcast_to(abi_ref[:, lanes], (V7X_SUBLANES, lw))
        er = jnp.zeros((V7X_SUBLANES, lw), F32)
        ei = er
        for i in range(nsteps):
            rows = slice(i * V7X_SUBLANES, (i + 1) * V7X_SUBLANES)
            er, ei = (ar * er - ai * ei + br[rows, :], ar * ei + ai * er + bi[rows, :])
            br[rows, :] = er
            bi[rows, :] = ei
        for d, row in ((1, 0), (2, 1), (4, 3)):
            pr = ajr_ref[row:row + 1, lanes]
            pi = aji_ref[row:row + 1, lanes]
            sr = pltpu.roll(er, d, 0)
            si = pltpu.roll(ei, d, 0)
            keep = sub >= d
            er, ei = (er + jnp.where(keep, pr * sr - pi * si, 0.0),
                      ei + jnp.where(keep, pr * si + pi * sr, 0.0))
        c0r = jnp.broadcast_to(sre_out[0, :, lanes], (V7X_SUBLANES, lw))
        c0i = jnp.broadcast_to(sim_out[0, :, lanes], (V7X_SUBLANES, lw))
        ajr = ajr_ref[:, lanes]
        aji = aji_ref[:, lanes]
        fr = ajr * c0r - aji * c0i + er
        fi = ajr * c0i + aji * c0r + ei
        cin_r = jnp.where(sub >= 1, pltpu.roll(fr, 1, 0), c0r)
        cin_i = jnp.where(sub >= 1, pltpu.roll(fi, 1, 0), c0i)
        sre_out[0, :, lanes] = fr[V7X_SUBLANES - 1:V7X_SUBLANES, :]
        sim_out[0, :, lanes] = fi[V7X_SUBLANES - 1:V7X_SUBLANES, :]

        cin2_r = jnp.concatenate([cin_r, cin_r], axis=0)
        cin2_i = jnp.concatenate([cin_i, cin_i], axis=0)
        for k in range(tc // pair):
            rows = slice(k * pair, (k + 1) * pair)
            pr = tpr_ref[rows, lanes]
            pi = tpi_ref[rows, lanes]
            xo[rows, :lw] = (br[rows, :] + (pr * cin2_r - pi * cin2_i)).astype(BF16)
            xo[rows, lw:] = (bi[rows, :] + (pr * cin2_i + pi * cin2_r)).astype(BF16)

    def output_proj(s):
        y = _dot(xb[s % 2][...], wc_ref[s]) + skip[:, s * V7X_LANES:(s + 1) * V7X_LANES]
        return _gelu_tanh(y).astype(BF16)

    outs = []
    input_proj(0)
    for s in range(ns):
        if s + 1 < ns:
            input_proj(s + 1)
        scan(s)
        outs.append(output_proj(s))
    g_perm = jnp.concatenate(outs, axis=1)
    gy_ref[...] = _dot(permt_ref[...], g_perm).astype(gy_ref.dtype)


def s5_prompt(proj, col_u, n_rows, bsz, seq, sp, *, tc=256):
    db = sp["d_skip"].shape[1]
    nch = sp["ab_re"].shape[1]
    nc = seq // tc
    nsteps = tc // V7X_SUBLANES
    tpr, tpi = _s5_powers(sp, np.repeat(np.arange(1, nsteps + 1), V7X_SUBLANES))
    ajr, aji = _s5_powers(sp, nsteps * np.arange(1, V7X_SUBLANES + 1))
    perm = np.zeros((tc, tc), np.float32)
    r = np.arange(tc)
    perm[r, (r % V7X_SUBLANES) * nsteps + r // V7X_SUBLANES] = 1.0
    full = lambda a: pl.BlockSpec(a.shape, lambda b, c: (0,) * a.ndim)
    consts = [sp["wb_re"], sp["wb_im"], sp["wc"], sp["d_skip"], sp["ab_re"], sp["ab_im"], tpr, tpi, ajr, aji,
              jnp.asarray(perm, BF16), jnp.asarray(perm.T, BF16)]
    gy, s_re, s_im = pl.pallas_call(
        functools.partial(_k_s5_prompt, tc=tc),
        grid=(bsz, nc),
        in_specs=[pl.BlockSpec((tc, db), lambda b, c: (b * nc + c, col_u))] + [full(a) for a in consts]
                 + [pl.BlockSpec(memory_space=pl.ANY)],
        out_specs=[pl.BlockSpec((tc, db), lambda b, c: (b * nc + c, 0)),
                   pl.BlockSpec((1, 1, nch), lambda b, c: (b, 0, 0)),
                   pl.BlockSpec((1, 1, nch), lambda b, c: (b, 0, 0))],
        out_shape=[jax.ShapeDtypeStruct((n_rows, db), BF16),
                   jax.ShapeDtypeStruct((bsz, 1, nch), F32), jax.ShapeDtypeStruct((bsz, 1, nch), F32)],
        scratch_shapes=[pltpu.VMEM((tc, S5_SLAB_STATES), F32)] * 4 + [pltpu.VMEM((tc, 2 * S5_SLAB_STATES), BF16)] * 2,
        input_output_aliases={1 + len(consts): 0},
        compiler_params=_cparams(2), name="s5_prompt",
    )(proj, *consts, jnp.zeros((n_rows, db), BF16))
    return gy, s_re, s_im


def _k_s5_sample(u_ref, x0r_ref, x0i_ref, wbr_ref, wbi_ref, wc_ref, d_ref, abr_ref, abi_ref, gy_any,
                 gy_ref, xr_out, xi_out):
    del gy_any
    u = u_ref[...]
    _s5_input_proj(u.astype(BF16), wbr_ref, wbi_ref, xr_out, xi_out)
    ar = abr_ref[...]
    ai = abi_ref[...]
    x0r = x0r_ref[...]
    x0i = x0i_ref[...]
    xr_out[...] = xr_out[...] + (ar * x0r - ai * x0i)
    xi_out[...] = xi_out[...] + (ar * x0i + ai * x0r)
    gy_ref[...] = _s5_output_proj(xr_out, xi_out, wc_ref, d_ref[...] * u).astype(gy_ref.dtype)


def s5_sample(proj, col_u, row0, gy_all, x0_re, x0_im, sp):
    bs = x0_re.shape[0]
    db = sp["d_skip"].shape[1]
    nch = sp["ab_re"].shape[1]
    rb = row0 // bs
    full = lambda a: pl.BlockSpec(a.shape, lambda i: (0,) * a.ndim)
    consts = [sp["wb_re"], sp["wb_im"], sp["wc"], sp["d_skip"], sp["ab_re"], sp["ab_im"]]
    x0r = x0_re.reshape(bs, nch)
    x0i = x0_im.reshape(bs, nch)
    gy_all, xr, xi = pl.pallas_call(
        _k_s5_sample,
        grid=(1,),
        in_specs=[pl.BlockSpec((bs, db), lambda i: (rb, col_u)), full(x0r), full(x0i)]
                 + [full(a) for a in consts] + [pl.BlockSpec(memory_space=pl.ANY)],
        out_specs=[pl.BlockSpec((bs, db), lambda i: (rb, 0)),
                   pl.BlockSpec((bs, nch), lambda i: (0, 0)), pl.BlockSpec((bs, nch), lambda i: (0, 0))],
        out_shape=[jax.ShapeDtypeStruct(gy_all.shape, gy_all.dtype),
                   jax.ShapeDtypeStruct((bs, nch), F32), jax.ShapeDtypeStruct((bs, nch), F32)],
        input_output_aliases={9: 0},
        compiler_params=_cparams(1), name="s5_sample",
    )(proj, x0r, x0i, *consts, gy_all)
    return gy_all, xr, xi


def _k_router(x_ref, g_ref, w_ref, b_ref, hn_ref, lg_ref):
    hn = _rmsnorm_rows(x_ref[...], g_ref[...])
    hn_ref[...] = hn
    x_hi, x_lo = _split_hi_lo(hn)
    w_hi, w_lo = _split_hi_lo(w_ref[...])
    lg_ref[...] = _dot(x_hi, w_hi) + (_dot(x_lo, w_hi) + _dot(x_hi, w_lo)) + b_ref[...]


def router(x, g, w_router, b_router, *, tm):
    t, k = x.shape
    ne = w_router.shape[1]
    w_pad = jnp.pad(w_router, ((0, 0), (0, V7X_LANES - ne)))
    b_pad = jnp.pad(b_router.astype(F32), (0, V7X_LANES - ne)).reshape(1, V7X_LANES)
    hn, lg = pl.pallas_call(
        _k_router, grid=(t // tm,),
        in_specs=[pl.BlockSpec((tm, k), lambda i: (i, 0)), pl.BlockSpec((1, k), lambda i: (0, 0)),
                  pl.BlockSpec((k, V7X_LANES), lambda i: (0, 0)), pl.BlockSpec((1, V7X_LANES), lambda i: (0, 0))],
        out_specs=[pl.BlockSpec((tm, k), lambda i: (i, 0)), pl.BlockSpec((tm, V7X_LANES), lambda i: (i, 0))],
        out_shape=[jax.ShapeDtypeStruct((t, k), F32), jax.ShapeDtypeStruct((t, V7X_LANES), F32)],
        compiler_params=_cparams(1), name="router",
    )(x, g.reshape(1, k), w_pad, b_pad)
    return hn, lg[:, :ne]


DMA_ISSUE_UNROLL = 8


def _k_gather_rows(nused_ref, tok_ref, src_hbm, o_ref, buf_ref, sems, *, tr):
    t = pl.program_id(0)
    n_used = nused_ref[0]

    def issue_tile(tile):
        slot = tile % 2
        base = tile * tr

        def issue(r, c):
            tok = tok_ref[base + r]
            pltpu.make_async_copy(src_hbm.at[pl.ds(tok, 1), :], buf_ref.at[slot, pl.ds(r, 1), :],
                                  sems.at[slot]).start()
            return c

        lax.fori_loop(0, tr, issue, 0, unroll=DMA_ISSUE_UNROLL)

    @pl.when(t == 0)
    def _():
        issue_tile(t)

    @pl.when(t + 1 < n_used)
    def _():
        issue_tile(t + 1)

    @pl.when(t < n_used)
    def _():
        slot = t % 2
        pltpu.make_async_copy(src_hbm.at[pl.ds(0, tr), :], buf_ref.at[slot], sems.at[slot]).wait()
        o_ref[...] = buf_ref[slot].astype(o_ref.dtype)

    @pl.when(t >= n_used)
    def _():
        o_ref[...] = jnp.zeros_like(o_ref)


def gather_rows(src, tok, n_used, *, tr):
    t, k = src.shape
    r_pad = tok.shape[0]
    nt = r_pad // tr
    return pl.pallas_call(
        functools.partial(_k_gather_rows, tr=tr),
        grid_spec=pltpu.PrefetchScalarGridSpec(
            num_scalar_prefetch=2, grid=(nt,),
            in_specs=[pl.BlockSpec(memory_space=pl.ANY)],
            out_specs=pl.BlockSpec((tr, k), lambda i, nu, tk: (i, 0)),
            scratch_shapes=[pltpu.VMEM((2, tr, k), src.dtype), pltpu.SemaphoreType.DMA((2,))]),
        out_shape=jax.ShapeDtypeStruct((r_pad, k), BF16),
        compiler_params=_cparams(1), name="moe_gather",
    )(n_used, tok, src)


def _new_group(te_ref, t):
    return jnp.logical_or(t == 0, te_ref[t] != te_ref[jnp.maximum(t - 1, 0)])


def _k_moe_up(nused_ref, te_ref, x_ref, wg_ref, wu_ref, o_ref, wgb_ref, wub_ref):
    t = pl.program_id(1)

    @pl.when(t < nused_ref[0])
    def _():
        @pl.when(_new_group(te_ref, t))
        def _():
            wgb_ref[...] = wg_ref[...].astype(BF16)
            wub_ref[...] = wu_ref[...].astype(BF16)

        x = x_ref[...]
        o_ref[...] = (_silu(_dot(x, wgb_ref[...])) * _dot(x, wub_ref[...])).astype(o_ref.dtype)

    @pl.when(t >= nused_ref[0])
    def _():
        o_ref[...] = jnp.zeros_like(o_ref)


def moe_up(xs, w_gate, w_up, tile_expert, n_used, *, tr, tn):
    r_pad, k = xs.shape
    ne, _, f = w_gate.shape
    nt = r_pad // tr
    row = lambda j, t, nu, te: (jnp.minimum(t, nu[0] - 1), 0)
    wmap = lambda j, t, nu, te: (te[t], 0, j)
    return pl.pallas_call(
        _k_moe_up,
        grid_spec=pltpu.PrefetchScalarGridSpec(
            num_scalar_prefetch=2, grid=(pl.cdiv(f, tn), nt),
            in_specs=[pl.BlockSpec((tr, k), row),
                      pl.BlockSpec((None, k, tn), wmap), pl.BlockSpec((None, k, tn), wmap)],
            out_specs=pl.BlockSpec((tr, tn), lambda j, t, nu, te: (t, j)),
            scratch_shapes=[pltpu.VMEM((k, tn), BF16), pltpu.VMEM((k, tn), BF16)]),
        out_shape=jax.ShapeDtypeStruct((r_pad, f), BF16),
        compiler_params=_cparams(2), name="moe_up",
    )(n_used, tile_expert, xs, w_gate, w_up)


def _k_moe_down(nused_ref, te_ref, x_ref, w_ref, o_ref, wb_ref):
    t = pl.program_id(1)

    @pl.when(t < nused_ref[0])
    def _():
        @pl.when(_new_group(te_ref, t))
        def _():
            wb_ref[...] = w_ref[...].astype(BF16)

        o_ref[...] = _dot(x_ref[...], wb_ref[...])

    @pl.when(t >= nused_ref[0])
    def _():
        o_ref[...] = jnp.zeros_like(o_ref)


def moe_down(hid, w_down, tile_expert, n_used, *, tr, tn):
    r_pad, f = hid.shape
    d = w_down.shape[2]
    nt = r_pad // tr
    return pl.pallas_call(
        _k_moe_down,
        grid_spec=pltpu.PrefetchScalarGridSpec(
            num_scalar_prefetch=2, grid=(d // tn, nt),
            in_specs=[pl.BlockSpec((tr, f), lambda j, t, nu, te: (jnp.minimum(t, nu[0] - 1), 0)),
                      pl.BlockSpec((None, f, tn), lambda j, t, nu, te: (te[t], 0, j))],
            out_specs=pl.BlockSpec((tr, tn), lambda j, t, nu, te: (t, j)),
            scratch_shapes=[pltpu.VMEM((f, tn), BF16)]),
        out_shape=jax.ShapeDtypeStruct((r_pad, d), F32),
        compiler_params=_cparams(2), name="moe_down",
    )(n_used, tile_expert, hid, w_down)


def _k_moe_combine(pos_ref, ys_hbm, gate_ref, res_ref, o_ref, buf0_ref, buf1_ref, sems, *, tr):
    t = pl.program_id(0)
    n_tiles = pl.num_programs(0)
    n_tok = n_tiles * tr

    def issue_tile(tile):
        slot = tile % 2
        base = tile * tr

        def issue(r, c):
            p0 = pos_ref[base + r]
            p1 = pos_ref[n_tok + base + r]
            pltpu.make_async_copy(ys_hbm.at[pl.ds(p0, 1), :], buf0_ref.at[slot, pl.ds(r, 1), :],
                                  sems.at[slot]).start()
            pltpu.make_async_copy(ys_hbm.at[pl.ds(p1, 1), :], buf1_ref.at[slot, pl.ds(r, 1), :],
                                  sems.at[slot]).start()
            return c

        lax.fori_loop(0, tr, issue, 0, unroll=DMA_ISSUE_UNROLL)

    @pl.when(t == 0)
    def _():
        issue_tile(t)

    @pl.when(t + 1 < n_tiles)
    def _():
        issue_tile(t + 1)

    slot = t % 2
    pltpu.make_async_copy(ys_hbm.at[pl.ds(0, tr), :], buf0_ref.at[slot], sems.at[slot]).wait()
    pltpu.make_async_copy(ys_hbm.at[pl.ds(0, tr), :], buf1_ref.at[slot], sems.at[slot]).wait()
    g = gate_ref[...]
    o_ref[...] = res_ref[...] + (g[:, 0:1] * buf0_ref[slot] + g[:, 1:2] * buf1_ref[slot])


def moe_combine(ys, pos, gates, res, *, tr):
    t, d = res.shape
    g_pad = jnp.pad(gates, ((0, 0), (0, V7X_LANES - gates.shape[1])))
    return pl.pallas_call(
        functools.partial(_k_moe_combine, tr=tr),
        grid_spec=pltpu.PrefetchScalarGridSpec(
            num_scalar_prefetch=1, grid=(t // tr,),
            in_specs=[pl.BlockSpec(memory_space=pl.ANY),
                      pl.BlockSpec((tr, V7X_LANES), lambda i, p: (i, 0)),
                      pl.BlockSpec((tr, d), lambda i, p: (i, 0))],
            out_specs=pl.BlockSpec((tr, d), lambda i, p: (i, 0)),
            scratch_shapes=[pltpu.VMEM((2, tr, d), F32), pltpu.VMEM((2, tr, d), F32),
                            pltpu.SemaphoreType.DMA((2,))]),
        out_shape=jax.ShapeDtypeStruct((t, d), F32),
        compiler_params=_cparams(1), name="moe_combine",
    )(pos, ys, g_pad, res)


def moe_layer(h, g_ffn, w_router, b_router, w_gate, w_up, w_down, *, tm, tr, tn_up, tn_down, tr_gather,
              tr_combine):
    t, d = h.shape
    ne = w_gate.shape[0]
    hn, logits = router(h, g_ffn, w_router, b_router, tm=tm)
    top_v, top_e = lax.top_k(logits, TOP_K)
    gates = jax.nn.softmax(top_v, axis=-1)
    flat_e = top_e.reshape(-1)
    onehot = (flat_e[:, None] == jnp.arange(ne, dtype=flat_e.dtype)[None, :]).astype(jnp.int32)
    rank = jnp.sum((jnp.cumsum(onehot, axis=0) - onehot) * onehot, axis=1)
    sizes = jnp.sum(onehot, axis=0)
    tiles_per = (sizes + tr - 1) // tr
    tile_end = jnp.cumsum(tiles_per)
    tile_start = tile_end - tiles_per
    n_used = tile_end[-1:].astype(jnp.int32)
    nt = (t * TOP_K) // tr + ne
    r_pad = nt * tr
    pos = (tile_start[flat_e] * tr + rank).astype(jnp.int32)
    src_tok = jnp.zeros((r_pad,), jnp.int32).at[pos].set(jnp.arange(t * TOP_K, dtype=jnp.int32) // TOP_K)
    tile_ids = jnp.minimum(jnp.arange(nt, dtype=jnp.int32), n_used[0] - 1)
    tile_expert = jnp.sum((tile_ids[:, None] >= tile_end[None, :]).astype(jnp.int32), axis=1).astype(jnp.int32)
    xs = gather_rows(hn, src_tok, n_used * (tr // tr_gather), tr=tr_gather)
    hid = moe_up(xs, w_gate, w_up, tile_expert, n_used, tr=tr, tn=tn_up)
    ys = moe_down(hid, w_down, tile_expert, n_used, tr=tr, tn=tn_down)
    return moe_combine(ys, pos.reshape(t, TOP_K).T.reshape(-1), gates, h, tr=tr_combine)


def _layer_weights(i, conv_w, conv_b, w_q, w_k, w_v, b_i, b_f, g_head, skip_a):
    nh = b_i.shape[1]
    pad = jnp.zeros((V7X_LANES - nh,), F32)
    b_if = jnp.concatenate([b_i[i].astype(F32), pad, b_f[i].astype(F32), pad]).reshape(1, 2 * V7X_LANES)
    da = conv_w.shape[-1]
    return dict(conv_w=conv_w[i], conv_b=conv_b[i].reshape(1, da), w_q=w_q[i], w_k=w_k[i], w_v=w_v[i],
                b_if=b_if, g_head=g_head[i].reshape(1, da), skip=skip_a[i].reshape(1, da))


def _reorder_w_in(w, d_a, nh, d_b, d):
    o0 = d_a
    i0 = 2 * d_a
    ub0 = i0 + 2 * nh
    ga0 = ub0 + d_b
    gb0 = ga0 + d
    k = w.shape[0]
    zpad = jnp.zeros((k, V7X_LANES - nh), w.dtype)
    cols = [w[:, :d_a], w[:, ub0:ub0 + d_b], w[:, o0:o0 + d_a], w[:, ga0:ga0 + d], w[:, gb0:gb0 + d],
            w[:, i0:i0 + nh], zpad, w[:, i0 + nh:i0 + 2 * nh], zpad]
    return jnp.concatenate(cols, axis=1).astype(BF16)


def _tile_plan(n_rows, seq):
    tm = next(c for c in (832, 640, 512, 256, 128, 64, 32, 16) if n_rows % c == 0)
    tc = next(c for c in (320, 256, 128, 64, 32, 16, 8) if n_rows % c == 0)
    tm_big = 2 * tm if n_rows % (2 * tm) == 0 else tm
    return dict(tm=tm, tm_big=tm_big, tn=512, tn_down=256, mlstm_chunk=256, s5_chunk=256,
                moe_tr=512, moe_tn_up=512, moe_tn_down=512, gather_tr=256, combine_tr=tc,
                norm_tm=min(seq, 1024))


def kernel(x_prompt, x_sample, p_prompt, p_sample, state_mlstm_C, state_mlstm_n, state_mlstm_m, state_mlstm_conv,
           state_s5_re, state_s5_im, g_mix, w_in, conv_w, conv_b, w_q, w_k, w_v, b_i, b_f, g_head, skip_a, w_proj_a,
           s5_log_dt, s5_A_re, s5_A_im, s5_B_re, s5_B_im, s5_C_re, s5_C_im, s5_D, w_glu_b, w_out, g_ffn,
           w_ff_gate, w_ff_up, w_ff_down, w_router, b_router, w_moe_gate, w_moe_up, w_moe_down,
           g_ple, w_ple, w_pg, g_final):
    bsz, seq, d = x_prompt.shape
    bs = x_sample.shape[0]
    depth = g_mix.shape[0]
    nh = b_i.shape[1]
    d_a = conv_w.shape[-1]
    d_b = s5_D.shape[-1]
    n_p = bsz * seq
    t = n_p + bs
    tl = _tile_plan(t, seq)
    tm, tmb, tn = tl["tm"], tl["tm_big"], tl["tn"]

    h = jnp.concatenate([x_prompt.reshape(n_p, d), x_sample.reshape(bs, d)], axis=0).astype(F32)
    p_all = jnp.concatenate([p_prompt.reshape(depth, n_p, -1), p_sample.reshape(depth, bs, -1)], axis=1)

    col_ua, col_ub, col_oa = 0, d_a // d_b, (d_a + d_b) // d_a
    col_ga = (2 * d_a + d_b) // tn
    col_gb = (2 * d_a + d_b + d) // tn
    col_if = (2 * d_a + d_b + 2 * d) // (2 * V7X_LANES)

    states = [[] for _ in range(11)]
    c_s_all = jnp.zeros(state_mlstm_C.shape, F32)
    for i in range(depth):
        lw = _layer_weights(i, conv_w, conv_b, w_q, w_k, w_v, b_i, b_f, g_head, skip_a)
        sp = _s5_params(dict(log_dt=s5_log_dt[i], a_re=s5_A_re[i], a_im=s5_A_im[i], b_re=s5_B_re[i],
                             b_im=s5_B_im[i], c_re=s5_C_re[i], c_im=s5_C_im[i], d_skip=s5_D[i]))
        proj = norm_matmul(h, g_mix[i], _reorder_w_in(w_in[i], d_a, nh, d_b, d), tm=tmb, tn=tn, name="in_proj")

        a_pre, c_p, n_pp, m_p, conv_p = mlstm_prompt(proj, col_ua, col_oa, col_if, t, bsz, seq, lw,
                                                     L=tl["mlstm_chunk"])
        a_pre, c_s_all, n_s, m_s, conv_s = mlstm_sample(proj, col_ua, col_oa, col_if, n_p, a_pre,
                                                        state_mlstm_C, i, c_s_all, state_mlstm_n[i].astype(F32),
                                                        state_mlstm_m[i].astype(F32),
                                                        state_mlstm_conv[i].astype(F32), lw)
        gy, sre_p, sim_p = s5_prompt(proj, col_ub, t, bsz, seq, sp, tc=tl["s5_chunk"])
        gy, sre_s, sim_s = s5_sample(proj, col_ub, n_p, gy, state_s5_re[i].astype(F32),
                                     state_s5_im[i].astype(F32), sp)
        b_gated = glu_gated_matmul(gy, w_glu_b[i], proj, col_gb, tm=tmb, tn=tn)
        mix = gated_add_matmul(a_pre, w_proj_a[i], proj, col_ga, b_gated, tm=tmb, tn=tn)
        h = matmul(mix, w_out[i], h, tm=tmb, tn=tn, name="out_proj")

        j = i // 2
        if i % 2 == 0:
            hid = norm_swiglu_up(h, g_ffn[i], w_ff_gate[j], w_ff_up[j], tm=tm, tn=tn)
            h = matmul(hid, w_ff_down[j], h, tm=tm, tn=tl["tn_down"], name="ffn_down")
        else:
            h = moe_layer(h, g_ffn[i], w_router[j], b_router[j], w_moe_gate[j], w_moe_up[j], w_moe_down[j],
                          tm=tm, tr=tl["moe_tr"], tn_up=tl["moe_tn_up"], tn_down=tl["moe_tn_down"],
                          tr_gather=tl["gather_tr"], tr_combine=tl["combine_tr"])
        h = ple_matmul(h, g_ple[i], p_all[i], w_pg[i], w_ple[i], tm=tm, tn=tn)

        g_s, p_s = S5_STATE, sre_p.shape[-1] // S5_STATE
        new = [c_p, n_pp, m_p, conv_p, sre_p.reshape(bsz, p_s, g_s), sim_p.reshape(bsz, p_s, g_s),
               n_s, m_s, conv_s, sre_s.reshape(bs, p_s, g_s), sim_s.reshape(bs, p_s, g_s)]
        for lst, s in zip(states, new):
            lst.append(s)

    y_prompt, y_sample = final_norm_split(h, g_final, n_p, tm=tl["norm_tm"])
    y_prompt = y_prompt.reshape(bsz, seq, d)
    y_sample = y_sample.reshape(bs, 1, d)
    st = [jnp.stack(lst) for lst in states]
    return (y_prompt, y_sample) + tuple(st[:6]) + (c_s_all,) + tuple(st[6:])
```

```python
import functools
import math

import numpy as np
import jax
import jax.numpy as jnp
from jax import lax
from jax.experimental import pallas as pl
from jax.experimental.pallas import tpu as pltpu

F32 = jnp.float32
BF16 = jnp.bfloat16
EPS = 1e-6

V7X_VMEM_BYTES = 64 * 1024 * 1024
V7X_LANES = 128
V7X_SUBLANES = 8
VMEM_LIMIT = 56 * 1024 * 1024

N_HEADS = 4
CONV_W = 4
S5_GROUP = 16
S5_STATE = 64
N_EXPERTS = 8
TOP_K = 2


def _cparams(n_axes, vmem=VMEM_LIMIT):
    return pltpu.CompilerParams(dimension_semantics=("arbitrary",) * n_axes, vmem_limit_bytes=vmem)


def _sigmoid(x):
    return 1.0 / (1.0 + jnp.exp(-x))


def _silu(x):
    return x * _sigmoid(x)


def _gelu_tanh(x):
    return 0.5 * x * (1.0 + jnp.tanh(math.sqrt(2.0 / math.pi) * (x + 0.044715 * (x * x * x))))


def _log_sigmoid(x):
    return jnp.minimum(x, 0.0) - jnp.log(1.0 + jnp.exp(-jnp.abs(x)))


def _dot(a, b):
    return jnp.dot(a, b, preferred_element_type=F32)


def _dot_nt(a, b):
    return lax.dot_general(a, b, (((1,), (1,)), ((), ())), preferred_element_type=F32)


def _dot_tn(a, b):
    return lax.dot_general(a, b, (((0,), (0,)), ((), ())), preferred_element_type=F32)


def _rmsnorm_rows(x, g):
    ms = jnp.mean(x * x, axis=-1, keepdims=True)
    return x * lax.rsqrt(ms + EPS) * g


def _row_tile_buffering(tm, k, dtype):
    two_copies = 2 * tm * k * jnp.dtype(dtype).itemsize
    return pl.Buffered(1) if two_copies > VMEM_LIMIT // 4 else None


def _row_chunks(tm):
    for rc in (256, 208, 128, 104, 64, 32, 16, 8):
        if tm % rc == 0:
            return rc
    return tm


def _norm_to_scratch(x_ref, g_ref, xn_ref, tm):
    rc = _row_chunks(tm)

    def body(r, c):
        rows = pl.ds(pl.multiple_of(r * rc, rc), rc)
        xn_ref[rows, :] = _rmsnorm_rows(x_ref[rows, :], g_ref[...]).astype(BF16)
        return c

    lax.fori_loop(0, tm // rc, body, 0)


def _wspec(w, layer, tn, col=lambda j: j):
    return pl.BlockSpec((None, w.shape[1], tn), lambda i, j: (layer, 0, col(j)))


def _k_in_proj(x_ref, g_ref, w1_ref, w2_ref, w3_ref, o_ref, xn_ref, *, tm, n1, n2):
    j = pl.program_id(1)

    @pl.when(j == 0)
    def _():
        _norm_to_scratch(x_ref, g_ref, xn_ref, tm)

    @pl.when(j < n1)
    def _():
        o_ref[...] = _dot(xn_ref[...], w1_ref[...].astype(BF16))

    @pl.when(jnp.logical_and(j >= n1, j < n1 + n2))
    def _():
        o_ref[...] = _dot(xn_ref[...], w2_ref[...])

    @pl.when(j >= n1 + n2)
    def _():
        o_ref[...] = _dot(xn_ref[...], w3_ref[...])


def in_proj(x, g, w_in, layer, w_shifted, w_gates, n_direct, *, tm, tn):
    t, k = x.shape
    n1 = n_direct // tn
    n2 = w_shifted.shape[1] // tn
    n = n_direct + w_shifted.shape[1] + tn
    return pl.pallas_call(
        functools.partial(_k_in_proj, tm=tm, n1=n1, n2=n2),
        grid=(t // tm, n1 + n2 + 1),
        in_specs=[pl.BlockSpec((tm, k), lambda i, j: (i, 0), pipeline_mode=_row_tile_buffering(tm, k, x.dtype)),
                  pl.BlockSpec((1, k), lambda i, j: (0, 0)),
                  _wspec(w_in, layer, tn, lambda j: jnp.minimum(j, n1 - 1)),
                  pl.BlockSpec((k, tn), lambda i, j: (0, jnp.clip(j - n1, 0, n2 - 1))),
                  pl.BlockSpec((k, tn), lambda i, j: (0, 0))],
        out_specs=pl.BlockSpec((tm, tn), lambda i, j: (i, j)),
        out_shape=jax.ShapeDtypeStruct((t, n), F32),
        scratch_shapes=[pltpu.VMEM((tm, k), BF16)],
        compiler_params=_cparams(2), name="in_proj",
    )(x, g.reshape(1, k), w_in, w_shifted, w_gates)


def _k_mm_res(x_ref, w_ref, r_ref, o_ref):
    o_ref[...] = r_ref[...] + _dot(x_ref[...], w_ref[...].astype(BF16))


def _k_mm(x_ref, w_ref, o_ref):
    o_ref[...] = _dot(x_ref[...], w_ref[...].astype(BF16)).astype(o_ref.dtype)


def matmul(x, w, layer, res=None, *, tm, tn, out_dtype=F32, name="mm"):
    t, k = x.shape
    n = w.shape[2]
    in_specs = [pl.BlockSpec((tm, k), lambda i, j: (i, 0), pipeline_mode=_row_tile_buffering(tm, k, x.dtype)),
                _wspec(w, layer, tn)]
    args = [x, w]
    body = _k_mm
    if res is not None:
        in_specs.append(pl.BlockSpec((tm, tn), lambda i, j: (i, j)))
        args.append(res)
        body = _k_mm_res
    return pl.pallas_call(
        body, grid=(t // tm, pl.cdiv(n, tn)), in_specs=in_specs,
        out_specs=pl.BlockSpec((tm, tn), lambda i, j: (i, j)),
        out_shape=jax.ShapeDtypeStruct((t, n), out_dtype),
        compiler_params=_cparams(2), name=name,
    )(*args)


def _k_glu_gated(x_ref, wv_ref, wg_ref, gb_ref, o_ref):
    x = x_ref[...]
    val = _dot(x, wv_ref[...].astype(BF16))
    gate = _dot(x, wg_ref[...].astype(BF16))
    o_ref[...] = _sigmoid(gb_ref[...]) * (val * _sigmoid(gate))


def glu_gated_matmul(x, w, layer, proj, gate_col0, *, tm, tn, name="glu_b"):
    t, k = x.shape
    n = w.shape[2] // 2
    nj = n // tn
    return pl.pallas_call(
        _k_glu_gated, grid=(t // tm, nj),
        in_specs=[pl.BlockSpec((tm, k), lambda i, j: (i, 0)),
                  _wspec(w, layer, tn),
                  _wspec(w, layer, tn, lambda j: j + nj),
                  pl.BlockSpec((tm, tn), lambda i, j: (i, gate_col0 + j))],
        out_specs=pl.BlockSpec((tm, tn), lambda i, j: (i, j)),
        out_shape=jax.ShapeDtypeStruct((t, n), F32),
        compiler_params=_cparams(2), name=name,
    )(x, w, w, proj)


def _k_mm_gated_add(x_ref, w_ref, ga_ref, b_ref, o_ref):
    acc = _dot(x_ref[...], w_ref[...].astype(BF16))
    o_ref[...] = (_sigmoid(ga_ref[...]) * acc + b_ref[...]).astype(o_ref.dtype)


def gated_add_matmul(x, w, layer, proj, gate_col0, b, *, tm, tn, name="proj_a"):
    t, k = x.shape
    n = w.shape[2]
    return pl.pallas_call(
        _k_mm_gated_add, grid=(t // tm, n // tn),
        in_specs=[pl.BlockSpec((tm, k), lambda i, j: (i, 0)),
                  _wspec(w, layer, tn),
                  pl.BlockSpec((tm, tn), lambda i, j: (i, gate_col0 + j)),
                  pl.BlockSpec((tm, tn), lambda i, j: (i, j))],
        out_specs=pl.BlockSpec((tm, tn), lambda i, j: (i, j)),
        out_shape=jax.ShapeDtypeStruct((t, n), BF16),
        compiler_params=_cparams(2), name=name,
    )(x, w, proj, b)


def _k_swiglu_up(x_ref, g_ref, wg_ref, wu_ref, o_ref, xn_ref, *, tm):
    @pl.when(pl.program_id(1) == 0)
    def _():
        _norm_to_scratch(x_ref, g_ref, xn_ref, tm)

    xn = xn_ref[...]
    gate = _dot(xn, wg_ref[...].astype(BF16))
    up = _dot(xn, wu_ref[...].astype(BF16))
    o_ref[...] = (_silu(gate) * up).astype(o_ref.dtype)


def norm_swiglu_up(x, g, w_gate, w_up, layer, *, tm, tn, name="ffn_up"):
    t, k = x.shape
    n = w_gate.shape[2]
    return pl.pallas_call(
        functools.partial(_k_swiglu_up, tm=tm),
        grid=(t // tm, pl.cdiv(n, tn)),
        in_specs=[pl.BlockSpec((tm, k), lambda i, j: (i, 0), pipeline_mode=_row_tile_buffering(tm, k, x.dtype)),
                  pl.BlockSpec((1, k), lambda i, j: (0, 0)),
                  _wspec(w_gate, layer, tn), _wspec(w_up, layer, tn)],
        out_specs=pl.BlockSpec((tm, tn), lambda i, j: (i, j)),
        out_shape=jax.ShapeDtypeStruct((t, n), BF16),
        scratch_shapes=[pltpu.VMEM((tm, k), BF16)],
        compiler_params=_cparams(2), name=name,
    )(x, g.reshape(1, k), w_gate, w_up)


def _k_ple(x_ref, g_ref, p_ref, wpg_ref, wple_ref, r_ref, o_ref, xn_ref, pb_ref, *, tm):
    @pl.when(pl.program_id(1) == 0)
    def _():
        _norm_to_scratch(x_ref, g_ref, xn_ref, tm)
        pb_ref[...] = p_ref[...].astype(BF16)

    gate = _dot(xn_ref[...], wpg_ref[...].astype(BF16))
    emb = _dot(pb_ref[...], wple_ref[...].astype(BF16))
    o_ref[...] = r_ref[...] + emb * _sigmoid(gate)


def ple_matmul(x, g, p, w_pg, w_ple, layer, *, tm, tn, name="ple"):
    t, k = x.shape
    kp = p.shape[2]
    return pl.pallas_call(
        functools.partial(_k_ple, tm=tm),
        grid=(t // tm, k // tn),
        in_specs=[pl.BlockSpec((tm, k), lambda i, j: (i, 0)),
                  pl.BlockSpec((1, k), lambda i, j: (0, 0)),
                  pl.BlockSpec((None, tm, kp), lambda i, j: (layer, i, 0)),
                  _wspec(w_pg, layer, tn), _wspec(w_ple, layer, tn),
                  pl.BlockSpec((tm, tn), lambda i, j: (i, j))],
        out_specs=pl.BlockSpec((tm, tn), lambda i, j: (i, j)),
        out_shape=jax.ShapeDtypeStruct((t, k), F32),
        scratch_shapes=[pltpu.VMEM((tm, k), BF16), pltpu.VMEM((tm, kp), BF16)],
        compiler_params=_cparams(2), name=name,
    )(x, g.reshape(1, k), p, w_pg, w_ple, x)


def _k_final_norm(xp_ref, xs_ref, g_ref, op_ref, os_ref, *, n_prompt_tiles):
    i = pl.program_id(0)

    @pl.when(i < n_prompt_tiles)
    def _():
        op_ref[...] = _rmsnorm_rows(xp_ref[...], g_ref[...])

    @pl.when(i == n_prompt_tiles)
    def _():
        os_ref[...] = _rmsnorm_rows(xs_ref[...], g_ref[...])


def final_norm_split(x, g, n_prompt, *, tm):
    t, k = x.shape
    bs = t - n_prompt
    npt = n_prompt // tm
    last = npt - 1
    return pl.pallas_call(
        functools.partial(_k_final_norm, n_prompt_tiles=npt), grid=(npt + 1,),
        in_specs=[pl.BlockSpec((tm, k), lambda i: (jnp.minimum(i, last), 0)),
                  pl.BlockSpec((bs, k), lambda i: (n_prompt // bs, 0)),
                  pl.BlockSpec((1, k), lambda i: (0, 0))],
        out_specs=[pl.BlockSpec((tm, k), lambda i: (jnp.minimum(i, last), 0)),
                   pl.BlockSpec((bs, k), lambda i: (0, 0))],
        out_shape=[jax.ShapeDtypeStruct((n_prompt, k), F32), jax.ShapeDtypeStruct((bs, k), F32)],
        compiler_params=_cparams(1), name="final_norm",
    )(x, x, g.reshape(1, k))


def _split_hi_lo(x):
    hi = x.astype(BF16)
    lo = (x - hi.astype(F32)).astype(BF16)
    return hi, lo


def _k_mlstm_prompt(u_ref, o_ref_in, if_ref, cw_ref, cb_ref, wq_ref, wk_ref, wv_ref, bif_ref, gh_ref, sk_ref,
                    tri_ref, a_init, a_ref, c_out, n_out, m_out, conv_out, upad_ref, *, L, dh):
    del a_init
    c_idx = pl.program_id(1)
    nh = N_HEADS

    @pl.when(c_idx == 0)
    def _():
        c_out[...] = jnp.zeros_like(c_out)
        n_out[...] = jnp.zeros_like(n_out)
        m_out[...] = jnp.zeros_like(m_out)
        upad_ref[pl.ds(0, 8), :] = jnp.zeros((8, nh * dh), F32)

    @pl.when(c_idx > 0)
    def _():
        upad_ref[pl.ds(0, 8), :] = upad_ref[pl.ds(L, 8), :]

    u = u_ref[...]
    upad_ref[pl.ds(8, L), :] = u
    conv = cb_ref[...] + u * cw_ref[CONV_W - 1:CONV_W, :]
    for j in range(CONV_W - 1):
        conv = conv + upad_ref[pl.ds(8 - (CONV_W - 1) + j, L), :] * cw_ref[j:j + 1, :]
    cact = _silu(conv)
    conv_out[0] = upad_ref[pl.ds(L, 8), :]

    pre = if_ref[...]
    li = pre[:, :V7X_LANES] + bif_ref[:, :V7X_LANES]
    lf = _log_sigmoid(pre[:, V7X_LANES:] + bif_ref[:, V7X_LANES:])
    tri = tri_ref[...]
    lf_hi, lf_mid = _split_hi_lo(lf)
    lf_lo = (lf - lf_hi.astype(F32) - lf_mid.astype(F32)).astype(BF16)
    bcum = _dot(tri, lf_hi) + _dot(tri, lf_mid) + _dot(tri, lf_lo)
    li_t = li.T
    b_t = bcum.T
    row_id = lax.broadcasted_iota(jnp.int32, (L, L), 0)
    col_id = lax.broadcasted_iota(jnp.int32, (L, L), 1)
    causal = col_id <= row_id
    lane = lax.broadcasted_iota(jnp.int32, (1, V7X_LANES), 1)
    m_row = m_out[0]
    m_new_row = m_row

    for h in range(nh):
        hs = slice(h * dh, (h + 1) * dh)
        ch = cact[:, hs].astype(BF16)
        uh = u[:, hs].astype(BF16)
        q = _dot(ch, wq_ref[h].astype(BF16))
        k = _dot(ch, wk_ref[h].astype(BF16)) * (dh ** -0.5)
        v = _dot(uh, wv_ref[h].astype(BF16))
        qb, kb, vb = q.astype(BF16), k.astype(BF16), v.astype(BF16)

        b_col = bcum[:, h:h + 1]
        li_col = li[:, h:h + 1]
        r_row = li_t[h:h + 1, :] - b_t[h:h + 1, :]
        m_prev = m_row[:, h:h + 1]
        d = jnp.where(causal, b_col + r_row, -jnp.inf)
        inter = b_col + m_prev
        m_t = jnp.maximum(inter, jnp.max(d, axis=-1, keepdims=True))
        w_inter = jnp.exp(inter - m_t)
        s = _dot_nt(qb, kb) * jnp.exp(d - m_t)
        c_prev = c_out[0, h]
        n_prev = n_out[0, h:h + 1, :]
        num = w_inter * _dot(qb, c_prev.astype(BF16)) + _dot(s.astype(BF16), vb)
        den = w_inter * jnp.sum(q * n_prev, axis=-1, keepdims=True) + jnp.sum(s, axis=-1, keepdims=True)
        hh = num / jnp.maximum(jnp.abs(den), jnp.exp(-m_t))

        b_last = b_col[L - 1:L, :]
        g_col = b_last - b_col + li_col
        m_new = jnp.maximum(b_last + m_prev, jnp.max(g_col, axis=0, keepdims=True))
        decay = jnp.exp(b_last + m_prev - m_new)
        wk_ = jnp.exp(g_col - m_new) * k
        c_out[0, h] = decay * c_prev + _dot_tn(wk_.astype(BF16), vb)
        n_out[0, h:h + 1, :] = decay * n_prev + jnp.sum(wk_, axis=0, keepdims=True)
        m_new_row = jnp.where(lane == h, m_new, m_new_row)

        hn = _rmsnorm_rows(hh, gh_ref[:, hs])
        gated = (hn + sk_ref[:, hs] * cact[:, hs]) * _sigmoid(o_ref_in[:, hs])
        a_ref[:, hs] = gated.astype(a_ref.dtype)

    m_out[0] = m_new_row


def mlstm_prompt(proj, col_u, col_o, col_if, n_rows, bsz, seq, lw, *, L):
    dh = lw["w_q"].shape[-1]
    da = N_HEADS * dh
    nc = seq // L
    tri = jnp.asarray(np.tril(np.ones((L, L), np.float32)), BF16)
    row_blk = lambda b, c: b * nc + c
    full = lambda *shape: pl.BlockSpec(shape, lambda b, c: (0,) * len(shape))
    outs = pl.pallas_call(
        functools.partial(_k_mlstm_prompt, L=L, dh=dh),
        grid=(bsz, nc),
        in_specs=[pl.BlockSpec((L, da), lambda b, c: (row_blk(b, c), col_u)),
                  pl.BlockSpec((L, da), lambda b, c: (row_blk(b, c), col_o)),
                  pl.BlockSpec((L, 2 * V7X_LANES), lambda b, c: (row_blk(b, c), col_if)),
                  full(CONV_W, da), full(1, da), full(N_HEADS, dh, dh), full(N_HEADS, dh, dh),
                  full(N_HEADS, dh, dh), full(1, 2 * V7X_LANES), full(1, da), full(1, da), full(L, L),
                  pl.BlockSpec(memory_space=pl.ANY)],
        out_specs=[pl.BlockSpec((L, da), lambda b, c: (row_blk(b, c), 0)),
                   pl.BlockSpec((1, N_HEADS, dh, dh), lambda b, c: (b, 0, 0, 0)),
                   pl.BlockSpec((1, N_HEADS, dh), lambda b, c: (b, 0, 0)),
                   pl.BlockSpec((1, 1, V7X_LANES), lambda b, c: (b, 0, 0)),
                   pl.BlockSpec((1, 8, da), lambda b, c: (b, 0, 0))],
        out_shape=[jax.ShapeDtypeStruct((n_rows, da), BF16),
                   jax.ShapeDtypeStruct((bsz, N_HEADS, dh, dh), F32),
                   jax.ShapeDtypeStruct((bsz, N_HEADS, dh), F32),
                   jax.ShapeDtypeStruct((bsz, 1, V7X_LANES), F32),
                   jax.ShapeDtypeStruct((bsz, 8, da), F32)],
        scratch_shapes=[pltpu.VMEM((L + 8, da), F32)],
        input_output_aliases={12: 0},
        compiler_params=_cparams(2), name="mlstm_prompt",
    )(proj, proj, proj, lw["conv_w"], lw["conv_b"], lw["w_q"], lw["w_k"], lw["w_v"], lw["b_if"],
      lw["g_head"], lw["skip"], tri, jnp.zeros((n_rows, da), BF16))
    a_pre, c_p, n_p, m_p, conv_p = outs
    return a_pre, c_p, n_p, m_p[:, 0, :N_HEADS], conv_p[:, 8 - (CONV_W - 1):, :]


def _k_mlstm_sample_pre(u_ref, if_ref, conv_ref, m_ref, cw_ref, cb_ref, wq_ref, wk_ref, wv_ref, bif_ref,
                        q_out, k_out, v_out, c_out, gates_out, conv_out, *, dh):
    nh = N_HEADS
    u = u_ref[...]
    conv = cb_ref[...] + u * cw_ref[CONV_W - 1:CONV_W, :]
    for j in range(CONV_W - 1):
        conv = conv + conv_ref[j] * cw_ref[j:j + 1, :]
        if j > 0:
            conv_out[j - 1] = conv_ref[j]
    conv_out[CONV_W - 2] = u
    cact = _silu(conv)
    c_out[...] = cact
    for h in range(nh):
        hs = slice(h * dh, (h + 1) * dh)
        ch = cact[:, hs].astype(BF16)
        q_out[:, hs] = _dot(ch, wq_ref[h].astype(BF16))
        k_out[:, hs] = _dot(ch, wk_ref[h].astype(BF16)) * (dh ** -0.5)
        v_out[:, hs] = _dot(u[:, hs].astype(BF16), wv_ref[h].astype(BF16))
    pre = if_ref[...]
    li = pre[:, :V7X_LANES] + bif_ref[:, :V7X_LANES]
    lf = _log_sigmoid(pre[:, V7X_LANES:] + bif_ref[:, V7X_LANES:])
    m_prev = m_ref[...]
    inter = lf + m_prev
    m_t = jnp.maximum(inter, li)
    gates_out[0] = jnp.exp(inter - m_t)
    gates_out[1] = jnp.exp(li - m_t)
    gates_out[2] = jnp.exp(-m_t)
    gates_out[3] = m_t


def _k_mlstm_sample_step(q_ref, k_ref, v_ref, gates_ref, c_ref, n_ref, cact_ref, o_ref_in, gh_ref, sk_ref,
                         a_any, c_any, c_out, n_out, a_out, hh_ref, *, bt, dh):
    del a_any, c_any
    i = pl.program_id(0)
    nh = N_HEADS
    q = q_ref[...]
    k = k_ref[...]
    v = v_ref[...]
    w_inter = gates_ref[0]
    w_new = gates_ref[1]
    e_neg_m = gates_ref[2]
    n_prev = n_ref[...]
    rows = pl.ds(pl.multiple_of(i * bt, bt), bt)
    for h in range(nh):
        hs = slice(h * dh, (h + 1) * dh)
        qh, kh, vh, nh_prev = q[:, hs], k[:, hs], v[:, hs], n_prev[:, hs]
        q_t = qh.T
        k_t = kh.T
        wi = w_inter[:, h:h + 1]
        wn = w_new[:, h:h + 1]
        s = jnp.sum(qh * kh, axis=-1, keepdims=True) * wn
        den = wi * jnp.sum(qh * nh_prev, axis=-1, keepdims=True) + s
        wv = wn * vh
        qc_rows = []
        for bl in range(bt):
            c_prev = c_ref[bl, h]
            qc_rows.append(jnp.sum(q_t[:, bl:bl + 1] * c_prev, axis=0, keepdims=True))
            c_out[bl, h] = wi[bl:bl + 1, :] * c_prev + k_t[:, bl:bl + 1] * wv[bl:bl + 1, :]
        qc = jnp.concatenate(qc_rows, axis=0)
        num = wi * qc + s * vh
        hh = num / jnp.maximum(jnp.abs(den), e_neg_m[:, h:h + 1])
        hh_ref[rows, hs] = hh
        n_out[:, hs] = wi * nh_prev + wn * kh

    @pl.when(i == pl.num_programs(0) - 1)
    def _():
        for h in range(nh):
            hs = slice(h * dh, (h + 1) * dh)
            hn = _rmsnorm_rows(hh_ref[:, hs], gh_ref[:, hs])
            a_out[:, hs] = ((hn + sk_ref[:, hs] * cact_ref[:, hs]) * _sigmoid(o_ref_in[:, hs])).astype(a_out.dtype)


def mlstm_sample(proj, col_u, col_o, col_if, row0, a_pre_all, c_all, layer, c_new_all, n0, m0, conv0, lw, *,
                 bt=8):
    _, bs, nh, dh, _ = c_all.shape
    da = nh * dh
    rb = row0 // bs
    m_pad = jnp.pad(m0, ((0, 0), (0, V7X_LANES - nh)))
    conv_t = jnp.transpose(conv0, (1, 0, 2))
    full = lambda *shape: pl.BlockSpec(shape, lambda i: (0,) * len(shape))
    q, k, v, cact, gates, conv_new = pl.pallas_call(
        functools.partial(_k_mlstm_sample_pre, dh=dh),
        grid=(1,),
        in_specs=[pl.BlockSpec((bs, da), lambda i: (rb, col_u)),
                  pl.BlockSpec((bs, 2 * V7X_LANES), lambda i: (rb, col_if)),
                  full(CONV_W - 1, bs, da), full(bs, V7X_LANES), full(CONV_W, da), full(1, da),
                  full(nh, dh, dh), full(nh, dh, dh), full(nh, dh, dh), full(1, 2 * V7X_LANES)],
        out_specs=[full(bs, da), full(bs, da), full(bs, da), full(bs, da), full(4, bs, V7X_LANES),
                   full(CONV_W - 1, bs, da)],
        out_shape=[jax.ShapeDtypeStruct((bs, da), F32)] * 4
                  + [jax.ShapeDtypeStruct((4, bs, V7X_LANES), F32),
                     jax.ShapeDtypeStruct((CONV_W - 1, bs, da), F32)],
        compiler_params=_cparams(1), name="mlstm_sample_pre",
    )(proj, proj, conv_t, m_pad, lw["conv_w"], lw["conv_b"], lw["w_q"], lw["w_k"], lw["w_v"], lw["b_if"])

    blk = lambda *shape: pl.BlockSpec(shape, lambda i: (i,) + (0,) * (len(shape) - 1))
    cst = lambda *shape: pl.BlockSpec(shape, lambda i: (0,) * len(shape))
    c_blk = pl.BlockSpec((None, bt, nh, dh, dh), lambda i: (layer, i, 0, 0, 0))
    c_new_all, n_new, a_pre_all = pl.pallas_call(
        functools.partial(_k_mlstm_sample_step, bt=bt, dh=dh),
        grid=(bs // bt,),
        in_specs=[blk(bt, da), blk(bt, da), blk(bt, da),
                  pl.BlockSpec((4, bt, V7X_LANES), lambda i: (0, i, 0)),
                  c_blk, blk(bt, da), cst(bs, da),
                  pl.BlockSpec((bs, da), lambda i: (rb, col_o)), cst(1, da), cst(1, da),
                  pl.BlockSpec(memory_space=pl.ANY), pl.BlockSpec(memory_space=pl.ANY)],
        out_specs=[c_blk, blk(bt, da), pl.BlockSpec((bs, da), lambda i: (rb, 0))],
        out_shape=[jax.ShapeDtypeStruct(c_new_all.shape, F32), jax.ShapeDtypeStruct((bs, da), F32),
                   jax.ShapeDtypeStruct(a_pre_all.shape, a_pre_all.dtype)],
        scratch_shapes=[pltpu.VMEM((bs, da), F32)],
        input_output_aliases={10: 2, 11: 0},
        compiler_params=_cparams(1), name="mlstm_sample_step",
    )(q, k, v, gates, c_all, n0.reshape(bs, da), cact, proj, lw["g_head"], lw["skip"], a_pre_all, c_new_all)
    m_new = gates[3][:, :nh]
    return a_pre_all, c_new_all, n_new.reshape(bs, nh, dh), m_new, jnp.transpose(conv_new, (1, 0, 2))


S5_SLAB_GROUPS = V7X_LANES // S5_GROUP
S5_SLAB_STATES = S5_SLAB_GROUPS * S5_STATE


def _s5_params(lp):
    g, p = lp["a_re"].shape
    dt = jnp.exp(lp["log_dt"].astype(F32))[:, None]
    a_re = lp["a_re"].astype(F32)
    a_im = lp["a_im"].astype(F32)
    lam_re = a_re * dt
    lam_im = a_im * dt
    mag = jnp.exp(lam_re)
    ab_re = mag * jnp.cos(lam_im)
    ab_im = mag * jnp.sin(lam_im)
    den = a_re * a_re + a_im * a_im
    nr = ab_re - 1.0
    ni = ab_im
    k_re = (nr * a_re + ni * a_im) / den
    k_im = (ni * a_re - nr * a_im) / den
    b_re = lp["b_re"].astype(F32)
    b_im = lp["b_im"].astype(F32)
    bb_re = k_re[..., None] * b_re - k_im[..., None] * b_im
    bb_im = k_re[..., None] * b_im + k_im[..., None] * b_re
    ns = g // S5_SLAB_GROUPS
    eye = jnp.eye(S5_SLAB_GROUPS, dtype=F32)

    def in_blockdiag(bb):
        bs = bb.reshape(ns, S5_SLAB_GROUPS, p, S5_GROUP)
        w = jnp.einsum("ab,sapc->sacbp", eye, bs)
        return w.reshape(ns, V7X_LANES, S5_SLAB_STATES).astype(BF16)

    def out_blockdiag(cc):
        cs = cc.astype(F32).reshape(ns, S5_SLAB_GROUPS, S5_GROUP, p)
        w = jnp.einsum("ab,sacp->sapbc", eye, cs)
        return w.reshape(ns, S5_SLAB_STATES, V7X_LANES)

    wc = jnp.concatenate([out_blockdiag(lp["c_re"]), -out_blockdiag(lp["c_im"])], axis=1).astype(BF16)
    return dict(lam_re=lam_re.reshape(1, g * p), lam_im=lam_im.reshape(1, g * p),
                ab_re=ab_re.reshape(1, g * p), ab_im=ab_im.reshape(1, g * p),
                wb_re=in_blockdiag(bb_re), wb_im=in_blockdiag(bb_im), wc=wc,
                d_skip=lp["d_skip"].astype(F32).reshape(1, g * S5_GROUP))


def _s5_powers(sp, ks):
    kk = jnp.asarray(ks, F32)[:, None]
    mag = jnp.exp(kk * sp["lam_re"])
    return mag * jnp.cos(kk * sp["lam_im"]), mag * jnp.sin(kk * sp["lam_im"])


def _s5_input_proj(u_bf, wbr_ref, wbi_ref, bur_ref, bui_ref):
    ns = wbr_ref.shape[0]
    for s in range(ns):
        us = u_bf[:, s * V7X_LANES:(s + 1) * V7X_LANES]
        cols = slice(s * S5_SLAB_STATES, (s + 1) * S5_SLAB_STATES)
        bur_ref[:, cols] = _dot(us, wbr_ref[s])
        bui_ref[:, cols] = _dot(us, wbi_ref[s])


def _s5_output_proj(xr_ref, xi_ref, wc_ref, skip):
    ns = wc_ref.shape[0]
    outs = []
    for s in range(ns):
        cols = slice(s * S5_SLAB_STATES, (s + 1) * S5_SLAB_STATES)
        xcat = jnp.concatenate([xr_ref[:, cols].astype(BF16), xi_ref[:, cols].astype(BF16)], axis=1)
        y = _dot(xcat, wc_ref[s]) + skip[:, s * V7X_LANES:(s + 1) * V7X_LANES]
        outs.append(_gelu_tanh(y).astype(BF16))
    return jnp.concatenate(outs, axis=1)


def _k_s5_prompt(u_ref, wbr_ref, wbi_ref, wc_ref, d_ref, abr_ref, abi_ref, tpr_ref, tpi_ref, ajr_ref, aji_ref,
                 perm_ref, permt_ref, gy_init, gy_ref, sre_out, sim_out,
                 bur0, bur1, bui0, bui1, xb0, xb1, *, tc):
    del gy_init
    c_idx = pl.program_id(1)
    nsteps = tc // V7X_SUBLANES
    ns = wbr_ref.shape[0]
    lw = S5_SLAB_STATES
    pair = 2 * V7X_SUBLANES
    bur, bui, xb = (bur0, bur1), (bui0, bui1), (xb0, xb1)

    @pl.when(c_idx == 0)
    def _():
        sre_out[...] = jnp.zeros_like(sre_out)
        sim_out[...] = jnp.zeros_like(sim_out)

    u = u_ref[...]
    u_hi, u_lo = _split_hi_lo(u)
    perm = perm_ref[...]
    up_hi = _dot(perm, u_hi)
    skip = d_ref[...] * (up_hi + _dot(perm, u_lo))
    up_bf = up_hi.astype(BF16)

    sub = lax.broadcasted_iota(jnp.int32, (V7X_SUBLANES, lw), 0)

    def input_proj(s):
        us = up_bf[:, s * V7X_LANES:(s + 1) * V7X_LANES]
        bur[s % 2][...] = _dot(us, wbr_ref[s])
        bui[s % 2][...] = _dot(us, wbi_ref[s])

    def scan(s):
        br, bi, xo = bur[s % 2], bui[s % 2], xb[s % 2]
        lanes = slice(s * lw, (s + 1) * lw)
        ar = jnp.broadcast_to(abr_ref[:, lanes], (V7X_SUBLANES, lw))
        ai = jnp.broadcast_to(abi_ref[:, lanes], (V7X_SUBLANES, lw))
        er = jnp.zeros((V7X_SUBLANES, lw), F32)
        ei = er
        for i in range(nsteps):
            rows = slice(i * V7X_SUBLANES, (i + 1) * V7X_SUBLANES)
            er, ei = (ar * er - ai * ei + br[rows, :], ar * ei + ai * er + bi[rows, :])
            br[rows, :] = er
            bi[rows, :] = ei
        for d, row in ((1, 0), (2, 1), (4, 3)):
            pr = ajr_ref[row:row + 1, lanes]
            pi = aji_ref[row:row + 1, lanes]
            sr = pltpu.roll(er, d, 0)
            si = pltpu.roll(ei, d, 0)
            keep = sub >= d
            er, ei = (er + jnp.where(keep, pr * sr - pi * si, 0.0),
                      ei + jnp.where(keep, pr * si + pi * sr, 0.0))
        c0r = jnp.broadcast_to(sre_out[0, :, lanes], (V7X_SUBLANES, lw))
        c0i = jnp.broadcast_to(sim_out[0, :, lanes], (V7X_SUBLANES, lw))
        ajr = ajr_ref[:, lanes]
        aji = aji_ref[:, lanes]
        fr = ajr * c0r - aji * c0i + er
        fi = ajr * c0i + aji * c0r + ei
        cin_r = jnp.where(sub >= 1, pltpu.roll(fr, 1, 0), c0r)
        cin_i = jnp.where(sub >= 1, pltpu.roll(fi, 1, 0), c0i)
        sre_out[0, :, lanes] = fr[V7X_SUBLANES - 1:V7X_SUBLANES, :]
        sim_out[0, :, lanes] = fi[V7X_SUBLANES - 1:V7X_SUBLANES, :]

        cin2_r = jnp.concatenate([cin_r, cin_r], axis=0)
        cin2_i = jnp.concatenate([cin_i, cin_i], axis=0)
        for k in range(tc // pair):
            rows = slice(k * pair, (k + 1) * pair)
            pr = tpr_ref[rows, lanes]
            pi = tpi_ref[rows, lanes]
            xo[rows, :lw] = (br[rows, :] + (pr * cin2_r - pi * cin2_i)).astype(BF16)
            xo[rows, lw:] = (bi[rows, :] + (pr * cin2_i + pi * cin2_r)).astype(BF16)

    def output_proj(s):
        y = _dot(xb[s % 2][...], wc_ref[s]) + skip[:, s * V7X_LANES:(s + 1) * V7X_LANES]
        return _gelu_tanh(y).astype(BF16)

    outs = []
    input_proj(0)
    for s in range(ns):
        if s + 1 < ns:
            input_proj(s + 1)
        scan(s)
        outs.append(output_proj(s))
    g_perm = jnp.concatenate(outs, axis=1)
    gy_ref[...] = _dot(permt_ref[...], g_perm).astype(gy_ref.dtype)


def s5_prompt(proj, col_u, n_rows, bsz, seq, sp, *, tc=256):
    db = sp["d_skip"].shape[1]
    nch = sp["ab_re"].shape[1]
    nc = seq // tc
    nsteps = tc // V7X_SUBLANES
    tpr, tpi = _s5_powers(sp, np.repeat(np.arange(1, nsteps + 1), V7X_SUBLANES))
    ajr, aji = _s5_powers(sp, nsteps * np.arange(1, V7X_SUBLANES + 1))
    perm = np.zeros((tc, tc), np.float32)
    r = np.arange(tc)
    perm[r, (r % V7X_SUBLANES) * nsteps + r // V7X_SUBLANES] = 1.0
    full = lambda a: pl.BlockSpec(a.shape, lambda b, c: (0,) * a.ndim)
    consts = [sp["wb_re"], sp["wb_im"], sp["wc"], sp["d_skip"], sp["ab_re"], sp["ab_im"], tpr, tpi, ajr, aji,
              jnp.asarray(perm, BF16), jnp.asarray(perm.T, BF16)]
    gy, s_re, s_im = pl.pallas_call(
        functools.partial(_k_s5_prompt, tc=tc),
        grid=(bsz, nc),
        in_specs=[pl.BlockSpec((tc, db), lambda b, c: (b * nc + c, col_u))] + [full(a) for a in consts]
                 + [pl.BlockSpec(memory_space=pl.ANY)],
        out_specs=[pl.BlockSpec((tc, db), lambda b, c: (b * nc + c, 0)),
                   pl.BlockSpec((1, 1, nch), lambda b, c: (b, 0, 0)),
                   pl.BlockSpec((1, 1, nch), lambda b, c: (b, 0, 0))],
        out_shape=[jax.ShapeDtypeStruct((n_rows, db), BF16),
                   jax.ShapeDtypeStruct((bsz, 1, nch), F32), jax.ShapeDtypeStruct((bsz, 1, nch), F32)],
        scratch_shapes=[pltpu.VMEM((tc, S5_SLAB_STATES), F32)] * 4 + [pltpu.VMEM((tc, 2 * S5_SLAB_STATES), BF16)] * 2,
        input_output_aliases={1 + len(consts): 0},
        compiler_params=_cparams(2), name="s5_prompt",
    )(proj, *consts, jnp.zeros((n_rows, db), BF16))
    return gy, s_re, s_im


def _k_s5_sample(u_ref, x0r_ref, x0i_ref, wbr_ref, wbi_ref, wc_ref, d_ref, abr_ref, abi_ref, gy_any,
                 gy_ref, xr_out, xi_out):
    del gy_any
    u = u_ref[...]
    _s5_input_proj(u.astype(BF16), wbr_ref, wbi_ref, xr_out, xi_out)
    ar = abr_ref[...]
    ai = abi_ref[...]
    x0r = x0r_ref[...]
    x0i = x0i_ref[...]
    xr_out[...] = xr_out[...] + (ar * x0r - ai * x0i)
    xi_out[...] = xi_out[...] + (ar * x0i + ai * x0r)
    gy_ref[...] = _s5_output_proj(xr_out, xi_out, wc_ref, d_ref[...] * u).astype(gy_ref.dtype)


def s5_sample(proj, col_u, row0, gy_all, x0_re, x0_im, sp):
    bs = x0_re.shape[0]
    db = sp["d_skip"].shape[1]
    nch = sp["ab_re"].shape[1]
    rb = row0 // bs
    full = lambda a: pl.BlockSpec(a.shape, lambda i: (0,) * a.ndim)
    consts = [sp["wb_re"], sp["wb_im"], sp["wc"], sp["d_skip"], sp["ab_re"], sp["ab_im"]]
    x0r = x0_re.reshape(bs, nch)
    x0i = x0_im.reshape(bs, nch)
    gy_all, xr, xi = pl.pallas_call(
        _k_s5_sample,
        grid=(1,),
        in_specs=[pl.BlockSpec((bs, db), lambda i: (rb, col_u)), full(x0r), full(x0i)]
                 + [full(a) for a in consts] + [pl.BlockSpec(memory_space=pl.ANY)],
        out_specs=[pl.BlockSpec((bs, db), lambda i: (rb, 0)),
                   pl.BlockSpec((bs, nch), lambda i: (0, 0)), pl.BlockSpec((bs, nch), lambda i: (0, 0))],
        out_shape=[jax.ShapeDtypeStruct(gy_all.shape, gy_all.dtype),
                   jax.ShapeDtypeStruct((bs, nch), F32), jax.ShapeDtypeStruct((bs, nch), F32)],
        input_output_aliases={9: 0},
        compiler_params=_cparams(1), name="s5_sample",
    )(proj, x0r, x0i, *consts, gy_all)
    return gy_all, xr, xi


def _k_router(x_ref, g_ref, w_ref, b_ref, hn_ref, lg_ref):
    hn = _rmsnorm_rows(x_ref[...], g_ref[...])
    hn_ref[...] = hn
    x_hi, x_lo = _split_hi_lo(hn)
    w_hi, w_lo = _split_hi_lo(w_ref[...])
    lg_ref[...] = _dot(x_hi, w_hi) + (_dot(x_lo, w_hi) + _dot(x_hi, w_lo)) + b_ref[...]


def router(x, g, w_router, b_router, *, tm):
    t, k = x.shape
    ne = w_router.shape[1]
    w_pad = jnp.pad(w_router, ((0, 0), (0, V7X_LANES - ne)))
    b_pad = jnp.pad(b_router.astype(F32), (0, V7X_LANES - ne)).reshape(1, V7X_LANES)
    hn, lg = pl.pallas_call(
        _k_router, grid=(t // tm,),
        in_specs=[pl.BlockSpec((tm, k), lambda i: (i, 0)), pl.BlockSpec((1, k), lambda i: (0, 0)),
                  pl.BlockSpec((k, V7X_LANES), lambda i: (0, 0)), pl.BlockSpec((1, V7X_LANES), lambda i: (0, 0))],
        out_specs=[pl.BlockSpec((tm, k), lambda i: (i, 0)), pl.BlockSpec((tm, V7X_LANES), lambda i: (i, 0))],
        out_shape=[jax.ShapeDtypeStruct((t, k), F32), jax.ShapeDtypeStruct((t, V7X_LANES), F32)],
        compiler_params=_cparams(1), name="router",
    )(x, g.reshape(1, k), w_pad, b_pad)
    return hn, lg[:, :ne]


DMA_ISSUE_UNROLL = 8


def _k_gather_rows(nused_ref, tok_ref, src_hbm, o_ref, buf_ref, sems, *, tr):
    t = pl.program_id(0)
    n_used = nused_ref[0]

    def issue_tile(tile):
        slot = tile % 2
        base = tile * tr

        def issue(r, c):
            tok = tok_ref[base + r]
            pltpu.make_async_copy(src_hbm.at[pl.ds(tok, 1), :], buf_ref.at[slot, pl.ds(r, 1), :],
                                  sems.at[slot]).start()
            return c

        lax.fori_loop(0, tr, issue, 0, unroll=DMA_ISSUE_UNROLL)

    @pl.when(t == 0)
    def _():
        issue_tile(t)

    @pl.when(t + 1 < n_used)
    def _():
        issue_tile(t + 1)

    @pl.when(t < n_used)
    def _():
        slot = t % 2
        pltpu.make_async_copy(src_hbm.at[pl.ds(0, tr), :], buf_ref.at[slot], sems.at[slot]).wait()
        o_ref[...] = buf_ref[slot].astype(o_ref.dtype)

    @pl.when(t >= n_used)
    def _():
        o_ref[...] = jnp.zeros_like(o_ref)


def gather_rows(src, tok, n_used, *, tr):
    t, k = src.shape
    r_pad = tok.shape[0]
    nt = r_pad // tr
    return pl.pallas_call(
        functools.partial(_k_gather_rows, tr=tr),
        grid_spec=pltpu.PrefetchScalarGridSpec(
            num_scalar_prefetch=2, grid=(nt,),
            in_specs=[pl.BlockSpec(memory_space=pl.ANY)],
            out_specs=pl.BlockSpec((tr, k), lambda i, nu, tk: (i, 0)),
            scratch_shapes=[pltpu.VMEM((2, tr, k), src.dtype), pltpu.SemaphoreType.DMA((2,))]),
        out_shape=jax.ShapeDtypeStruct((r_pad, k), BF16),
        compiler_params=_cparams(1), name="moe_gather",
    )(n_used, tok, src)


def _new_group(te_ref, t):
    return jnp.logical_or(t == 0, te_ref[t] != te_ref[jnp.maximum(t - 1, 0)])


def _k_moe_up(nused_ref, te_ref, x_ref, wg_ref, wu_ref, o_ref, wgb_ref, wub_ref):
    t = pl.program_id(1)

    @pl.when(t < nused_ref[0])
    def _():
        @pl.when(_new_group(te_ref, t))
        def _():
            wgb_ref[...] = wg_ref[...].astype(BF16)
            wub_ref[...] = wu_ref[...].astype(BF16)

        x = x_ref[...]
        o_ref[...] = (_silu(_dot(x, wgb_ref[...])) * _dot(x, wub_ref[...])).astype(o_ref.dtype)

    @pl.when(t >= nused_ref[0])
    def _():
        o_ref[...] = jnp.zeros_like(o_ref)


def moe_up(xs, w_gate, w_up, tile_expert, n_used, *, tr, tn):
    r_pad, k = xs.shape
    ne, _, f = w_gate.shape
    nt = r_pad // tr
    row = lambda j, t, nu, te: (jnp.minimum(t, nu[0] - 1), 0)
    wmap = lambda j, t, nu, te: (te[t], 0, j)
    return pl.pallas_call(
        _k_moe_up,
        grid_spec=pltpu.PrefetchScalarGridSpec(
            num_scalar_prefetch=2, grid=(pl.cdiv(f, tn), nt),
            in_specs=[pl.BlockSpec((tr, k), row),
                      pl.BlockSpec((None, k, tn), wmap), pl.BlockSpec((None, k, tn), wmap)],
            out_specs=pl.BlockSpec((tr, tn), lambda j, t, nu, te: (t, j)),
            scratch_shapes=[pltpu.VMEM((k, tn), BF16), pltpu.VMEM((k, tn), BF16)]),
        out_shape=jax.ShapeDtypeStruct((r_pad, f), BF16),
        compiler_params=_cparams(2), name="moe_up",
    )(n_used, tile_expert, xs, w_gate, w_up)


def _k_moe_down(nused_ref, te_ref, x_ref, w_ref, o_ref, wb_ref):
    t = pl.program_id(1)

    @pl.when(t < nused_ref[0])
    def _():
        @pl.when(_new_group(te_ref, t))
        def _():
            wb_ref[...] = w_ref[...].astype(BF16)

        o_ref[...] = _dot(x_ref[...], wb_ref[...])

    @pl.when(t >= nused_ref[0])
    def _():
        o_ref[...] = jnp.zeros_like(o_ref)


def moe_down(hid, w_down, tile_expert, n_used, *, tr, tn):
    r_pad, f = hid.shape
    d = w_down.shape[2]
    nt = r_pad // tr
    return pl.pallas_call(
        _k_moe_down,
        grid_spec=pltpu.PrefetchScalarGridSpec(
            num_scalar_prefetch=2, grid=(d // tn, nt),
            in_specs=[pl.BlockSpec((tr, f), lambda j, t, nu, te: (jnp.minimum(t, nu[0] - 1), 0)),
                      pl.BlockSpec((None, f, tn), lambda j, t, nu, te: (te[t], 0, j))],
            out_specs=pl.BlockSpec((tr, tn), lambda j, t, nu, te: (t, j)),
            scratch_shapes=[pltpu.VMEM((f, tn), BF16)]),
        out_shape=jax.ShapeDtypeStruct((r_pad, d), F32),
        compiler_params=_cparams(2), name="moe_down",
    )(n_used, tile_expert, hid, w_down)


def _k_moe_combine(pos_ref, ys_hbm, gate_ref, res_ref, o_ref, buf0_ref, buf1_ref, sems, *, tr):
    t = pl.program_id(0)
    n_tiles = pl.num_programs(0)
    n_tok = n_tiles * tr

    def issue_tile(tile):
        slot = tile % 2
        base = tile * tr

        def issue(r, c):
            p0 = pos_ref[base + r]
            p1 = pos_ref[n_tok + base + r]
            pltpu.make_async_copy(ys_hbm.at[pl.ds(p0, 1), :], buf0_ref.at[slot, pl.ds(r, 1), :],
                                  sems.at[slot]).start()
            pltpu.make_async_copy(ys_hbm.at[pl.ds(p1, 1), :], buf1_ref.at[slot, pl.ds(r, 1), :],
                                  sems.at[slot]).start()
            return c

        lax.fori_loop(0, tr, issue, 0, unroll=DMA_ISSUE_UNROLL)

    @pl.when(t == 0)
    def _():
        issue_tile(t)

    @pl.when(t + 1 < n_tiles)
    def _():
        issue_tile(t + 1)

    slot = t % 2
    pltpu.make_async_copy(ys_hbm.at[pl.ds(0, tr), :], buf0_ref.at[slot], sems.at[slot]).wait()
    pltpu.make_async_copy(ys_hbm.at[pl.ds(0, tr), :], buf1_ref.at[slot], sems.at[slot]).wait()
    g = gate_ref[...]
    o_ref[...] = res_ref[...] + (g[:, 0:1] * buf0_ref[slot] + g[:, 1:2] * buf1_ref[slot])


def moe_combine(ys, pos, gates, res, *, tr):
    t, d = res.shape
    g_pad = jnp.pad(gates, ((0, 0), (0, V7X_LANES - gates.shape[1])))
    return pl.pallas_call(
        functools.partial(_k_moe_combine, tr=tr),
        grid_spec=pltpu.PrefetchScalarGridSpec(
            num_scalar_prefetch=1, grid=(t // tr,),
            in_specs=[pl.BlockSpec(memory_space=pl.ANY),
                      pl.BlockSpec((tr, V7X_LANES), lambda i, p: (i, 0)),
                      pl.BlockSpec((tr, d), lambda i, p: (i, 0))],
            out_specs=pl.BlockSpec((tr, d), lambda i, p: (i, 0)),
            scratch_shapes=[pltpu.VMEM((2, tr, d), F32), pltpu.VMEM((2, tr, d), F32),
                            pltpu.SemaphoreType.DMA((2,))]),
        out_shape=jax.ShapeDtypeStruct((t, d), F32),
        compiler_params=_cparams(1), name="moe_combine",
    )(pos, ys, g_pad, res)


def moe_layer(h, g_ffn, w_router, b_router, w_gate, w_up, w_down, *, tm, tr, tn_up, tn_down, tr_gather,
              tr_combine):
    t, d = h.shape
    ne = w_gate.shape[0]
    hn, logits = router(h, g_ffn, w_router, b_router, tm=tm)
    top_v, top_e = lax.top_k(logits, TOP_K)
    gates = jax.nn.softmax(top_v, axis=-1)
    flat_e = top_e.reshape(-1)
    onehot = (flat_e[:, None] == jnp.arange(ne, dtype=flat_e.dtype)[None, :]).astype(jnp.int32)
    rank = jnp.sum((jnp.cumsum(onehot, axis=0) - onehot) * onehot, axis=1)
    sizes = jnp.sum(onehot, axis=0)
    tiles_per = (sizes + tr - 1) // tr
    tile_end = jnp.cumsum(tiles_per)
    tile_start = tile_end - tiles_per
    n_used = tile_end[-1:].astype(jnp.int32)
    nt = (t * TOP_K) // tr + ne
    r_pad = nt * tr
    pos = (tile_start[flat_e] * tr + rank).astype(jnp.int32)
    src_tok = jnp.zeros((r_pad,), jnp.int32).at[pos].set(jnp.arange(t * TOP_K, dtype=jnp.int32) // TOP_K)
    tile_ids = jnp.minimum(jnp.arange(nt, dtype=jnp.int32), n_used[0] - 1)
    tile_expert = jnp.sum((tile_ids[:, None] >= tile_end[None, :]).astype(jnp.int32), axis=1).astype(jnp.int32)
    xs = gather_rows(hn, src_tok, n_used * (tr // tr_gather), tr=tr_gather)
    hid = moe_up(xs, w_gate, w_up, tile_expert, n_used, tr=tr, tn=tn_up)
    ys = moe_down(hid, w_down, tile_expert, n_used, tr=tr, tn=tn_down)
    return moe_combine(ys, pos.reshape(t, TOP_K).T.reshape(-1), gates, h, tr=tr_combine)


def _layer_weights(i, conv_w, conv_b, w_q, w_k, w_v, b_i, b_f, g_head, skip_a):
    nh = b_i.shape[1]
    pad = jnp.zeros((V7X_LANES - nh,), F32)
    b_if = jnp.concatenate([b_i[i].astype(F32), pad, b_f[i].astype(F32), pad]).reshape(1, 2 * V7X_LANES)
    da = conv_w.shape[-1]
    return dict(conv_w=conv_w[i], conv_b=conv_b[i].reshape(1, da), w_q=w_q[i], w_k=w_k[i], w_v=w_v[i],
                b_if=b_if, g_head=g_head[i].reshape(1, da), skip=skip_a[i].reshape(1, da))


def _split_w_in(w, d_a, nh, d_b, d, tn):
    i0 = 2 * d_a
    ub0 = i0 + 2 * nh
    k = w.shape[0]
    shifted = w[:, ub0:ub0 + d_b + 2 * d].astype(BF16)
    zpad = jnp.zeros((k, V7X_LANES - nh), BF16)
    gates = jnp.concatenate([w[:, i0:i0 + nh].astype(BF16), zpad, w[:, i0 + nh:i0 + 2 * nh].astype(BF16), zpad,
                             jnp.zeros((k, tn - 2 * V7X_LANES), BF16)], axis=1)
    return shifted, gates


def _tile_plan(n_rows, seq):
    tm = next(c for c in (832, 640, 512, 256, 128, 64, 32, 16) if n_rows % c == 0)
    tc = next(c for c in (320, 256, 128, 64, 32, 16, 8) if n_rows % c == 0)
    tm_big = 2 * tm if n_rows % (2 * tm) == 0 else tm
    return dict(tm=tm, tm_big=tm_big, tn=512, tn_down=256, mlstm_chunk=256, s5_chunk=256,
                moe_tr=512, moe_tn_up=512, moe_tn_down=512, gather_tr=256, combine_tr=tc,
                norm_tm=min(seq, 1024))


def kernel(x_prompt, x_sample, p_prompt, p_sample, state_mlstm_C, state_mlstm_n, state_mlstm_m, state_mlstm_conv,
           state_s5_re, state_s5_im, g_mix, w_in, conv_w, conv_b, w_q, w_k, w_v, b_i, b_f, g_head, skip_a, w_proj_a,
           s5_log_dt, s5_A_re, s5_A_im, s5_B_re, s5_B_im, s5_C_re, s5_C_im, s5_D, w_glu_b, w_out, g_ffn,
           w_ff_gate, w_ff_up, w_ff_down, w_router, b_router, w_moe_gate, w_moe_up, w_moe_down,
           g_ple, w_ple, w_pg, g_final):
    bsz, seq, d = x_prompt.shape
    bs = x_sample.shape[0]
    depth = g_mix.shape[0]
    nh = b_i.shape[1]
    d_a = conv_w.shape[-1]
    d_b = s5_D.shape[-1]
    n_p = bsz * seq
    t = n_p + bs
    tl = _tile_plan(t, seq)
    tm, tmb, tn = tl["tm"], tl["tm_big"], tl["tn"]

    h = jnp.concatenate([x_prompt.reshape(n_p, d), x_sample.reshape(bs, d)], axis=0).astype(F32)
    p_all = jnp.concatenate([p_prompt.reshape(depth, n_p, -1), p_sample.reshape(depth, bs, -1)], axis=1)

    col_ua, col_oa, col_ub = 0, 1, 2 * d_a // d_b
    col_ga = (2 * d_a + d_b) // tn
    col_gb = (2 * d_a + d_b + d) // tn
    col_if = (2 * d_a + d_b + 2 * d) // (2 * V7X_LANES)

    states = [[] for _ in range(11)]
    c_s_all = jnp.zeros(state_mlstm_C.shape, F32)
    for i in range(depth):
        lw = _layer_weights(i, conv_w, conv_b, w_q, w_k, w_v, b_i, b_f, g_head, skip_a)
        sp = _s5_params(dict(log_dt=s5_log_dt[i], a_re=s5_A_re[i], a_im=s5_A_im[i], b_re=s5_B_re[i],
                             b_im=s5_B_im[i], c_re=s5_C_re[i], c_im=s5_C_im[i], d_skip=s5_D[i]))
        w_shifted, w_gates = _split_w_in(w_in[i], d_a, nh, d_b, d, tn)
        proj = in_proj(h, g_mix[i], w_in, i, w_shifted, w_gates, 2 * d_a, tm=tmb, tn=tn)

        a_pre, c_p, n_pp, m_p, conv_p = mlstm_prompt(proj, col_ua, col_oa, col_if, t, bsz, seq, lw,
                                                     L=tl["mlstm_chunk"])
        a_pre, c_s_all, n_s, m_s, conv_s = mlstm_sample(proj, col_ua, col_oa, col_if, n_p, a_pre,
                                                        state_mlstm_C, i, c_s_all, state_mlstm_n[i].astype(F32),
                                                        state_mlstm_m[i].astype(F32),
                                                        state_mlstm_conv[i].astype(F32), lw)
        gy, sre_p, sim_p = s5_prompt(proj, col_ub, t, bsz, seq, sp, tc=tl["s5_chunk"])
        gy, sre_s, sim_s = s5_sample(proj, col_ub, n_p, gy, state_s5_re[i].astype(F32),
                                     state_s5_im[i].astype(F32), sp)
        b_gated = glu_gated_matmul(gy, w_glu_b, i, proj, col_gb, tm=tmb, tn=tn)
        mix = gated_add_matmul(a_pre, w_proj_a, i, proj, col_ga, b_gated, tm=tmb, tn=tn)
        h = matmul(mix, w_out, i, h, tm=tmb, tn=tn, name="out_proj")

        j = i // 2
        if i % 2 == 0:
            hid = norm_swiglu_up(h, g_ffn[i], w_ff_gate, w_ff_up, j, tm=tmb, tn=tn)
            h = matmul(hid, w_ff_down, j, h, tm=tmb, tn=tl["tn_down"], name="ffn_down")
        else:
            h = moe_layer(h, g_ffn[i], w_router[j], b_router[j], w_moe_gate[j], w_moe_up[j], w_moe_down[j],
                          tm=tm, tr=tl["moe_tr"], tn_up=tl["moe_tn_up"], tn_down=tl["moe_tn_down"],
                          tr_gather=tl["gather_tr"], tr_combine=tl["combine_tr"])
        h = ple_matmul(h, g_ple[i], p_all, w_pg, w_ple, i, tm=tm, tn=tn)

        g_s, p_s = S5_STATE, sre_p.shape[-1] // S5_STATE
        new = [c_p, n_pp, m_p, conv_p, sre_p.reshape(bsz, p_s, g_s), sim_p.reshape(bsz, p_s, g_s),
               n_s, m_s, conv_s, sre_s.reshape(bs, p_s, g_s), sim_s.reshape(bs, p_s, g_s)]
        for lst, s in zip(states, new):
            lst.append(s)

    y_prompt, y_sample = final_norm_split(h, g_final, n_p, tm=tl["norm_tm"])
    y_prompt = y_prompt.reshape(bsz, seq, d)
    y_sample = y_sample.reshape(bs, 1, d)
    st = [jnp.stack(lst) for lst in states]
    return (y_prompt, y_sample) + tuple(st[:6]) + (c_s_all,) + tuple(st[6:])
```

```python
import functools
import math

import numpy as np
import jax
import jax.numpy as jnp
from jax import lax
from jax.experimental import pallas as pl
from jax.experimental.pallas import tpu as pltpu

F32 = jnp.float32
BF16 = jnp.bfloat16
EPS = 1e-6

V7X_VMEM_BYTES = 64 * 1024 * 1024
V7X_LANES = 128
V7X_SUBLANES = 8
VMEM_LIMIT = 56 * 1024 * 1024

N_HEADS = 4
CONV_W = 4
S5_GROUP = 16
S5_STATE = 64
N_EXPERTS = 8
TOP_K = 2


def _cparams(n_axes, vmem=VMEM_LIMIT):
    return pltpu.CompilerParams(dimension_semantics=("arbitrary",) * n_axes, vmem_limit_bytes=vmem)


def _sigmoid(x):
    return 1.0 / (1.0 + jnp.exp(-x))


def _silu(x):
    return x * _sigmoid(x)


def _gelu_tanh(x):
    return 0.5 * x * (1.0 + jnp.tanh(math.sqrt(2.0 / math.pi) * (x + 0.044715 * (x * x * x))))


def _log_sigmoid(x):
    return jnp.minimum(x, 0.0) - jnp.log(1.0 + jnp.exp(-jnp.abs(x)))


def _dot(a, b):
    return jnp.dot(a, b, preferred_element_type=F32)


def _dot_nt(a, b):
    return lax.dot_general(a, b, (((1,), (1,)), ((), ())), preferred_element_type=F32)


def _dot_tn(a, b):
    return lax.dot_general(a, b, (((0,), (0,)), ((), ())), preferred_element_type=F32)


def _rmsnorm_rows(x, g):
    ms = jnp.mean(x * x, axis=-1, keepdims=True)
    return x * lax.rsqrt(ms + EPS) * g


def _row_tile_buffering(tm, k, dtype):
    two_copies = 2 * tm * k * jnp.dtype(dtype).itemsize
    return pl.Buffered(1) if two_copies > VMEM_LIMIT // 4 else None


def _row_chunks(tm):
    for rc in (256, 208, 128, 104, 64, 32, 16, 8):
        if tm % rc == 0:
            return rc
    return tm


def _norm_to_scratch(x_ref, g_ref, xn_ref, tm):
    rc = _row_chunks(tm)

    def body(r, c):
        rows = pl.ds(pl.multiple_of(r * rc, rc), rc)
        xn_ref[rows, :] = _rmsnorm_rows(x_ref[rows, :], g_ref[...]).astype(BF16)
        return c

    lax.fori_loop(0, tm // rc, body, 0)


def _wspec(w, layer, tn, col=lambda j: j):
    return pl.BlockSpec((None, w.shape[1], tn), lambda i, j: (layer, 0, col(j)))


def _k_in_proj(x_ref, g_ref, wa_ref, wb_ref, wif_ref, o_ref, xn_ref, *, tm, n_head, n_main, nh):
    j = pl.program_id(1)

    @pl.when(j == 0)
    def _():
        _norm_to_scratch(x_ref, g_ref, xn_ref, tm)

    @pl.when(j < n_head)
    def _():
        o_ref[...] = _dot_nt(xn_ref[...], wa_ref[0].astype(BF16))

    @pl.when(jnp.logical_and(j >= n_head, j < n_main))
    def _():
        o_ref[...] = _dot_nt(xn_ref[...], wb_ref[0].astype(BF16))

    @pl.when(j == n_main)
    def _():
        pre = _dot_nt(xn_ref[...], wif_ref[0].astype(BF16))
        o_ref[...] = jnp.zeros_like(o_ref)
        o_ref[:, 0:nh] = pre[:, 0:nh]
        o_ref[:, V7X_LANES:V7X_LANES + nh] = pre[:, nh:2 * nh]


def in_proj(x, g, w_in, layer, d_a, d_b, d, nh, *, tm, tn):
    t, k = x.shape
    assert (2 * nh) % V7X_SUBLANES == 0 and (2 * d_a) % tn == 0 and (d_b + 2 * d) % tn == 0
    wt = jnp.swapaxes(w_in, 1, 2)
    n_head = 2 * d_a // tn
    n_main = n_head + (d_b + 2 * d) // tn
    if_row = 2 * d_a

    def rows(nrows, start):
        return pl.BlockSpec((pl.Element(1), pl.Element(nrows), pl.Element(k)), lambda i, j: (layer, start(j), 0))

    return pl.pallas_call(
        functools.partial(_k_in_proj, tm=tm, n_head=n_head, n_main=n_main, nh=nh),
        grid=(t // tm, n_main + 1),
        in_specs=[pl.BlockSpec((tm, k), lambda i, j: (i, 0), pipeline_mode=_row_tile_buffering(tm, k, x.dtype)),
                  pl.BlockSpec((1, k), lambda i, j: (0, 0)),
                  rows(tn, lambda j: jnp.minimum(j, n_head - 1) * tn),
                  rows(tn, lambda j: (jnp.clip(j, n_head, n_main - 1) * (tn // V7X_SUBLANES)
                                      + 2 * nh // V7X_SUBLANES) * V7X_SUBLANES),
                  rows(2 * nh, lambda j: if_row)],
        out_specs=pl.BlockSpec((tm, tn), lambda i, j: (i, j)),
        out_shape=jax.ShapeDtypeStruct((t, (n_main + 1) * tn), F32),
        scratch_shapes=[pltpu.VMEM((tm, k), BF16)],
        compiler_params=_cparams(2), name="in_proj",
    )(x, g.reshape(1, k), wt, wt, wt)


def _k_mm_res(x_ref, w_ref, r_ref, o_ref):
    o_ref[...] = r_ref[...] + _dot(x_ref[...], w_ref[...].astype(BF16))


def _k_mm(x_ref, w_ref, o_ref):
    o_ref[...] = _dot(x_ref[...], w_ref[...].astype(BF16)).astype(o_ref.dtype)


def matmul(x, w, layer, res=None, *, tm, tn, out_dtype=F32, name="mm"):
    t, k = x.shape
    n = w.shape[2]
    in_specs = [pl.BlockSpec((tm, k), lambda i, j: (i, 0), pipeline_mode=_row_tile_buffering(tm, k, x.dtype)),
                _wspec(w, layer, tn)]
    args = [x, w]
    body = _k_mm
    if res is not None:
        in_specs.append(pl.BlockSpec((tm, tn), lambda i, j: (i, j)))
        args.append(res)
        body = _k_mm_res
    return pl.pallas_call(
        body, grid=(t // tm, pl.cdiv(n, tn)), in_specs=in_specs,
        out_specs=pl.BlockSpec((tm, tn), lambda i, j: (i, j)),
        out_shape=jax.ShapeDtypeStruct((t, n), out_dtype),
        compiler_params=_cparams(2), name=name,
    )(*args)


def _k_glu_gated(x_ref, wv_ref, wg_ref, gb_ref, o_ref):
    x = x_ref[...]
    val = _dot(x, wv_ref[...].astype(BF16))
    gate = _dot(x, wg_ref[...].astype(BF16))
    o_ref[...] = _sigmoid(gb_ref[...]) * (val * _sigmoid(gate))


def glu_gated_matmul(x, w, layer, proj, gate_col0, *, tm, tn, name="glu_b"):
    t, k = x.shape
    n = w.shape[2] // 2
    nj = n // tn
    return pl.pallas_call(
        _k_glu_gated, grid=(t // tm, nj),
        in_specs=[pl.BlockSpec((tm, k), lambda i, j: (i, 0)),
                  _wspec(w, layer, tn),
                  _wspec(w, layer, tn, lambda j: j + nj),
                  pl.BlockSpec((tm, tn), lambda i, j: (i, gate_col0 + j))],
        out_specs=pl.BlockSpec((tm, tn), lambda i, j: (i, j)),
        out_shape=jax.ShapeDtypeStruct((t, n), F32),
        compiler_params=_cparams(2), name=name,
    )(x, w, w, proj)


def _k_mm_gated_add(x_ref, w_ref, ga_ref, b_ref, o_ref):
    acc = _dot(x_ref[...], w_ref[...].astype(BF16))
    o_ref[...] = (_sigmoid(ga_ref[...]) * acc + b_ref[...]).astype(o_ref.dtype)


def gated_add_matmul(x, w, layer, proj, gate_col0, b, *, tm, tn, name="proj_a"):
    t, k = x.shape
    n = w.shape[2]
    return pl.pallas_call(
        _k_mm_gated_add, grid=(t // tm, n // tn),
        in_specs=[pl.BlockSpec((tm, k), lambda i, j: (i, 0)),
                  _wspec(w, layer, tn),
                  pl.BlockSpec((tm, tn), lambda i, j: (i, gate_col0 + j)),
                  pl.BlockSpec((tm, tn), lambda i, j: (i, j))],
        out_specs=pl.BlockSpec((tm, tn), lambda i, j: (i, j)),
        out_shape=jax.ShapeDtypeStruct((t, n), BF16),
        compiler_params=_cparams(2), name=name,
    )(x, w, proj, b)


def _k_swiglu_up(x_ref, g_ref, wg_ref, wu_ref, o_ref, xn_ref, *, tm):
    @pl.when(pl.program_id(1) == 0)
    def _():
        _norm_to_scratch(x_ref, g_ref, xn_ref, tm)

    xn = xn_ref[...]
    gate = _dot(xn, wg_ref[...].astype(BF16))
    up = _dot(xn, wu_ref[...].astype(BF16))
    o_ref[...] = (_silu(gate) * up).astype(o_ref.dtype)


def norm_swiglu_up(x, g, w_gate, w_up, layer, *, tm, tn, name="ffn_up"):
    t, k = x.shape
    n = w_gate.shape[2]
    return pl.pallas_call(
        functools.partial(_k_swiglu_up, tm=tm),
        grid=(t // tm, pl.cdiv(n, tn)),
        in_specs=[pl.BlockSpec((tm, k), lambda i, j: (i, 0), pipeline_mode=_row_tile_buffering(tm, k, x.dtype)),
                  pl.BlockSpec((1, k), lambda i, j: (0, 0)),
                  _wspec(w_gate, layer, tn), _wspec(w_up, layer, tn)],
        out_specs=pl.BlockSpec((tm, tn), lambda i, j: (i, j)),
        out_shape=jax.ShapeDtypeStruct((t, n), BF16),
        scratch_shapes=[pltpu.VMEM((tm, k), BF16)],
        compiler_params=_cparams(2), name=name,
    )(x, g.reshape(1, k), w_gate, w_up)


def _k_ple(x_ref, g_ref, p_ref, wpg_ref, wple_ref, r_ref, o_ref, xn_ref, pb_ref, *, tm):
    @pl.when(pl.program_id(1) == 0)
    def _():
        _norm_to_scratch(x_ref, g_ref, xn_ref, tm)
        pb_ref[...] = p_ref[...].astype(BF16)

    gate = _dot(xn_ref[...], wpg_ref[...].astype(BF16))
    emb = _dot(pb_ref[...], wple_ref[...].astype(BF16))
    o_ref[...] = r_ref[...] + emb * _sigmoid(gate)


def ple_matmul(x, g, p, w_pg, w_ple, layer, *, tm, tn, name="ple"):
    t, k = x.shape
    kp = p.shape[2]
    return pl.pallas_call(
        functools.partial(_k_ple, tm=tm),
        grid=(t // tm, k // tn),
        in_specs=[pl.BlockSpec((tm, k), lambda i, j: (i, 0)),
                  pl.BlockSpec((1, k), lambda i, j: (0, 0)),
                  pl.BlockSpec((None, tm, kp), lambda i, j: (layer, i, 0)),
                  _wspec(w_pg, layer, tn), _wspec(w_ple, layer, tn),
                  pl.BlockSpec((tm, tn), lambda i, j: (i, j))],
        out_specs=pl.BlockSpec((tm, tn), lambda i, j: (i, j)),
        out_shape=jax.ShapeDtypeStruct((t, k), F32),
        scratch_shapes=[pltpu.VMEM((tm, k), BF16), pltpu.VMEM((tm, kp), BF16)],
        compiler_params=_cparams(2), name=name,
    )(x, g.reshape(1, k), p, w_pg, w_ple, x)


def _k_final_norm(xp_ref, xs_ref, g_ref, op_ref, os_ref, *, n_prompt_tiles):
    i = pl.program_id(0)

    @pl.when(i < n_prompt_tiles)
    def _():
        op_ref[...] = _rmsnorm_rows(xp_ref[...], g_ref[...])

    @pl.when(i == n_prompt_tiles)
    def _():
        os_ref[...] = _rmsnorm_rows(xs_ref[...], g_ref[...])


def final_norm_split(x, g, n_prompt, *, tm):
    t, k = x.shape
    bs = t - n_prompt
    npt = n_prompt // tm
    last = npt - 1
    return pl.pallas_call(
        functools.partial(_k_final_norm, n_prompt_tiles=npt), grid=(npt + 1,),
        in_specs=[pl.BlockSpec((tm, k), lambda i: (jnp.minimum(i, last), 0)),
                  pl.BlockSpec((bs, k), lambda i: (n_prompt // bs, 0)),
                  pl.BlockSpec((1, k), lambda i: (0, 0))],
        out_specs=[pl.BlockSpec((tm, k), lambda i: (jnp.minimum(i, last), 0)),
                   pl.BlockSpec((bs, k), lambda i: (0, 0))],
        out_shape=[jax.ShapeDtypeStruct((n_prompt, k), F32), jax.ShapeDtypeStruct((bs, k), F32)],
        compiler_params=_cparams(1), name="final_norm",
    )(x, x, g.reshape(1, k))


def _split_hi_lo(x):
    hi = x.astype(BF16)
    lo = (x - hi.astype(F32)).astype(BF16)
    return hi, lo


def _k_mlstm_prompt(u_ref, o_ref_in, if_ref, cw_ref, cb_ref, wq_ref, wk_ref, wv_ref, bif_ref, gh_ref, sk_ref,
                    tri_ref, a_init, a_ref, c_out, n_out, m_out, conv_out, upad_ref, *, L, dh):
    del a_init
    c_idx = pl.program_id(1)
    nh = N_HEADS

    @pl.when(c_idx == 0)
    def _():
        c_out[...] = jnp.zeros_like(c_out)
        n_out[...] = jnp.zeros_like(n_out)
        m_out[...] = jnp.zeros_like(m_out)
        upad_ref[pl.ds(0, 8), :] = jnp.zeros((8, nh * dh), F32)

    @pl.when(c_idx > 0)
    def _():
        upad_ref[pl.ds(0, 8), :] = upad_ref[pl.ds(L, 8), :]

    u = u_ref[...]
    upad_ref[pl.ds(8, L), :] = u
    conv = cb_ref[...] + u * cw_ref[CONV_W - 1:CONV_W, :]
    for j in range(CONV_W - 1):
        conv = conv + upad_ref[pl.ds(8 - (CONV_W - 1) + j, L), :] * cw_ref[j:j + 1, :]
    cact = _silu(conv)
    conv_out[0] = upad_ref[pl.ds(L, 8), :]

    pre = if_ref[...]
    li = pre[:, :V7X_LANES] + bif_ref[:, :V7X_LANES]
    lf = _log_sigmoid(pre[:, V7X_LANES:] + bif_ref[:, V7X_LANES:])
    tri = tri_ref[...]
    lf_hi, lf_mid = _split_hi_lo(lf)
    lf_lo = (lf - lf_hi.astype(F32) - lf_mid.astype(F32)).astype(BF16)
    bcum = _dot(tri, lf_hi) + _dot(tri, lf_mid) + _dot(tri, lf_lo)
    li_t = li.T
    b_t = bcum.T
    row_id = lax.broadcasted_iota(jnp.int32, (L, L), 0)
    col_id = lax.broadcasted_iota(jnp.int32, (L, L), 1)
    causal = col_id <= row_id
    lane = lax.broadcasted_iota(jnp.int32, (1, V7X_LANES), 1)
    m_row = m_out[0]
    m_new_row = m_row

    for h in range(nh):
        hs = slice(h * dh, (h + 1) * dh)
        ch = cact[:, hs].astype(BF16)
        uh = u[:, hs].astype(BF16)
        q = _dot(ch, wq_ref[h].astype(BF16))
        k = _dot(ch, wk_ref[h].astype(BF16)) * (dh ** -0.5)
        v = _dot(uh, wv_ref[h].astype(BF16))
        qb, kb, vb = q.astype(BF16), k.astype(BF16), v.astype(BF16)

        b_col = bcum[:, h:h + 1]
        li_col = li[:, h:h + 1]
        r_row = li_t[h:h + 1, :] - b_t[h:h + 1, :]
        m_prev = m_row[:, h:h + 1]
        d = jnp.where(causal, b_col + r_row, -jnp.inf)
        inter = b_col + m_prev
        m_t = jnp.maximum(inter, jnp.max(d, axis=-1, keepdims=True))
        w_inter = jnp.exp(inter - m_t)
        s = _dot_nt(qb, kb) * jnp.exp(d - m_t)
        c_prev = c_out[0, h]
        n_prev = n_out[0, h:h + 1, :]
        num = w_inter * _dot(qb, c_prev.astype(BF16)) + _dot(s.astype(BF16), vb)
        den = w_inter * jnp.sum(q * n_prev, axis=-1, keepdims=True) + jnp.sum(s, axis=-1, keepdims=True)
        hh = num / jnp.maximum(jnp.abs(den), jnp.exp(-m_t))

        b_last = b_col[L - 1:L, :]
        g_col = b_last - b_col + li_col
        m_new = jnp.maximum(b_last + m_prev, jnp.max(g_col, axis=0, keepdims=True))
        decay = jnp.exp(b_last + m_prev - m_new)
        wk_ = jnp.exp(g_col - m_new) * k
        c_out[0, h] = decay * c_prev + _dot_tn(wk_.astype(BF16), vb)
        n_out[0, h:h + 1, :] = decay * n_prev + jnp.sum(wk_, axis=0, keepdims=True)
        m_new_row = jnp.where(lane == h, m_new, m_new_row)

        hn = _rmsnorm_rows(hh, gh_ref[:, hs])
        gated = (hn + sk_ref[:, hs] * cact[:, hs]) * _sigmoid(o_ref_in[:, hs])
        a_ref[:, hs] = gated.astype(a_ref.dtype)

    m_out[0] = m_new_row


def mlstm_prompt(proj, col_u, col_o, col_if, n_rows, bsz, seq, lw, *, L):
    dh = lw["w_q"].shape[-1]
    da = N_HEADS * dh
    nc = seq // L
    tri = jnp.asarray(np.tril(np.ones((L, L), np.float32)), BF16)
    row_blk = lambda b, c: b * nc + c
    full = lambda *shape: pl.BlockSpec(shape, lambda b, c: (0,) * len(shape))
    outs = pl.pallas_call(
        functools.partial(_k_mlstm_prompt, L=L, dh=dh),
        grid=(bsz, nc),
        in_specs=[pl.BlockSpec((L, da), lambda b, c: (row_blk(b, c), col_u)),
                  pl.BlockSpec((L, da), lambda b, c: (row_blk(b, c), col_o)),
                  pl.BlockSpec((L, 2 * V7X_LANES), lambda b, c: (row_blk(b, c), col_if)),
                  full(CONV_W, da), full(1, da), full(N_HEADS, dh, dh), full(N_HEADS, dh, dh),
                  full(N_HEADS, dh, dh), full(1, 2 * V7X_LANES), full(1, da), full(1, da), full(L, L),
                  pl.BlockSpec(memory_space=pl.ANY)],
        out_specs=[pl.BlockSpec((L, da), lambda b, c: (row_blk(b, c), 0)),
                   pl.BlockSpec((1, N_HEADS, dh, dh), lambda b, c: (b, 0, 0, 0)),
                   pl.BlockSpec((1, N_HEADS, dh), lambda b, c: (b, 0, 0)),
                   pl.BlockSpec((1, 1, V7X_LANES), lambda b, c: (b, 0, 0)),
                   pl.BlockSpec((1, 8, da), lambda b, c: (b, 0, 0))],
        out_shape=[jax.ShapeDtypeStruct((n_rows, da), BF16),
                   jax.ShapeDtypeStruct((bsz, N_HEADS, dh, dh), F32),
                   jax.ShapeDtypeStruct((bsz, N_HEADS, dh), F32),
                   jax.ShapeDtypeStruct((bsz, 1, V7X_LANES), F32),
                   jax.ShapeDtypeStruct((bsz, 8, da), F32)],
        scratch_shapes=[pltpu.VMEM((L + 8, da), F32)],
        input_output_aliases={12: 0},
        compiler_params=_cparams(2), name="mlstm_prompt",
    )(proj, proj, proj, lw["conv_w"], lw["conv_b"], lw["w_q"], lw["w_k"], lw["w_v"], lw["b_if"],
      lw["g_head"], lw["skip"], tri, jnp.zeros((n_rows, da), BF16))
    a_pre, c_p, n_p, m_p, conv_p = outs
    return a_pre, c_p, n_p, m_p[:, 0, :N_HEADS], conv_p[:, 8 - (CONV_W - 1):, :]


def _k_mlstm_sample_pre(u_ref, if_ref, conv_ref, m_ref, cw_ref, cb_ref, wq_ref, wk_ref, wv_ref, bif_ref,
                        q_out, k_out, v_out, c_out, gates_out, conv_out, *, dh):
    nh = N_HEADS
    u = u_ref[...]
    conv = cb_ref[...] + u * cw_ref[CONV_W - 1:CONV_W, :]
    for j in range(CONV_W - 1):
        conv = conv + conv_ref[j] * cw_ref[j:j + 1, :]
        if j > 0:
            conv_out[j - 1] = conv_ref[j]
    conv_out[CONV_W - 2] = u
    cact = _silu(conv)
    c_out[...] = cact
    for h in range(nh):
        hs = slice(h * dh, (h + 1) * dh)
        ch = cact[:, hs].astype(BF16)
        q_out[:, hs] = _dot(ch, wq_ref[h].astype(BF16))
        k_out[:, hs] = _dot(ch, wk_ref[h].astype(BF16)) * (dh ** -0.5)
        v_out[:, hs] = _dot(u[:, hs].astype(BF16), wv_ref[h].astype(BF16))
    pre = if_ref[...]
    li = pre[:, :V7X_LANES] + bif_ref[:, :V7X_LANES]
    lf = _log_sigmoid(pre[:, V7X_LANES:] + bif_ref[:, V7X_LANES:])
    m_prev = m_ref[...]
    inter = lf + m_prev
    m_t = jnp.maximum(inter, li)
    gates_out[0] = jnp.exp(inter - m_t)
    gates_out[1] = jnp.exp(li - m_t)
    gates_out[2] = jnp.exp(-m_t)
    gates_out[3] = m_t


def _k_mlstm_sample_step(q_ref, k_ref, v_ref, gates_ref, c_ref, n_ref, cact_ref, o_ref_in, gh_ref, sk_ref,
                         a_any, c_any, c_out, n_out, a_out, hh_ref, *, bt, dh):
    del a_any, c_any
    i = pl.program_id(0)
    nh = N_HEADS
    q = q_ref[...]
    k = k_ref[...]
    v = v_ref[...]
    w_inter = gates_ref[0]
    w_new = gates_ref[1]
    e_neg_m = gates_ref[2]
    n_prev = n_ref[...]
    rows = pl.ds(pl.multiple_of(i * bt, bt), bt)
    for h in range(nh):
        hs = slice(h * dh, (h + 1) * dh)
        qh, kh, vh, nh_prev = q[:, hs], k[:, hs], v[:, hs], n_prev[:, hs]
        q_t = qh.T
        k_t = kh.T
        wi = w_inter[:, h:h + 1]
        wn = w_new[:, h:h + 1]
        s = jnp.sum(qh * kh, axis=-1, keepdims=True) * wn
        den = wi * jnp.sum(qh * nh_prev, axis=-1, keepdims=True) + s
        wv = wn * vh
        qc_rows = []
        for bl in range(bt):
            c_prev = c_ref[bl, h]
            qc_rows.append(jnp.sum(q_t[:, bl:bl + 1] * c_prev, axis=0, keepdims=True))
            c_out[bl, h] = wi[bl:bl + 1, :] * c_prev + k_t[:, bl:bl + 1] * wv[bl:bl + 1, :]
        qc = jnp.concatenate(qc_rows, axis=0)
        num = wi * qc + s * vh
        hh = num / jnp.maximum(jnp.abs(den), e_neg_m[:, h:h + 1])
        hh_ref[rows, hs] = hh
        n_out[:, hs] = wi * nh_prev + wn * kh

    @pl.when(i == pl.num_programs(0) - 1)
    def _():
        for h in range(nh):
            hs = slice(h * dh, (h + 1) * dh)
            hn = _rmsnorm_rows(hh_ref[:, hs], gh_ref[:, hs])
            a_out[:, hs] = ((hn + sk_ref[:, hs] * cact_ref[:, hs]) * _sigmoid(o_ref_in[:, hs])).astype(a_out.dtype)


def mlstm_sample(proj, col_u, col_o, col_if, row0, a_pre_all, c_all, layer, c_new_all, n0, m0, conv0, lw, *,
                 bt=8):
    _, bs, nh, dh, _ = c_all.shape
    da = nh * dh
    rb = row0 // bs
    m_pad = jnp.pad(m0, ((0, 0), (0, V7X_LANES - nh)))
    conv_t = jnp.transpose(conv0, (1, 0, 2))
    full = lambda *shape: pl.BlockSpec(shape, lambda i: (0,) * len(shape))
    q, k, v, cact, gates, conv_new = pl.pallas_call(
        functools.partial(_k_mlstm_sample_pre, dh=dh),
        grid=(1,),
        in_specs=[pl.BlockSpec((bs, da), lambda i: (rb, col_u)),
                  pl.BlockSpec((bs, 2 * V7X_LANES), lambda i: (rb, col_if)),
                  full(CONV_W - 1, bs, da), full(bs, V7X_LANES), full(CONV_W, da), full(1, da),
                  full(nh, dh, dh), full(nh, dh, dh), full(nh, dh, dh), full(1, 2 * V7X_LANES)],
        out_specs=[full(bs, da), full(bs, da), full(bs, da), full(bs, da), full(4, bs, V7X_LANES),
                   full(CONV_W - 1, bs, da)],
        out_shape=[jax.ShapeDtypeStruct((bs, da), F32)] * 4
                  + [jax.ShapeDtypeStruct((4, bs, V7X_LANES), F32),
                     jax.ShapeDtypeStruct((CONV_W - 1, bs, da), F32)],
        compiler_params=_cparams(1), name="mlstm_sample_pre",
    )(proj, proj, conv_t, m_pad, lw["conv_w"], lw["conv_b"], lw["w_q"], lw["w_k"], lw["w_v"], lw["b_if"])

    blk = lambda *shape: pl.BlockSpec(shape, lambda i: (i,) + (0,) * (len(shape) - 1))
    cst = lambda *shape: pl.BlockSpec(shape, lambda i: (0,) * len(shape))
    c_blk = pl.BlockSpec((None, bt, nh, dh, dh), lambda i: (layer, i, 0, 0, 0))
    c_new_all, n_new, a_pre_all = pl.pallas_call(
        functools.partial(_k_mlstm_sample_step, bt=bt, dh=dh),
        grid=(bs // bt,),
        in_specs=[blk(bt, da), blk(bt, da), blk(bt, da),
                  pl.BlockSpec((4, bt, V7X_LANES), lambda i: (0, i, 0)),
                  c_blk, blk(bt, da), cst(bs, da),
                  pl.BlockSpec((bs, da), lambda i: (rb, col_o)), cst(1, da), cst(1, da),
                  pl.BlockSpec(memory_space=pl.ANY), pl.BlockSpec(memory_space=pl.ANY)],
        out_specs=[c_blk, blk(bt, da), pl.BlockSpec((bs, da), lambda i: (rb, 0))],
        out_shape=[jax.ShapeDtypeStruct(c_new_all.shape, F32), jax.ShapeDtypeStruct((bs, da), F32),
                   jax.ShapeDtypeStruct(a_pre_all.shape, a_pre_all.dtype)],
        scratch_shapes=[pltpu.VMEM((bs, da), F32)],
        input_output_aliases={10: 2, 11: 0},
        compiler_params=_cparams(1), name="mlstm_sample_step",
    )(q, k, v, gates, c_all, n0.reshape(bs, da), cact, proj, lw["g_head"], lw["skip"], a_pre_all, c_new_all)
    m_new = gates[3][:, :nh]
    return a_pre_all, c_new_all, n_new.reshape(bs, nh, dh), m_new, jnp.transpose(conv_new, (1, 0, 2))


S5_SLAB_GROUPS = V7X_LANES // S5_GROUP
S5_SLAB_STATES = S5_SLAB_GROUPS * S5_STATE


def _s5_params(lp):
    g, p = lp["a_re"].shape
    dt = jnp.exp(lp["log_dt"].astype(F32))[:, None]
    a_re = lp["a_re"].astype(F32)
    a_im = lp["a_im"].astype(F32)
    lam_re = a_re * dt
    lam_im = a_im * dt
    mag = jnp.exp(lam_re)
    ab_re = mag * jnp.cos(lam_im)
    ab_im = mag * jnp.sin(lam_im)
    den = a_re * a_re + a_im * a_im
    nr = ab_re - 1.0
    ni = ab_im
    k_re = (nr * a_re + ni * a_im) / den
    k_im = (ni * a_re - nr * a_im) / den
    b_re = lp["b_re"].astype(F32)
    b_im = lp["b_im"].astype(F32)
    bb_re = k_re[..., None] * b_re - k_im[..., None] * b_im
    bb_im = k_re[..., None] * b_im + k_im[..., None] * b_re
    ns = g // S5_SLAB_GROUPS
    eye = jnp.eye(S5_SLAB_GROUPS, dtype=F32)

    def in_blockdiag(bb):
        bs = bb.reshape(ns, S5_SLAB_GROUPS, p, S5_GROUP)
        w = jnp.einsum("ab,sapc->sacbp", eye, bs)
        return w.reshape(ns, V7X_LANES, S5_SLAB_STATES).astype(BF16)

    def out_blockdiag(cc):
        cs = cc.astype(F32).reshape(ns, S5_SLAB_GROUPS, S5_GROUP, p)
        w = jnp.einsum("ab,sacp->sapbc", eye, cs)
        return w.reshape(ns, S5_SLAB_STATES, V7X_LANES)

    wc = jnp.concatenate([out_blockdiag(lp["c_re"]), -out_blockdiag(lp["c_im"])], axis=1).astype(BF16)
    return dict(lam_re=lam_re.reshape(1, g * p), lam_im=lam_im.reshape(1, g * p),
                ab_re=ab_re.reshape(1, g * p), ab_im=ab_im.reshape(1, g * p),
                wb_re=in_blockdiag(bb_re), wb_im=in_blockdiag(bb_im), wc=wc,
                d_skip=lp["d_skip"].astype(F32).reshape(1, g * S5_GROUP))


def _s5_powers(sp, ks):
    kk = jnp.asarray(ks, F32)[:, None]
    mag = jnp.exp(kk * sp["lam_re"])
    return mag * jnp.cos(kk * sp["lam_im"]), mag * jnp.sin(kk * sp["lam_im"])


def _s5_input_proj(u_bf, wbr_ref, wbi_ref, bur_ref, bui_ref):
    ns = wbr_ref.shape[0]
    for s in range(ns):
        us = u_bf[:, s * V7X_LANES:(s + 1) * V7X_LANES]
        cols = slice(s * S5_SLAB_STATES, (s + 1) * S5_SLAB_STATES)
        bur_ref[:, cols] = _dot(us, wbr_ref[s])
        bui_ref[:, cols] = _dot(us, wbi_ref[s])


def _s5_output_proj(xr_ref, xi_ref, wc_ref, skip):
    ns = wc_ref.shape[0]
    outs = []
    for s in range(ns):
        cols = slice(s * S5_SLAB_STATES, (s + 1) * S5_SLAB_STATES)
        xcat = jnp.concatenate([xr_ref[:, cols].astype(BF16), xi_ref[:, cols].astype(BF16)], axis=1)
        y = _dot(xcat, wc_ref[s]) + skip[:, s * V7X_LANES:(s + 1) * V7X_LANES]
        outs.append(_gelu_tanh(y).astype(BF16))
    return jnp.concatenate(outs, axis=1)


def _k_s5_prompt(u_ref, wbr_ref, wbi_ref, wc_ref, d_ref, abr_ref, abi_ref, tpr_ref, tpi_ref, ajr_ref, aji_ref,
                 perm_ref, permt_ref, gy_init, gy_ref, sre_out, sim_out,
                 bur0, bur1, bui0, bui1, xb0, xb1, *, tc):
    del gy_init
    c_idx = pl.program_id(1)
    nsteps = tc // V7X_SUBLANES
    ns = wbr_ref.shape[0]
    lw = S5_SLAB_STATES
    pair = 2 * V7X_SUBLANES
    bur, bui, xb = (bur0, bur1), (bui0, bui1), (xb0, xb1)

    @pl.when(c_idx == 0)
    def _():
        sre_out[...] = jnp.zeros_like(sre_out)
        sim_out[...] = jnp.zeros_like(sim_out)

    u = u_ref[...]
    u_hi, u_lo = _split_hi_lo(u)
    perm = perm_ref[...]
    up_hi = _dot(perm, u_hi)
    skip = d_ref[...] * (up_hi + _dot(perm, u_lo))
    up_bf = up_hi.astype(BF16)

    sub = lax.broadcasted_iota(jnp.int32, (V7X_SUBLANES, lw), 0)

    def input_proj(s):
        us = up_bf[:, s * V7X_LANES:(s + 1) * V7X_LANES]
        bur[s % 2][...] = _dot(us, wbr_ref[s])
        bui[s % 2][...] = _dot(us, wbi_ref[s])

    def scan(s):
        br, bi, xo = bur[s % 2], bui[s % 2], xb[s % 2]
        lanes = slice(s * lw, (s + 1) * lw)
        ar = jnp.broadcast_to(abr_ref[:, lanes], (V7X_SUBLANES, lw))
        ai = jnp.broadcast_to(abi_ref[:, lanes], (V7X_SUBLANES, lw))
        er = jnp.zeros((V7X_SUBLANES, lw), F32)
        ei = er
        for i in range(nsteps):
            rows = slice(i * V7X_SUBLANES, (i + 1) * V7X_SUBLANES)
            er, ei = (ar * er - ai * ei + br[rows, :], ar * ei + ai * er + bi[rows, :])
            br[rows, :] = er
            bi[rows, :] = ei
        for d, row in ((1, 0), (2, 1), (4, 3)):
            pr = ajr_ref[row:row + 1, lanes]
            pi = aji_ref[row:row + 1, lanes]
            sr = pltpu.roll(er, d, 0)
            si = pltpu.roll(ei, d, 0)
            keep = sub >= d
            er, ei = (er + jnp.where(keep, pr * sr - pi * si, 0.0),
                      ei + jnp.where(keep, pr * si + pi * sr, 0.0))
        c0r = jnp.broadcast_to(sre_out[0, :, lanes], (V7X_SUBLANES, lw))
        c0i = jnp.broadcast_to(sim_out[0, :, lanes], (V7X_SUBLANES, lw))
        ajr = ajr_ref[:, lanes]
        aji = aji_ref[:, lanes]
        fr = ajr * c0r - aji * c0i + er
        fi = ajr * c0i + aji * c0r + ei
        cin_r = jnp.where(sub >= 1, pltpu.roll(fr, 1, 0), c0r)
        cin_i = jnp.where(sub >= 1, pltpu.roll(fi, 1, 0), c0i)
        sre_out[0, :, lanes] = fr[V7X_SUBLANES - 1:V7X_SUBLANES, :]
        sim_out[0, :, lanes] = fi[V7X_SUBLANES - 1:V7X_SUBLANES, :]

        cin2_r = jnp.concatenate([cin_r, cin_r], axis=0)
        cin2_i = jnp.concatenate([cin_i, cin_i], axis=0)
        for k in range(tc // pair):
            rows = slice(k * pair, (k + 1) * pair)
            pr = tpr_ref[rows, lanes]
            pi = tpi_ref[rows, lanes]
            xo[rows, :lw] = (br[rows, :] + (pr * cin2_r - pi * cin2_i)).astype(BF16)
            xo[rows, lw:] = (bi[rows, :] + (pr * cin2_i + pi * cin2_r)).astype(BF16)

    def output_proj(s):
        y = _dot(xb[s % 2][...], wc_ref[s]) + skip[:, s * V7X_LANES:(s + 1) * V7X_LANES]
        return _gelu_tanh(y).astype(BF16)

    outs = []
    input_proj(0)
    for s in range(ns):
        if s + 1 < ns:
            input_proj(s + 1)
        scan(s)
        outs.append(output_proj(s))
    g_perm = jnp.concatenate(outs, axis=1)
    gy_ref[...] = _dot(permt_ref[...], g_perm).astype(gy_ref.dtype)


def s5_prompt(proj, col_u, n_rows, bsz, seq, sp, *, tc=256):
    db = sp["d_skip"].shape[1]
    nch = sp["ab_re"].shape[1]
    nc = seq // tc
    nsteps = tc // V7X_SUBLANES
    tpr, tpi = _s5_powers(sp, np.repeat(np.arange(1, nsteps + 1), V7X_SUBLANES))
    ajr, aji = _s5_powers(sp, nsteps * np.arange(1, V7X_SUBLANES + 1))
    perm = np.zeros((tc, tc), np.float32)
    r = np.arange(tc)
    perm[r, (r % V7X_SUBLANES) * nsteps + r // V7X_SUBLANES] = 1.0
    full = lambda a: pl.BlockSpec(a.shape, lambda b, c: (0,) * a.ndim)
    consts = [sp["wb_re"], sp["wb_im"], sp["wc"], sp["d_skip"], sp["ab_re"], sp["ab_im"], tpr, tpi, ajr, aji,
              jnp.asarray(perm, BF16), jnp.asarray(perm.T, BF16)]
    gy, s_re, s_im = pl.pallas_call(
        functools.partial(_k_s5_prompt, tc=tc),
        grid=(bsz, nc),
        in_specs=[pl.BlockSpec((tc, db), lambda b, c: (b * nc + c, col_u))] + [full(a) for a in consts]
                 + [pl.BlockSpec(memory_space=pl.ANY)],
        out_specs=[pl.BlockSpec((tc, db), lambda b, c: (b * nc + c, 0)),
                   pl.BlockSpec((1, 1, nch), lambda b, c: (b, 0, 0)),
                   pl.BlockSpec((1, 1, nch), lambda b, c: (b, 0, 0))],
        out_shape=[jax.ShapeDtypeStruct((n_rows, db), BF16),
                   jax.ShapeDtypeStruct((bsz, 1, nch), F32), jax.ShapeDtypeStruct((bsz, 1, nch), F32)],
        scratch_shapes=[pltpu.VMEM((tc, S5_SLAB_STATES), F32)] * 4 + [pltpu.VMEM((tc, 2 * S5_SLAB_STATES), BF16)] * 2,
        input_output_aliases={1 + len(consts): 0},
        compiler_params=_cparams(2), name="s5_prompt",
    )(proj, *consts, jnp.zeros((n_rows, db), BF16))
    return gy, s_re, s_im


def _k_s5_sample(u_ref, x0r_ref, x0i_ref, wbr_ref, wbi_ref, wc_ref, d_ref, abr_ref, abi_ref, gy_any,
                 gy_ref, xr_out, xi_out):
    del gy_any
    u = u_ref[...]
    _s5_input_proj(u.astype(BF16), wbr_ref, wbi_ref, xr_out, xi_out)
    ar = abr_ref[...]
    ai = abi_ref[...]
    x0r = x0r_ref[...]
    x0i = x0i_ref[...]
    xr_out[...] = xr_out[...] + (ar * x0r - ai * x0i)
    xi_out[...] = xi_out[...] + (ar * x0i + ai * x0r)
    gy_ref[...] = _s5_output_proj(xr_out, xi_out, wc_ref, d_ref[...] * u).astype(gy_ref.dtype)


def s5_sample(proj, col_u, row0, gy_all, x0_re, x0_im, sp):
    bs = x0_re.shape[0]
    db = sp["d_skip"].shape[1]
    nch = sp["ab_re"].shape[1]
    rb = row0 // bs
    full = lambda a: pl.BlockSpec(a.shape, lambda i: (0,) * a.ndim)
    consts = [sp["wb_re"], sp["wb_im"], sp["wc"], sp["d_skip"], sp["ab_re"], sp["ab_im"]]
    x0r = x0_re.reshape(bs, nch)
    x0i = x0_im.reshape(bs, nch)
    gy_all, xr, xi = pl.pallas_call(
        _k_s5_sample,
        grid=(1,),
        in_specs=[pl.BlockSpec((bs, db), lambda i: (rb, col_u)), full(x0r), full(x0i)]
                 + [full(a) for a in consts] + [pl.BlockSpec(memory_space=pl.ANY)],
        out_specs=[pl.BlockSpec((bs, db), lambda i: (rb, 0)),
                   pl.BlockSpec((bs, nch), lambda i: (0, 0)), pl.BlockSpec((bs, nch), lambda i: (0, 0))],
        out_shape=[jax.ShapeDtypeStruct(gy_all.shape, gy_all.dtype),
                   jax.ShapeDtypeStruct((bs, nch), F32), jax.ShapeDtypeStruct((bs, nch), F32)],
        input_output_aliases={9: 0},
        compiler_params=_cparams(1), name="s5_sample",
    )(proj, x0r, x0i, *consts, gy_all)
    return gy_all, xr, xi


def _k_router(x_ref, g_ref, w_ref, b_ref, hn_ref, lg_ref):
    hn = _rmsnorm_rows(x_ref[...], g_ref[...])
    hn_ref[...] = hn
    x_hi, x_lo = _split_hi_lo(hn)
    w_hi, w_lo = _split_hi_lo(w_ref[...])
    lg_ref[...] = _dot(x_hi, w_hi) + (_dot(x_lo, w_hi) + _dot(x_hi, w_lo)) + b_ref[...]


def router(x, g, w_router, b_router, *, tm):
    t, k = x.shape
    ne = w_router.shape[1]
    w_pad = jnp.pad(w_router, ((0, 0), (0, V7X_LANES - ne)))
    b_pad = jnp.pad(b_router.astype(F32), (0, V7X_LANES - ne)).reshape(1, V7X_LANES)
    hn, lg = pl.pallas_call(
        _k_router, grid=(t // tm,),
        in_specs=[pl.BlockSpec((tm, k), lambda i: (i, 0)), pl.BlockSpec((1, k), lambda i: (0, 0)),
                  pl.BlockSpec((k, V7X_LANES), lambda i: (0, 0)), pl.BlockSpec((1, V7X_LANES), lambda i: (0, 0))],
        out_specs=[pl.BlockSpec((tm, k), lambda i: (i, 0)), pl.BlockSpec((tm, V7X_LANES), lambda i: (i, 0))],
        out_shape=[jax.ShapeDtypeStruct((t, k), F32), jax.ShapeDtypeStruct((t, V7X_LANES), F32)],
        compiler_params=_cparams(1), name="router",
    )(x, g.reshape(1, k), w_pad, b_pad)
    return hn, lg[:, :ne]


DMA_ISSUE_UNROLL = 8


def _k_gather_rows(nused_ref, tok_ref, src_hbm, o_ref, buf_ref, sems, *, tr):
    t = pl.program_id(0)
    n_used = nused_ref[0]

    def issue_tile(tile):
        slot = tile % 2
        base = tile * tr

        def issue(r, c):
            tok = tok_ref[base + r]
            pltpu.make_async_copy(src_hbm.at[pl.ds(tok, 1), :], buf_ref.at[slot, pl.ds(r, 1), :],
                                  sems.at[slot]).start()
            return c

        lax.fori_loop(0, tr, issue, 0, unroll=DMA_ISSUE_UNROLL)

    @pl.when(t == 0)
    def _():
        issue_tile(t)

    @pl.when(t + 1 < n_used)
    def _():
        issue_tile(t + 1)

    @pl.when(t < n_used)
    def _():
        slot = t % 2
        pltpu.make_async_copy(src_hbm.at[pl.ds(0, tr), :], buf_ref.at[slot], sems.at[slot]).wait()
        o_ref[...] = buf_ref[slot].astype(o_ref.dtype)

    @pl.when(t >= n_used)
    def _():
        o_ref[...] = jnp.zeros_like(o_ref)


def gather_rows(src, tok, n_used, *, tr):
    t, k = src.shape
    r_pad = tok.shape[0]
    nt = r_pad // tr
    return pl.pallas_call(
        functools.partial(_k_gather_rows, tr=tr),
        grid_spec=pltpu.PrefetchScalarGridSpec(
            num_scalar_prefetch=2, grid=(nt,),
            in_specs=[pl.BlockSpec(memory_space=pl.ANY)],
            out_specs=pl.BlockSpec((tr, k), lambda i, nu, tk: (i, 0)),
            scratch_shapes=[pltpu.VMEM((2, tr, k), src.dtype), pltpu.SemaphoreType.DMA((2,))]),
        out_shape=jax.ShapeDtypeStruct((r_pad, k), BF16),
        compiler_params=_cparams(1), name="moe_gather",
    )(n_used, tok, src)


def _expert_tile_pipeline(n, t0, src_hbm, src_buf, src_sem, dst_hbm, dst_buf, dst_sem, col0, width, tr, compute):
    def rows(k):
        return pl.ds(pl.multiple_of((t0 + k) * tr, tr), tr)

    def fetch(k, slot):
        return pltpu.make_async_copy(src_hbm.at[rows(k), :], src_buf.at[slot], src_sem.at[slot])

    def writeback(k, slot):
        return pltpu.make_async_copy(dst_buf.at[slot, :, pl.ds(0, width)],
                                     dst_hbm.at[rows(k), pl.ds(col0, width)], dst_sem.at[slot])

    fetch(0, 0).start()

    def body(k, c):
        slot = lax.rem(k, 2)

        @pl.when(k + 1 < n)
        def _():
            fetch(k + 1, 1 - slot).start()

        fetch(k, slot).wait()

        @pl.when(k >= 2)
        def _():
            writeback(k - 2, slot).wait()

        dst_buf[slot] = compute(src_buf[slot])
        writeback(k, slot).start()
        return c

    lax.fori_loop(0, n, body, 0)

    @pl.when(n >= 2)
    def _():
        writeback(n - 2, lax.rem(n, 2)).wait()
    writeback(n - 1, lax.rem(n - 1, 2)).wait()


def _zero_tiles(first, last, dst_hbm, dst_buf, dst_sem, col0, width, tr):
    dst_buf[0] = jnp.zeros(dst_buf.shape[1:], dst_buf.dtype)

    def body(t, c):
        cp = pltpu.make_async_copy(dst_buf.at[0, :, pl.ds(0, width)],
                                   dst_hbm.at[pl.ds(pl.multiple_of(t * tr, tr), tr), pl.ds(col0, width)],
                                   dst_sem.at[0])
        cp.start()
        cp.wait()
        return c

    lax.fori_loop(first, last, body, 0)


def _expert_grid_body(tstart_ref, ntile_ref, nused_ref, src_hbm, dst_hbm, src_buf, dst_buf, src_sem, dst_sem,
                      load_weights, compute, *, tr, tn, n_cols, nt):
    j = pl.program_id(0)
    e = pl.program_id(1)
    n = ntile_ref[e]
    nj = pl.cdiv(n_cols, tn)
    col0 = pl.multiple_of(j * tn, tn)

    @pl.when(n > 0)
    def _():
        load_weights()

    def run(width):
        @pl.when(n > 0)
        def _():
            _expert_tile_pipeline(n, tstart_ref[e], src_hbm, src_buf, src_sem, dst_hbm, dst_buf, dst_sem,
                                  col0, width, tr, compute)

        @pl.when(e == pl.num_programs(1) - 1)
        def _():
            _zero_tiles(nused_ref[0], nt, dst_hbm, dst_buf, dst_sem, col0, width, tr)

    last_width = n_cols - (nj - 1) * tn
    if last_width == tn:
        run(tn)
    else:
        pl.when(j < nj - 1)(lambda: run(tn))
        pl.when(j == nj - 1)(lambda: run(last_width))


def _k_moe_up(tstart_ref, ntile_ref, nused_ref, xs_hbm, wg_ref, wu_ref, hid_hbm, wgb_ref, wub_ref,
              x_buf, o_buf, x_sem, o_sem, **static):
    def load_weights():
        wgb_ref[...] = wg_ref[...].astype(BF16)
        wub_ref[...] = wu_ref[...].astype(BF16)

    def compute(x):
        return (_silu(_dot(x, wgb_ref[...])) * _dot(x, wub_ref[...])).astype(o_buf.dtype)

    _expert_grid_body(tstart_ref, ntile_ref, nused_ref, xs_hbm, hid_hbm, x_buf, o_buf, x_sem, o_sem,
                      load_weights, compute, **static)


def moe_up(xs, w_gate, w_up, tile_start, tiles_per, n_used, *, tr, tn):
    r_pad, k = xs.shape
    ne, _, f = w_gate.shape
    wmap = lambda j, e, ts, tp, nu: (e, 0, j)
    return pl.pallas_call(
        functools.partial(_k_moe_up, tr=tr, tn=tn, n_cols=f, nt=r_pad // tr),
        grid_spec=pltpu.PrefetchScalarGridSpec(
            num_scalar_prefetch=3, grid=(pl.cdiv(f, tn), ne),
            in_specs=[pl.BlockSpec(memory_space=pl.ANY),
                      pl.BlockSpec((None, k, tn), wmap), pl.BlockSpec((None, k, tn), wmap)],
            out_specs=pl.BlockSpec(memory_space=pl.ANY),
            scratch_shapes=[pltpu.VMEM((k, tn), BF16), pltpu.VMEM((k, tn), BF16),
                            pltpu.VMEM((2, tr, k), xs.dtype), pltpu.VMEM((2, tr, tn), BF16),
                            pltpu.SemaphoreType.DMA((2,)), pltpu.SemaphoreType.DMA((2,))]),
        out_shape=jax.ShapeDtypeStruct((r_pad, f), BF16),
        compiler_params=_cparams(2), name="moe_up",
    )(tile_start, tiles_per, n_used, xs, w_gate, w_up)


def _k_moe_down(tstart_ref, ntile_ref, nused_ref, hid_hbm, w_ref, ys_hbm, wb_ref, h_buf, o_buf, h_sem, o_sem,
                **static):
    def load_weights():
        wb_ref[...] = w_ref[...].astype(BF16)

    def compute(x):
        return _dot(x, wb_ref[...])

    _expert_grid_body(tstart_ref, ntile_ref, nused_ref, hid_hbm, ys_hbm, h_buf, o_buf, h_sem, o_sem,
                      load_weights, compute, **static)


def moe_down(hid, w_down, tile_start, tiles_per, n_used, *, tr, tn):
    r_pad, f = hid.shape
    ne, _, d = w_down.shape
    return pl.pallas_call(
        functools.partial(_k_moe_down, tr=tr, tn=tn, n_cols=d, nt=r_pad // tr),
        grid_spec=pltpu.PrefetchScalarGridSpec(
            num_scalar_prefetch=3, grid=(pl.cdiv(d, tn), ne),
            in_specs=[pl.BlockSpec(memory_space=pl.ANY),
                      pl.BlockSpec((None, f, tn), lambda j, e, ts, tp, nu: (e, 0, j))],
            out_specs=pl.BlockSpec(memory_space=pl.ANY),
            scratch_shapes=[pltpu.VMEM((f, tn), BF16), pltpu.VMEM((2, tr, f), hid.dtype),
                            pltpu.VMEM((2, tr, tn), F32),
                            pltpu.SemaphoreType.DMA((2,)), pltpu.SemaphoreType.DMA((2,))]),
        out_shape=jax.ShapeDtypeStruct((r_pad, d), F32),
        compiler_params=_cparams(2), name="moe_down",
    )(tile_start, tiles_per, n_used, hid, w_down)


def _k_moe_combine(pos_ref, ys_hbm, gate_ref, res_ref, o_ref, buf0_ref, buf1_ref, sems, *, tr):
    t = pl.program_id(0)
    n_tiles = pl.num_programs(0)
    n_tok = n_tiles * tr

    def issue_tile(tile):
        slot = tile % 2
        base = tile * tr

        def issue(r, c):
            p0 = pos_ref[base + r]
            p1 = pos_ref[n_tok + base + r]
            pltpu.make_async_copy(ys_hbm.at[pl.ds(p0, 1), :], buf0_ref.at[slot, pl.ds(r, 1), :],
                                  sems.at[slot]).start()
            pltpu.make_async_copy(ys_hbm.at[pl.ds(p1, 1), :], buf1_ref.at[slot, pl.ds(r, 1), :],
                                  sems.at[slot]).start()
            return c

        lax.fori_loop(0, tr, issue, 0, unroll=DMA_ISSUE_UNROLL)

    @pl.when(t == 0)
    def _():
        issue_tile(t)

    @pl.when(t + 1 < n_tiles)
    def _():
        issue_tile(t + 1)

    slot = t % 2
    pltpu.make_async_copy(ys_hbm.at[pl.ds(0, tr), :], buf0_ref.at[slot], sems.at[slot]).wait()
    pltpu.make_async_copy(ys_hbm.at[pl.ds(0, tr), :], buf1_ref.at[slot], sems.at[slot]).wait()
    g = gate_ref[...]
    o_ref[...] = res_ref[...] + (g[:, 0:1] * buf0_ref[slot] + g[:, 1:2] * buf1_ref[slot])


def moe_combine(ys, pos, gates, res, *, tr):
    t, d = res.shape
    g_pad = jnp.pad(gates, ((0, 0), (0, V7X_LANES - gates.shape[1])))
    return pl.pallas_call(
        functools.partial(_k_moe_combine, tr=tr),
        grid_spec=pltpu.PrefetchScalarGridSpec(
            num_scalar_prefetch=1, grid=(t // tr,),
            in_specs=[pl.BlockSpec(memory_space=pl.ANY),
                      pl.BlockSpec((tr, V7X_LANES), lambda i, p: (i, 0)),
                      pl.BlockSpec((tr, d), lambda i, p: (i, 0))],
            out_specs=pl.BlockSpec((tr, d), lambda i, p: (i, 0)),
            scratch_shapes=[pltpu.VMEM((2, tr, d), F32), pltpu.VMEM((2, tr, d), F32),
                            pltpu.SemaphoreType.DMA((2,))]),
        out_shape=jax.ShapeDtypeStruct((t, d), F32),
        compiler_params=_cparams(1), name="moe_combine",
    )(pos, ys, g_pad, res)


def moe_layer(h, g_ffn, w_router, b_router, w_gate, w_up, w_down, *, tm, tr, tn_up, tn_down, tr_gather,
              tr_combine):
    t, d = h.shape
    ne = w_gate.shape[0]
    hn, logits = router(h, g_ffn, w_router, b_router, tm=tm)
    top_v, top_e = lax.top_k(logits, TOP_K)
    gates = jax.nn.softmax(top_v, axis=-1)
    flat_e = top_e.reshape(-1)
    onehot = (flat_e[:, None] == jnp.arange(ne, dtype=flat_e.dtype)[None, :]).astype(jnp.int32)
    rank = jnp.sum((jnp.cumsum(onehot, axis=0) - onehot) * onehot, axis=1)
    sizes = jnp.sum(onehot, axis=0)
    tiles_per = (sizes + tr - 1) // tr
    tile_end = jnp.cumsum(tiles_per)
    tile_start = tile_end - tiles_per
    n_used = tile_end[-1:].astype(jnp.int32)
    nt = (t * TOP_K) // tr + ne
    r_pad = nt * tr
    pos = (tile_start[flat_e] * tr + rank).astype(jnp.int32)
    src_tok = jnp.zeros((r_pad,), jnp.int32).at[pos].set(jnp.arange(t * TOP_K, dtype=jnp.int32) // TOP_K)
    tile_start = tile_start.astype(jnp.int32)
    tiles_per = tiles_per.astype(jnp.int32)
    xs = gather_rows(hn, src_tok, n_used * (tr // tr_gather), tr=tr_gather)
    hid = moe_up(xs, w_gate, w_up, tile_start, tiles_per, n_used, tr=tr, tn=tn_up)
    ys = moe_down(hid, w_down, tile_start, tiles_per, n_used, tr=tr, tn=tn_down)
    return moe_combine(ys, pos.reshape(t, TOP_K).T.reshape(-1), gates, h, tr=tr_combine)


def _layer_weights(i, conv_w, conv_b, w_q, w_k, w_v, b_i, b_f, g_head, skip_a):
    nh = b_i.shape[1]
    pad = jnp.zeros((V7X_LANES - nh,), F32)
    b_if = jnp.concatenate([b_i[i].astype(F32), pad, b_f[i].astype(F32), pad]).reshape(1, 2 * V7X_LANES)
    da = conv_w.shape[-1]
    return dict(conv_w=conv_w[i], conv_b=conv_b[i].reshape(1, da), w_q=w_q[i], w_k=w_k[i], w_v=w_v[i],
                b_if=b_if, g_head=g_head[i].reshape(1, da), skip=skip_a[i].reshape(1, da))


def _tile_plan(n_rows, seq):
    tm = next(c for c in (832, 640, 512, 256, 128, 64, 32, 16) if n_rows % c == 0)
    tc = next(c for c in (320, 256, 128, 64, 32, 16, 8) if n_rows % c == 0)
    tm_big = 2 * tm if n_rows % (2 * tm) == 0 else tm
    return dict(tm=tm, tm_big=tm_big, tn=512, tn_down=256, mlstm_chunk=256, s5_chunk=256,
                moe_tr=256, moe_tn_up=512, moe_tn_down=512, gather_tr=256, combine_tr=tc,
                norm_tm=min(seq, 1024))


def kernel(x_prompt, x_sample, p_prompt, p_sample, state_mlstm_C, state_mlstm_n, state_mlstm_m, state_mlstm_conv,
           state_s5_re, state_s5_im, g_mix, w_in, conv_w, conv_b, w_q, w_k, w_v, b_i, b_f, g_head, skip_a, w_proj_a,
           s5_log_dt, s5_A_re, s5_A_im, s5_B_re, s5_B_im, s5_C_re, s5_C_im, s5_D, w_glu_b, w_out, g_ffn,
           w_ff_gate, w_ff_up, w_ff_down, w_router, b_router, w_moe_gate, w_moe_up, w_moe_down,
           g_ple, w_ple, w_pg, g_final):
    bsz, seq, d = x_prompt.shape
    bs = x_sample.shape[0]
    depth = g_mix.shape[0]
    nh = b_i.shape[1]
    d_a = conv_w.shape[-1]
    d_b = s5_D.shape[-1]
    n_p = bsz * seq
    t = n_p + bs
    tl = _tile_plan(t, seq)
    tm, tmb, tn = tl["tm"], tl["tm_big"], tl["tn"]

    h = jnp.concatenate([x_prompt.reshape(n_p, d), x_sample.reshape(bs, d)], axis=0).astype(F32)
    p_all = jnp.concatenate([p_prompt.reshape(depth, n_p, -1), p_sample.reshape(depth, bs, -1)], axis=1)

    col_ua, col_oa, col_ub = 0, 1, 2 * d_a // d_b
    col_ga = (2 * d_a + d_b) // tn
    col_gb = (2 * d_a + d_b + d) // tn
    col_if = (2 * d_a + d_b + 2 * d) // (2 * V7X_LANES)

    states = [[] for _ in range(11)]
    c_s_all = jnp.zeros(state_mlstm_C.shape, F32)
    for i in range(depth):
        lw = _layer_weights(i, conv_w, conv_b, w_q, w_k, w_v, b_i, b_f, g_head, skip_a)
        sp = _s5_params(dict(log_dt=s5_log_dt[i], a_re=s5_A_re[i], a_im=s5_A_im[i], b_re=s5_B_re[i],
                             b_im=s5_B_im[i], c_re=s5_C_re[i], c_im=s5_C_im[i], d_skip=s5_D[i]))
        proj = in_proj(h, g_mix[i], w_in, i, d_a, d_b, d, nh, tm=tmb, tn=tn)

        a_pre, c_p, n_pp, m_p, conv_p = mlstm_prompt(proj, col_ua, col_oa, col_if, t, bsz, seq, lw,
                                                     L=tl["mlstm_chunk"])
        a_pre, c_s_all, n_s, m_s, conv_s = mlstm_sample(proj, col_ua, col_oa, col_if, n_p, a_pre,
                                                        state_mlstm_C, i, c_s_all, state_mlstm_n[i].astype(F32),
                                                        state_mlstm_m[i].astype(F32),
                                                        state_mlstm_conv[i].astype(F32), lw)
        gy, sre_p, sim_p = s5_prompt(proj, col_ub, t, bsz, seq, sp, tc=tl["s5_chunk"])
        gy, sre_s, sim_s = s5_sample(proj, col_ub, n_p, gy, state_s5_re[i].astype(F32),
                                     state_s5_im[i].astype(F32), sp)
        b_gated = glu_gated_matmul(gy, w_glu_b, i, proj, col_gb, tm=tmb, tn=tn)
        mix = gated_add_matmul(a_pre, w_proj_a, i, proj, col_ga, b_gated, tm=tmb, tn=tn)
        h = matmul(mix, w_out, i, h, tm=tmb, tn=tn, name="out_proj")

        j = i // 2
        if i % 2 == 0:
            hid = norm_swiglu_up(h, g_ffn[i], w_ff_gate, w_ff_up, j, tm=tmb, tn=tn)
            h = matmul(hid, w_ff_down, j, h, tm=tmb, tn=tl["tn_down"], name="ffn_down")
        else:
            h = moe_layer(h, g_ffn[i], w_router[j], b_router[j], w_moe_gate[j], w_moe_up[j], w_moe_down[j],
                          tm=tm, tr=tl["moe_tr"], tn_up=tl["moe_tn_up"], tn_down=tl["moe_tn_down"],
                          tr_gather=tl["gather_tr"], tr_combine=tl["combine_tr"])
        h = ple_matmul(h, g_ple[i], p_all, w_pg, w_ple, i, tm=tm, tn=tn)

        g_s, p_s = S5_STATE, sre_p.shape[-1] // S5_STATE
        new = [c_p, n_pp, m_p, conv_p, sre_p.reshape(bsz, p_s, g_s), sim_p.reshape(bsz, p_s, g_s),
               n_s, m_s, conv_s, sre_s.reshape(bs, p_s, g_s), sim_s.reshape(bs, p_s, g_s)]
        for lst, s in zip(states, new):
            lst.append(s)

    y_prompt, y_sample = final_norm_split(h, g_final, n_p, tm=tl["norm_tm"])
    y_prompt = y_prompt.reshape(bsz, seq, d)
    y_sample = y_sample.reshape(bs, 1, d)
    st = [jnp.stack(lst) for lst in states]
    return (y_prompt, y_sample) + tuple(st[:6]) + (c_s_all,) + tuple(st[6:])
```

```python
import functools
import math

import numpy as np
import jax
import jax.numpy as jnp
from jax import lax
from jax.experimental import pallas as pl
from jax.experimental.pallas import tpu as pltpu

F32 = jnp.float32
BF16 = jnp.bfloat16
EPS = 1e-6

V7X_VMEM_BYTES = 64 * 1024 * 1024
V7X_LANES = 128
V7X_SUBLANES = 8
VMEM_LIMIT = 56 * 1024 * 1024

N_HEADS = 4
CONV_W = 4
S5_GROUP = 16
S5_STATE = 64
N_EXPERTS = 8
TOP_K = 2


def _cparams(n_axes, vmem=VMEM_LIMIT):
    return pltpu.CompilerParams(dimension_semantics=("arbitrary",) * n_axes, vmem_limit_bytes=vmem)


def _sigmoid(x):
    return 1.0 / (1.0 + jnp.exp(-x))


def _silu(x):
    return x * _sigmoid(x)


def _gelu_tanh(x):
    return 0.5 * x * (1.0 + jnp.tanh(math.sqrt(2.0 / math.pi) * (x + 0.044715 * (x * x * x))))


def _log_sigmoid(x):
    return jnp.minimum(x, 0.0) - jnp.log(1.0 + jnp.exp(-jnp.abs(x)))


def _dot(a, b):
    return jnp.dot(a, b, preferred_element_type=F32)


def _dot_nt(a, b):
    return lax.dot_general(a, b, (((1,), (1,)), ((), ())), preferred_element_type=F32)


def _dot_tn(a, b):
    return lax.dot_general(a, b, (((0,), (0,)), ((), ())), preferred_element_type=F32)


def _rmsnorm_rows(x, g):
    ms = jnp.mean(x * x, axis=-1, keepdims=True)
    return x * lax.rsqrt(ms + EPS) * g


def _row_tile_buffering(tm, k, dtype):
    two_copies = 2 * tm * k * jnp.dtype(dtype).itemsize
    return pl.Buffered(1) if two_copies > VMEM_LIMIT // 4 else None


def _row_chunks(tm):
    for rc in (256, 208, 128, 104, 64, 32, 16, 8):
        if tm % rc == 0:
            return rc
    return tm


def _norm_to_scratch(x_ref, g_ref, xn_ref, tm):
    rc = _row_chunks(tm)

    def body(r, c):
        rows = pl.ds(pl.multiple_of(r * rc, rc), rc)
        xn_ref[rows, :] = _rmsnorm_rows(x_ref[rows, :], g_ref[...]).astype(BF16)
        return c

    lax.fori_loop(0, tm // rc, body, 0)


def _wspec(w, layer, tn, col=lambda j: j):
    return pl.BlockSpec((None, w.shape[1], tn), lambda i, j: (layer, 0, col(j)))


def _k_in_proj(x_ref, g_ref, wa_ref, wb_ref, wif_ref, o_ref, xn_ref, *, tm, n_head, n_main, nh):
    j = pl.program_id(1)

    @pl.when(j == 0)
    def _():
        _norm_to_scratch(x_ref, g_ref, xn_ref, tm)

    @pl.when(j < n_head)
    def _():
        o_ref[...] = _dot_nt(xn_ref[...], wa_ref[0].astype(BF16))

    @pl.when(jnp.logical_and(j >= n_head, j < n_main))
    def _():
        o_ref[...] = _dot_nt(xn_ref[...], wb_ref[0].astype(BF16))

    @pl.when(j == n_main)
    def _():
        pre = _dot_nt(xn_ref[...], wif_ref[0].astype(BF16))
        o_ref[...] = jnp.zeros_like(o_ref)
        o_ref[:, 0:nh] = pre[:, 0:nh]
        o_ref[:, V7X_LANES:V7X_LANES + nh] = pre[:, nh:2 * nh]


def in_proj(x, g, w_in, layer, d_a, d_b, d, nh, *, tm, tn):
    t, k = x.shape
    assert (2 * nh) % V7X_SUBLANES == 0 and (2 * d_a) % tn == 0 and (d_b + 2 * d) % tn == 0
    wt = jnp.swapaxes(w_in, 1, 2)
    n_head = 2 * d_a // tn
    n_main = n_head + (d_b + 2 * d) // tn
    if_row = 2 * d_a

    def rows(nrows, start):
        return pl.BlockSpec((pl.Element(1), pl.Element(nrows), pl.Element(k)), lambda i, j: (layer, start(j), 0))

    return pl.pallas_call(
        functools.partial(_k_in_proj, tm=tm, n_head=n_head, n_main=n_main, nh=nh),
        grid=(t // tm, n_main + 1),
        in_specs=[pl.BlockSpec((tm, k), lambda i, j: (i, 0), pipeline_mode=_row_tile_buffering(tm, k, x.dtype)),
                  pl.BlockSpec((1, k), lambda i, j: (0, 0)),
                  rows(tn, lambda j: jnp.minimum(j, n_head - 1) * tn),
                  rows(tn, lambda j: (jnp.clip(j, n_head, n_main - 1) * (tn // V7X_SUBLANES)
                                      + 2 * nh // V7X_SUBLANES) * V7X_SUBLANES),
                  rows(2 * nh, lambda j: if_row)],
        out_specs=pl.BlockSpec((tm, tn), lambda i, j: (i, j)),
        out_shape=jax.ShapeDtypeStruct((t, (n_main + 1) * tn), F32),
        scratch_shapes=[pltpu.VMEM((tm, k), BF16)],
        compiler_params=_cparams(2), name="in_proj",
    )(x, g.reshape(1, k), wt, wt, wt)


def _k_mm_res(x_ref, w_ref, r_ref, o_ref):
    o_ref[...] = r_ref[...] + _dot(x_ref[...], w_ref[...].astype(BF16))


def _k_mm(x_ref, w_ref, o_ref):
    o_ref[...] = _dot(x_ref[...], w_ref[...].astype(BF16)).astype(o_ref.dtype)


def matmul(x, w, layer, res=None, *, tm, tn, out_dtype=F32, name="mm"):
    t, k = x.shape
    n = w.shape[2]
    in_specs = [pl.BlockSpec((tm, k), lambda i, j: (i, 0), pipeline_mode=_row_tile_buffering(tm, k, x.dtype)),
                _wspec(w, layer, tn)]
    args = [x, w]
    body = _k_mm
    if res is not None:
        in_specs.append(pl.BlockSpec((tm, tn), lambda i, j: (i, j)))
        args.append(res)
        body = _k_mm_res
    return pl.pallas_call(
        body, grid=(t // tm, pl.cdiv(n, tn)), in_specs=in_specs,
        out_specs=pl.BlockSpec((tm, tn), lambda i, j: (i, j)),
        out_shape=jax.ShapeDtypeStruct((t, n), out_dtype),
        compiler_params=_cparams(2), name=name,
    )(*args)


def _k_glu_gated(x_ref, wv_ref, wg_ref, gb_ref, o_ref):
    x = x_ref[...]
    val = _dot(x, wv_ref[...].astype(BF16))
    gate = _dot(x, wg_ref[...].astype(BF16))
    o_ref[...] = _sigmoid(gb_ref[...]) * (val * _sigmoid(gate))


def glu_gated_matmul(x, w, layer, proj, gate_col0, *, tm, tn, name="glu_b"):
    t, k = x.shape
    n = w.shape[2] // 2
    nj = n // tn
    return pl.pallas_call(
        _k_glu_gated, grid=(t // tm, nj),
        in_specs=[pl.BlockSpec((tm, k), lambda i, j: (i, 0)),
                  _wspec(w, layer, tn),
                  _wspec(w, layer, tn, lambda j: j + nj),
                  pl.BlockSpec((tm, tn), lambda i, j: (i, gate_col0 + j))],
        out_specs=pl.BlockSpec((tm, tn), lambda i, j: (i, j)),
        out_shape=jax.ShapeDtypeStruct((t, n), F32),
        compiler_params=_cparams(2), name=name,
    )(x, w, w, proj)


def _k_mm_gated_add(x_ref, w_ref, ga_ref, b_ref, o_ref):
    acc = _dot(x_ref[...], w_ref[...].astype(BF16))
    o_ref[...] = (_sigmoid(ga_ref[...]) * acc + b_ref[...]).astype(o_ref.dtype)


def gated_add_matmul(x, w, layer, proj, gate_col0, b, *, tm, tn, name="proj_a"):
    t, k = x.shape
    n = w.shape[2]
    return pl.pallas_call(
        _k_mm_gated_add, grid=(t // tm, n // tn),
        in_specs=[pl.BlockSpec((tm, k), lambda i, j: (i, 0)),
                  _wspec(w, layer, tn),
                  pl.BlockSpec((tm, tn), lambda i, j: (i, gate_col0 + j)),
                  pl.BlockSpec((tm, tn), lambda i, j: (i, j))],
        out_specs=pl.BlockSpec((tm, tn), lambda i, j: (i, j)),
        out_shape=jax.ShapeDtypeStruct((t, n), BF16),
        compiler_params=_cparams(2), name=name,
    )(x, w, proj, b)


def _k_swiglu_up(x_ref, g_ref, wg_ref, wu_ref, o_ref, xn_ref, *, tm):
    @pl.when(pl.program_id(1) == 0)
    def _():
        _norm_to_scratch(x_ref, g_ref, xn_ref, tm)

    xn = xn_ref[...]
    gate = _dot(xn, wg_ref[...].astype(BF16))
    up = _dot(xn, wu_ref[...].astype(BF16))
    o_ref[...] = (_silu(gate) * up).astype(o_ref.dtype)


def norm_swiglu_up(x, g, w_gate, w_up, layer, *, tm, tn, name="ffn_up"):
    t, k = x.shape
    n = w_gate.shape[2]
    return pl.pallas_call(
        functools.partial(_k_swiglu_up, tm=tm),
        grid=(t // tm, pl.cdiv(n, tn)),
        in_specs=[pl.BlockSpec((tm, k), lambda i, j: (i, 0), pipeline_mode=_row_tile_buffering(tm, k, x.dtype)),
                  pl.BlockSpec((1, k), lambda i, j: (0, 0)),
                  _wspec(w_gate, layer, tn), _wspec(w_up, layer, tn)],
        out_specs=pl.BlockSpec((tm, tn), lambda i, j: (i, j)),
        out_shape=jax.ShapeDtypeStruct((t, n), BF16),
        scratch_shapes=[pltpu.VMEM((tm, k), BF16)],
        compiler_params=_cparams(2), name=name,
    )(x, g.reshape(1, k), w_gate, w_up)


def _k_ple(x_ref, g_ref, p_ref, wpg_ref, wple_ref, r_ref, o_ref, xn_ref, pb_ref, *, tm):
    @pl.when(pl.program_id(1) == 0)
    def _():
        _norm_to_scratch(x_ref, g_ref, xn_ref, tm)
        pb_ref[...] = p_ref[...].astype(BF16)

    gate = _dot(xn_ref[...], wpg_ref[...].astype(BF16))
    emb = _dot(pb_ref[...], wple_ref[...].astype(BF16))
    o_ref[...] = r_ref[...] + emb * _sigmoid(gate)


def ple_matmul(x, g, p, w_pg, w_ple, layer, *, tm, tn, name="ple"):
    t, k = x.shape
    kp = p.shape[2]
    return pl.pallas_call(
        functools.partial(_k_ple, tm=tm),
        grid=(t // tm, k // tn),
        in_specs=[pl.BlockSpec((tm, k), lambda i, j: (i, 0)),
                  pl.BlockSpec((1, k), lambda i, j: (0, 0)),
                  pl.BlockSpec((None, tm, kp), lambda i, j: (layer, i, 0)),
                  _wspec(w_pg, layer, tn), _wspec(w_ple, layer, tn),
                  pl.BlockSpec((tm, tn), lambda i, j: (i, j))],
        out_specs=pl.BlockSpec((tm, tn), lambda i, j: (i, j)),
        out_shape=jax.ShapeDtypeStruct((t, k), F32),
        scratch_shapes=[pltpu.VMEM((tm, k), BF16), pltpu.VMEM((tm, kp), BF16)],
        compiler_params=_cparams(2), name=name,
    )(x, g.reshape(1, k), p, w_pg, w_ple, x)


def _k_final_norm(xp_ref, xs_ref, g_ref, op_ref, os_ref, *, n_prompt_tiles):
    i = pl.program_id(0)

    @pl.when(i < n_prompt_tiles)
    def _():
        op_ref[...] = _rmsnorm_rows(xp_ref[...], g_ref[...])

    @pl.when(i == n_prompt_tiles)
    def _():
        os_ref[...] = _rmsnorm_rows(xs_ref[...], g_ref[...])


def final_norm_split(x, g, n_prompt, *, tm):
    t, k = x.shape
    bs = t - n_prompt
    npt = n_prompt // tm
    last = npt - 1
    return pl.pallas_call(
        functools.partial(_k_final_norm, n_prompt_tiles=npt), grid=(npt + 1,),
        in_specs=[pl.BlockSpec((tm, k), lambda i: (jnp.minimum(i, last), 0)),
                  pl.BlockSpec((bs, k), lambda i: (n_prompt // bs, 0)),
                  pl.BlockSpec((1, k), lambda i: (0, 0))],
        out_specs=[pl.BlockSpec((tm, k), lambda i: (jnp.minimum(i, last), 0)),
                   pl.BlockSpec((bs, k), lambda i: (0, 0))],
        out_shape=[jax.ShapeDtypeStruct((n_prompt, k), F32), jax.ShapeDtypeStruct((bs, k), F32)],
        compiler_params=_cparams(1), name="final_norm",
    )(x, x, g.reshape(1, k))


def _split_hi_lo(x):
    hi = x.astype(BF16)
    lo = (x - hi.astype(F32)).astype(BF16)
    return hi, lo


def _k_mlstm_prompt(u_ref, o_ref_in, if_ref, cw_ref, cb_ref, wq_ref, wk_ref, wv_ref, bif_ref, gh_ref, sk_ref,
                    tri_ref, a_init, a_ref, c_out, n_out, m_out, conv_out, upad_ref, *, L, dh):
    del a_init
    c_idx = pl.program_id(1)
    nh = N_HEADS

    @pl.when(c_idx == 0)
    def _():
        c_out[...] = jnp.zeros_like(c_out)
        n_out[...] = jnp.zeros_like(n_out)
        m_out[...] = jnp.zeros_like(m_out)
        upad_ref[pl.ds(0, 8), :] = jnp.zeros((8, nh * dh), F32)

    @pl.when(c_idx > 0)
    def _():
        upad_ref[pl.ds(0, 8), :] = upad_ref[pl.ds(L, 8), :]

    u = u_ref[...]
    upad_ref[pl.ds(8, L), :] = u
    conv = cb_ref[...] + u * cw_ref[CONV_W - 1:CONV_W, :]
    for j in range(CONV_W - 1):
        conv = conv + upad_ref[pl.ds(8 - (CONV_W - 1) + j, L), :] * cw_ref[j:j + 1, :]
    cact = _silu(conv)
    conv_out[0] = upad_ref[pl.ds(L, 8), :]

    pre = if_ref[...]
    li = pre[:, :V7X_LANES] + bif_ref[:, :V7X_LANES]
    lf = _log_sigmoid(pre[:, V7X_LANES:] + bif_ref[:, V7X_LANES:])
    tri = tri_ref[...]
    lf_hi, lf_mid = _split_hi_lo(lf)
    lf_lo = (lf - lf_hi.astype(F32) - lf_mid.astype(F32)).astype(BF16)
    bcum = _dot(tri, lf_hi) + _dot(tri, lf_mid) + _dot(tri, lf_lo)
    li_t = li.T
    b_t = bcum.T
    row_id = lax.broadcasted_iota(jnp.int32, (L, L), 0)
    col_id = lax.broadcasted_iota(jnp.int32, (L, L), 1)
    causal = col_id <= row_id
    lane = lax.broadcasted_iota(jnp.int32, (1, V7X_LANES), 1)
    m_row = m_out[0]
    m_new_row = m_row

    for h in range(nh):
        hs = slice(h * dh, (h + 1) * dh)
        ch = cact[:, hs].astype(BF16)
        uh = u[:, hs].astype(BF16)
        q = _dot(ch, wq_ref[h].astype(BF16))
        k = _dot(ch, wk_ref[h].astype(BF16)) * (dh ** -0.5)
        v = _dot(uh, wv_ref[h].astype(BF16))
        qb, kb, vb = q.astype(BF16), k.astype(BF16), v.astype(BF16)

        b_col = bcum[:, h:h + 1]
        li_col = li[:, h:h + 1]
        r_row = li_t[h:h + 1, :] - b_t[h:h + 1, :]
        m_prev = m_row[:, h:h + 1]
        d = jnp.where(causal, b_col + r_row, -jnp.inf)
        inter = b_col + m_prev
        m_t = jnp.maximum(inter, jnp.max(d, axis=-1, keepdims=True))
        w_inter = jnp.exp(inter - m_t)
        s = _dot_nt(qb, kb) * jnp.exp(d - m_t)
        c_prev = c_out[0, h]
        n_prev = n_out[0, h:h + 1, :]
        num = w_inter * _dot(qb, c_prev.astype(BF16)) + _dot(s.astype(BF16), vb)
        den = w_inter * jnp.sum(q * n_prev, axis=-1, keepdims=True) + jnp.sum(s, axis=-1, keepdims=True)
        hh = num / jnp.maximum(jnp.abs(den), jnp.exp(-m_t))

        b_last = b_col[L - 1:L, :]
        g_col = b_last - b_col + li_col
        m_new = jnp.maximum(b_last + m_prev, jnp.max(g_col, axis=0, keepdims=True))
        decay = jnp.exp(b_last + m_prev - m_new)
        wk_ = jnp.exp(g_col - m_new) * k
        c_out[0, h] = decay * c_prev + _dot_tn(wk_.astype(BF16), vb)
        n_out[0, h:h + 1, :] = decay * n_prev + jnp.sum(wk_, axis=0, keepdims=True)
        m_new_row = jnp.where(lane == h, m_new, m_new_row)

        hn = _rmsnorm_rows(hh, gh_ref[:, hs])
        gated = (hn + sk_ref[:, hs] * cact[:, hs]) * _sigmoid(o_ref_in[:, hs])
        a_ref[:, hs] = gated.astype(a_ref.dtype)

    m_out[0] = m_new_row


def mlstm_prompt(proj, col_u, col_o, col_if, n_rows, bsz, seq, lw, *, L):
    dh = lw["w_q"].shape[-1]
    da = N_HEADS * dh
    nc = seq // L
    tri = jnp.asarray(np.tril(np.ones((L, L), np.float32)), BF16)
    row_blk = lambda b, c: b * nc + c
    full = lambda *shape: pl.BlockSpec(shape, lambda b, c: (0,) * len(shape))
    outs = pl.pallas_call(
        functools.partial(_k_mlstm_prompt, L=L, dh=dh),
        grid=(bsz, nc),
        in_specs=[pl.BlockSpec((L, da), lambda b, c: (row_blk(b, c), col_u)),
                  pl.BlockSpec((L, da), lambda b, c: (row_blk(b, c), col_o)),
                  pl.BlockSpec((L, 2 * V7X_LANES), lambda b, c: (row_blk(b, c), col_if)),
                  full(CONV_W, da), full(1, da), full(N_HEADS, dh, dh), full(N_HEADS, dh, dh),
                  full(N_HEADS, dh, dh), full(1, 2 * V7X_LANES), full(1, da), full(1, da), full(L, L),
                  pl.BlockSpec(memory_space=pl.ANY)],
        out_specs=[pl.BlockSpec((L, da), lambda b, c: (row_blk(b, c), 0)),
                   pl.BlockSpec((1, N_HEADS, dh, dh), lambda b, c: (b, 0, 0, 0)),
                   pl.BlockSpec((1, N_HEADS, dh), lambda b, c: (b, 0, 0)),
                   pl.BlockSpec((1, 1, V7X_LANES), lambda b, c: (b, 0, 0)),
                   pl.BlockSpec((1, 8, da), lambda b, c: (b, 0, 0))],
        out_shape=[jax.ShapeDtypeStruct((n_rows, da), BF16),
                   jax.ShapeDtypeStruct((bsz, N_HEADS, dh, dh), F32),
                   jax.ShapeDtypeStruct((bsz, N_HEADS, dh), F32),
                   jax.ShapeDtypeStruct((bsz, 1, V7X_LANES), F32),
                   jax.ShapeDtypeStruct((bsz, 8, da), F32)],
        scratch_shapes=[pltpu.VMEM((L + 8, da), F32)],
        input_output_aliases={12: 0},
        compiler_params=_cparams(2), name="mlstm_prompt",
    )(proj, proj, proj, lw["conv_w"], lw["conv_b"], lw["w_q"], lw["w_k"], lw["w_v"], lw["b_if"],
      lw["g_head"], lw["skip"], tri, jnp.zeros((n_rows, da), BF16))
    a_pre, c_p, n_p, m_p, conv_p = outs
    return a_pre, c_p, n_p, m_p[:, 0, :N_HEADS], conv_p[:, 8 - (CONV_W - 1):, :]


def _k_mlstm_sample_pre(u_ref, if_ref, conv_ref, m_ref, cw_ref, cb_ref, wq_ref, wk_ref, wv_ref, bif_ref,
                        q_out, k_out, v_out, c_out, gates_out, conv_out, *, dh):
    nh = N_HEADS
    u = u_ref[...]
    conv = cb_ref[...] + u * cw_ref[CONV_W - 1:CONV_W, :]
    for j in range(CONV_W - 1):
        conv = conv + conv_ref[j] * cw_ref[j:j + 1, :]
        if j > 0:
            conv_out[j - 1] = conv_ref[j]
    conv_out[CONV_W - 2] = u
    cact = _silu(conv)
    c_out[...] = cact
    for h in range(nh):
        hs = slice(h * dh, (h + 1) * dh)
        ch = cact[:, hs].astype(BF16)
        q_out[:, hs] = _dot(ch, wq_ref[h].astype(BF16))
        k_out[:, hs] = _dot(ch, wk_ref[h].astype(BF16)) * (dh ** -0.5)
        v_out[:, hs] = _dot(u[:, hs].astype(BF16), wv_ref[h].astype(BF16))
    pre = if_ref[...]
    li = pre[:, :V7X_LANES] + bif_ref[:, :V7X_LANES]
    lf = _log_sigmoid(pre[:, V7X_LANES:] + bif_ref[:, V7X_LANES:])
    m_prev = m_ref[...]
    inter = lf + m_prev
    m_t = jnp.maximum(inter, li)
    gates_out[0] = jnp.exp(inter - m_t)
    gates_out[1] = jnp.exp(li - m_t)
    gates_out[2] = jnp.exp(-m_t)
    gates_out[3] = m_t


def _k_mlstm_sample_step(q_ref, k_ref, v_ref, gates_ref, c_ref, n_ref, cact_ref, o_ref_in, gh_ref, sk_ref,
                         a_any, c_any, c_out, n_out, a_out, hh_ref, *, bt, dh):
    del a_any, c_any
    i = pl.program_id(0)
    nh = N_HEADS
    q = q_ref[...]
    k = k_ref[...]
    v = v_ref[...]
    w_inter = gates_ref[0]
    w_new = gates_ref[1]
    e_neg_m = gates_ref[2]
    n_prev = n_ref[...]
    rows = pl.ds(pl.multiple_of(i * bt, bt), bt)
    for h in range(nh):
        hs = slice(h * dh, (h + 1) * dh)
        qh, kh, vh, nh_prev = q[:, hs], k[:, hs], v[:, hs], n_prev[:, hs]
        q_t = qh.T
        k_t = kh.T
        wi = w_inter[:, h:h + 1]
        wn = w_new[:, h:h + 1]
        s = jnp.sum(qh * kh, axis=-1, keepdims=True) * wn
        den = wi * jnp.sum(qh * nh_prev, axis=-1, keepdims=True) + s
        wv = wn * vh
        qc_rows = []
        for bl in range(bt):
            c_prev = c_ref[bl, h]
            qc_rows.append(jnp.sum(q_t[:, bl:bl + 1] * c_prev, axis=0, keepdims=True))
            c_out[bl, h] = wi[bl:bl + 1, :] * c_prev + k_t[:, bl:bl + 1] * wv[bl:bl + 1, :]
        qc = jnp.concatenate(qc_rows, axis=0)
        num = wi * qc + s * vh
        hh = num / jnp.maximum(jnp.abs(den), e_neg_m[:, h:h + 1])
        hh_ref[rows, hs] = hh
        n_out[:, hs] = wi * nh_prev + wn * kh

    @pl.when(i == pl.num_programs(0) - 1)
    def _():
        for h in range(nh):
            hs = slice(h * dh, (h + 1) * dh)
            hn = _rmsnorm_rows(hh_ref[:, hs], gh_ref[:, hs])
            a_out[:, hs] = ((hn + sk_ref[:, hs] * cact_ref[:, hs]) * _sigmoid(o_ref_in[:, hs])).astype(a_out.dtype)


def mlstm_sample(proj, col_u, col_o, col_if, row0, a_pre_all, c_all, layer, c_new_all, n0, m0, conv0, lw, *,
                 bt=8):
    _, bs, nh, dh, _ = c_all.shape
    da = nh * dh
    rb = row0 // bs
    m_pad = jnp.pad(m0, ((0, 0), (0, V7X_LANES - nh)))
    conv_t = jnp.transpose(conv0, (1, 0, 2))
    full = lambda *shape: pl.BlockSpec(shape, lambda i: (0,) * len(shape))
    q, k, v, cact, gates, conv_new = pl.pallas_call(
        functools.partial(_k_mlstm_sample_pre, dh=dh),
        grid=(1,),
        in_specs=[pl.BlockSpec((bs, da), lambda i: (rb, col_u)),
                  pl.BlockSpec((bs, 2 * V7X_LANES), lambda i: (rb, col_if)),
                  full(CONV_W - 1, bs, da), full(bs, V7X_LANES), full(CONV_W, da), full(1, da),
                  full(nh, dh, dh), full(nh, dh, dh), full(nh, dh, dh), full(1, 2 * V7X_LANES)],
        out_specs=[full(bs, da), full(bs, da), full(bs, da), full(bs, da), full(4, bs, V7X_LANES),
                   full(CONV_W - 1, bs, da)],
        out_shape=[jax.ShapeDtypeStruct((bs, da), F32)] * 4
                  + [jax.ShapeDtypeStruct((4, bs, V7X_LANES), F32),
                     jax.ShapeDtypeStruct((CONV_W - 1, bs, da), F32)],
        compiler_params=_cparams(1), name="mlstm_sample_pre",
    )(proj, proj, conv_t, m_pad, lw["conv_w"], lw["conv_b"], lw["w_q"], lw["w_k"], lw["w_v"], lw["b_if"])

    blk = lambda *shape: pl.BlockSpec(shape, lambda i: (i,) + (0,) * (len(shape) - 1))
    cst = lambda *shape: pl.BlockSpec(shape, lambda i: (0,) * len(shape))
    c_blk = pl.BlockSpec((None, bt, nh, dh, dh), lambda i: (layer, i, 0, 0, 0))
    c_new_all, n_new, a_pre_all = pl.pallas_call(
        functools.partial(_k_mlstm_sample_step, bt=bt, dh=dh),
        grid=(bs // bt,),
        in_specs=[blk(bt, da), blk(bt, da), blk(bt, da),
                  pl.BlockSpec((4, bt, V7X_LANES), lambda i: (0, i, 0)),
                  c_blk, blk(bt, da), cst(bs, da),
                  pl.BlockSpec((bs, da), lambda i: (rb, col_o)), cst(1, da), cst(1, da),
                  pl.BlockSpec(memory_space=pl.ANY), pl.BlockSpec(memory_space=pl.ANY)],
        out_specs=[c_blk, blk(bt, da), pl.BlockSpec((bs, da), lambda i: (rb, 0))],
        out_shape=[jax.ShapeDtypeStruct(c_new_all.shape, F32), jax.ShapeDtypeStruct((bs, da), F32),
                   jax.ShapeDtypeStruct(a_pre_all.shape, a_pre_all.dtype)],
        scratch_shapes=[pltpu.VMEM((bs, da), F32)],
        input_output_aliases={10: 2, 11: 0},
        compiler_params=_cparams(1), name="mlstm_sample_step",
    )(q, k, v, gates, c_all, n0.reshape(bs, da), cact, proj, lw["g_head"], lw["skip"], a_pre_all, c_new_all)
    m_new = gates[3][:, :nh]
    return a_pre_all, c_new_all, n_new.reshape(bs, nh, dh), m_new, jnp.transpose(conv_new, (1, 0, 2))


S5_SLAB_GROUPS = V7X_LANES // S5_GROUP
S5_SLAB_STATES = S5_SLAB_GROUPS * S5_STATE


def _s5_params(lp):
    g, p = lp["a_re"].shape
    dt = jnp.exp(lp["log_dt"].astype(F32))[:, None]
    a_re = lp["a_re"].astype(F32)
    a_im = lp["a_im"].astype(F32)
    lam_re = a_re * dt
    lam_im = a_im * dt
    mag = jnp.exp(lam_re)
    ab_re = mag * jnp.cos(lam_im)
    ab_im = mag * jnp.sin(lam_im)
    den = a_re * a_re + a_im * a_im
    nr = ab_re - 1.0
    ni = ab_im
    k_re = (nr * a_re + ni * a_im) / den
    k_im = (ni * a_re - nr * a_im) / den
    b_re = lp["b_re"].astype(F32)
    b_im = lp["b_im"].astype(F32)
    bb_re = k_re[..., None] * b_re - k_im[..., None] * b_im
    bb_im = k_re[..., None] * b_im + k_im[..., None] * b_re
    ns = g // S5_SLAB_GROUPS
    eye = jnp.eye(S5_SLAB_GROUPS, dtype=F32)

    def in_blockdiag(bb):
        bs = bb.reshape(ns, S5_SLAB_GROUPS, p, S5_GROUP)
        w = jnp.einsum("ab,sapc->sacbp", eye, bs)
        return w.reshape(ns, V7X_LANES, S5_SLAB_STATES).astype(BF16)

    def out_blockdiag(cc):
        cs = cc.astype(F32).reshape(ns, S5_SLAB_GROUPS, S5_GROUP, p)
        w = jnp.einsum("ab,sacp->sapbc", eye, cs)
        return w.reshape(ns, S5_SLAB_STATES, V7X_LANES)

    wc = jnp.concatenate([out_blockdiag(lp["c_re"]), -out_blockdiag(lp["c_im"])], axis=1).astype(BF16)
    return dict(lam_re=lam_re.reshape(1, g * p), lam_im=lam_im.reshape(1, g * p),
                ab_re=ab_re.reshape(1, g * p), ab_im=ab_im.reshape(1, g * p),
                wb_re=in_blockdiag(bb_re), wb_im=in_blockdiag(bb_im), wc=wc,
                d_skip=lp["d_skip"].astype(F32).reshape(1, g * S5_GROUP))


def _s5_powers(sp, ks):
    kk = jnp.asarray(ks, F32)[:, None]
    mag = jnp.exp(kk * sp["lam_re"])
    return mag * jnp.cos(kk * sp["lam_im"]), mag * jnp.sin(kk * sp["lam_im"])


def _s5_input_proj(u_bf, wbr_ref, wbi_ref, bur_ref, bui_ref):
    ns = wbr_ref.shape[0]
    for s in range(ns):
        us = u_bf[:, s * V7X_LANES:(s + 1) * V7X_LANES]
        cols = slice(s * S5_SLAB_STATES, (s + 1) * S5_SLAB_STATES)
        bur_ref[:, cols] = _dot(us, wbr_ref[s])
        bui_ref[:, cols] = _dot(us, wbi_ref[s])


def _s5_output_proj(xr_ref, xi_ref, wc_ref, skip):
    ns = wc_ref.shape[0]
    outs = []
    for s in range(ns):
        cols = slice(s * S5_SLAB_STATES, (s + 1) * S5_SLAB_STATES)
        xcat = jnp.concatenate([xr_ref[:, cols].astype(BF16), xi_ref[:, cols].astype(BF16)], axis=1)
        y = _dot(xcat, wc_ref[s]) + skip[:, s * V7X_LANES:(s + 1) * V7X_LANES]
        outs.append(_gelu_tanh(y).astype(BF16))
    return jnp.concatenate(outs, axis=1)


def _k_s5_prompt(u_ref, wbr_ref, wbi_ref, wc_ref, d_ref, abr_ref, abi_ref, tpr_ref, tpi_ref, ajr_ref, aji_ref,
                 perm_ref, permt_ref, gy_init, gy_ref, sre_out, sim_out,
                 bur0, bur1, bui0, bui1, xb0, xb1, *, tc):
    del gy_init
    c_idx = pl.program_id(1)
    nsteps = tc // V7X_SUBLANES
    ns = wbr_ref.shape[0]
    lw = S5_SLAB_STATES
    pair = 2 * V7X_SUBLANES
    bur, bui, xb = (bur0, bur1), (bui0, bui1), (xb0, xb1)

    @pl.when(c_idx == 0)
    def _():
        sre_out[...] = jnp.zeros_like(sre_out)
        sim_out[...] = jnp.zeros_like(sim_out)

    u = u_ref[...]
    u_hi, u_lo = _split_hi_lo(u)
    perm = perm_ref[...]
    up_hi = _dot(perm, u_hi)
    skip = d_ref[...] * (up_hi + _dot(perm, u_lo))
    up_bf = up_hi.astype(BF16)

    sub = lax.broadcasted_iota(jnp.int32, (V7X_SUBLANES, lw), 0)

    def input_proj(s):
        us = up_bf[:, s * V7X_LANES:(s + 1) * V7X_LANES]
        bur[s % 2][...] = _dot(us, wbr_ref[s])
        bui[s % 2][...] = _dot(us, wbi_ref[s])

    def scan(s):
        br, bi, xo = bur[s % 2], bui[s % 2], xb[s % 2]
        lanes = slice(s * lw, (s + 1) * lw)
        ar = jnp.broadcast_to(abr_ref[:, lanes], (V7X_SUBLANES, lw))
        ai = jnp.broadcast_to(abi_ref[:, lanes], (V7X_SUBLANES, lw))
        er = jnp.zeros((V7X_SUBLANES, lw), F32)
        ei = er
        for i in range(nsteps):
            rows = slice(i * V7X_SUBLANES, (i + 1) * V7X_SUBLANES)
            er, ei = (ar * er - ai * ei + br[rows, :], ar * ei + ai * er + bi[rows, :])
            br[rows, :] = er
            bi[rows, :] = ei
        for d, row in ((1, 0), (2, 1), (4, 3)):
            pr = ajr_ref[row:row + 1, lanes]
            pi = aji_ref[row:row + 1, lanes]
            sr = pltpu.roll(er, d, 0)
            si = pltpu.roll(ei, d, 0)
            keep = sub >= d
            er, ei = (er + jnp.where(keep, pr * sr - pi * si, 0.0),
                      ei + jnp.where(keep, pr * si + pi * sr, 0.0))
        c0r = jnp.broadcast_to(sre_out[0, :, lanes], (V7X_SUBLANES, lw))
        c0i = jnp.broadcast_to(sim_out[0, :, lanes], (V7X_SUBLANES, lw))
        ajr = ajr_ref[:, lanes]
        aji = aji_ref[:, lanes]
        fr = ajr * c0r - aji * c0i + er
        fi = ajr * c0i + aji * c0r + ei
        cin_r = jnp.where(sub >= 1, pltpu.roll(fr, 1, 0), c0r)
        cin_i = jnp.where(sub >= 1, pltpu.roll(fi, 1, 0), c0i)
        sre_out[0, :, lanes] = fr[V7X_SUBLANES - 1:V7X_SUBLANES, :]
        sim_out[0, :, lanes] = fi[V7X_SUBLANES - 1:V7X_SUBLANES, :]

        cin2_r = jnp.concatenate([cin_r, cin_r], axis=0)
        cin2_i = jnp.concatenate([cin_i, cin_i], axis=0)
        for k in range(tc // pair):
            rows = slice(k * pair, (k + 1) * pair)
            pr = tpr_ref[rows, lanes]
            pi = tpi_ref[rows, lanes]
            xo[rows, :lw] = (br[rows, :] + (pr * cin2_r - pi * cin2_i)).astype(BF16)
            xo[rows, lw:] = (bi[rows, :] + (pr * cin2_i + pi * cin2_r)).astype(BF16)

    def output_proj(s):
        y = _dot(xb[s % 2][...], wc_ref[s]) + skip[:, s * V7X_LANES:(s + 1) * V7X_LANES]
        return _gelu_tanh(y).astype(BF16)

    outs = []
    input_proj(0)
    for s in range(ns):
        if s + 1 < ns:
            input_proj(s + 1)
        scan(s)
        outs.append(output_proj(s))
    g_perm = jnp.concatenate(outs, axis=1)
    gy_ref[...] = _dot(permt_ref[...], g_perm).astype(gy_ref.dtype)


def s5_prompt(proj, col_u, n_rows, bsz, seq, sp, *, tc=256):
    db = sp["d_skip"].shape[1]
    nch = sp["ab_re"].shape[1]
    nc = seq // tc
    nsteps = tc // V7X_SUBLANES
    tpr, tpi = _s5_powers(sp, np.repeat(np.arange(1, nsteps + 1), V7X_SUBLANES))
    ajr, aji = _s5_powers(sp, nsteps * np.arange(1, V7X_SUBLANES + 1))
    perm = np.zeros((tc, tc), np.float32)
    r = np.arange(tc)
    perm[r, (r % V7X_SUBLANES) * nsteps + r // V7X_SUBLANES] = 1.0
    full = lambda a: pl.BlockSpec(a.shape, lambda b, c: (0,) * a.ndim)
    consts = [sp["wb_re"], sp["wb_im"], sp["wc"], sp["d_skip"], sp["ab_re"], sp["ab_im"], tpr, tpi, ajr, aji,
              jnp.asarray(perm, BF16), jnp.asarray(perm.T, BF16)]
    gy, s_re, s_im = pl.pallas_call(
        functools.partial(_k_s5_prompt, tc=tc),
        grid=(bsz, nc),
        in_specs=[pl.BlockSpec((tc, db), lambda b, c: (b * nc + c, col_u))] + [full(a) for a in consts]
                 + [pl.BlockSpec(memory_space=pl.ANY)],
        out_specs=[pl.BlockSpec((tc, db), lambda b, c: (b * nc + c, 0)),
                   pl.BlockSpec((1, 1, nch), lambda b, c: (b, 0, 0)),
                   pl.BlockSpec((1, 1, nch), lambda b, c: (b, 0, 0))],
        out_shape=[jax.ShapeDtypeStruct((n_rows, db), BF16),
                   jax.ShapeDtypeStruct((bsz, 1, nch), F32), jax.ShapeDtypeStruct((bsz, 1, nch), F32)],
        scratch_shapes=[pltpu.VMEM((tc, S5_SLAB_STATES), F32)] * 4 + [pltpu.VMEM((tc, 2 * S5_SLAB_STATES), BF16)] * 2,
        input_output_aliases={1 + len(consts): 0},
        compiler_params=_cparams(2), name="s5_prompt",
    )(proj, *consts, jnp.zeros((n_rows, db), BF16))
    return gy, s_re, s_im


def _k_s5_sample(u_ref, x0r_ref, x0i_ref, wbr_ref, wbi_ref, wc_ref, d_ref, abr_ref, abi_ref, gy_any,
                 gy_ref, xr_out, xi_out):
    del gy_any
    u = u_ref[...]
    _s5_input_proj(u.astype(BF16), wbr_ref, wbi_ref, xr_out, xi_out)
    ar = abr_ref[...]
    ai = abi_ref[...]
    x0r = x0r_ref[...]
    x0i = x0i_ref[...]
    xr_out[...] = xr_out[...] + (ar * x0r - ai * x0i)
    xi_out[...] = xi_out[...] + (ar * x0i + ai * x0r)
    gy_ref[...] = _s5_output_proj(xr_out, xi_out, wc_ref, d_ref[...] * u).astype(gy_ref.dtype)


def s5_sample(proj, col_u, row0, gy_all, x0_re, x0_im, sp):
    bs = x0_re.shape[0]
    db = sp["d_skip"].shape[1]
    nch = sp["ab_re"].shape[1]
    rb = row0 // bs
    full = lambda a: pl.BlockSpec(a.shape, lambda i: (0,) * a.ndim)
    consts = [sp["wb_re"], sp["wb_im"], sp["wc"], sp["d_skip"], sp["ab_re"], sp["ab_im"]]
    x0r = x0_re.reshape(bs, nch)
    x0i = x0_im.reshape(bs, nch)
    gy_all, xr, xi = pl.pallas_call(
        _k_s5_sample,
        grid=(1,),
        in_specs=[pl.BlockSpec((bs, db), lambda i: (rb, col_u)), full(x0r), full(x0i)]
                 + [full(a) for a in consts] + [pl.BlockSpec(memory_space=pl.ANY)],
        out_specs=[pl.BlockSpec((bs, db), lambda i: (rb, 0)),
                   pl.BlockSpec((bs, nch), lambda i: (0, 0)), pl.BlockSpec((bs, nch), lambda i: (0, 0))],
        out_shape=[jax.ShapeDtypeStruct(gy_all.shape, gy_all.dtype),
                   jax.ShapeDtypeStruct((bs, nch), F32), jax.ShapeDtypeStruct((bs, nch), F32)],
        input_output_aliases={9: 0},
        compiler_params=_cparams(1), name="s5_sample",
    )(proj, x0r, x0i, *consts, gy_all)
    return gy_all, xr, xi


def _k_router(x_ref, g_ref, w_ref, b_ref, hn_ref, lg_ref):
    hn = _rmsnorm_rows(x_ref[...], g_ref[...])
    hn_ref[...] = hn
    x_hi, x_lo = _split_hi_lo(hn)
    w_hi, w_lo = _split_hi_lo(w_ref[...])
    lg_ref[...] = _dot(x_hi, w_hi) + (_dot(x_lo, w_hi) + _dot(x_hi, w_lo)) + b_ref[...]


def router(x, g, w_router, b_router, *, tm):
    t, k = x.shape
    ne = w_router.shape[1]
    w_pad = jnp.pad(w_router, ((0, 0), (0, V7X_LANES - ne)))
    b_pad = jnp.pad(b_router.astype(F32), (0, V7X_LANES - ne)).reshape(1, V7X_LANES)
    hn, lg = pl.pallas_call(
        _k_router, grid=(t // tm,),
        in_specs=[pl.BlockSpec((tm, k), lambda i: (i, 0)), pl.BlockSpec((1, k), lambda i: (0, 0)),
                  pl.BlockSpec((k, V7X_LANES), lambda i: (0, 0)), pl.BlockSpec((1, V7X_LANES), lambda i: (0, 0))],
        out_specs=[pl.BlockSpec((tm, k), lambda i: (i, 0)), pl.BlockSpec((tm, V7X_LANES), lambda i: (i, 0))],
        out_shape=[jax.ShapeDtypeStruct((t, k), F32), jax.ShapeDtypeStruct((t, V7X_LANES), F32)],
        compiler_params=_cparams(1), name="router",
    )(x, g.reshape(1, k), w_pad, b_pad)
    return hn, lg[:, :ne]


DMA_ISSUE_UNROLL = 8


def _k_gather_rows(nused_ref, tok_ref, src_hbm, o_ref, buf_ref, sems, *, tr):
    t = pl.program_id(0)
    n_used = nused_ref[0]

    def issue_tile(tile):
        slot = tile % 2
        base = tile * tr

        def issue(r, c):
            tok = tok_ref[base + r]
            pltpu.make_async_copy(src_hbm.at[pl.ds(tok, 1), :], buf_ref.at[slot, pl.ds(r, 1), :],
                                  sems.at[slot]).start()
            return c

        lax.fori_loop(0, tr, issue, 0, unroll=DMA_ISSUE_UNROLL)

    @pl.when(t == 0)
    def _():
        issue_tile(t)

    @pl.when(t + 1 < n_used)
    def _():
        issue_tile(t + 1)

    @pl.when(t < n_used)
    def _():
        slot = t % 2
        pltpu.make_async_copy(src_hbm.at[pl.ds(0, tr), :], buf_ref.at[slot], sems.at[slot]).wait()
        o_ref[...] = buf_ref[slot].astype(o_ref.dtype)

    @pl.when(t >= n_used)
    def _():
        o_ref[...] = jnp.zeros_like(o_ref)


def gather_rows(src, tok, n_used, *, tr):
    t, k = src.shape
    r_pad = tok.shape[0]
    nt = r_pad // tr
    return pl.pallas_call(
        functools.partial(_k_gather_rows, tr=tr),
        grid_spec=pltpu.PrefetchScalarGridSpec(
            num_scalar_prefetch=2, grid=(nt,),
            in_specs=[pl.BlockSpec(memory_space=pl.ANY)],
            out_specs=pl.BlockSpec((tr, k), lambda i, nu, tk: (i, 0)),
            scratch_shapes=[pltpu.VMEM((2, tr, k), src.dtype), pltpu.SemaphoreType.DMA((2,))]),
        out_shape=jax.ShapeDtypeStruct((r_pad, k), BF16),
        compiler_params=_cparams(1), name="moe_gather",
    )(n_used, tok, src)


TILE_PIPELINE_SLOTS = 3


def _expert_tile_pipeline(n, t0, src_hbm, src_buf, src_sem, dst_hbm, dst_buf, dst_sem, col0, width, tr,
                          load_weights, compute):
    ns = TILE_PIPELINE_SLOTS

    def rows(k):
        return pl.ds(pl.multiple_of((t0 + k) * tr, tr), tr)

    def fetch(k):
        slot = lax.rem(k, ns)
        return pltpu.make_async_copy(src_hbm.at[rows(k), :], src_buf.at[slot], src_sem.at[slot])

    def writeback(k):
        slot = lax.rem(k, ns)
        return pltpu.make_async_copy(dst_buf.at[slot, :, pl.ds(0, width)],
                                     dst_hbm.at[rows(k), pl.ds(col0, width)], dst_sem.at[slot])

    for a in range(ns - 1):
        pl.when(a < n)(lambda a=a: fetch(a).start())
    load_weights()

    def body(k, c):
        @pl.when(k + ns - 1 < n)
        def _():
            fetch(k + ns - 1).start()

        fetch(k).wait()

        @pl.when(k >= ns)
        def _():
            writeback(k - ns).wait()

        slot = lax.rem(k, ns)
        dst_buf[slot] = compute(src_buf[slot])
        writeback(k).start()
        return c

    lax.fori_loop(0, n, body, 0)
    for a in range(ns, 0, -1):
        pl.when(n >= a)(lambda a=a: writeback(n - a).wait())


def _zero_tiles(first, last, dst_hbm, dst_buf, dst_sem, col0, width, tr):
    dst_buf[0] = jnp.zeros(dst_buf.shape[1:], dst_buf.dtype)

    def body(t, c):
        cp = pltpu.make_async_copy(dst_buf.at[0, :, pl.ds(0, width)],
                                   dst_hbm.at[pl.ds(pl.multiple_of(t * tr, tr), tr), pl.ds(col0, width)],
                                   dst_sem.at[0])
        cp.start()
        cp.wait()
        return c

    lax.fori_loop(first, last, body, 0)


def _expert_grid_body(tstart_ref, ntile_ref, nused_ref, src_hbm, dst_hbm, src_buf, dst_buf, src_sem, dst_sem,
                      load_weights, compute, *, tr, tn, n_cols, nt):
    j = pl.program_id(0)
    e = pl.program_id(1)
    n = ntile_ref[e]
    nj = pl.cdiv(n_cols, tn)
    col0 = pl.multiple_of(j * tn, tn)

    def run(width):
        @pl.when(n > 0)
        def _():
            _expert_tile_pipeline(n, tstart_ref[e], src_hbm, src_buf, src_sem, dst_hbm, dst_buf, dst_sem,
                                  col0, width, tr, load_weights, compute)

        @pl.when(e == pl.num_programs(1) - 1)
        def _():
            _zero_tiles(nused_ref[0], nt, dst_hbm, dst_buf, dst_sem, col0, width, tr)

    last_width = n_cols - (nj - 1) * tn
    if last_width == tn:
        run(tn)
    else:
        pl.when(j < nj - 1)(lambda: run(tn))
        pl.when(j == nj - 1)(lambda: run(last_width))


def _k_moe_up(tstart_ref, ntile_ref, nused_ref, xs_hbm, wg_ref, wu_ref, hid_hbm, wgb_ref, wub_ref,
              x_buf, o_buf, x_sem, o_sem, **static):
    def load_weights():
        wgb_ref[...] = wg_ref[...].astype(BF16)
        wub_ref[...] = wu_ref[...].astype(BF16)

    def compute(x):
        return (_silu(_dot(x, wgb_ref[...])) * _dot(x, wub_ref[...])).astype(o_buf.dtype)

    _expert_grid_body(tstart_ref, ntile_ref, nused_ref, xs_hbm, hid_hbm, x_buf, o_buf, x_sem, o_sem,
                      load_weights, compute, **static)


def moe_up(xs, w_gate, w_up, tile_start, tiles_per, n_used, *, tr, tn):
    r_pad, k = xs.shape
    ne, _, f = w_gate.shape
    wmap = lambda j, e, ts, tp, nu: (e, 0, j)
    return pl.pallas_call(
        functools.partial(_k_moe_up, tr=tr, tn=tn, n_cols=f, nt=r_pad // tr),
        grid_spec=pltpu.PrefetchScalarGridSpec(
            num_scalar_prefetch=3, grid=(pl.cdiv(f, tn), ne),
            in_specs=[pl.BlockSpec(memory_space=pl.ANY),
                      pl.BlockSpec((None, k, tn), wmap), pl.BlockSpec((None, k, tn), wmap)],
            out_specs=pl.BlockSpec(memory_space=pl.ANY),
            scratch_shapes=[pltpu.VMEM((k, tn), BF16), pltpu.VMEM((k, tn), BF16),
                            pltpu.VMEM((TILE_PIPELINE_SLOTS, tr, k), xs.dtype),
                            pltpu.VMEM((TILE_PIPELINE_SLOTS, tr, tn), BF16),
                            pltpu.SemaphoreType.DMA((TILE_PIPELINE_SLOTS,)),
                            pltpu.SemaphoreType.DMA((TILE_PIPELINE_SLOTS,))]),
        out_shape=jax.ShapeDtypeStruct((r_pad, f), BF16),
        compiler_params=_cparams(2), name="moe_up",
    )(tile_start, tiles_per, n_used, xs, w_gate, w_up)


def _k_moe_down(tstart_ref, ntile_ref, nused_ref, hid_hbm, w_ref, ys_hbm, wb_ref, h_buf, o_buf, h_sem, o_sem,
                **static):
    def load_weights():
        wb_ref[...] = w_ref[...].astype(BF16)

    def compute(x):
        return _dot(x, wb_ref[...])

    _expert_grid_body(tstart_ref, ntile_ref, nused_ref, hid_hbm, ys_hbm, h_buf, o_buf, h_sem, o_sem,
                      load_weights, compute, **static)


def moe_down(hid, w_down, tile_start, tiles_per, n_used, *, tr, tn):
    r_pad, f = hid.shape
    ne, _, d = w_down.shape
    return pl.pallas_call(
        functools.partial(_k_moe_down, tr=tr, tn=tn, n_cols=d, nt=r_pad // tr),
        grid_spec=pltpu.PrefetchScalarGridSpec(
            num_scalar_prefetch=3, grid=(pl.cdiv(d, tn), ne),
            in_specs=[pl.BlockSpec(memory_space=pl.ANY),
                      pl.BlockSpec((None, f, tn), lambda j, e, ts, tp, nu: (e, 0, j))],
            out_specs=pl.BlockSpec(memory_space=pl.ANY),
            scratch_shapes=[pltpu.VMEM((f, tn), BF16), pltpu.VMEM((TILE_PIPELINE_SLOTS, tr, f), hid.dtype),
                            pltpu.VMEM((TILE_PIPELINE_SLOTS, tr, tn), F32),
                            pltpu.SemaphoreType.DMA((TILE_PIPELINE_SLOTS,)),
                            pltpu.SemaphoreType.DMA((TILE_PIPELINE_SLOTS,))]),
        out_shape=jax.ShapeDtypeStruct((r_pad, d), F32),
        compiler_params=_cparams(2), name="moe_down",
    )(tile_start, tiles_per, n_used, hid, w_down)


def _k_moe_combine(pos_ref, ys_hbm, gate_ref, res_ref, o_ref, buf0_ref, buf1_ref, sems, *, tr):
    t = pl.program_id(0)
    n_tiles = pl.num_programs(0)
    n_tok = n_tiles * tr

    def issue_tile(tile):
        slot = tile % 2
        base = tile * tr

        def issue(r, c):
            p0 = pos_ref[base + r]
            p1 = pos_ref[n_tok + base + r]
            pltpu.make_async_copy(ys_hbm.at[pl.ds(p0, 1), :], buf0_ref.at[slot, pl.ds(r, 1), :],
                                  sems.at[slot]).start()
            pltpu.make_async_copy(ys_hbm.at[pl.ds(p1, 1), :], buf1_ref.at[slot, pl.ds(r, 1), :],
                                  sems.at[slot]).start()
            return c

        lax.fori_loop(0, tr, issue, 0, unroll=DMA_ISSUE_UNROLL)

    @pl.when(t == 0)
    def _():
        issue_tile(t)

    @pl.when(t + 1 < n_tiles)
    def _():
        issue_tile(t + 1)

    slot = t % 2
    pltpu.make_async_copy(ys_hbm.at[pl.ds(0, tr), :], buf0_ref.at[slot], sems.at[slot]).wait()
    pltpu.make_async_copy(ys_hbm.at[pl.ds(0, tr), :], buf1_ref.at[slot], sems.at[slot]).wait()
    g = gate_ref[...]
    o_ref[...] = res_ref[...] + (g[:, 0:1] * buf0_ref[slot] + g[:, 1:2] * buf1_ref[slot])


def moe_combine(ys, pos, gates, res, *, tr):
    t, d = res.shape
    g_pad = jnp.pad(gates, ((0, 0), (0, V7X_LANES - gates.shape[1])))
    return pl.pallas_call(
        functools.partial(_k_moe_combine, tr=tr),
        grid_spec=pltpu.PrefetchScalarGridSpec(
            num_scalar_prefetch=1, grid=(t // tr,),
            in_specs=[pl.BlockSpec(memory_space=pl.ANY),
                      pl.BlockSpec((tr, V7X_LANES), lambda i, p: (i, 0)),
                      pl.BlockSpec((tr, d), lambda i, p: (i, 0))],
            out_specs=pl.BlockSpec((tr, d), lambda i, p: (i, 0)),
            scratch_shapes=[pltpu.VMEM((2, tr, d), F32), pltpu.VMEM((2, tr, d), F32),
                            pltpu.SemaphoreType.DMA((2,))]),
        out_shape=jax.ShapeDtypeStruct((t, d), F32),
        compiler_params=_cparams(1), name="moe_combine",
    )(pos, ys, g_pad, res)


def moe_layer(h, g_ffn, w_router, b_router, w_gate, w_up, w_down, *, tm, tr, tn_up, tn_down, tr_gather,
              tr_combine):
    t, d = h.shape
    ne = w_gate.shape[0]
    hn, logits = router(h, g_ffn, w_router, b_router, tm=tm)
    top_v, top_e = lax.top_k(logits, TOP_K)
    gates = jax.nn.softmax(top_v, axis=-1)
    flat_e = top_e.reshape(-1)
    onehot = (flat_e[:, None] == jnp.arange(ne, dtype=flat_e.dtype)[None, :]).astype(jnp.int32)
    rank = jnp.sum((jnp.cumsum(onehot, axis=0) - onehot) * onehot, axis=1)
    sizes = jnp.sum(onehot, axis=0)
    tiles_per = (sizes + tr - 1) // tr
    tile_end = jnp.cumsum(tiles_per)
    tile_start = tile_end - tiles_per
    n_used = tile_end[-1:].astype(jnp.int32)
    nt = (t * TOP_K) // tr + ne
    r_pad = nt * tr
    pos = (tile_start[flat_e] * tr + rank).astype(jnp.int32)
    src_tok = jnp.zeros((r_pad,), jnp.int32).at[pos].set(jnp.arange(t * TOP_K, dtype=jnp.int32) // TOP_K)
    tile_start = tile_start.astype(jnp.int32)
    tiles_per = tiles_per.astype(jnp.int32)
    xs = gather_rows(hn, src_tok, n_used * (tr // tr_gather), tr=tr_gather)
    hid = moe_up(xs, w_gate, w_up, tile_start, tiles_per, n_used, tr=tr, tn=tn_up)
    ys = moe_down(hid, w_down, tile_start, tiles_per, n_used, tr=tr, tn=tn_down)
    return moe_combine(ys, pos.reshape(t, TOP_K).T.reshape(-1), gates, h, tr=tr_combine)


def _layer_weights(i, conv_w, conv_b, w_q, w_k, w_v, b_i, b_f, g_head, skip_a):
    nh = b_i.shape[1]
    pad = jnp.zeros((V7X_LANES - nh,), F32)
    b_if = jnp.concatenate([b_i[i].astype(F32), pad, b_f[i].astype(F32), pad]).reshape(1, 2 * V7X_LANES)
    da = conv_w.shape[-1]
    return dict(conv_w=conv_w[i], conv_b=conv_b[i].reshape(1, da), w_q=w_q[i], w_k=w_k[i], w_v=w_v[i],
                b_if=b_if, g_head=g_head[i].reshape(1, da), skip=skip_a[i].reshape(1, da))


def _tile_plan(n_rows, seq):
    tm = next(c for c in (832, 640, 512, 256, 128, 64, 32, 16) if n_rows % c == 0)
    tc = next(c for c in (320, 256, 128, 64, 32, 16, 8) if n_rows % c == 0)
    tm_big = 2 * tm if n_rows % (2 * tm) == 0 else tm
    return dict(tm=tm, tm_big=tm_big, tn=512, tn_down=256, mlstm_chunk=256, s5_chunk=256,
                moe_tr=256, moe_tn_up=512, moe_tn_down=512, gather_tr=256, combine_tr=tc,
                norm_tm=min(seq, 1024))


def kernel(x_prompt, x_sample, p_prompt, p_sample, state_mlstm_C, state_mlstm_n, state_mlstm_m, state_mlstm_conv,
           state_s5_re, state_s5_im, g_mix, w_in, conv_w, conv_b, w_q, w_k, w_v, b_i, b_f, g_head, skip_a, w_proj_a,
           s5_log_dt, s5_A_re, s5_A_im, s5_B_re, s5_B_im, s5_C_re, s5_C_im, s5_D, w_glu_b, w_out, g_ffn,
           w_ff_gate, w_ff_up, w_ff_down, w_router, b_router, w_moe_gate, w_moe_up, w_moe_down,
           g_ple, w_ple, w_pg, g_final):
    bsz, seq, d = x_prompt.shape
    bs = x_sample.shape[0]
    depth = g_mix.shape[0]
    nh = b_i.shape[1]
    d_a = conv_w.shape[-1]
    d_b = s5_D.shape[-1]
    n_p = bsz * seq
    t = n_p + bs
    tl = _tile_plan(t, seq)
    tm, tmb, tn = tl["tm"], tl["tm_big"], tl["tn"]

    h = jnp.concatenate([x_prompt.reshape(n_p, d), x_sample.reshape(bs, d)], axis=0).astype(F32)
    p_all = jnp.concatenate([p_prompt.reshape(depth, n_p, -1), p_sample.reshape(depth, bs, -1)], axis=1)

    col_ua, col_oa, col_ub = 0, 1, 2 * d_a // d_b
    col_ga = (2 * d_a + d_b) // tn
    col_gb = (2 * d_a + d_b + d) // tn
    col_if = (2 * d_a + d_b + 2 * d) // (2 * V7X_LANES)

    states = [[] for _ in range(11)]
    c_s_all = jnp.zeros(state_mlstm_C.shape, F32)
    for i in range(depth):
        lw = _layer_weights(i, conv_w, conv_b, w_q, w_k, w_v, b_i, b_f, g_head, skip_a)
        sp = _s5_params(dict(log_dt=s5_log_dt[i], a_re=s5_A_re[i], a_im=s5_A_im[i], b_re=s5_B_re[i],
                             b_im=s5_B_im[i], c_re=s5_C_re[i], c_im=s5_C_im[i], d_skip=s5_D[i]))
        proj = in_proj(h, g_mix[i], w_in, i, d_a, d_b, d, nh, tm=tmb, tn=tn)

        a_pre, c_p, n_pp, m_p, conv_p = mlstm_prompt(proj, col_ua, col_oa, col_if, t, bsz, seq, lw,
                                                     L=tl["mlstm_chunk"])
        a_pre, c_s_all, n_s, m_s, conv_s = mlstm_sample(proj, col_ua, col_oa, col_if, n_p, a_pre,
                                                        state_mlstm_C, i, c_s_all, state_mlstm_n[i].astype(F32),
                                                        state_mlstm_m[i].astype(F32),
                                                        state_mlstm_conv[i].astype(F32), lw)
        gy, sre_p, sim_p = s5_prompt(proj, col_ub, t, bsz, seq, sp, tc=tl["s5_chunk"])
        gy, sre_s, sim_s = s5_sample(proj, col_ub, n_p, gy, state_s5_re[i].astype(F32),
                                     state_s5_im[i].astype(F32), sp)
        b_gated = glu_gated_matmul(gy, w_glu_b, i, proj, col_gb, tm=tmb, tn=tn)
        mix = gated_add_matmul(a_pre, w_proj_a, i, proj, col_ga, b_gated, tm=tmb, tn=tn)
        h = matmul(mix, w_out, i, h, tm=tmb, tn=tn, name="out_proj")

        j = i // 2
        if i % 2 == 0:
            hid = norm_swiglu_up(h, g_ffn[i], w_ff_gate, w_ff_up, j, tm=tmb, tn=tn)
            h = matmul(hid, w_ff_down, j, h, tm=tmb, tn=tl["tn_down"], name="ffn_down")
        else:
            h = moe_layer(h, g_ffn[i], w_router[j], b_router[j], w_moe_gate[j], w_moe_up[j], w_moe_down[j],
                          tm=tm, tr=tl["moe_tr"], tn_up=tl["moe_tn_up"], tn_down=tl["moe_tn_down"],
                          tr_gather=tl["gather_tr"], tr_combine=tl["combine_tr"])
        h = ple_matmul(h, g_ple[i], p_all, w_pg, w_ple, i, tm=tm, tn=tn)

        g_s, p_s = S5_STATE, sre_p.shape[-1] // S5_STATE
        new = [c_p, n_pp, m_p, conv_p, sre_p.reshape(bsz, p_s, g_s), sim_p.reshape(bsz, p_s, g_s),
               n_s, m_s, conv_s, sre_s.reshape(bs, p_s, g_s), sim_s.reshape(bs, p_s, g_s)]
        for lst, s in zip(states, new):
            lst.append(s)

    y_prompt, y_sample = final_norm_split(h, g_final, n_p, tm=tl["norm_tm"])
    y_prompt = y_prompt.reshape(bsz, seq, d)
    y_sample = y_sample.reshape(bs, 1, d)
    st = [jnp.stack(lst) for lst in states]
    return (y_prompt, y_sample) + tuple(st[:6]) + (c_s_all,) + tuple(st[6:])
```

```python
import functools
import math

import numpy as np
import jax
import jax.numpy as jnp
from jax import lax
from jax.experimental import pallas as pl
from jax.experimental.pallas import tpu as pltpu

F32 = jnp.float32
BF16 = jnp.bfloat16
EPS = 1e-6

V7X_VMEM_BYTES = 64 * 1024 * 1024
V7X_LANES = 128
V7X_SUBLANES = 8
VMEM_LIMIT = 56 * 1024 * 1024

N_HEADS = 4
CONV_W = 4
S5_GROUP = 16
S5_STATE = 64
N_EXPERTS = 8
TOP_K = 2


def _cparams(n_axes, vmem=VMEM_LIMIT):
    return pltpu.CompilerParams(dimension_semantics=("arbitrary",) * n_axes, vmem_limit_bytes=vmem)


def _sigmoid(x):
    return 1.0 / (1.0 + jnp.exp(-x))


def _silu(x):
    return x * _sigmoid(x)


def _gelu_tanh(x):
    return 0.5 * x * (1.0 + jnp.tanh(math.sqrt(2.0 / math.pi) * (x + 0.044715 * (x * x * x))))


def _log_sigmoid(x):
    return jnp.minimum(x, 0.0) - jnp.log(1.0 + jnp.exp(-jnp.abs(x)))


def _dot(a, b):
    return jnp.dot(a, b, preferred_element_type=F32)


def _dot_nt(a, b):
    return lax.dot_general(a, b, (((1,), (1,)), ((), ())), preferred_element_type=F32)


def _dot_tn(a, b):
    return lax.dot_general(a, b, (((0,), (0,)), ((), ())), preferred_element_type=F32)


def _rmsnorm_rows(x, g):
    ms = jnp.mean(x * x, axis=-1, keepdims=True)
    return x * lax.rsqrt(ms + EPS) * g


def _row_tile_buffering(tm, k, dtype):
    two_copies = 2 * tm * k * jnp.dtype(dtype).itemsize
    return pl.Buffered(1) if two_copies > VMEM_LIMIT // 4 else None


def _row_chunks(tm):
    for rc in (256, 208, 128, 104, 64, 32, 16, 8):
        if tm % rc == 0:
            return rc
    return tm


def _norm_to_scratch(x_ref, g_ref, xn_ref, tm):
    rc = _row_chunks(tm)

    def body(r, c):
        rows = pl.ds(pl.multiple_of(r * rc, rc), rc)
        xn_ref[rows, :] = _rmsnorm_rows(x_ref[rows, :], g_ref[...]).astype(BF16)
        return c

    lax.fori_loop(0, tm // rc, body, 0)


def _wspec(w, layer, tn, col=lambda j: j):
    return pl.BlockSpec((None, w.shape[1], tn), lambda i, j: (layer, 0, col(j)))


def _k_in_proj(x_ref, g_ref, wa_ref, wb_ref, wif_ref, o_ref, xn_ref, *, tm, n_head, n_main, nh):
    j = pl.program_id(1)

    @pl.when(j == 0)
    def _():
        _norm_to_scratch(x_ref, g_ref, xn_ref, tm)

    @pl.when(j < n_head)
    def _():
        o_ref[...] = _dot_nt(xn_ref[...], wa_ref[0].astype(BF16))

    @pl.when(jnp.logical_and(j >= n_head, j < n_main))
    def _():
        o_ref[...] = _dot_nt(xn_ref[...], wb_ref[0].astype(BF16))

    @pl.when(j == n_main)
    def _():
        pre = _dot_nt(xn_ref[...], wif_ref[0].astype(BF16))
        o_ref[...] = jnp.zeros_like(o_ref)
        o_ref[:, 0:nh] = pre[:, 0:nh]
        o_ref[:, V7X_LANES:V7X_LANES + nh] = pre[:, nh:2 * nh]


def in_proj(x, g, w_in, layer, d_a, d_b, d, nh, *, tm, tn):
    t, k = x.shape
    assert (2 * nh) % V7X_SUBLANES == 0 and (2 * d_a) % tn == 0 and (d_b + 2 * d) % tn == 0
    wt = jnp.swapaxes(w_in, 1, 2)
    n_head = 2 * d_a // tn
    n_main = n_head + (d_b + 2 * d) // tn
    if_row = 2 * d_a

    def rows(nrows, start):
        return pl.BlockSpec((pl.Element(1), pl.Element(nrows), pl.Element(k)), lambda i, j: (layer, start(j), 0))

    return pl.pallas_call(
        functools.partial(_k_in_proj, tm=tm, n_head=n_head, n_main=n_main, nh=nh),
        grid=(t // tm, n_main + 1),
        in_specs=[pl.BlockSpec((tm, k), lambda i, j: (i, 0), pipeline_mode=_row_tile_buffering(tm, k, x.dtype)),
                  pl.BlockSpec((1, k), lambda i, j: (0, 0)),
                  rows(tn, lambda j: jnp.minimum(j, n_head - 1) * tn),
                  rows(tn, lambda j: (jnp.clip(j, n_head, n_main - 1) * (tn // V7X_SUBLANES)
                                      + 2 * nh // V7X_SUBLANES) * V7X_SUBLANES),
                  rows(2 * nh, lambda j: if_row)],
        out_specs=pl.BlockSpec((tm, tn), lambda i, j: (i, j)),
        out_shape=jax.ShapeDtypeStruct((t, (n_main + 1) * tn), F32),
        scratch_shapes=[pltpu.VMEM((tm, k), BF16)],
        compiler_params=_cparams(2), name="in_proj",
    )(x, g.reshape(1, k), wt, wt, wt)


def _k_mm_res(x_ref, w_ref, r_ref, o_ref):
    o_ref[...] = r_ref[...] + _dot(x_ref[...], w_ref[...].astype(BF16))


def _k_mm(x_ref, w_ref, o_ref):
    o_ref[...] = _dot(x_ref[...], w_ref[...].astype(BF16)).astype(o_ref.dtype)


def matmul(x, w, layer, res=None, *, tm, tn, out_dtype=F32, name="mm"):
    t, k = x.shape
    n = w.shape[2]
    in_specs = [pl.BlockSpec((tm, k), lambda i, j: (i, 0), pipeline_mode=_row_tile_buffering(tm, k, x.dtype)),
                _wspec(w, layer, tn)]
    args = [x, w]
    body = _k_mm
    if res is not None:
        in_specs.append(pl.BlockSpec((tm, tn), lambda i, j: (i, j)))
        args.append(res)
        body = _k_mm_res
    return pl.pallas_call(
        body, grid=(t // tm, pl.cdiv(n, tn)), in_specs=in_specs,
        out_specs=pl.BlockSpec((tm, tn), lambda i, j: (i, j)),
        out_shape=jax.ShapeDtypeStruct((t, n), out_dtype),
        compiler_params=_cparams(2), name=name,
    )(*args)


def _k_glu_gated(x_ref, wv_ref, wg_ref, gb_ref, o_ref):
    x = x_ref[...]
    val = _dot(x, wv_ref[...].astype(BF16))
    gate = _dot(x, wg_ref[...].astype(BF16))
    o_ref[...] = _sigmoid(gb_ref[...]) * (val * _sigmoid(gate))


def glu_gated_matmul(x, w, layer, proj, gate_col0, *, tm, tn, name="glu_b"):
    t, k = x.shape
    n = w.shape[2] // 2
    nj = n // tn
    return pl.pallas_call(
        _k_glu_gated, grid=(t // tm, nj),
        in_specs=[pl.BlockSpec((tm, k), lambda i, j: (i, 0)),
                  _wspec(w, layer, tn),
                  _wspec(w, layer, tn, lambda j: j + nj),
                  pl.BlockSpec((tm, tn), lambda i, j: (i, gate_col0 + j))],
        out_specs=pl.BlockSpec((tm, tn), lambda i, j: (i, j)),
        out_shape=jax.ShapeDtypeStruct((t, n), F32),
        compiler_params=_cparams(2), name=name,
    )(x, w, w, proj)


def _k_mm_gated_add(x_ref, w_ref, ga_ref, b_ref, o_ref):
    acc = _dot(x_ref[...], w_ref[...].astype(BF16))
    o_ref[...] = (_sigmoid(ga_ref[...]) * acc + b_ref[...]).astype(o_ref.dtype)


def gated_add_matmul(x, w, layer, proj, gate_col0, b, *, tm, tn, name="proj_a"):
    t, k = x.shape
    n = w.shape[2]
    return pl.pallas_call(
        _k_mm_gated_add, grid=(t // tm, n // tn),
        in_specs=[pl.BlockSpec((tm, k), lambda i, j: (i, 0)),
                  _wspec(w, layer, tn),
                  pl.BlockSpec((tm, tn), lambda i, j: (i, gate_col0 + j)),
                  pl.BlockSpec((tm, tn), lambda i, j: (i, j))],
        out_specs=pl.BlockSpec((tm, tn), lambda i, j: (i, j)),
        out_shape=jax.ShapeDtypeStruct((t, n), BF16),
        compiler_params=_cparams(2), name=name,
    )(x, w, proj, b)


def _k_swiglu_up(x_ref, g_ref, wg_ref, wu_ref, o_ref, xn_ref, *, tm):
    @pl.when(pl.program_id(1) == 0)
    def _():
        _norm_to_scratch(x_ref, g_ref, xn_ref, tm)

    xn = xn_ref[...]
    gate = _dot(xn, wg_ref[...].astype(BF16))
    up = _dot(xn, wu_ref[...].astype(BF16))
    o_ref[...] = (_silu(gate) * up).astype(o_ref.dtype)


def norm_swiglu_up(x, g, w_gate, w_up, layer, *, tm, tn, name="ffn_up"):
    t, k = x.shape
    n = w_gate.shape[2]
    return pl.pallas_call(
        functools.partial(_k_swiglu_up, tm=tm),
        grid=(t // tm, pl.cdiv(n, tn)),
        in_specs=[pl.BlockSpec((tm, k), lambda i, j: (i, 0), pipeline_mode=_row_tile_buffering(tm, k, x.dtype)),
                  pl.BlockSpec((1, k), lambda i, j: (0, 0)),
                  _wspec(w_gate, layer, tn), _wspec(w_up, layer, tn)],
        out_specs=pl.BlockSpec((tm, tn), lambda i, j: (i, j)),
        out_shape=jax.ShapeDtypeStruct((t, n), BF16),
        scratch_shapes=[pltpu.VMEM((tm, k), BF16)],
        compiler_params=_cparams(2), name=name,
    )(x, g.reshape(1, k), w_gate, w_up)


def _cast_rows_to(dst_ref, src_ref, rc):
    def body(r, c):
        rows = pl.ds(pl.multiple_of(r * rc, rc), rc)
        dst_ref[rows, :] = src_ref[rows, :].astype(dst_ref.dtype)
        return c

    lax.fori_loop(0, src_ref.shape[0] // rc, body, 0)


def _k_ple(x_ref, g_ref, p_ref, wpg_ref, wple_ref, o_ref, wpg_bf, wple_bf, *, tn):
    @pl.when(pl.program_id(0) == 0)
    def _():
        _cast_rows_to(wpg_bf, wpg_ref, V7X_LANES)
        _cast_rows_to(wple_bf, wple_ref, V7X_LANES)

    x = x_ref[...]
    xn = _rmsnorm_rows(x, g_ref[...]).astype(BF16)
    pb = p_ref[...].astype(BF16)
    for c in range(x.shape[1] // tn):
        cols = slice(c * tn, (c + 1) * tn)
        gate = _dot(xn, wpg_bf[:, cols])
        emb = _dot(pb, wple_bf[:, cols])
        o_ref[:, cols] = x[:, cols] + emb * _sigmoid(gate)


def ple_matmul(x, g, p, w_pg, w_ple, layer, *, tm, tn, name="ple"):
    t, k = x.shape
    kp = p.shape[2]
    resident = lambda w: pl.BlockSpec((None,) + w.shape[1:], lambda i: (layer, 0, 0), pipeline_mode=pl.Buffered(1))
    return pl.pallas_call(
        functools.partial(_k_ple, tn=tn),
        grid=(t // tm,),
        in_specs=[pl.BlockSpec((tm, k), lambda i: (i, 0)),
                  pl.BlockSpec((1, k), lambda i: (0, 0)),
                  pl.BlockSpec((None, tm, kp), lambda i: (layer, i, 0)),
                  resident(w_pg), resident(w_ple)],
        out_specs=pl.BlockSpec((tm, k), lambda i: (i, 0)),
        out_shape=jax.ShapeDtypeStruct((t, k), F32),
        scratch_shapes=[pltpu.VMEM(w_pg.shape[1:], BF16), pltpu.VMEM(w_ple.shape[1:], BF16)],
        compiler_params=_cparams(1), name=name,
    )(x, g.reshape(1, k), p, w_pg, w_ple)


def _k_final_norm(xp_ref, xs_ref, g_ref, op_ref, os_ref, *, n_prompt_tiles):
    i = pl.program_id(0)

    @pl.when(i < n_prompt_tiles)
    def _():
        op_ref[...] = _rmsnorm_rows(xp_ref[...], g_ref[...])

    @pl.when(i == n_prompt_tiles)
    def _():
        os_ref[...] = _rmsnorm_rows(xs_ref[...], g_ref[...])


def final_norm_split(x, g, n_prompt, *, tm):
    t, k = x.shape
    bs = t - n_prompt
    npt = n_prompt // tm
    last = npt - 1
    return pl.pallas_call(
        functools.partial(_k_final_norm, n_prompt_tiles=npt), grid=(npt + 1,),
        in_specs=[pl.BlockSpec((tm, k), lambda i: (jnp.minimum(i, last), 0)),
                  pl.BlockSpec((bs, k), lambda i: (n_prompt // bs, 0)),
                  pl.BlockSpec((1, k), lambda i: (0, 0))],
        out_specs=[pl.BlockSpec((tm, k), lambda i: (jnp.minimum(i, last), 0)),
                   pl.BlockSpec((bs, k), lambda i: (0, 0))],
        out_shape=[jax.ShapeDtypeStruct((n_prompt, k), F32), jax.ShapeDtypeStruct((bs, k), F32)],
        compiler_params=_cparams(1), name="final_norm",
    )(x, x, g.reshape(1, k))


def _split_hi_lo(x):
    hi = x.astype(BF16)
    lo = (x - hi.astype(F32)).astype(BF16)
    return hi, lo


def _k_mlstm_prompt(u_ref, o_ref_in, if_ref, cw_ref, cb_ref, wq_ref, wk_ref, wv_ref, bif_ref, gh_ref, sk_ref,
                    tri_ref, a_init, a_ref, c_out, n_out, m_out, conv_out, upad_ref, *, L, dh):
    del a_init
    c_idx = pl.program_id(1)
    nh = N_HEADS

    @pl.when(c_idx == 0)
    def _():
        c_out[...] = jnp.zeros_like(c_out)
        n_out[...] = jnp.zeros_like(n_out)
        m_out[...] = jnp.zeros_like(m_out)
        upad_ref[pl.ds(0, 8), :] = jnp.zeros((8, nh * dh), F32)

    @pl.when(c_idx > 0)
    def _():
        upad_ref[pl.ds(0, 8), :] = upad_ref[pl.ds(L, 8), :]

    u = u_ref[...]
    upad_ref[pl.ds(8, L), :] = u
    conv = cb_ref[...] + u * cw_ref[CONV_W - 1:CONV_W, :]
    for j in range(CONV_W - 1):
        conv = conv + upad_ref[pl.ds(8 - (CONV_W - 1) + j, L), :] * cw_ref[j:j + 1, :]
    cact = _silu(conv)
    conv_out[0] = upad_ref[pl.ds(L, 8), :]

    pre = if_ref[...]
    li = pre[:, :V7X_LANES] + bif_ref[:, :V7X_LANES]
    lf = _log_sigmoid(pre[:, V7X_LANES:] + bif_ref[:, V7X_LANES:])
    tri = tri_ref[...]
    lf_hi, lf_mid = _split_hi_lo(lf)
    lf_lo = (lf - lf_hi.astype(F32) - lf_mid.astype(F32)).astype(BF16)
    bcum = _dot(tri, lf_hi) + _dot(tri, lf_mid) + _dot(tri, lf_lo)
    li_t = li.T
    b_t = bcum.T
    row_id = lax.broadcasted_iota(jnp.int32, (L, L), 0)
    col_id = lax.broadcasted_iota(jnp.int32, (L, L), 1)
    causal = col_id <= row_id
    lane = lax.broadcasted_iota(jnp.int32, (1, V7X_LANES), 1)
    m_row = m_out[0]
    m_new_row = m_row

    for h in range(nh):
        hs = slice(h * dh, (h + 1) * dh)
        ch = cact[:, hs].astype(BF16)
        uh = u[:, hs].astype(BF16)
        q = _dot(ch, wq_ref[h].astype(BF16))
        k = _dot(ch, wk_ref[h].astype(BF16)) * (dh ** -0.5)
        v = _dot(uh, wv_ref[h].astype(BF16))
        qb, kb, vb = q.astype(BF16), k.astype(BF16), v.astype(BF16)

        b_col = bcum[:, h:h + 1]
        li_col = li[:, h:h + 1]
        r_row = li_t[h:h + 1, :] - b_t[h:h + 1, :]
        m_prev = m_row[:, h:h + 1]
        d = jnp.where(causal, b_col + r_row, -jnp.inf)
        inter = b_col + m_prev
        m_t = jnp.maximum(inter, jnp.max(d, axis=-1, keepdims=True))
        w_inter = jnp.exp(inter - m_t)
        s = _dot_nt(qb, kb) * jnp.exp(d - m_t)
        c_prev = c_out[0, h]
        n_prev = n_out[0, h:h + 1, :]
        num = w_inter * _dot(qb, c_prev.astype(BF16)) + _dot(s.astype(BF16), vb)
        den = w_inter * jnp.sum(q * n_prev, axis=-1, keepdims=True) + jnp.sum(s, axis=-1, keepdims=True)
        hh = num / jnp.maximum(jnp.abs(den), jnp.exp(-m_t))

        b_last = b_col[L - 1:L, :]
        g_col = b_last - b_col + li_col
        m_new = jnp.maximum(b_last + m_prev, jnp.max(g_col, axis=0, keepdims=True))
        decay = jnp.exp(b_last + m_prev - m_new)
        wk_ = jnp.exp(g_col - m_new) * k
        c_out[0, h] = decay * c_prev + _dot_tn(wk_.astype(BF16), vb)
        n_out[0, h:h + 1, :] = decay * n_prev + jnp.sum(wk_, axis=0, keepdims=True)
        m_new_row = jnp.where(lane == h, m_new, m_new_row)

        hn = _rmsnorm_rows(hh, gh_ref[:, hs])
        gated = (hn + sk_ref[:, hs] * cact[:, hs]) * _sigmoid(o_ref_in[:, hs])
        a_ref[:, hs] = gated.astype(a_ref.dtype)

    m_out[0] = m_new_row


def mlstm_prompt(proj, col_u, col_o, col_if, n_rows, bsz, seq, lw, *, L):
    dh = lw["w_q"].shape[-1]
    da = N_HEADS * dh
    nc = seq // L
    tri = jnp.asarray(np.tril(np.ones((L, L), np.float32)), BF16)
    row_blk = lambda b, c: b * nc + c
    full = lambda *shape: pl.BlockSpec(shape, lambda b, c: (0,) * len(shape))
    outs = pl.pallas_call(
        functools.partial(_k_mlstm_prompt, L=L, dh=dh),
        grid=(bsz, nc),
        in_specs=[pl.BlockSpec((L, da), lambda b, c: (row_blk(b, c), col_u)),
                  pl.BlockSpec((L, da), lambda b, c: (row_blk(b, c), col_o)),
                  pl.BlockSpec((L, 2 * V7X_LANES), lambda b, c: (row_blk(b, c), col_if)),
                  full(CONV_W, da), full(1, da), full(N_HEADS, dh, dh), full(N_HEADS, dh, dh),
                  full(N_HEADS, dh, dh), full(1, 2 * V7X_LANES), full(1, da), full(1, da), full(L, L),
                  pl.BlockSpec(memory_space=pl.ANY)],
        out_specs=[pl.BlockSpec((L, da), lambda b, c: (row_blk(b, c), 0)),
                   pl.BlockSpec((1, N_HEADS, dh, dh), lambda b, c: (b, 0, 0, 0)),
                   pl.BlockSpec((1, N_HEADS, dh), lambda b, c: (b, 0, 0)),
                   pl.BlockSpec((1, 1, V7X_LANES), lambda b, c: (b, 0, 0)),
                   pl.BlockSpec((1, 8, da), lambda b, c: (b, 0, 0))],
        out_shape=[jax.ShapeDtypeStruct((n_rows, da), BF16),
                   jax.ShapeDtypeStruct((bsz, N_HEADS, dh, dh), F32),
                   jax.ShapeDtypeStruct((bsz, N_HEADS, dh), F32),
                   jax.ShapeDtypeStruct((bsz, 1, V7X_LANES), F32),
                   jax.ShapeDtypeStruct((bsz, 8, da), F32)],
        scratch_shapes=[pltpu.VMEM((L + 8, da), F32)],
        input_output_aliases={12: 0},
        compiler_params=_cparams(2), name="mlstm_prompt",
    )(proj, proj, proj, lw["conv_w"], lw["conv_b"], lw["w_q"], lw["w_k"], lw["w_v"], lw["b_if"],
      lw["g_head"], lw["skip"], tri, jnp.zeros((n_rows, da), BF16))
    a_pre, c_p, n_p, m_p, conv_p = outs
    return a_pre, c_p, n_p, m_p[:, 0, :N_HEADS], conv_p[:, 8 - (CONV_W - 1):, :]


def _k_mlstm_sample_pre(u_ref, if_ref, conv_ref, m_ref, cw_ref, cb_ref, wq_ref, wk_ref, wv_ref, bif_ref,
                        q_out, k_out, v_out, c_out, gates_out, conv_out, *, dh):
    nh = N_HEADS
    u = u_ref[...]
    conv = cb_ref[...] + u * cw_ref[CONV_W - 1:CONV_W, :]
    for j in range(CONV_W - 1):
        conv = conv + conv_ref[j] * cw_ref[j:j + 1, :]
        if j > 0:
            conv_out[j - 1] = conv_ref[j]
    conv_out[CONV_W - 2] = u
    cact = _silu(conv)
    c_out[...] = cact
    for h in range(nh):
        hs = slice(h * dh, (h + 1) * dh)
        ch = cact[:, hs].astype(BF16)
        q_out[:, hs] = _dot(ch, wq_ref[h].astype(BF16))
        k_out[:, hs] = _dot(ch, wk_ref[h].astype(BF16)) * (dh ** -0.5)
        v_out[:, hs] = _dot(u[:, hs].astype(BF16), wv_ref[h].astype(BF16))
    pre = if_ref[...]
    li = pre[:, :V7X_LANES] + bif_ref[:, :V7X_LANES]
    lf = _log_sigmoid(pre[:, V7X_LANES:] + bif_ref[:, V7X_LANES:])
    m_prev = m_ref[...]
    inter = lf + m_prev
    m_t = jnp.maximum(inter, li)
    gates_out[0] = jnp.exp(inter - m_t)
    gates_out[1] = jnp.exp(li - m_t)
    gates_out[2] = jnp.exp(-m_t)
    gates_out[3] = m_t


def _k_mlstm_sample_step(q_ref, k_ref, v_ref, gates_ref, c_ref, n_ref, cact_ref, o_ref_in, gh_ref, sk_ref,
                         a_any, c_any, c_out, n_out, a_out, hh_ref, *, bt, dh):
    del a_any, c_any
    i = pl.program_id(0)
    nh = N_HEADS
    q = q_ref[...]
    k = k_ref[...]
    v = v_ref[...]
    w_inter = gates_ref[0]
    w_new = gates_ref[1]
    e_neg_m = gates_ref[2]
    n_prev = n_ref[...]
    rows = pl.ds(pl.multiple_of(i * bt, bt), bt)
    for h in range(nh):
        hs = slice(h * dh, (h + 1) * dh)
        qh, kh, vh, nh_prev = q[:, hs], k[:, hs], v[:, hs], n_prev[:, hs]
        q_t = qh.T
        k_t = kh.T
        wi = w_inter[:, h:h + 1]
        wn = w_new[:, h:h + 1]
        s = jnp.sum(qh * kh, axis=-1, keepdims=True) * wn
        den = wi * jnp.sum(qh * nh_prev, axis=-1, keepdims=True) + s
        wv = wn * vh
        qc_rows = []
        for bl in range(bt):
            c_prev = c_ref[bl, h]
            qc_rows.append(jnp.sum(q_t[:, bl:bl + 1] * c_prev, axis=0, keepdims=True))
            c_out[bl, h] = wi[bl:bl + 1, :] * c_prev + k_t[:, bl:bl + 1] * wv[bl:bl + 1, :]
        qc = jnp.concatenate(qc_rows, axis=0)
        num = wi * qc + s * vh
        hh = num / jnp.maximum(jnp.abs(den), e_neg_m[:, h:h + 1])
        hh_ref[rows, hs] = hh
        n_out[:, hs] = wi * nh_prev + wn * kh

    @pl.when(i == pl.num_programs(0) - 1)
    def _():
        for h in range(nh):
            hs = slice(h * dh, (h + 1) * dh)
            hn = _rmsnorm_rows(hh_ref[:, hs], gh_ref[:, hs])
            a_out[:, hs] = ((hn + sk_ref[:, hs] * cact_ref[:, hs]) * _sigmoid(o_ref_in[:, hs])).astype(a_out.dtype)


def mlstm_sample(proj, col_u, col_o, col_if, row0, a_pre_all, c_all, layer, c_new_all, n0, m0, conv0, lw, *,
                 bt=8):
    _, bs, nh, dh, _ = c_all.shape
    da = nh * dh
    rb = row0 // bs
    m_pad = jnp.pad(m0, ((0, 0), (0, V7X_LANES - nh)))
    conv_t = jnp.transpose(conv0, (1, 0, 2))
    full = lambda *shape: pl.BlockSpec(shape, lambda i: (0,) * len(shape))
    q, k, v, cact, gates, conv_new = pl.pallas_call(
        functools.partial(_k_mlstm_sample_pre, dh=dh),
        grid=(1,),
        in_specs=[pl.BlockSpec((bs, da), lambda i: (rb, col_u)),
                  pl.BlockSpec((bs, 2 * V7X_LANES), lambda i: (rb, col_if)),
                  full(CONV_W - 1, bs, da), full(bs, V7X_LANES), full(CONV_W, da), full(1, da),
                  full(nh, dh, dh), full(nh, dh, dh), full(nh, dh, dh), full(1, 2 * V7X_LANES)],
        out_specs=[full(bs, da), full(bs, da), full(bs, da), full(bs, da), full(4, bs, V7X_LANES),
                   full(CONV_W - 1, bs, da)],
        out_shape=[jax.ShapeDtypeStruct((bs, da), F32)] * 4
                  + [jax.ShapeDtypeStruct((4, bs, V7X_LANES), F32),
                     jax.ShapeDtypeStruct((CONV_W - 1, bs, da), F32)],
        compiler_params=_cparams(1), name="mlstm_sample_pre",
    )(proj, proj, conv_t, m_pad, lw["conv_w"], lw["conv_b"], lw["w_q"], lw["w_k"], lw["w_v"], lw["b_if"])

    blk = lambda *shape: pl.BlockSpec(shape, lambda i: (i,) + (0,) * (len(shape) - 1))
    cst = lambda *shape: pl.BlockSpec(shape, lambda i: (0,) * len(shape))
    c_blk = pl.BlockSpec((None, bt, nh, dh, dh), lambda i: (layer, i, 0, 0, 0))
    c_new_all, n_new, a_pre_all = pl.pallas_call(
        functools.partial(_k_mlstm_sample_step, bt=bt, dh=dh),
        grid=(bs // bt,),
        in_specs=[blk(bt, da), blk(bt, da), blk(bt, da),
                  pl.BlockSpec((4, bt, V7X_LANES), lambda i: (0, i, 0)),
                  c_blk, blk(bt, da), cst(bs, da),
                  pl.BlockSpec((bs, da), lambda i: (rb, col_o)), cst(1, da), cst(1, da),
                  pl.BlockSpec(memory_space=pl.ANY), pl.BlockSpec(memory_space=pl.ANY)],
        out_specs=[c_blk, blk(bt, da), pl.BlockSpec((bs, da), lambda i: (rb, 0))],
        out_shape=[jax.ShapeDtypeStruct(c_new_all.shape, F32), jax.ShapeDtypeStruct((bs, da), F32),
                   jax.ShapeDtypeStruct(a_pre_all.shape, a_pre_all.dtype)],
        scratch_shapes=[pltpu.VMEM((bs, da), F32)],
        input_output_aliases={10: 2, 11: 0},
        compiler_params=_cparams(1), name="mlstm_sample_step",
    )(q, k, v, gates, c_all, n0.reshape(bs, da), cact, proj, lw["g_head"], lw["skip"], a_pre_all, c_new_all)
    m_new = gates[3][:, :nh]
    return a_pre_all, c_new_all, n_new.reshape(bs, nh, dh), m_new, jnp.transpose(conv_new, (1, 0, 2))


S5_SLAB_GROUPS = V7X_LANES // S5_GROUP
S5_SLAB_STATES = S5_SLAB_GROUPS * S5_STATE


def _s5_params(lp):
    g, p = lp["a_re"].shape
    dt = jnp.exp(lp["log_dt"].astype(F32))[:, None]
    a_re = lp["a_re"].astype(F32)
    a_im = lp["a_im"].astype(F32)
    lam_re = a_re * dt
    lam_im = a_im * dt
    mag = jnp.exp(lam_re)
    ab_re = mag * jnp.cos(lam_im)
    ab_im = mag * jnp.sin(lam_im)
    den = a_re * a_re + a_im * a_im
    nr = ab_re - 1.0
    ni = ab_im
    k_re = (nr * a_re + ni * a_im) / den
    k_im = (ni * a_re - nr * a_im) / den
    b_re = lp["b_re"].astype(F32)
    b_im = lp["b_im"].astype(F32)
    bb_re = k_re[..., None] * b_re - k_im[..., None] * b_im
    bb_im = k_re[..., None] * b_im + k_im[..., None] * b_re
    ns = g // S5_SLAB_GROUPS
    eye = jnp.eye(S5_SLAB_GROUPS, dtype=F32)

    def in_blockdiag(bb):
        bs = bb.reshape(ns, S5_SLAB_GROUPS, p, S5_GROUP)
        w = jnp.einsum("ab,sapc->sacbp", eye, bs)
        return w.reshape(ns, V7X_LANES, S5_SLAB_STATES).astype(BF16)

    def out_blockdiag(cc):
        cs = cc.astype(F32).reshape(ns, S5_SLAB_GROUPS, S5_GROUP, p)
        w = jnp.einsum("ab,sacp->sapbc", eye, cs)
        return w.reshape(ns, S5_SLAB_STATES, V7X_LANES)

    wc = jnp.concatenate([out_blockdiag(lp["c_re"]), -out_blockdiag(lp["c_im"])], axis=1).astype(BF16)
    return dict(lam_re=lam_re.reshape(1, g * p), lam_im=lam_im.reshape(1, g * p),
                ab_re=ab_re.reshape(1, g * p), ab_im=ab_im.reshape(1, g * p),
                wb_re=in_blockdiag(bb_re), wb_im=in_blockdiag(bb_im), wc=wc,
                d_skip=lp["d_skip"].astype(F32).reshape(1, g * S5_GROUP))


def _s5_powers(sp, ks):
    kk = jnp.asarray(ks, F32)[:, None]
    mag = jnp.exp(kk * sp["lam_re"])
    return mag * jnp.cos(kk * sp["lam_im"]), mag * jnp.sin(kk * sp["lam_im"])


def _s5_input_proj(u_bf, wbr_ref, wbi_ref, bur_ref, bui_ref):
    ns = wbr_ref.shape[0]
    for s in range(ns):
        us = u_bf[:, s * V7X_LANES:(s + 1) * V7X_LANES]
        cols = slice(s * S5_SLAB_STATES, (s + 1) * S5_SLAB_STATES)
        bur_ref[:, cols] = _dot(us, wbr_ref[s])
        bui_ref[:, cols] = _dot(us, wbi_ref[s])


def _s5_output_proj(xr_ref, xi_ref, wc_ref, skip):
    ns = wc_ref.shape[0]
    outs = []
    for s in range(ns):
        cols = slice(s * S5_SLAB_STATES, (s + 1) * S5_SLAB_STATES)
        xcat = jnp.concatenate([xr_ref[:, cols].astype(BF16), xi_ref[:, cols].astype(BF16)], axis=1)
        y = _dot(xcat, wc_ref[s]) + skip[:, s * V7X_LANES:(s + 1) * V7X_LANES]
        outs.append(_gelu_tanh(y).astype(BF16))
    return jnp.concatenate(outs, axis=1)


def _k_s5_prompt(u_ref, wbr_ref, wbi_ref, wc_ref, d_ref, abr_ref, abi_ref, tpr_ref, tpi_ref, ajr_ref, aji_ref,
                 perm_ref, permt_ref, gy_init, gy_ref, sre_out, sim_out,
                 bur0, bur1, bui0, bui1, xb0, xb1, *, tc):
    del gy_init
    c_idx = pl.program_id(1)
    nsteps = tc // V7X_SUBLANES
    ns = wbr_ref.shape[0]
    lw = S5_SLAB_STATES
    pair = 2 * V7X_SUBLANES
    bur, bui, xb = (bur0, bur1), (bui0, bui1), (xb0, xb1)

    @pl.when(c_idx == 0)
    def _():
        sre_out[...] = jnp.zeros_like(sre_out)
        sim_out[...] = jnp.zeros_like(sim_out)

    u = u_ref[...]
    u_hi, u_lo = _split_hi_lo(u)
    perm = perm_ref[...]
    up_hi = _dot(perm, u_hi)
    skip = d_ref[...] * (up_hi + _dot(perm, u_lo))
    up_bf = up_hi.astype(BF16)

    sub = lax.broadcasted_iota(jnp.int32, (V7X_SUBLANES, lw), 0)

    def input_proj(s):
        us = up_bf[:, s * V7X_LANES:(s + 1) * V7X_LANES]
        bur[s % 2][...] = _dot(us, wbr_ref[s])
        bui[s % 2][...] = _dot(us, wbi_ref[s])

    def scan(s):
        br, bi, xo = bur[s % 2], bui[s % 2], xb[s % 2]
        lanes = slice(s * lw, (s + 1) * lw)
        ar = jnp.broadcast_to(abr_ref[:, lanes], (V7X_SUBLANES, lw))
        ai = jnp.broadcast_to(abi_ref[:, lanes], (V7X_SUBLANES, lw))
        er = jnp.zeros((V7X_SUBLANES, lw), F32)
        ei = er
        for i in range(nsteps):
            rows = slice(i * V7X_SUBLANES, (i + 1) * V7X_SUBLANES)
            er, ei = (ar * er - ai * ei + br[rows, :], ar * ei + ai * er + bi[rows, :])
            br[rows, :] = er
            bi[rows, :] = ei
        for d, row in ((1, 0), (2, 1), (4, 3)):
            pr = ajr_ref[row:row + 1, lanes]
            pi = aji_ref[row:row + 1, lanes]
            sr = pltpu.roll(er, d, 0)
            si = pltpu.roll(ei, d, 0)
            keep = sub >= d
            er, ei = (er + jnp.where(keep, pr * sr - pi * si, 0.0),
                      ei + jnp.where(keep, pr * si + pi * sr, 0.0))
        c0r = jnp.broadcast_to(sre_out[0, :, lanes], (V7X_SUBLANES, lw))
        c0i = jnp.broadcast_to(sim_out[0, :, lanes], (V7X_SUBLANES, lw))
        ajr = ajr_ref[:, lanes]
        aji = aji_ref[:, lanes]
        fr = ajr * c0r - aji * c0i + er
        fi = ajr * c0i + aji * c0r + ei
        cin_r = jnp.where(sub >= 1, pltpu.roll(fr, 1, 0), c0r)
        cin_i = jnp.where(sub >= 1, pltpu.roll(fi, 1, 0), c0i)
        sre_out[0, :, lanes] = fr[V7X_SUBLANES - 1:V7X_SUBLANES, :]
        sim_out[0, :, lanes] = fi[V7X_SUBLANES - 1:V7X_SUBLANES, :]

        cin2_r = jnp.concatenate([cin_r, cin_r], axis=0)
        cin2_i = jnp.concatenate([cin_i, cin_i], axis=0)
        for k in range(tc // pair):
            rows = slice(k * pair, (k + 1) * pair)
            pr = tpr_ref[rows, lanes]
            pi = tpi_ref[rows, lanes]
            xo[rows, :lw] = (br[rows, :] + (pr * cin2_r - pi * cin2_i)).astype(BF16)
            xo[rows, lw:] = (bi[rows, :] + (pr * cin2_i + pi * cin2_r)).astype(BF16)

    def output_proj(s):
        y = _dot(xb[s % 2][...], wc_ref[s]) + skip[:, s * V7X_LANES:(s + 1) * V7X_LANES]
        return _gelu_tanh(y).astype(BF16)

    outs = []
    input_proj(0)
    for s in range(ns):
        if s + 1 < ns:
            input_proj(s + 1)
        scan(s)
        outs.append(output_proj(s))
    g_perm = jnp.concatenate(outs, axis=1)
    gy_ref[...] = _dot(permt_ref[...], g_perm).astype(gy_ref.dtype)


def s5_prompt(proj, col_u, n_rows, bsz, seq, sp, *, tc=256):
    db = sp["d_skip"].shape[1]
    nch = sp["ab_re"].shape[1]
    nc = seq // tc
    nsteps = tc // V7X_SUBLANES
    tpr, tpi = _s5_powers(sp, np.repeat(np.arange(1, nsteps + 1), V7X_SUBLANES))
    ajr, aji = _s5_powers(sp, nsteps * np.arange(1, V7X_SUBLANES + 1))
    perm = np.zeros((tc, tc), np.float32)
    r = np.arange(tc)
    perm[r, (r % V7X_SUBLANES) * nsteps + r // V7X_SUBLANES] = 1.0
    full = lambda a: pl.BlockSpec(a.shape, lambda b, c: (0,) * a.ndim)
    consts = [sp["wb_re"], sp["wb_im"], sp["wc"], sp["d_skip"], sp["ab_re"], sp["ab_im"], tpr, tpi, ajr, aji,
              jnp.asarray(perm, BF16), jnp.asarray(perm.T, BF16)]
    gy, s_re, s_im = pl.pallas_call(
        functools.partial(_k_s5_prompt, tc=tc),
        grid=(bsz, nc),
        in_specs=[pl.BlockSpec((tc, db), lambda b, c: (b * nc + c, col_u))] + [full(a) for a in consts]
                 + [pl.BlockSpec(memory_space=pl.ANY)],
        out_specs=[pl.BlockSpec((tc, db), lambda b, c: (b * nc + c, 0)),
                   pl.BlockSpec((1, 1, nch), lambda b, c: (b, 0, 0)),
                   pl.BlockSpec((1, 1, nch), lambda b, c: (b, 0, 0))],
        out_shape=[jax.ShapeDtypeStruct((n_rows, db), BF16),
                   jax.ShapeDtypeStruct((bsz, 1, nch), F32), jax.ShapeDtypeStruct((bsz, 1, nch), F32)],
        scratch_shapes=[pltpu.VMEM((tc, S5_SLAB_STATES), F32)] * 4 + [pltpu.VMEM((tc, 2 * S5_SLAB_STATES), BF16)] * 2,
        input_output_aliases={1 + len(consts): 0},
        compiler_params=_cparams(2), name="s5_prompt",
    )(proj, *consts, jnp.zeros((n_rows, db), BF16))
    return gy, s_re, s_im


def _k_s5_sample(u_ref, x0r_ref, x0i_ref, wbr_ref, wbi_ref, wc_ref, d_ref, abr_ref, abi_ref, gy_any,
                 gy_ref, xr_out, xi_out):
    del gy_any
    u = u_ref[...]
    _s5_input_proj(u.astype(BF16), wbr_ref, wbi_ref, xr_out, xi_out)
    ar = abr_ref[...]
    ai = abi_ref[...]
    x0r = x0r_ref[...]
    x0i = x0i_ref[...]
    xr_out[...] = xr_out[...] + (ar * x0r - ai * x0i)
    xi_out[...] = xi_out[...] + (ar * x0i + ai * x0r)
    gy_ref[...] = _s5_output_proj(xr_out, xi_out, wc_ref, d_ref[...] * u).astype(gy_ref.dtype)


def s5_sample(proj, col_u, row0, gy_all, x0_re, x0_im, sp):
    bs = x0_re.shape[0]
    db = sp["d_skip"].shape[1]
    nch = sp["ab_re"].shape[1]
    rb = row0 // bs
    full = lambda a: pl.BlockSpec(a.shape, lambda i: (0,) * a.ndim)
    consts = [sp["wb_re"], sp["wb_im"], sp["wc"], sp["d_skip"], sp["ab_re"], sp["ab_im"]]
    x0r = x0_re.reshape(bs, nch)
    x0i = x0_im.reshape(bs, nch)
    gy_all, xr, xi = pl.pallas_call(
        _k_s5_sample,
        grid=(1,),
        in_specs=[pl.BlockSpec((bs, db), lambda i: (rb, col_u)), full(x0r), full(x0i)]
                 + [full(a) for a in consts] + [pl.BlockSpec(memory_space=pl.ANY)],
        out_specs=[pl.BlockSpec((bs, db), lambda i: (rb, 0)),
                   pl.BlockSpec((bs, nch), lambda i: (0, 0)), pl.BlockSpec((bs, nch), lambda i: (0, 0))],
        out_shape=[jax.ShapeDtypeStruct(gy_all.shape, gy_all.dtype),
                   jax.ShapeDtypeStruct((bs, nch), F32), jax.ShapeDtypeStruct((bs, nch), F32)],
        input_output_aliases={9: 0},
        compiler_params=_cparams(1), name="s5_sample",
    )(proj, x0r, x0i, *consts, gy_all)
    return gy_all, xr, xi


def _k_router(x_ref, g_ref, w_ref, b_ref, hn_ref, lg_ref):
    hn = _rmsnorm_rows(x_ref[...], g_ref[...])
    hn_ref[...] = hn
    x_hi, x_lo = _split_hi_lo(hn)
    w_hi, w_lo = _split_hi_lo(w_ref[...])
    lg_ref[...] = _dot(x_hi, w_hi) + (_dot(x_lo, w_hi) + _dot(x_hi, w_lo)) + b_ref[...]


def router(x, g, w_router, b_router, *, tm):
    t, k = x.shape
    ne = w_router.shape[1]
    w_pad = jnp.pad(w_router, ((0, 0), (0, V7X_LANES - ne)))
    b_pad = jnp.pad(b_router.astype(F32), (0, V7X_LANES - ne)).reshape(1, V7X_LANES)
    hn, lg = pl.pallas_call(
        _k_router, grid=(t // tm,),
        in_specs=[pl.BlockSpec((tm, k), lambda i: (i, 0)), pl.BlockSpec((1, k), lambda i: (0, 0)),
                  pl.BlockSpec((k, V7X_LANES), lambda i: (0, 0)), pl.BlockSpec((1, V7X_LANES), lambda i: (0, 0))],
        out_specs=[pl.BlockSpec((tm, k), lambda i: (i, 0)), pl.BlockSpec((tm, V7X_LANES), lambda i: (i, 0))],
        out_shape=[jax.ShapeDtypeStruct((t, k), F32), jax.ShapeDtypeStruct((t, V7X_LANES), F32)],
        compiler_params=_cparams(1), name="router",
    )(x, g.reshape(1, k), w_pad, b_pad)
    return hn, lg[:, :ne]


DMA_ISSUE_UNROLL = 8


def _k_gather_rows(nused_ref, tok_ref, src_hbm, o_ref, buf_ref, sems, *, tr):
    t = pl.program_id(0)
    n_used = nused_ref[0]

    def issue_tile(tile):
        slot = tile % 2
        base = tile * tr

        def issue(r, c):
            tok = tok_ref[base + r]
            pltpu.make_async_copy(src_hbm.at[pl.ds(tok, 1), :], buf_ref.at[slot, pl.ds(r, 1), :],
                                  sems.at[slot]).start()
            return c

        lax.fori_loop(0, tr, issue, 0, unroll=DMA_ISSUE_UNROLL)

    @pl.when(t == 0)
    def _():
        issue_tile(t)

    @pl.when(t + 1 < n_used)
    def _():
        issue_tile(t + 1)

    @pl.when(t < n_used)
    def _():
        slot = t % 2
        pltpu.make_async_copy(src_hbm.at[pl.ds(0, tr), :], buf_ref.at[slot], sems.at[slot]).wait()
        o_ref[...] = buf_ref[slot].astype(o_ref.dtype)

    @pl.when(t >= n_used)
    def _():
        o_ref[...] = jnp.zeros_like(o_ref)


def gather_rows(src, tok, n_used, *, tr):
    t, k = src.shape
    r_pad = tok.shape[0]
    nt = r_pad // tr
    return pl.pallas_call(
        functools.partial(_k_gather_rows, tr=tr),
        grid_spec=pltpu.PrefetchScalarGridSpec(
            num_scalar_prefetch=2, grid=(nt,),
            in_specs=[pl.BlockSpec(memory_space=pl.ANY)],
            out_specs=pl.BlockSpec((tr, k), lambda i, nu, tk: (i, 0)),
            scratch_shapes=[pltpu.VMEM((2, tr, k), src.dtype), pltpu.SemaphoreType.DMA((2,))]),
        out_shape=jax.ShapeDtypeStruct((r_pad, k), BF16),
        compiler_params=_cparams(1), name="moe_gather",
    )(n_used, tok, src)


def _new_group(te_ref, t):
    return jnp.logical_or(t == 0, te_ref[t] != te_ref[jnp.maximum(t - 1, 0)])


def _k_moe_up(nused_ref, te_ref, x_ref, wg_ref, wu_ref, o_ref, wgb_ref, wub_ref):
    t = pl.program_id(1)

    @pl.when(t < nused_ref[0])
    def _():
        @pl.when(_new_group(te_ref, t))
        def _():
            wgb_ref[...] = wg_ref[...].astype(BF16)
            wub_ref[...] = wu_ref[...].astype(BF16)

        x = x_ref[...]
        o_ref[...] = (_silu(_dot(x, wgb_ref[...])) * _dot(x, wub_ref[...])).astype(o_ref.dtype)

    @pl.when(t >= nused_ref[0])
    def _():
        o_ref[...] = jnp.zeros_like(o_ref)


def moe_up(xs, w_gate, w_up, tile_expert, n_used, *, tr, tn):
    r_pad, k = xs.shape
    ne, _, f = w_gate.shape
    nt = r_pad // tr
    row = lambda j, t, nu, te: (jnp.minimum(t, nu[0] - 1), 0)
    wmap = lambda j, t, nu, te: (te[t], 0, j)
    return pl.pallas_call(
        _k_moe_up,
        grid_spec=pltpu.PrefetchScalarGridSpec(
            num_scalar_prefetch=2, grid=(pl.cdiv(f, tn), nt),
            in_specs=[pl.BlockSpec((tr, k), row),
                      pl.BlockSpec((None, k, tn), wmap), pl.BlockSpec((None, k, tn), wmap)],
            out_specs=pl.BlockSpec((tr, tn), lambda j, t, nu, te: (t, j)),
            scratch_shapes=[pltpu.VMEM((k, tn), BF16), pltpu.VMEM((k, tn), BF16)]),
        out_shape=jax.ShapeDtypeStruct((r_pad, f), BF16),
        compiler_params=_cparams(2), name="moe_up",
    )(n_used, tile_expert, xs, w_gate, w_up)


def _k_moe_down(nused_ref, te_ref, x_ref, w_ref, o_ref, wb_ref):
    t = pl.program_id(1)

    @pl.when(t < nused_ref[0])
    def _():
        @pl.when(_new_group(te_ref, t))
        def _():
            wb_ref[...] = w_ref[...].astype(BF16)

        o_ref[...] = _dot(x_ref[...], wb_ref[...])

    @pl.when(t >= nused_ref[0])
    def _():
        o_ref[...] = jnp.zeros_like(o_ref)


def moe_down(hid, w_down, tile_expert, n_used, *, tr, tn):
    r_pad, f = hid.shape
    d = w_down.shape[2]
    nt = r_pad // tr
    return pl.pallas_call(
        _k_moe_down,
        grid_spec=pltpu.PrefetchScalarGridSpec(
            num_scalar_prefetch=2, grid=(d // tn, nt),
            in_specs=[pl.BlockSpec((tr, f), lambda j, t, nu, te: (jnp.minimum(t, nu[0] - 1), 0)),
                      pl.BlockSpec((None, f, tn), lambda j, t, nu, te: (te[t], 0, j))],
            out_specs=pl.BlockSpec((tr, tn), lambda j, t, nu, te: (t, j)),
            scratch_shapes=[pltpu.VMEM((f, tn), BF16)]),
        out_shape=jax.ShapeDtypeStruct((r_pad, d), F32),
        compiler_params=_cparams(2), name="moe_down",
    )(n_used, tile_expert, hid, w_down)


def _k_moe_combine(pos_ref, ys_hbm, gate_ref, res_ref, o_ref, buf0_ref, buf1_ref, sems, *, tr):
    t = pl.program_id(0)
    n_tiles = pl.num_programs(0)
    n_tok = n_tiles * tr

    def issue_tile(tile):
        slot = tile % 2
        base = tile * tr

        def issue(r, c):
            p0 = pos_ref[base + r]
            p1 = pos_ref[n_tok + base + r]
            pltpu.make_async_copy(ys_hbm.at[pl.ds(p0, 1), :], buf0_ref.at[slot, pl.ds(r, 1), :],
                                  sems.at[slot]).start()
            pltpu.make_async_copy(ys_hbm.at[pl.ds(p1, 1), :], buf1_ref.at[slot, pl.ds(r, 1), :],
                                  sems.at[slot]).start()
            return c

        lax.fori_loop(0, tr, issue, 0, unroll=DMA_ISSUE_UNROLL)

    @pl.when(t == 0)
    def _():
        issue_tile(t)

    @pl.when(t + 1 < n_tiles)
    def _():
        issue_tile(t + 1)

    slot = t % 2
    pltpu.make_async_copy(ys_hbm.at[pl.ds(0, tr), :], buf0_ref.at[slot], sems.at[slot]).wait()
    pltpu.make_async_copy(ys_hbm.at[pl.ds(0, tr), :], buf1_ref.at[slot], sems.at[slot]).wait()
    g = gate_ref[...]
    o_ref[...] = res_ref[...] + (g[:, 0:1] * buf0_ref[slot] + g[:, 1:2] * buf1_ref[slot])


def moe_combine(ys, pos, gates, res, *, tr):
    t, d = res.shape
    g_pad = jnp.pad(gates, ((0, 0), (0, V7X_LANES - gates.shape[1])))
    return pl.pallas_call(
        functools.partial(_k_moe_combine, tr=tr),
        grid_spec=pltpu.PrefetchScalarGridSpec(
            num_scalar_prefetch=1, grid=(t // tr,),
            in_specs=[pl.BlockSpec(memory_space=pl.ANY),
                      pl.BlockSpec((tr, V7X_LANES), lambda i, p: (i, 0)),
                      pl.BlockSpec((tr, d), lambda i, p: (i, 0))],
            out_specs=pl.BlockSpec((tr, d), lambda i, p: (i, 0)),
            scratch_shapes=[pltpu.VMEM((2, tr, d), F32), pltpu.VMEM((2, tr, d), F32),
                            pltpu.SemaphoreType.DMA((2,))]),
        out_shape=jax.ShapeDtypeStruct((t, d), F32),
        compiler_params=_cparams(1), name="moe_combine",
    )(pos, ys, g_pad, res)


def moe_layer(h, g_ffn, w_router, b_router, w_gate, w_up, w_down, *, tm, tr, tn_up, tn_down, tr_gather,
              tr_combine):
    t, d = h.shape
    ne = w_gate.shape[0]
    hn, logits = router(h, g_ffn, w_router, b_router, tm=tm)
    top_v, top_e = lax.top_k(logits, TOP_K)
    gates = jax.nn.softmax(top_v, axis=-1)
    flat_e = top_e.reshape(-1)
    onehot = (flat_e[:, None] == jnp.arange(ne, dtype=flat_e.dtype)[None, :]).astype(jnp.int32)
    rank = jnp.sum((jnp.cumsum(onehot, axis=0) - onehot) * onehot, axis=1)
    sizes = jnp.sum(onehot, axis=0)
    tiles_per = (sizes + tr - 1) // tr
    tile_end = jnp.cumsum(tiles_per)
    tile_start = tile_end - tiles_per
    n_used = tile_end[-1:].astype(jnp.int32)
    nt = (t * TOP_K) // tr + ne
    r_pad = nt * tr
    pos = (tile_start[flat_e] * tr + rank).astype(jnp.int32)
    src_tok = jnp.zeros((r_pad,), jnp.int32).at[pos].set(jnp.arange(t * TOP_K, dtype=jnp.int32) // TOP_K)
    tile_ids = jnp.minimum(jnp.arange(nt, dtype=jnp.int32), n_used[0] - 1)
    tile_expert = jnp.sum((tile_ids[:, None] >= tile_end[None, :]).astype(jnp.int32), axis=1).astype(jnp.int32)
    xs = gather_rows(hn, src_tok, n_used * (tr // tr_gather), tr=tr_gather)
    hid = moe_up(xs, w_gate, w_up, tile_expert, n_used, tr=tr, tn=tn_up)
    ys = moe_down(hid, w_down, tile_expert, n_used, tr=tr, tn=tn_down)
    return moe_combine(ys, pos.reshape(t, TOP_K).T.reshape(-1), gates, h, tr=tr_combine)


def _layer_weights(i, conv_w, conv_b, w_q, w_k, w_v, b_i, b_f, g_head, skip_a):
    nh = b_i.shape[1]
    pad = jnp.zeros((V7X_LANES - nh,), F32)
    b_if = jnp.concatenate([b_i[i].astype(F32), pad, b_f[i].astype(F32), pad]).reshape(1, 2 * V7X_LANES)
    da = conv_w.shape[-1]
    return dict(conv_w=conv_w[i], conv_b=conv_b[i].reshape(1, da), w_q=w_q[i], w_k=w_k[i], w_v=w_v[i],
                b_if=b_if, g_head=g_head[i].reshape(1, da), skip=skip_a[i].reshape(1, da))


def _tile_plan(n_rows, seq):
    tm = next(c for c in (832, 640, 512, 256, 128, 64, 32, 16) if n_rows % c == 0)
    tc = next(c for c in (320, 256, 128, 64, 32, 16, 8) if n_rows % c == 0)
    tm_big = 2 * tm if n_rows % (2 * tm) == 0 else tm
    return dict(tm=tm, tm_big=tm_big, tn=512, tn_down=256, mlstm_chunk=256, s5_chunk=256,
                moe_tr=384, moe_tn_up=512, moe_tn_down=512, gather_tr=384, combine_tr=tc, ple_tm=tm // 2,
                norm_tm=min(seq, 1024))


def kernel(x_prompt, x_sample, p_prompt, p_sample, state_mlstm_C, state_mlstm_n, state_mlstm_m, state_mlstm_conv,
           state_s5_re, state_s5_im, g_mix, w_in, conv_w, conv_b, w_q, w_k, w_v, b_i, b_f, g_head, skip_a, w_proj_a,
           s5_log_dt, s5_A_re, s5_A_im, s5_B_re, s5_B_im, s5_C_re, s5_C_im, s5_D, w_glu_b, w_out, g_ffn,
           w_ff_gate, w_ff_up, w_ff_down, w_router, b_router, w_moe_gate, w_moe_up, w_moe_down,
           g_ple, w_ple, w_pg, g_final):
    bsz, seq, d = x_prompt.shape
    bs = x_sample.shape[0]
    depth = g_mix.shape[0]
    nh = b_i.shape[1]
    d_a = conv_w.shape[-1]
    d_b = s5_D.shape[-1]
    n_p = bsz * seq
    t = n_p + bs
    tl = _tile_plan(t, seq)
    tm, tmb, tn = tl["tm"], tl["tm_big"], tl["tn"]

    h = jnp.concatenate([x_prompt.reshape(n_p, d), x_sample.reshape(bs, d)], axis=0).astype(F32)
    p_all = jnp.concatenate([p_prompt.reshape(depth, n_p, -1), p_sample.reshape(depth, bs, -1)], axis=1)

    col_ua, col_oa, col_ub = 0, 1, 2 * d_a // d_b
    col_ga = (2 * d_a + d_b) // tn
    col_gb = (2 * d_a + d_b + d) // tn
    col_if = (2 * d_a + d_b + 2 * d) // (2 * V7X_LANES)

    states = [[] for _ in range(11)]
    c_s_all = jnp.zeros(state_mlstm_C.shape, F32)
    for i in range(depth):
        lw = _layer_weights(i, conv_w, conv_b, w_q, w_k, w_v, b_i, b_f, g_head, skip_a)
        sp = _s5_params(dict(log_dt=s5_log_dt[i], a_re=s5_A_re[i], a_im=s5_A_im[i], b_re=s5_B_re[i],
                             b_im=s5_B_im[i], c_re=s5_C_re[i], c_im=s5_C_im[i], d_skip=s5_D[i]))
        proj = in_proj(h, g_mix[i], w_in, i, d_a, d_b, d, nh, tm=tmb, tn=tn)

        a_pre, c_p, n_pp, m_p, conv_p = mlstm_prompt(proj, col_ua, col_oa, col_if, t, bsz, seq, lw,
                                                     L=tl["mlstm_chunk"])
        a_pre, c_s_all, n_s, m_s, conv_s = mlstm_sample(proj, col_ua, col_oa, col_if, n_p, a_pre,
                                                        state_mlstm_C, i, c_s_all, state_mlstm_n[i].astype(F32),
                                                        state_mlstm_m[i].astype(F32),
                                                        state_mlstm_conv[i].astype(F32), lw)
        gy, sre_p, sim_p = s5_prompt(proj, col_ub, t, bsz, seq, sp, tc=tl["s5_chunk"])
        gy, sre_s, sim_s = s5_sample(proj, col_ub, n_p, gy, state_s5_re[i].astype(F32),
                                     state_s5_im[i].astype(F32), sp)
        b_gated = glu_gated_matmul(gy, w_glu_b, i, proj, col_gb, tm=tmb, tn=tn)
        mix = gated_add_matmul(a_pre, w_proj_a, i, proj, col_ga, b_gated, tm=tmb, tn=tn)
        h = matmul(mix, w_out, i, h, tm=tmb, tn=tn, name="out_proj")

        j = i // 2
        if i % 2 == 0:
            hid = norm_swiglu_up(h, g_ffn[i], w_ff_gate, w_ff_up, j, tm=tmb, tn=tn)
            h = matmul(hid, w_ff_down, j, h, tm=tmb, tn=tl["tn_down"], name="ffn_down")
        else:
            h = moe_layer(h, g_ffn[i], w_router[j], b_router[j], w_moe_gate[j], w_moe_up[j], w_moe_down[j],
                          tm=tm, tr=tl["moe_tr"], tn_up=tl["moe_tn_up"], tn_down=tl["moe_tn_down"],
                          tr_gather=tl["gather_tr"], tr_combine=tl["combine_tr"])
        h = ple_matmul(h, g_ple[i], p_all, w_pg, w_ple, i, tm=tl["ple_tm"], tn=tn)

        g_s, p_s = S5_STATE, sre_p.shape[-1] // S5_STATE
        new = [c_p, n_pp, m_p, conv_p, sre_p.reshape(bsz, p_s, g_s), sim_p.reshape(bsz, p_s, g_s),
               n_s, m_s, conv_s, sre_s.reshape(bs, p_s, g_s), sim_s.reshape(bs, p_s, g_s)]
        for lst, s in zip(states, new):
            lst.append(s)

    y_prompt, y_sample = final_norm_split(h, g_final, n_p, tm=tl["norm_tm"])
    y_prompt = y_prompt.reshape(bsz, seq, d)
    y_sample = y_sample.reshape(bs, 1, d)
    st = [jnp.stack(lst) for lst in states]
    return (y_prompt, y_sample) + tuple(st[:6]) + (c_s_all,) + tuple(st[6:])
```

```python
import functools
import math

import numpy as np
import jax
import jax.numpy as jnp
from jax import lax
from jax.experimental import pallas as pl
from jax.experimental.pallas import tpu as pltpu

F32 = jnp.float32
BF16 = jnp.bfloat16
EPS = 1e-6

V7X_VMEM_BYTES = 64 * 1024 * 1024
V7X_LANES = 128
V7X_SUBLANES = 8
VMEM_LIMIT = 56 * 1024 * 1024

N_HEADS = 4
CONV_W = 4
S5_GROUP = 16
S5_STATE = 64
N_EXPERTS = 8
TOP_K = 2


def _cparams(n_axes, vmem=VMEM_LIMIT):
    return pltpu.CompilerParams(dimension_semantics=("arbitrary",) * n_axes, vmem_limit_bytes=vmem)


def _sigmoid(x):
    return 1.0 / (1.0 + jnp.exp(-x))


def _silu(x):
    return x * _sigmoid(x)


def _gelu_tanh(x):
    return 0.5 * x * (1.0 + jnp.tanh(math.sqrt(2.0 / math.pi) * (x + 0.044715 * (x * x * x))))


def _log_sigmoid(x):
    return jnp.minimum(x, 0.0) - jnp.log(1.0 + jnp.exp(-jnp.abs(x)))


def _dot(a, b):
    return jnp.dot(a, b, preferred_element_type=F32)


def _dot_nt(a, b):
    return lax.dot_general(a, b, (((1,), (1,)), ((), ())), preferred_element_type=F32)


def _dot_tn(a, b):
    return lax.dot_general(a, b, (((0,), (0,)), ((), ())), preferred_element_type=F32)


def _rmsnorm_rows(x, g):
    ms = jnp.mean(x * x, axis=-1, keepdims=True)
    return x * lax.rsqrt(ms + EPS) * g


def _row_tile_buffering(tm, k, dtype):
    two_copies = 2 * tm * k * jnp.dtype(dtype).itemsize
    return pl.Buffered(1) if two_copies > VMEM_LIMIT // 4 else None


def _row_chunks(tm):
    for rc in (256, 208, 128, 104, 64, 32, 16, 8):
        if tm % rc == 0:
            return rc
    return tm


def _norm_to_scratch(x_ref, g_ref, xn_ref, tm):
    rc = _row_chunks(tm)

    def body(r, c):
        rows = pl.ds(pl.multiple_of(r * rc, rc), rc)
        xn_ref[rows, :] = _rmsnorm_rows(x_ref[rows, :], g_ref[...]).astype(BF16)
        return c

    lax.fori_loop(0, tm // rc, body, 0)


def _wspec(w, layer, tn, col=lambda j: j):
    return pl.BlockSpec((None, w.shape[1], tn), lambda i, j: (layer, 0, col(j)))


def _k_in_proj(x_ref, g_ref, wa_ref, wb_ref, wif_ref, o_ref, xn_ref, *, tm, n_head, n_main, nh):
    j = pl.program_id(1)

    @pl.when(j == 0)
    def _():
        _norm_to_scratch(x_ref, g_ref, xn_ref, tm)

    @pl.when(j < n_head)
    def _():
        o_ref[...] = _dot_nt(xn_ref[...], wa_ref[0].astype(BF16))

    @pl.when(jnp.logical_and(j >= n_head, j < n_main))
    def _():
        o_ref[...] = _dot_nt(xn_ref[...], wb_ref[0].astype(BF16))

    @pl.when(j == n_main)
    def _():
        pre = _dot_nt(xn_ref[...], wif_ref[0].astype(BF16))
        o_ref[...] = jnp.zeros_like(o_ref)
        o_ref[:, 0:nh] = pre[:, 0:nh]
        o_ref[:, V7X_LANES:V7X_LANES + nh] = pre[:, nh:2 * nh]


def in_proj(x, g, w_in, layer, d_a, d_b, d, nh, *, tm, tn):
    t, k = x.shape
    assert (2 * nh) % V7X_SUBLANES == 0 and (2 * d_a) % tn == 0 and (d_b + 2 * d) % tn == 0
    wt = jnp.swapaxes(w_in, 1, 2)
    n_head = 2 * d_a // tn
    n_main = n_head + (d_b + 2 * d) // tn
    if_row = 2 * d_a

    def rows(nrows, start):
        return pl.BlockSpec((pl.Element(1), pl.Element(nrows), pl.Element(k)), lambda i, j: (layer, start(j), 0))

    return pl.pallas_call(
        functools.partial(_k_in_proj, tm=tm, n_head=n_head, n_main=n_main, nh=nh),
        grid=(t // tm, n_main + 1),
        in_specs=[pl.BlockSpec((tm, k), lambda i, j: (i, 0), pipeline_mode=_row_tile_buffering(tm, k, x.dtype)),
                  pl.BlockSpec((1, k), lambda i, j: (0, 0)),
                  rows(tn, lambda j: jnp.minimum(j, n_head - 1) * tn),
                  rows(tn, lambda j: (jnp.clip(j, n_head, n_main - 1) * (tn // V7X_SUBLANES)
                                      + 2 * nh // V7X_SUBLANES) * V7X_SUBLANES),
                  rows(2 * nh, lambda j: if_row)],
        out_specs=pl.BlockSpec((tm, tn), lambda i, j: (i, j)),
        out_shape=jax.ShapeDtypeStruct((t, (n_main + 1) * tn), F32),
        scratch_shapes=[pltpu.VMEM((tm, k), BF16)],
        compiler_params=_cparams(2), name="in_proj",
    )(x, g.reshape(1, k), wt, wt, wt)


def _k_mm_res(x_ref, w_ref, r_ref, o_ref):
    o_ref[...] = r_ref[...] + _dot(x_ref[...], w_ref[...].astype(BF16))


def _k_mm(x_ref, w_ref, o_ref):
    o_ref[...] = _dot(x_ref[...], w_ref[...].astype(BF16)).astype(o_ref.dtype)


def matmul(x, w, layer, res=None, *, tm, tn, out_dtype=F32, name="mm"):
    t, k = x.shape
    n = w.shape[2]
    in_specs = [pl.BlockSpec((tm, k), lambda i, j: (i, 0), pipeline_mode=_row_tile_buffering(tm, k, x.dtype)),
                _wspec(w, layer, tn)]
    args = [x, w]
    body = _k_mm
    if res is not None:
        in_specs.append(pl.BlockSpec((tm, tn), lambda i, j: (i, j)))
        args.append(res)
        body = _k_mm_res
    return pl.pallas_call(
        body, grid=(t // tm, pl.cdiv(n, tn)), in_specs=in_specs,
        out_specs=pl.BlockSpec((tm, tn), lambda i, j: (i, j)),
        out_shape=jax.ShapeDtypeStruct((t, n), out_dtype),
        compiler_params=_cparams(2), name=name,
    )(*args)


def _k_branch_mix(a_ref, gy_ref, wp_ref, wv_ref, wg_ref, ga_ref, gb_ref, o_ref):
    a = a_ref[...]
    gy = gy_ref[...]
    a_out = _dot(a, wp_ref[...].astype(BF16))
    val = _dot(gy, wv_ref[...].astype(BF16))
    gate = _dot(gy, wg_ref[...].astype(BF16))
    b_out = val * _sigmoid(gate)
    o_ref[...] = (_sigmoid(ga_ref[...]) * a_out + _sigmoid(gb_ref[...]) * b_out).astype(o_ref.dtype)


def branch_mix(a_pre, gy, w_proj_a, w_glu_b, layer, proj, col_ga, col_gb, *, tm, tn):
    t, k = a_pre.shape
    n = w_proj_a.shape[2]
    nj = n // tn
    rows = pl.BlockSpec((tm, k), lambda i, j: (i, 0), pipeline_mode=pl.Buffered(1))
    return pl.pallas_call(
        _k_branch_mix, grid=(t // tm, nj),
        in_specs=[rows, rows,
                  _wspec(w_proj_a, layer, tn), _wspec(w_glu_b, layer, tn),
                  _wspec(w_glu_b, layer, tn, lambda j: j + nj),
                  pl.BlockSpec((tm, tn), lambda i, j: (i, col_ga + j)),
                  pl.BlockSpec((tm, tn), lambda i, j: (i, col_gb + j))],
        out_specs=pl.BlockSpec((tm, tn), lambda i, j: (i, j)),
        out_shape=jax.ShapeDtypeStruct((t, n), BF16),
        compiler_params=_cparams(2), name="branch_mix",
    )(a_pre, gy, w_proj_a, w_glu_b, w_glu_b, proj, proj)


def _k_swiglu_up(x_ref, g_ref, wg_ref, wu_ref, o_ref, xn_ref, *, tm):
    @pl.when(pl.program_id(1) == 0)
    def _():
        _norm_to_scratch(x_ref, g_ref, xn_ref, tm)

    xn = xn_ref[...]
    gate = _dot(xn, wg_ref[...].astype(BF16))
    up = _dot(xn, wu_ref[...].astype(BF16))
    o_ref[...] = (_silu(gate) * up).astype(o_ref.dtype)


def norm_swiglu_up(x, g, w_gate, w_up, layer, *, tm, tn, name="ffn_up"):
    t, k = x.shape
    n = w_gate.shape[2]
    return pl.pallas_call(
        functools.partial(_k_swiglu_up, tm=tm),
        grid=(t // tm, pl.cdiv(n, tn)),
        in_specs=[pl.BlockSpec((tm, k), lambda i, j: (i, 0), pipeline_mode=_row_tile_buffering(tm, k, x.dtype)),
                  pl.BlockSpec((1, k), lambda i, j: (0, 0)),
                  _wspec(w_gate, layer, tn), _wspec(w_up, layer, tn)],
        out_specs=pl.BlockSpec((tm, tn), lambda i, j: (i, j)),
        out_shape=jax.ShapeDtypeStruct((t, n), BF16),
        scratch_shapes=[pltpu.VMEM((tm, k), BF16)],
        compiler_params=_cparams(2), name=name,
    )(x, g.reshape(1, k), w_gate, w_up)


def _cast_rows_to(dst_ref, src_ref, rc):
    def body(r, c):
        rows = pl.ds(pl.multiple_of(r * rc, rc), rc)
        dst_ref[rows, :] = src_ref[rows, :].astype(dst_ref.dtype)
        return c

    lax.fori_loop(0, src_ref.shape[0] // rc, body, 0)


def _k_ple(x_ref, g_ref, p_ref, wpg_ref, wple_ref, o_ref, wpg_bf, wple_bf, *, tn):
    @pl.when(pl.program_id(0) == 0)
    def _():
        _cast_rows_to(wpg_bf, wpg_ref, V7X_LANES)
        _cast_rows_to(wple_bf, wple_ref, V7X_LANES)

    x = x_ref[...]
    xn = _rmsnorm_rows(x, g_ref[...]).astype(BF16)
    pb = p_ref[...].astype(BF16)
    for c in range(x.shape[1] // tn):
        cols = slice(c * tn, (c + 1) * tn)
        gate = _dot(xn, wpg_bf[:, cols])
        emb = _dot(pb, wple_bf[:, cols])
        o_ref[:, cols] = x[:, cols] + emb * _sigmoid(gate)


def ple_matmul(x, g, p, w_pg, w_ple, layer, *, tm, tn, name="ple"):
    t, k = x.shape
    kp = p.shape[2]
    resident = lambda w: pl.BlockSpec((None,) + w.shape[1:], lambda i: (layer, 0, 0), pipeline_mode=pl.Buffered(1))
    return pl.pallas_call(
        functools.partial(_k_ple, tn=tn),
        grid=(t // tm,),
        in_specs=[pl.BlockSpec((tm, k), lambda i: (i, 0)),
                  pl.BlockSpec((1, k), lambda i: (0, 0)),
                  pl.BlockSpec((None, tm, kp), lambda i: (layer, i, 0)),
                  resident(w_pg), resident(w_ple)],
        out_specs=pl.BlockSpec((tm, k), lambda i: (i, 0)),
        out_shape=jax.ShapeDtypeStruct((t, k), F32),
        scratch_shapes=[pltpu.VMEM(w_pg.shape[1:], BF16), pltpu.VMEM(w_ple.shape[1:], BF16)],
        compiler_params=_cparams(1), name=name,
    )(x, g.reshape(1, k), p, w_pg, w_ple)


def _k_final_norm(xp_ref, xs_ref, g_ref, op_ref, os_ref, *, n_prompt_tiles):
    i = pl.program_id(0)

    @pl.when(i < n_prompt_tiles)
    def _():
        op_ref[...] = _rmsnorm_rows(xp_ref[...], g_ref[...])

    @pl.when(i == n_prompt_tiles)
    def _():
        os_ref[...] = _rmsnorm_rows(xs_ref[...], g_ref[...])


def final_norm_split(x, g, n_prompt, *, tm):
    t, k = x.shape
    bs = t - n_prompt
    npt = n_prompt // tm
    last = npt - 1
    return pl.pallas_call(
        functools.partial(_k_final_norm, n_prompt_tiles=npt), grid=(npt + 1,),
        in_specs=[pl.BlockSpec((tm, k), lambda i: (jnp.minimum(i, last), 0)),
                  pl.BlockSpec((bs, k), lambda i: (n_prompt // bs, 0)),
                  pl.BlockSpec((1, k), lambda i: (0, 0))],
        out_specs=[pl.BlockSpec((tm, k), lambda i: (jnp.minimum(i, last), 0)),
                   pl.BlockSpec((bs, k), lambda i: (0, 0))],
        out_shape=[jax.ShapeDtypeStruct((n_prompt, k), F32), jax.ShapeDtypeStruct((bs, k), F32)],
        compiler_params=_cparams(1), name="final_norm",
    )(x, x, g.reshape(1, k))


def _split_hi_lo(x):
    hi = x.astype(BF16)
    lo = (x - hi.astype(F32)).astype(BF16)
    return hi, lo


def _k_mlstm_prompt(u_ref, o_ref_in, if_ref, cw_ref, cb_ref, wq_ref, wk_ref, wv_ref, bif_ref, gh_ref, sk_ref,
                    tri_ref, a_init, a_ref, c_out, n_out, m_out, conv_out, upad_ref, *, L, dh):
    del a_init
    c_idx = pl.program_id(1)
    nh = N_HEADS

    @pl.when(c_idx == 0)
    def _():
        c_out[...] = jnp.zeros_like(c_out)
        n_out[...] = jnp.zeros_like(n_out)
        m_out[...] = jnp.zeros_like(m_out)
        upad_ref[pl.ds(0, 8), :] = jnp.zeros((8, nh * dh), F32)

    @pl.when(c_idx > 0)
    def _():
        upad_ref[pl.ds(0, 8), :] = upad_ref[pl.ds(L, 8), :]

    u = u_ref[...]
    upad_ref[pl.ds(8, L), :] = u
    conv = cb_ref[...] + u * cw_ref[CONV_W - 1:CONV_W, :]
    for j in range(CONV_W - 1):
        conv = conv + upad_ref[pl.ds(8 - (CONV_W - 1) + j, L), :] * cw_ref[j:j + 1, :]
    cact = _silu(conv)
    conv_out[0] = upad_ref[pl.ds(L, 8), :]

    pre = if_ref[...]
    li = pre[:, :V7X_LANES] + bif_ref[:, :V7X_LANES]
    lf = _log_sigmoid(pre[:, V7X_LANES:] + bif_ref[:, V7X_LANES:])
    tri = tri_ref[...]
    lf_hi, lf_mid = _split_hi_lo(lf)
    lf_lo = (lf - lf_hi.astype(F32) - lf_mid.astype(F32)).astype(BF16)
    bcum = _dot(tri, lf_hi) + _dot(tri, lf_mid) + _dot(tri, lf_lo)
    li_t = li.T
    b_t = bcum.T
    row_id = lax.broadcasted_iota(jnp.int32, (L, L), 0)
    col_id = lax.broadcasted_iota(jnp.int32, (L, L), 1)
    causal = col_id <= row_id
    lane = lax.broadcasted_iota(jnp.int32, (1, V7X_LANES), 1)
    m_row = m_out[0]
    m_new_row = m_row

    for h in range(nh):
        hs = slice(h * dh, (h + 1) * dh)
        ch = cact[:, hs].astype(BF16)
        uh = u[:, hs].astype(BF16)
        q = _dot(ch, wq_ref[h].astype(BF16))
        k = _dot(ch, wk_ref[h].astype(BF16)) * (dh ** -0.5)
        v = _dot(uh, wv_ref[h].astype(BF16))
        qb, kb, vb = q.astype(BF16), k.astype(BF16), v.astype(BF16)

        b_col = bcum[:, h:h + 1]
        li_col = li[:, h:h + 1]
        r_row = li_t[h:h + 1, :] - b_t[h:h + 1, :]
        m_prev = m_row[:, h:h + 1]
        d = jnp.where(causal, b_col + r_row, -jnp.inf)
        inter = b_col + m_prev
        m_t = jnp.maximum(inter, jnp.max(d, axis=-1, keepdims=True))
        w_inter = jnp.exp(inter - m_t)
        s = _dot_nt(qb, kb) * jnp.exp(d - m_t)
        c_prev = c_out[0, h]
        n_prev = n_out[0, h:h + 1, :]
        num = w_inter * _dot(qb, c_prev.astype(BF16)) + _dot(s.astype(BF16), vb)
        den = w_inter * jnp.sum(q * n_prev, axis=-1, keepdims=True) + jnp.sum(s, axis=-1, keepdims=True)
        hh = num / jnp.maximum(jnp.abs(den), jnp.exp(-m_t))

        b_last = b_col[L - 1:L, :]
        g_col = b_last - b_col + li_col
        m_new = jnp.maximum(b_last + m_prev, jnp.max(g_col, axis=0, keepdims=True))
        decay = jnp.exp(b_last + m_prev - m_new)
        wk_ = jnp.exp(g_col - m_new) * k
        c_out[0, h] = decay * c_prev + _dot_tn(wk_.astype(BF16), vb)
        n_out[0, h:h + 1, :] = decay * n_prev + jnp.sum(wk_, axis=0, keepdims=True)
        m_new_row = jnp.where(lane == h, m_new, m_new_row)

        hn = _rmsnorm_rows(hh, gh_ref[:, hs])
        gated = (hn + sk_ref[:, hs] * cact[:, hs]) * _sigmoid(o_ref_in[:, hs])
        a_ref[:, hs] = gated.astype(a_ref.dtype)

    m_out[0] = m_new_row


def mlstm_prompt(proj, col_u, col_o, col_if, n_rows, bsz, seq, lw, *, L):
    dh = lw["w_q"].shape[-1]
    da = N_HEADS * dh
    nc = seq // L
    tri = jnp.asarray(np.tril(np.ones((L, L), np.float32)), BF16)
    row_blk = lambda b, c: b * nc + c
    full = lambda *shape: pl.BlockSpec(shape, lambda b, c: (0,) * len(shape))
    outs = pl.pallas_call(
        functools.partial(_k_mlstm_prompt, L=L, dh=dh),
        grid=(bsz, nc),
        in_specs=[pl.BlockSpec((L, da), lambda b, c: (row_blk(b, c), col_u)),
                  pl.BlockSpec((L, da), lambda b, c: (row_blk(b, c), col_o)),
                  pl.BlockSpec((L, 2 * V7X_LANES), lambda b, c: (row_blk(b, c), col_if)),
                  full(CONV_W, da), full(1, da), full(N_HEADS, dh, dh), full(N_HEADS, dh, dh),
                  full(N_HEADS, dh, dh), full(1, 2 * V7X_LANES), full(1, da), full(1, da), full(L, L),
                  pl.BlockSpec(memory_space=pl.ANY)],
        out_specs=[pl.BlockSpec((L, da), lambda b, c: (row_blk(b, c), 0)),
                   pl.BlockSpec((1, N_HEADS, dh, dh), lambda b, c: (b, 0, 0, 0)),
                   pl.BlockSpec((1, N_HEADS, dh), lambda b, c: (b, 0, 0)),
                   pl.BlockSpec((1, 1, V7X_LANES), lambda b, c: (b, 0, 0)),
                   pl.BlockSpec((1, 8, da), lambda b, c: (b, 0, 0))],
        out_shape=[jax.ShapeDtypeStruct((n_rows, da), BF16),
                   jax.ShapeDtypeStruct((bsz, N_HEADS, dh, dh), F32),
                   jax.ShapeDtypeStruct((bsz, N_HEADS, dh), F32),
                   jax.ShapeDtypeStruct((bsz, 1, V7X_LANES), F32),
                   jax.ShapeDtypeStruct((bsz, 8, da), F32)],
        scratch_shapes=[pltpu.VMEM((L + 8, da), F32)],
        input_output_aliases={12: 0},
        compiler_params=_cparams(2), name="mlstm_prompt",
    )(proj, proj, proj, lw["conv_w"], lw["conv_b"], lw["w_q"], lw["w_k"], lw["w_v"], lw["b_if"],
      lw["g_head"], lw["skip"], tri, jnp.zeros((n_rows, da), BF16))
    a_pre, c_p, n_p, m_p, conv_p = outs
    return a_pre, c_p, n_p, m_p[:, 0, :N_HEADS], conv_p[:, 8 - (CONV_W - 1):, :]


def _k_mlstm_sample_pre(u_ref, if_ref, conv_ref, m_ref, cw_ref, cb_ref, wq_ref, wk_ref, wv_ref, bif_ref,
                        q_out, k_out, v_out, c_out, gates_out, conv_out, *, dh):
    nh = N_HEADS
    u = u_ref[...]
    conv = cb_ref[...] + u * cw_ref[CONV_W - 1:CONV_W, :]
    for j in range(CONV_W - 1):
        conv = conv + conv_ref[j] * cw_ref[j:j + 1, :]
        if j > 0:
            conv_out[j - 1] = conv_ref[j]
    conv_out[CONV_W - 2] = u
    cact = _silu(conv)
    c_out[...] = cact
    for h in range(nh):
        hs = slice(h * dh, (h + 1) * dh)
        ch = cact[:, hs].astype(BF16)
        q_out[:, hs] = _dot(ch, wq_ref[h].astype(BF16))
        k_out[:, hs] = _dot(ch, wk_ref[h].astype(BF16)) * (dh ** -0.5)
        v_out[:, hs] = _dot(u[:, hs].astype(BF16), wv_ref[h].astype(BF16))
    pre = if_ref[...]
    li = pre[:, :V7X_LANES] + bif_ref[:, :V7X_LANES]
    lf = _log_sigmoid(pre[:, V7X_LANES:] + bif_ref[:, V7X_LANES:])
    m_prev = m_ref[...]
    inter = lf + m_prev
    m_t = jnp.maximum(inter, li)
    gates_out[0] = jnp.exp(inter - m_t)
    gates_out[1] = jnp.exp(li - m_t)
    gates_out[2] = jnp.exp(-m_t)
    gates_out[3] = m_t


def _k_mlstm_sample_step(q_ref, k_ref, v_ref, gates_ref, c_ref, n_ref, cact_ref, o_ref_in, gh_ref, sk_ref,
                         a_any, c_any, c_out, n_out, a_out, hh_ref, *, bt, dh):
    del a_any, c_any
    i = pl.program_id(0)
    nh = N_HEADS
    q = q_ref[...]
    k = k_ref[...]
    v = v_ref[...]
    w_inter = gates_ref[0]
    w_new = gates_ref[1]
    e_neg_m = gates_ref[2]
    n_prev = n_ref[...]
    rows = pl.ds(pl.multiple_of(i * bt, bt), bt)
    for h in range(nh):
        hs = slice(h * dh, (h + 1) * dh)
        qh, kh, vh, nh_prev = q[:, hs], k[:, hs], v[:, hs], n_prev[:, hs]
        q_t = qh.T
        k_t = kh.T
        wi = w_inter[:, h:h + 1]
        wn = w_new[:, h:h + 1]
        s = jnp.sum(qh * kh, axis=-1, keepdims=True) * wn
        den = wi * jnp.sum(qh * nh_prev, axis=-1, keepdims=True) + s
        wv = wn * vh
        qc_rows = []
        for bl in range(bt):
            c_prev = c_ref[bl, h]
            qc_rows.append(jnp.sum(q_t[:, bl:bl + 1] * c_prev, axis=0, keepdims=True))
            c_out[bl, h] = wi[bl:bl + 1, :] * c_prev + k_t[:, bl:bl + 1] * wv[bl:bl + 1, :]
        qc = jnp.concatenate(qc_rows, axis=0)
        num = wi * qc + s * vh
        hh = num / jnp.maximum(jnp.abs(den), e_neg_m[:, h:h + 1])
        hh_ref[rows, hs] = hh
        n_out[:, hs] = wi * nh_prev + wn * kh

    @pl.when(i == pl.num_programs(0) - 1)
    def _():
        for h in range(nh):
            hs = slice(h * dh, (h + 1) * dh)
            hn = _rmsnorm_rows(hh_ref[:, hs], gh_ref[:, hs])
            a_out[:, hs] = ((hn + sk_ref[:, hs] * cact_ref[:, hs]) * _sigmoid(o_ref_in[:, hs])).astype(a_out.dtype)


def mlstm_sample(proj, col_u, col_o, col_if, row0, a_pre_all, c_all, layer, c_new_all, n0, m0, conv0, lw, *,
                 bt=8):
    _, bs, nh, dh, _ = c_all.shape
    da = nh * dh
    rb = row0 // bs
    m_pad = jnp.pad(m0, ((0, 0), (0, V7X_LANES - nh)))
    conv_t = jnp.transpose(conv0, (1, 0, 2))
    full = lambda *shape: pl.BlockSpec(shape, lambda i: (0,) * len(shape))
    q, k, v, cact, gates, conv_new = pl.pallas_call(
        functools.partial(_k_mlstm_sample_pre, dh=dh),
        grid=(1,),
        in_specs=[pl.BlockSpec((bs, da), lambda i: (rb, col_u)),
                  pl.BlockSpec((bs, 2 * V7X_LANES), lambda i: (rb, col_if)),
                  full(CONV_W - 1, bs, da), full(bs, V7X_LANES), full(CONV_W, da), full(1, da),
                  full(nh, dh, dh), full(nh, dh, dh), full(nh, dh, dh), full(1, 2 * V7X_LANES)],
        out_specs=[full(bs, da), full(bs, da), full(bs, da), full(bs, da), full(4, bs, V7X_LANES),
                   full(CONV_W - 1, bs, da)],
        out_shape=[jax.ShapeDtypeStruct((bs, da), F32)] * 4
                  + [jax.ShapeDtypeStruct((4, bs, V7X_LANES), F32),
                     jax.ShapeDtypeStruct((CONV_W - 1, bs, da), F32)],
        compiler_params=_cparams(1), name="mlstm_sample_pre",
    )(proj, proj, conv_t, m_pad, lw["conv_w"], lw["conv_b"], lw["w_q"], lw["w_k"], lw["w_v"], lw["b_if"])

    blk = lambda *shape: pl.BlockSpec(shape, lambda i: (i,) + (0,) * (len(shape) - 1))
    cst = lambda *shape: pl.BlockSpec(shape, lambda i: (0,) * len(shape))
    c_blk = pl.BlockSpec((None, bt, nh, dh, dh), lambda i: (layer, i, 0, 0, 0))
    c_new_all, n_new, a_pre_all = pl.pallas_call(
        functools.partial(_k_mlstm_sample_step, bt=bt, dh=dh),
        grid=(bs // bt,),
        in_specs=[blk(bt, da), blk(bt, da), blk(bt, da),
                  pl.BlockSpec((4, bt, V7X_LANES), lambda i: (0, i, 0)),
                  c_blk, blk(bt, da), cst(bs, da),
                  pl.BlockSpec((bs, da), lambda i: (rb, col_o)), cst(1, da), cst(1, da),
                  pl.BlockSpec(memory_space=pl.ANY), pl.BlockSpec(memory_space=pl.ANY)],
        out_specs=[c_blk, blk(bt, da), pl.BlockSpec((bs, da), lambda i: (rb, 0))],
        out_shape=[jax.ShapeDtypeStruct(c_new_all.shape, F32), jax.ShapeDtypeStruct((bs, da), F32),
                   jax.ShapeDtypeStruct(a_pre_all.shape, a_pre_all.dtype)],
        scratch_shapes=[pltpu.VMEM((bs, da), F32)],
        input_output_aliases={10: 2, 11: 0},
        compiler_params=_cparams(1), name="mlstm_sample_step",
    )(q, k, v, gates, c_all, n0.reshape(bs, da), cact, proj, lw["g_head"], lw["skip"], a_pre_all, c_new_all)
    m_new = gates[3][:, :nh]
    return a_pre_all, c_new_all, n_new.reshape(bs, nh, dh), m_new, jnp.transpose(conv_new, (1, 0, 2))


S5_SLAB_GROUPS = V7X_LANES // S5_GROUP
S5_SLAB_STATES = S5_SLAB_GROUPS * S5_STATE


def _s5_params(lp):
    g, p = lp["a_re"].shape
    dt = jnp.exp(lp["log_dt"].astype(F32))[:, None]
    a_re = lp["a_re"].astype(F32)
    a_im = lp["a_im"].astype(F32)
    lam_re = a_re * dt
    lam_im = a_im * dt
    mag = jnp.exp(lam_re)
    ab_re = mag * jnp.cos(lam_im)
    ab_im = mag * jnp.sin(lam_im)
    den = a_re * a_re + a_im * a_im
    nr = ab_re - 1.0
    ni = ab_im
    k_re = (nr * a_re + ni * a_im) / den
    k_im = (ni * a_re - nr * a_im) / den
    b_re = lp["b_re"].astype(F32)
    b_im = lp["b_im"].astype(F32)
    bb_re = k_re[..., None] * b_re - k_im[..., None] * b_im
    bb_im = k_re[..., None] * b_im + k_im[..., None] * b_re
    ns = g // S5_SLAB_GROUPS
    eye = jnp.eye(S5_SLAB_GROUPS, dtype=F32)

    def in_blockdiag(bb):
        bs = bb.reshape(ns, S5_SLAB_GROUPS, p, S5_GROUP)
        w = jnp.einsum("ab,sapc->sacbp", eye, bs)
        return w.reshape(ns, V7X_LANES, S5_SLAB_STATES).astype(BF16)

    def out_blockdiag(cc):
        cs = cc.astype(F32).reshape(ns, S5_SLAB_GROUPS, S5_GROUP, p)
        w = jnp.einsum("ab,sacp->sapbc", eye, cs)
        return w.reshape(ns, S5_SLAB_STATES, V7X_LANES)

    wc = jnp.concatenate([out_blockdiag(lp["c_re"]), -out_blockdiag(lp["c_im"])], axis=1).astype(BF16)
    return dict(lam_re=lam_re.reshape(1, g * p), lam_im=lam_im.reshape(1, g * p),
                ab_re=ab_re.reshape(1, g * p), ab_im=ab_im.reshape(1, g * p),
                wb_re=in_blockdiag(bb_re), wb_im=in_blockdiag(bb_im), wc=wc,
                d_skip=lp["d_skip"].astype(F32).reshape(1, g * S5_GROUP))


def _s5_powers(sp, ks):
    kk = jnp.asarray(ks, F32)[:, None]
    mag = jnp.exp(kk * sp["lam_re"])
    return mag * jnp.cos(kk * sp["lam_im"]), mag * jnp.sin(kk * sp["lam_im"])


def _s5_input_proj(u_bf, wbr_ref, wbi_ref, bur_ref, bui_ref):
    ns = wbr_ref.shape[0]
    for s in range(ns):
        us = u_bf[:, s * V7X_LANES:(s + 1) * V7X_LANES]
        cols = slice(s * S5_SLAB_STATES, (s + 1) * S5_SLAB_STATES)
        bur_ref[:, cols] = _dot(us, wbr_ref[s])
        bui_ref[:, cols] = _dot(us, wbi_ref[s])


def _s5_output_proj(xr_ref, xi_ref, wc_ref, skip):
    ns = wc_ref.shape[0]
    outs = []
    for s in range(ns):
        cols = slice(s * S5_SLAB_STATES, (s + 1) * S5_SLAB_STATES)
        xcat = jnp.concatenate([xr_ref[:, cols].astype(BF16), xi_ref[:, cols].astype(BF16)], axis=1)
        y = _dot(xcat, wc_ref[s]) + skip[:, s * V7X_LANES:(s + 1) * V7X_LANES]
        outs.append(_gelu_tanh(y).astype(BF16))
    return jnp.concatenate(outs, axis=1)


def _k_s5_prompt(u_ref, wbr_ref, wbi_ref, wc_ref, d_ref, abr_ref, abi_ref, tpr_ref, tpi_ref, ajr_ref, aji_ref,
                 perm_ref, permt_ref, gy_init, gy_ref, sre_out, sim_out,
                 bur0, bur1, bui0, bui1, xb0, xb1, *, tc):
    del gy_init
    c_idx = pl.program_id(1)
    nsteps = tc // V7X_SUBLANES
    ns = wbr_ref.shape[0]
    lw = S5_SLAB_STATES
    pair = 2 * V7X_SUBLANES
    bur, bui, xb = (bur0, bur1), (bui0, bui1), (xb0, xb1)

    @pl.when(c_idx == 0)
    def _():
        sre_out[...] = jnp.zeros_like(sre_out)
        sim_out[...] = jnp.zeros_like(sim_out)

    u = u_ref[...]
    u_hi, u_lo = _split_hi_lo(u)
    perm = perm_ref[...]
    up_hi = _dot(perm, u_hi)
    skip = d_ref[...] * (up_hi + _dot(perm, u_lo))
    up_bf = up_hi.astype(BF16)

    sub = lax.broadcasted_iota(jnp.int32, (V7X_SUBLANES, lw), 0)

    def input_proj(s):
        us = up_bf[:, s * V7X_LANES:(s + 1) * V7X_LANES]
        bur[s % 2][...] = _dot(us, wbr_ref[s])
        bui[s % 2][...] = _dot(us, wbi_ref[s])

    def scan(s):
        br, bi, xo = bur[s % 2], bui[s % 2], xb[s % 2]
        lanes = slice(s * lw, (s + 1) * lw)
        ar = jnp.broadcast_to(abr_ref[:, lanes], (V7X_SUBLANES, lw))
        ai = jnp.broadcast_to(abi_ref[:, lanes], (V7X_SUBLANES, lw))
        er = jnp.zeros((V7X_SUBLANES, lw), F32)
        ei = er
        for i in range(nsteps):
            rows = slice(i * V7X_SUBLANES, (i + 1) * V7X_SUBLANES)
            er, ei = (ar * er - ai * ei + br[rows, :], ar * ei + ai * er + bi[rows, :])
            br[rows, :] = er
            bi[rows, :] = ei
        for d, row in ((1, 0), (2, 1), (4, 3)):
            pr = ajr_ref[row:row + 1, lanes]
            pi = aji_ref[row:row + 1, lanes]
            sr = pltpu.roll(er, d, 0)
            si = pltpu.roll(ei, d, 0)
            keep = sub >= d
            er, ei = (er + jnp.where(keep, pr * sr - pi * si, 0.0),
                      ei + jnp.where(keep, pr * si + pi * sr, 0.0))
        c0r = jnp.broadcast_to(sre_out[0, :, lanes], (V7X_SUBLANES, lw))
        c0i = jnp.broadcast_to(sim_out[0, :, lanes], (V7X_SUBLANES, lw))
        ajr = ajr_ref[:, lanes]
        aji = aji_ref[:, lanes]
        fr = ajr * c0r - aji * c0i + er
        fi = ajr * c0i + aji * c0r + ei
        cin_r = jnp.where(sub >= 1, pltpu.roll(fr, 1, 0), c0r)
        cin_i = jnp.where(sub >= 1, pltpu.roll(fi, 1, 0), c0i)
        sre_out[0, :, lanes] = fr[V7X_SUBLANES - 1:V7X_SUBLANES, :]
        sim_out[0, :, lanes] = fi[V7X_SUBLANES - 1:V7X_SUBLANES, :]

        cin2_r = jnp.concatenate([cin_r, cin_r], axis=0)
        cin2_i = jnp.concatenate([cin_i, cin_i], axis=0)
        for k in range(tc // pair):
            rows = slice(k * pair, (k + 1) * pair)
            pr = tpr_ref[rows, lanes]
            pi = tpi_ref[rows, lanes]
            xo[rows, :lw] = (br[rows, :] + (pr * cin2_r - pi * cin2_i)).astype(BF16)
            xo[rows, lw:] = (bi[rows, :] + (pr * cin2_i + pi * cin2_r)).astype(BF16)

    def output_proj(s):
        y = _dot(xb[s % 2][...], wc_ref[s]) + skip[:, s * V7X_LANES:(s + 1) * V7X_LANES]
        return _gelu_tanh(y).astype(BF16)

    outs = []
    input_proj(0)
    for s in range(ns):
        if s + 1 < ns:
            input_proj(s + 1)
        scan(s)
        outs.append(output_proj(s))
    g_perm = jnp.concatenate(outs, axis=1)
    gy_ref[...] = _dot(permt_ref[...], g_perm).astype(gy_ref.dtype)


def s5_prompt(proj, col_u, n_rows, bsz, seq, sp, *, tc=256):
    db = sp["d_skip"].shape[1]
    nch = sp["ab_re"].shape[1]
    nc = seq // tc
    nsteps = tc // V7X_SUBLANES
    tpr, tpi = (jnp.repeat(a, V7X_SUBLANES, axis=0) for a in _s5_powers(sp, np.arange(1, nsteps + 1)))
    ajr, aji = _s5_powers(sp, nsteps * np.arange(1, V7X_SUBLANES + 1))
    perm = np.zeros((tc, tc), np.float32)
    r = np.arange(tc)
    perm[r, (r % V7X_SUBLANES) * nsteps + r // V7X_SUBLANES] = 1.0
    full = lambda a: pl.BlockSpec(a.shape, lambda b, c: (0,) * a.ndim)
    consts = [sp["wb_re"], sp["wb_im"], sp["wc"], sp["d_skip"], sp["ab_re"], sp["ab_im"], tpr, tpi, ajr, aji,
              jnp.asarray(perm, BF16), jnp.asarray(perm.T, BF16)]
    gy, s_re, s_im = pl.pallas_call(
        functools.partial(_k_s5_prompt, tc=tc),
        grid=(bsz, nc),
        in_specs=[pl.BlockSpec((tc, db), lambda b, c: (b * nc + c, col_u))] + [full(a) for a in consts]
                 + [pl.BlockSpec(memory_space=pl.ANY)],
        out_specs=[pl.BlockSpec((tc, db), lambda b, c: (b * nc + c, 0)),
                   pl.BlockSpec((1, 1, nch), lambda b, c: (b, 0, 0)),
                   pl.BlockSpec((1, 1, nch), lambda b, c: (b, 0, 0))],
        out_shape=[jax.ShapeDtypeStruct((n_rows, db), BF16),
                   jax.ShapeDtypeStruct((bsz, 1, nch), F32), jax.ShapeDtypeStruct((bsz, 1, nch), F32)],
        scratch_shapes=[pltpu.VMEM((tc, S5_SLAB_STATES), F32)] * 4 + [pltpu.VMEM((tc, 2 * S5_SLAB_STATES), BF16)] * 2,
        input_output_aliases={1 + len(consts): 0},
        compiler_params=_cparams(2), name="s5_prompt",
    )(proj, *consts, jnp.zeros((n_rows, db), BF16))
    return gy, s_re, s_im


def _k_s5_sample(u_ref, x0r_ref, x0i_ref, wbr_ref, wbi_ref, wc_ref, d_ref, abr_ref, abi_ref, gy_any,
                 gy_ref, xr_out, xi_out):
    del gy_any
    u = u_ref[...]
    _s5_input_proj(u.astype(BF16), wbr_ref, wbi_ref, xr_out, xi_out)
    ar = abr_ref[...]
    ai = abi_ref[...]
    x0r = x0r_ref[...]
    x0i = x0i_ref[...]
    xr_out[...] = xr_out[...] + (ar * x0r - ai * x0i)
    xi_out[...] = xi_out[...] + (ar * x0i + ai * x0r)
    gy_ref[...] = _s5_output_proj(xr_out, xi_out, wc_ref, d_ref[...] * u).astype(gy_ref.dtype)


def s5_sample(proj, col_u, row0, gy_all, x0_re, x0_im, sp):
    bs = x0_re.shape[0]
    db = sp["d_skip"].shape[1]
    nch = sp["ab_re"].shape[1]
    rb = row0 // bs
    full = lambda a: pl.BlockSpec(a.shape, lambda i: (0,) * a.ndim)
    consts = [sp["wb_re"], sp["wb_im"], sp["wc"], sp["d_skip"], sp["ab_re"], sp["ab_im"]]
    x0r = x0_re.reshape(bs, nch)
    x0i = x0_im.reshape(bs, nch)
    gy_all, xr, xi = pl.pallas_call(
        _k_s5_sample,
        grid=(1,),
        in_specs=[pl.BlockSpec((bs, db), lambda i: (rb, col_u)), full(x0r), full(x0i)]
                 + [full(a) for a in consts] + [pl.BlockSpec(memory_space=pl.ANY)],
        out_specs=[pl.BlockSpec((bs, db), lambda i: (rb, 0)),
                   pl.BlockSpec((bs, nch), lambda i: (0, 0)), pl.BlockSpec((bs, nch), lambda i: (0, 0))],
        out_shape=[jax.ShapeDtypeStruct(gy_all.shape, gy_all.dtype),
                   jax.ShapeDtypeStruct((bs, nch), F32), jax.ShapeDtypeStruct((bs, nch), F32)],
        input_output_aliases={9: 0},
        compiler_params=_cparams(1), name="s5_sample",
    )(proj, x0r, x0i, *consts, gy_all)
    return gy_all, xr, xi


def _k_router(x_ref, g_ref, w_ref, b_ref, hn_ref, lg_ref):
    hn = _rmsnorm_rows(x_ref[...], g_ref[...])
    hn_ref[...] = hn
    x_hi, x_lo = _split_hi_lo(hn)
    w_hi, w_lo = _split_hi_lo(w_ref[...])
    lg_ref[...] = _dot(x_hi, w_hi) + (_dot(x_lo, w_hi) + _dot(x_hi, w_lo)) + b_ref[...]


def router(x, g, w_router, b_router, *, tm):
    t, k = x.shape
    ne = w_router.shape[1]
    w_pad = jnp.pad(w_router, ((0, 0), (0, V7X_LANES - ne)))
    b_pad = jnp.pad(b_router.astype(F32), (0, V7X_LANES - ne)).reshape(1, V7X_LANES)
    hn, lg = pl.pallas_call(
        _k_router, grid=(t // tm,),
        in_specs=[pl.BlockSpec((tm, k), lambda i: (i, 0)), pl.BlockSpec((1, k), lambda i: (0, 0)),
                  pl.BlockSpec((k, V7X_LANES), lambda i: (0, 0)), pl.BlockSpec((1, V7X_LANES), lambda i: (0, 0))],
        out_specs=[pl.BlockSpec((tm, k), lambda i: (i, 0)), pl.BlockSpec((tm, V7X_LANES), lambda i: (i, 0))],
        out_shape=[jax.ShapeDtypeStruct((t, k), F32), jax.ShapeDtypeStruct((t, V7X_LANES), F32)],
        compiler_params=_cparams(1), name="router",
    )(x, g.reshape(1, k), w_pad, b_pad)
    return hn, lg[:, :ne]


DMA_ISSUE_UNROLL = 8


def _k_gather_rows(nused_ref, tok_ref, src_hbm, o_ref, buf_ref, sems, *, tr):
    t = pl.program_id(0)
    n_used = nused_ref[0]

    def issue_tile(tile):
        slot = tile % 2
        base = tile * tr

        def issue(r, c):
            tok = tok_ref[base + r]
            pltpu.make_async_copy(src_hbm.at[pl.ds(tok, 1), :], buf_ref.at[slot, pl.ds(r, 1), :],
                                  sems.at[slot]).start()
            return c

        lax.fori_loop(0, tr, issue, 0, unroll=DMA_ISSUE_UNROLL)

    @pl.when(t == 0)
    def _():
        issue_tile(t)

    @pl.when(t + 1 < n_used)
    def _():
        issue_tile(t + 1)

    @pl.when(t < n_used)
    def _():
        slot = t % 2
        pltpu.make_async_copy(src_hbm.at[pl.ds(0, tr), :], buf_ref.at[slot], sems.at[slot]).wait()
        o_ref[...] = buf_ref[slot].astype(o_ref.dtype)

    @pl.when(t >= n_used)
    def _():
        o_ref[...] = jnp.zeros_like(o_ref)


def gather_rows(src, tok, n_used, *, tr):
    t, k = src.shape
    r_pad = tok.shape[0]
    nt = r_pad // tr
    return pl.pallas_call(
        functools.partial(_k_gather_rows, tr=tr),
        grid_spec=pltpu.PrefetchScalarGridSpec(
            num_scalar_prefetch=2, grid=(nt,),
            in_specs=[pl.BlockSpec(memory_space=pl.ANY)],
            out_specs=pl.BlockSpec((tr, k), lambda i, nu, tk: (i, 0)),
            scratch_shapes=[pltpu.VMEM((2, tr, k), src.dtype), pltpu.SemaphoreType.DMA((2,))]),
        out_shape=jax.ShapeDtypeStruct((r_pad, k), BF16),
        compiler_params=_cparams(1), name="moe_gather",
    )(n_used, tok, src)


def _new_group(te_ref, t):
    return jnp.logical_or(t == 0, te_ref[t] != te_ref[jnp.maximum(t - 1, 0)])


def _k_moe_up(nused_ref, te_ref, x_ref, wg_ref, wu_ref, o_ref, wgb_ref, wub_ref):
    t = pl.program_id(1)

    @pl.when(t < nused_ref[0])
    def _():
        @pl.when(_new_group(te_ref, t))
        def _():
            wgb_ref[...] = wg_ref[...].astype(BF16)
            wub_ref[...] = wu_ref[...].astype(BF16)

        x = x_ref[...]
        o_ref[...] = (_silu(_dot(x, wgb_ref[...])) * _dot(x, wub_ref[...])).astype(o_ref.dtype)

    @pl.when(t >= nused_ref[0])
    def _():
        o_ref[...] = jnp.zeros_like(o_ref)


def moe_up(xs, w_gate, w_up, tile_expert, n_used, *, tr, tn):
    r_pad, k = xs.shape
    ne, _, f = w_gate.shape
    nt = r_pad // tr
    row = lambda j, t, nu, te: (jnp.minimum(t, nu[0] - 1), 0)
    wmap = lambda j, t, nu, te: (te[t], 0, j)
    return pl.pallas_call(
        _k_moe_up,
        grid_spec=pltpu.PrefetchScalarGridSpec(
            num_scalar_prefetch=2, grid=(pl.cdiv(f, tn), nt),
            in_specs=[pl.BlockSpec((tr, k), row),
                      pl.BlockSpec((None, k, tn), wmap), pl.BlockSpec((None, k, tn), wmap)],
            out_specs=pl.BlockSpec((tr, tn), lambda j, t, nu, te: (t, j)),
            scratch_shapes=[pltpu.VMEM((k, tn), BF16), pltpu.VMEM((k, tn), BF16)]),
        out_shape=jax.ShapeDtypeStruct((r_pad, f), BF16),
        compiler_params=_cparams(2), name="moe_up",
    )(n_used, tile_expert, xs, w_gate, w_up)


def _k_moe_down(nused_ref, te_ref, x_ref, w_ref, o_ref, wb_ref):
    t = pl.program_id(1)

    @pl.when(t < nused_ref[0])
    def _():
        @pl.when(_new_group(te_ref, t))
        def _():
            wb_ref[...] = w_ref[...].astype(BF16)

        o_ref[...] = _dot(x_ref[...], wb_ref[...])

    @pl.when(t >= nused_ref[0])
    def _():
        o_ref[...] = jnp.zeros_like(o_ref)


def moe_down(hid, w_down, tile_expert, n_used, *, tr, tn):
    r_pad, f = hid.shape
    d = w_down.shape[2]
    nt = r_pad // tr
    return pl.pallas_call(
        _k_moe_down,
        grid_spec=pltpu.PrefetchScalarGridSpec(
            num_scalar_prefetch=2, grid=(d // tn, nt),
            in_specs=[pl.BlockSpec((tr, f), lambda j, t, nu, te: (jnp.minimum(t, nu[0] - 1), 0)),
                      pl.BlockSpec((None, f, tn), lambda j, t, nu, te: (te[t], 0, j))],
            out_specs=pl.BlockSpec((tr, tn), lambda j, t, nu, te: (t, j)),
            scratch_shapes=[pltpu.VMEM((f, tn), BF16)]),
        out_shape=jax.ShapeDtypeStruct((r_pad, d), F32),
        compiler_params=_cparams(2), name="moe_down",
    )(n_used, tile_expert, hid, w_down)


def _k_moe_combine(pos_ref, ys_hbm, gate_ref, res_ref, o_ref, buf0_ref, buf1_ref, sems, *, tr):
    t = pl.program_id(0)
    n_tiles = pl.num_programs(0)
    n_tok = n_tiles * tr

    def issue_tile(tile):
        slot = tile % 2
        base = tile * tr

        def issue(r, c):
            p0 = pos_ref[base + r]
            p1 = pos_ref[n_tok + base + r]
            pltpu.make_async_copy(ys_hbm.at[pl.ds(p0, 1), :], buf0_ref.at[slot, pl.ds(r, 1), :],
                                  sems.at[slot]).start()
            pltpu.make_async_copy(ys_hbm.at[pl.ds(p1, 1), :], buf1_ref.at[slot, pl.ds(r, 1), :],
                                  sems.at[slot]).start()
            return c

        lax.fori_loop(0, tr, issue, 0, unroll=DMA_ISSUE_UNROLL)

    @pl.when(t == 0)
    def _():
        issue_tile(t)

    @pl.when(t + 1 < n_tiles)
    def _():
        issue_tile(t + 1)

    slot = t % 2
    pltpu.make_async_copy(ys_hbm.at[pl.ds(0, tr), :], buf0_ref.at[slot], sems.at[slot]).wait()
    pltpu.make_async_copy(ys_hbm.at[pl.ds(0, tr), :], buf1_ref.at[slot], sems.at[slot]).wait()
    g = gate_ref[...]
    o_ref[...] = res_ref[...] + (g[:, 0:1] * buf0_ref[slot] + g[:, 1:2] * buf1_ref[slot])


def moe_combine(ys, pos, gates, res, *, tr):
    t, d = res.shape
    g_pad = jnp.pad(gates, ((0, 0), (0, V7X_LANES - gates.shape[1])))
    return pl.pallas_call(
        functools.partial(_k_moe_combine, tr=tr),
        grid_spec=pltpu.PrefetchScalarGridSpec(
            num_scalar_prefetch=1, grid=(t // tr,),
            in_specs=[pl.BlockSpec(memory_space=pl.ANY),
                      pl.BlockSpec((tr, V7X_LANES), lambda i, p: (i, 0)),
                      pl.BlockSpec((tr, d), lambda i, p: (i, 0))],
            out_specs=pl.BlockSpec((tr, d), lambda i, p: (i, 0)),
            scratch_shapes=[pltpu.VMEM((2, tr, d), F32), pltpu.VMEM((2, tr, d), F32),
                            pltpu.SemaphoreType.DMA((2,))]),
        out_shape=jax.ShapeDtypeStruct((t, d), F32),
        compiler_params=_cparams(1), name="moe_combine",
    )(pos, ys, g_pad, res)


def moe_layer(h, g_ffn, w_router, b_router, w_gate, w_up, w_down, *, tm, tr, tn_up, tn_down, tr_gather,
              tr_combine):
    t, d = h.shape
    ne = w_gate.shape[0]
    hn, logits = router(h, g_ffn, w_router, b_router, tm=tm)
    top_v, top_e = lax.top_k(logits, TOP_K)
    gates = jax.nn.softmax(top_v, axis=-1)
    flat_e = top_e.reshape(-1)
    onehot = (flat_e[:, None] == jnp.arange(ne, dtype=flat_e.dtype)[None, :]).astype(jnp.int32)
    rank = jnp.sum((jnp.cumsum(onehot, axis=0) - onehot) * onehot, axis=1)
    sizes = jnp.sum(onehot, axis=0)
    tiles_per = (sizes + tr - 1) // tr
    tile_end = jnp.cumsum(tiles_per)
    tile_start = tile_end - tiles_per
    n_used = tile_end[-1:].astype(jnp.int32)
    nt = (t * TOP_K) // tr + ne
    r_pad = nt * tr
    pos = (tile_start[flat_e] * tr + rank).astype(jnp.int32)
    src_tok = jnp.zeros((r_pad,), jnp.int32).at[pos].set(jnp.arange(t * TOP_K, dtype=jnp.int32) // TOP_K)
    tile_ids = jnp.minimum(jnp.arange(nt, dtype=jnp.int32), n_used[0] - 1)
    tile_expert = jnp.sum((tile_ids[:, None] >= tile_end[None, :]).astype(jnp.int32), axis=1).astype(jnp.int32)
    xs = gather_rows(hn, src_tok, n_used * (tr // tr_gather), tr=tr_gather)
    hid = moe_up(xs, w_gate, w_up, tile_expert, n_used, tr=tr, tn=tn_up)
    ys = moe_down(hid, w_down, tile_expert, n_used, tr=tr, tn=tn_down)
    return moe_combine(ys, pos.reshape(t, TOP_K).T.reshape(-1), gates, h, tr=tr_combine)


def _layer_weights(i, conv_w, conv_b, w_q, w_k, w_v, b_i, b_f, g_head, skip_a):
    nh = b_i.shape[1]
    pad = jnp.zeros((V7X_LANES - nh,), F32)
    b_if = jnp.concatenate([b_i[i].astype(F32), pad, b_f[i].astype(F32), pad]).reshape(1, 2 * V7X_LANES)
    da = conv_w.shape[-1]
    return dict(conv_w=conv_w[i], conv_b=conv_b[i].reshape(1, da), w_q=w_q[i], w_k=w_k[i], w_v=w_v[i],
                b_if=b_if, g_head=g_head[i].reshape(1, da), skip=skip_a[i].reshape(1, da))


def _tile_plan(n_rows, seq):
    tm = next(c for c in (832, 640, 512, 256, 128, 64, 32, 16) if n_rows % c == 0)
    tc = next(c for c in (320, 256, 128, 64, 32, 16, 8) if n_rows % c == 0)
    tm_big = 2 * tm if n_rows % (2 * tm) == 0 else tm
    return dict(tm=tm, tm_big=tm_big, tn=512, tn_down=256, mlstm_chunk=256, s5_chunk=256,
                moe_tr=512, moe_tn_up=512, moe_tn_down=512, gather_tr=256, combine_tr=tc, ple_tm=tm // 2,
                norm_tm=min(seq, 1024))


def kernel(x_prompt, x_sample, p_prompt, p_sample, state_mlstm_C, state_mlstm_n, state_mlstm_m, state_mlstm_conv,
           state_s5_re, state_s5_im, g_mix, w_in, conv_w, conv_b, w_q, w_k, w_v, b_i, b_f, g_head, skip_a, w_proj_a,
           s5_log_dt, s5_A_re, s5_A_im, s5_B_re, s5_B_im, s5_C_re, s5_C_im, s5_D, w_glu_b, w_out, g_ffn,
           w_ff_gate, w_ff_up, w_ff_down, w_router, b_router, w_moe_gate, w_moe_up, w_moe_down,
           g_ple, w_ple, w_pg, g_final):
    bsz, seq, d = x_prompt.shape
    bs = x_sample.shape[0]
    depth = g_mix.shape[0]
    nh = b_i.shape[1]
    d_a = conv_w.shape[-1]
    d_b = s5_D.shape[-1]
    n_p = bsz * seq
    t = n_p + bs
    tl = _tile_plan(t, seq)
    tm, tmb, tn = tl["tm"], tl["tm_big"], tl["tn"]

    h = jnp.concatenate([x_prompt.reshape(n_p, d), x_sample.reshape(bs, d)], axis=0).astype(F32)
    p_all = jnp.concatenate([p_prompt.reshape(depth, n_p, -1), p_sample.reshape(depth, bs, -1)], axis=1)

    col_ua, col_oa, col_ub = 0, 1, 2 * d_a // d_b
    col_ga = (2 * d_a + d_b) // tn
    col_gb = (2 * d_a + d_b + d) // tn
    col_if = (2 * d_a + d_b + 2 * d) // (2 * V7X_LANES)

    states = [[] for _ in range(11)]
    c_s_all = jnp.zeros(state_mlstm_C.shape, F32)
    for i in range(depth):
        lw = _layer_weights(i, conv_w, conv_b, w_q, w_k, w_v, b_i, b_f, g_head, skip_a)
        sp = _s5_params(dict(log_dt=s5_log_dt[i], a_re=s5_A_re[i], a_im=s5_A_im[i], b_re=s5_B_re[i],
                             b_im=s5_B_im[i], c_re=s5_C_re[i], c_im=s5_C_im[i], d_skip=s5_D[i]))
        proj = in_proj(h, g_mix[i], w_in, i, d_a, d_b, d, nh, tm=tmb, tn=tn)

        a_pre, c_p, n_pp, m_p, conv_p = mlstm_prompt(proj, col_ua, col_oa, col_if, t, bsz, seq, lw,
                                                     L=tl["mlstm_chunk"])
        a_pre, c_s_all, n_s, m_s, conv_s = mlstm_sample(proj, col_ua, col_oa, col_if, n_p, a_pre,
                                                        state_mlstm_C, i, c_s_all, state_mlstm_n[i].astype(F32),
                                                        state_mlstm_m[i].astype(F32),
                                                        state_mlstm_conv[i].astype(F32), lw)
        gy, sre_p, sim_p = s5_prompt(proj, col_ub, t, bsz, seq, sp, tc=tl["s5_chunk"])
        gy, sre_s, sim_s = s5_sample(proj, col_ub, n_p, gy, state_s5_re[i].astype(F32),
                                     state_s5_im[i].astype(F32), sp)
        mix = branch_mix(a_pre, gy, w_proj_a, w_glu_b, i, proj, col_ga, col_gb, tm=tmb, tn=tn)
        h = matmul(mix, w_out, i, h, tm=tmb, tn=tn, name="out_proj")

        j = i // 2
        if i % 2 == 0:
            hid = norm_swiglu_up(h, g_ffn[i], w_ff_gate, w_ff_up, j, tm=tmb, tn=tn)
            h = matmul(hid, w_ff_down, j, h, tm=tmb, tn=tl["tn_down"], name="ffn_down")
        else:
            h = moe_layer(h, g_ffn[i], w_router[j], b_router[j], w_moe_gate[j], w_moe_up[j], w_moe_down[j],
                          tm=tm, tr=tl["moe_tr"], tn_up=tl["moe_tn_up"], tn_down=tl["moe_tn_down"],
                          tr_gather=tl["gather_tr"], tr_combine=tl["combine_tr"])
        h = ple_matmul(h, g_ple[i], p_all, w_pg, w_ple, i, tm=tl["ple_tm"], tn=tn)

        g_s, p_s = S5_STATE, sre_p.shape[-1] // S5_STATE
        new = [c_p, n_pp, m_p, conv_p, sre_p.reshape(bsz, p_s, g_s), sim_p.reshape(bsz, p_s, g_s),
               n_s, m_s, conv_s, sre_s.reshape(bs, p_s, g_s), sim_s.reshape(bs, p_s, g_s)]
        for lst, s in zip(states, new):
            lst.append(s)

    y_prompt, y_sample = final_norm_split(h, g_final, n_p, tm=tl["norm_tm"])
    y_prompt = y_prompt.reshape(bsz, seq, d)
    y_sample = y_sample.reshape(bs, 1, d)
    st = [jnp.stack(lst) for lst in states]
    return (y_prompt, y_sample) + tuple(st[:6]) + (c_s_all,) + tuple(st[6:])
```

```python
import functools
import math

import numpy as np
import jax
import jax.numpy as jnp
from jax import lax
from jax.experimental import pallas as pl
from jax.experimental.pallas import tpu as pltpu

F32 = jnp.float32
BF16 = jnp.bfloat16
EPS = 1e-6

V7X_VMEM_BYTES = 64 * 1024 * 1024
V7X_LANES = 128
V7X_SUBLANES = 8
VMEM_LIMIT = 56 * 1024 * 1024

N_HEADS = 4
CONV_W = 4
S5_GROUP = 16
S5_STATE = 64
N_EXPERTS = 8
TOP_K = 2


def _cparams(n_axes, vmem=VMEM_LIMIT):
    return pltpu.CompilerParams(dimension_semantics=("arbitrary",) * n_axes, vmem_limit_bytes=vmem)


def _sigmoid(x):
    return 1.0 / (1.0 + jnp.exp(-x))


def _silu(x):
    return x * _sigmoid(x)


def _gelu_tanh(x):
    return 0.5 * x * (1.0 + jnp.tanh(math.sqrt(2.0 / math.pi) * (x + 0.044715 * (x * x * x))))


def _log_sigmoid(x):
    return jnp.minimum(x, 0.0) - jnp.log(1.0 + jnp.exp(-jnp.abs(x)))


def _dot(a, b):
    return jnp.dot(a, b, preferred_element_type=F32)


def _dot_nt(a, b):
    return lax.dot_general(a, b, (((1,), (1,)), ((), ())), preferred_element_type=F32)


def _dot_tn(a, b):
    return lax.dot_general(a, b, (((0,), (0,)), ((), ())), preferred_element_type=F32)


def _rmsnorm_rows(x, g):
    ms = jnp.mean(x * x, axis=-1, keepdims=True)
    return x * lax.rsqrt(ms + EPS) * g


def _row_tile_buffering(tm, k, dtype):
    two_copies = 2 * tm * k * jnp.dtype(dtype).itemsize
    return pl.Buffered(1) if two_copies > VMEM_LIMIT // 4 else None


def _row_chunks(tm):
    for rc in (256, 208, 128, 104, 64, 32, 16, 8):
        if tm % rc == 0:
            return rc
    return tm


def _norm_to_scratch(x_ref, g_ref, xn_ref, tm):
    rc = _row_chunks(tm)

    def body(r, c):
        rows = pl.ds(pl.multiple_of(r * rc, rc), rc)
        xn_ref[rows, :] = _rmsnorm_rows(x_ref[rows, :], g_ref[...]).astype(BF16)
        return c

    lax.fori_loop(0, tm // rc, body, 0)


def _wspec(w, layer, tn, col=lambda j: j):
    return pl.BlockSpec((None, w.shape[1], tn), lambda i, j: (layer, 0, col(j)))


def _k_in_proj(x_ref, g_ref, wa_ref, wb_ref, wif_ref, o_ref, xn_ref, *, tm, n_head, n_main, nh):
    j = pl.program_id(1)

    @pl.when(j == 0)
    def _():
        _norm_to_scratch(x_ref, g_ref, xn_ref, tm)

    @pl.when(j < n_head)
    def _():
        o_ref[...] = _dot_nt(xn_ref[...], wa_ref[0].astype(BF16))

    @pl.when(jnp.logical_and(j >= n_head, j < n_main))
    def _():
        o_ref[...] = _dot_nt(xn_ref[...], wb_ref[0].astype(BF16))

    @pl.when(j == n_main)
    def _():
        pre = _dot_nt(xn_ref[...], wif_ref[0].astype(BF16))
        o_ref[...] = jnp.zeros_like(o_ref)
        o_ref[:, 0:nh] = pre[:, 0:nh]
        o_ref[:, V7X_LANES:V7X_LANES + nh] = pre[:, nh:2 * nh]


def in_proj(x, g, w_in, layer, d_a, d_b, d, nh, *, tm, tn):
    t, k = x.shape
    assert (2 * nh) % V7X_SUBLANES == 0 and (2 * d_a) % tn == 0 and (d_b + 2 * d) % tn == 0
    wt = jnp.swapaxes(w_in, 1, 2)
    n_head = 2 * d_a // tn
    n_main = n_head + (d_b + 2 * d) // tn
    if_row = 2 * d_a

    def rows(nrows, start):
        return pl.BlockSpec((pl.Element(1), pl.Element(nrows), pl.Element(k)), lambda i, j: (layer, start(j), 0))

    return pl.pallas_call(
        functools.partial(_k_in_proj, tm=tm, n_head=n_head, n_main=n_main, nh=nh),
        grid=(t // tm, n_main + 1),
        in_specs=[pl.BlockSpec((tm, k), lambda i, j: (i, 0), pipeline_mode=_row_tile_buffering(tm, k, x.dtype)),
                  pl.BlockSpec((1, k), lambda i, j: (0, 0)),
                  rows(tn, lambda j: jnp.minimum(j, n_head - 1) * tn),
                  rows(tn, lambda j: (jnp.clip(j, n_head, n_main - 1) * (tn // V7X_SUBLANES)
                                      + 2 * nh // V7X_SUBLANES) * V7X_SUBLANES),
                  rows(2 * nh, lambda j: if_row)],
        out_specs=pl.BlockSpec((tm, tn), lambda i, j: (i, j)),
        out_shape=jax.ShapeDtypeStruct((t, (n_main + 1) * tn), F32),
        scratch_shapes=[pltpu.VMEM((tm, k), BF16)],
        compiler_params=_cparams(2), name="in_proj",
    )(x, g.reshape(1, k), wt, wt, wt)


def _k_mm_res(x_ref, w_ref, r_ref, o_ref):
    o_ref[...] = r_ref[...] + _dot(x_ref[...], w_ref[...].astype(BF16))


def _k_mm(x_ref, w_ref, o_ref):
    o_ref[...] = _dot(x_ref[...], w_ref[...].astype(BF16)).astype(o_ref.dtype)


def matmul(x, w, layer, res=None, *, tm, tn, out_dtype=F32, name="mm"):
    t, k = x.shape
    n = w.shape[2]
    in_specs = [pl.BlockSpec((tm, k), lambda i, j: (i, 0), pipeline_mode=_row_tile_buffering(tm, k, x.dtype)),
                _wspec(w, layer, tn)]
    args = [x, w]
    body = _k_mm
    if res is not None:
        in_specs.append(pl.BlockSpec((tm, tn), lambda i, j: (i, j)))
        args.append(res)
        body = _k_mm_res
    return pl.pallas_call(
        body, grid=(t // tm, pl.cdiv(n, tn)), in_specs=in_specs,
        out_specs=pl.BlockSpec((tm, tn), lambda i, j: (i, j)),
        out_shape=jax.ShapeDtypeStruct((t, n), out_dtype),
        compiler_params=_cparams(2), name=name,
    )(*args)


def _k_branch_mix(a_ref, gy_ref, wp_ref, wv_ref, wg_ref, ga_ref, gb_ref, o_ref):
    a = a_ref[...]
    gy = gy_ref[...]
    a_out = _dot(a, wp_ref[...].astype(BF16))
    val = _dot(gy, wv_ref[...].astype(BF16))
    gate = _dot(gy, wg_ref[...].astype(BF16))
    b_out = val * _sigmoid(gate)
    o_ref[...] = (_sigmoid(ga_ref[...]) * a_out + _sigmoid(gb_ref[...]) * b_out).astype(o_ref.dtype)


def branch_mix(a_pre, gy, w_proj_a, w_glu_b, layer, proj, col_ga, col_gb, *, tm, tn):
    t, k = a_pre.shape
    n = w_proj_a.shape[2]
    nj = n // tn
    rows = pl.BlockSpec((tm, k), lambda i, j: (i, 0), pipeline_mode=pl.Buffered(1))
    return pl.pallas_call(
        _k_branch_mix, grid=(t // tm, nj),
        in_specs=[rows, rows,
                  _wspec(w_proj_a, layer, tn), _wspec(w_glu_b, layer, tn),
                  _wspec(w_glu_b, layer, tn, lambda j: j + nj),
                  pl.BlockSpec((tm, tn), lambda i, j: (i, col_ga + j)),
                  pl.BlockSpec((tm, tn), lambda i, j: (i, col_gb + j))],
        out_specs=pl.BlockSpec((tm, tn), lambda i, j: (i, j)),
        out_shape=jax.ShapeDtypeStruct((t, n), BF16),
        compiler_params=_cparams(2), name="branch_mix",
    )(a_pre, gy, w_proj_a, w_glu_b, w_glu_b, proj, proj)


def _k_swiglu_up(x_ref, g_ref, wg_ref, wu_ref, o_ref, xn_ref, *, tm):
    @pl.when(pl.program_id(1) == 0)
    def _():
        _norm_to_scratch(x_ref, g_ref, xn_ref, tm)

    xn = xn_ref[...]
    gate = _dot(xn, wg_ref[...].astype(BF16))
    up = _dot(xn, wu_ref[...].astype(BF16))
    o_ref[...] = (_silu(gate) * up).astype(o_ref.dtype)


def norm_swiglu_up(x, g, w_gate, w_up, layer, *, tm, tn, name="ffn_up"):
    t, k = x.shape
    n = w_gate.shape[2]
    return pl.pallas_call(
        functools.partial(_k_swiglu_up, tm=tm),
        grid=(t // tm, pl.cdiv(n, tn)),
        in_specs=[pl.BlockSpec((tm, k), lambda i, j: (i, 0), pipeline_mode=_row_tile_buffering(tm, k, x.dtype)),
                  pl.BlockSpec((1, k), lambda i, j: (0, 0)),
                  _wspec(w_gate, layer, tn), _wspec(w_up, layer, tn)],
        out_specs=pl.BlockSpec((tm, tn), lambda i, j: (i, j)),
        out_shape=jax.ShapeDtypeStruct((t, n), BF16),
        scratch_shapes=[pltpu.VMEM((tm, k), BF16)],
        compiler_params=_cparams(2), name=name,
    )(x, g.reshape(1, k), w_gate, w_up)


def _cast_rows_to(dst_ref, src_ref, rc):
    def body(r, c):
        rows = pl.ds(pl.multiple_of(r * rc, rc), rc)
        dst_ref[rows, :] = src_ref[rows, :].astype(dst_ref.dtype)
        return c

    lax.fori_loop(0, src_ref.shape[0] // rc, body, 0)


def _k_ple(x_ref, g_ref, p_ref, wpg_ref, wple_ref, o_ref, wpg_bf, wple_bf, *, tn):
    @pl.when(pl.program_id(0) == 0)
    def _():
        _cast_rows_to(wpg_bf, wpg_ref, V7X_LANES)
        _cast_rows_to(wple_bf, wple_ref, V7X_LANES)

    x = x_ref[...]
    xn = _rmsnorm_rows(x, g_ref[...]).astype(BF16)
    pb = p_ref[...].astype(BF16)
    for c in range(x.shape[1] // tn):
        cols = slice(c * tn, (c + 1) * tn)
        gate = _dot(xn, wpg_bf[:, cols])
        emb = _dot(pb, wple_bf[:, cols])
        o_ref[:, cols] = x[:, cols] + emb * _sigmoid(gate)


def ple_matmul(x, g, p, w_pg, w_ple, layer, *, tm, tn, name="ple"):
    t, k = x.shape
    kp = p.shape[2]
    resident = lambda w: pl.BlockSpec((None,) + w.shape[1:], lambda i: (layer, 0, 0), pipeline_mode=pl.Buffered(1))
    return pl.pallas_call(
        functools.partial(_k_ple, tn=tn),
        grid=(t // tm,),
        in_specs=[pl.BlockSpec((tm, k), lambda i: (i, 0)),
                  pl.BlockSpec((1, k), lambda i: (0, 0)),
                  pl.BlockSpec((None, tm, kp), lambda i: (layer, i, 0)),
                  resident(w_pg), resident(w_ple)],
        out_specs=pl.BlockSpec((tm, k), lambda i: (i, 0)),
        out_shape=jax.ShapeDtypeStruct((t, k), F32),
        scratch_shapes=[pltpu.VMEM(w_pg.shape[1:], BF16), pltpu.VMEM(w_ple.shape[1:], BF16)],
        compiler_params=_cparams(1), name=name,
    )(x, g.reshape(1, k), p, w_pg, w_ple)


def _k_final_norm(xp_ref, xs_ref, g_ref, op_ref, os_ref, *, n_prompt_tiles):
    i = pl.program_id(0)

    @pl.when(i < n_prompt_tiles)
    def _():
        op_ref[...] = _rmsnorm_rows(xp_ref[...], g_ref[...])

    @pl.when(i == n_prompt_tiles)
    def _():
        os_ref[...] = _rmsnorm_rows(xs_ref[...], g_ref[...])


def final_norm_split(x, g, n_prompt, *, tm):
    t, k = x.shape
    bs = t - n_prompt
    npt = n_prompt // tm
    last = npt - 1
    return pl.pallas_call(
        functools.partial(_k_final_norm, n_prompt_tiles=npt), grid=(npt + 1,),
        in_specs=[pl.BlockSpec((tm, k), lambda i: (jnp.minimum(i, last), 0)),
                  pl.BlockSpec((bs, k), lambda i: (n_prompt // bs, 0)),
                  pl.BlockSpec((1, k), lambda i: (0, 0))],
        out_specs=[pl.BlockSpec((tm, k), lambda i: (jnp.minimum(i, last), 0)),
                   pl.BlockSpec((bs, k), lambda i: (0, 0))],
        out_shape=[jax.ShapeDtypeStruct((n_prompt, k), F32), jax.ShapeDtypeStruct((bs, k), F32)],
        compiler_params=_cparams(1), name="final_norm",
    )(x, x, g.reshape(1, k))


def _split_hi_lo(x):
    hi = x.astype(BF16)
    lo = (x - hi.astype(F32)).astype(BF16)
    return hi, lo


def _k_mlstm_prompt(u_ref, o_ref_in, if_ref, cw_ref, cb_ref, wq_ref, wk_ref, wv_ref, bif_ref, gh_ref, sk_ref,
                    tri_ref, a_init, a_ref, c_out, n_out, m_out, conv_out, upad_ref, *, L, dh):
    del a_init
    c_idx = pl.program_id(1)
    nh = N_HEADS

    @pl.when(c_idx == 0)
    def _():
        c_out[...] = jnp.zeros_like(c_out)
        n_out[...] = jnp.zeros_like(n_out)
        m_out[...] = jnp.zeros_like(m_out)
        upad_ref[pl.ds(0, 8), :] = jnp.zeros((8, nh * dh), F32)

    @pl.when(c_idx > 0)
    def _():
        upad_ref[pl.ds(0, 8), :] = upad_ref[pl.ds(L, 8), :]

    u = u_ref[...]
    upad_ref[pl.ds(8, L), :] = u
    conv = cb_ref[...] + u * cw_ref[CONV_W - 1:CONV_W, :]
    for j in range(CONV_W - 1):
        conv = conv + upad_ref[pl.ds(8 - (CONV_W - 1) + j, L), :] * cw_ref[j:j + 1, :]
    cact = _silu(conv)
    conv_out[0] = upad_ref[pl.ds(L, 8), :]

    pre = if_ref[...]
    li = pre[:, :V7X_LANES] + bif_ref[:, :V7X_LANES]
    lf = _log_sigmoid(pre[:, V7X_LANES:] + bif_ref[:, V7X_LANES:])
    tri = tri_ref[...]
    lf_hi, lf_mid = _split_hi_lo(lf)
    lf_lo = (lf - lf_hi.astype(F32) - lf_mid.astype(F32)).astype(BF16)
    bcum = _dot(tri, lf_hi) + _dot(tri, lf_mid) + _dot(tri, lf_lo)
    li_t = li.T
    b_t = bcum.T
    row_id = lax.broadcasted_iota(jnp.int32, (L, L), 0)
    col_id = lax.broadcasted_iota(jnp.int32, (L, L), 1)
    causal = col_id <= row_id
    lane = lax.broadcasted_iota(jnp.int32, (1, V7X_LANES), 1)
    m_row = m_out[0]
    m_new_row = m_row

    for h in range(nh):
        hs = slice(h * dh, (h + 1) * dh)
        ch = cact[:, hs].astype(BF16)
        uh = u[:, hs].astype(BF16)
        q = _dot(ch, wq_ref[h].astype(BF16))
        k = _dot(ch, wk_ref[h].astype(BF16)) * (dh ** -0.5)
        v = _dot(uh, wv_ref[h].astype(BF16))
        qb, kb, vb = q.astype(BF16), k.astype(BF16), v.astype(BF16)

        b_col = bcum[:, h:h + 1]
        li_col = li[:, h:h + 1]
        r_row = li_t[h:h + 1, :] - b_t[h:h + 1, :]
        m_prev = m_row[:, h:h + 1]
        d = jnp.where(causal, b_col + r_row, -jnp.inf)
        inter = b_col + m_prev
        m_t = jnp.maximum(inter, jnp.max(d, axis=-1, keepdims=True))
        w_inter = jnp.exp(inter - m_t)
        s = _dot_nt(qb, kb) * jnp.exp(d - m_t)
        c_prev = c_out[0, h]
        n_prev = n_out[0, h:h + 1, :]
        num = w_inter * _dot(qb, c_prev.astype(BF16)) + _dot(s.astype(BF16), vb)
        den = w_inter * jnp.sum(q * n_prev, axis=-1, keepdims=True) + jnp.sum(s, axis=-1, keepdims=True)
        hh = num / jnp.maximum(jnp.abs(den), jnp.exp(-m_t))

        b_last = b_col[L - 1:L, :]
        g_col = b_last - b_col + li_col
        m_new = jnp.maximum(b_last + m_prev, jnp.max(g_col, axis=0, keepdims=True))
        decay = jnp.exp(b_last + m_prev - m_new)
        wk_ = jnp.exp(g_col - m_new) * k
        c_out[0, h] = decay * c_prev + _dot_tn(wk_.astype(BF16), vb)
        n_out[0, h:h + 1, :] = decay * n_prev + jnp.sum(wk_, axis=0, keepdims=True)
        m_new_row = jnp.where(lane == h, m_new, m_new_row)

        hn = _rmsnorm_rows(hh, gh_ref[:, hs])
        gated = (hn + sk_ref[:, hs] * cact[:, hs]) * _sigmoid(o_ref_in[:, hs])
        a_ref[:, hs] = gated.astype(a_ref.dtype)

    m_out[0] = m_new_row


def mlstm_prompt(proj, col_u, col_o, col_if, n_rows, bsz, seq, lw, *, L):
    dh = lw["w_q"].shape[-1]
    da = N_HEADS * dh
    nc = seq // L
    tri = jnp.asarray(np.tril(np.ones((L, L), np.float32)), BF16)
    row_blk = lambda b, c: b * nc + c
    full = lambda *shape: pl.BlockSpec(shape, lambda b, c: (0,) * len(shape))
    outs = pl.pallas_call(
        functools.partial(_k_mlstm_prompt, L=L, dh=dh),
        grid=(bsz, nc),
        in_specs=[pl.BlockSpec((L, da), lambda b, c: (row_blk(b, c), col_u)),
                  pl.BlockSpec((L, da), lambda b, c: (row_blk(b, c), col_o)),
                  pl.BlockSpec((L, 2 * V7X_LANES), lambda b, c: (row_blk(b, c), col_if)),
                  full(CONV_W, da), full(1, da), full(N_HEADS, dh, dh), full(N_HEADS, dh, dh),
                  full(N_HEADS, dh, dh), full(1, 2 * V7X_LANES), full(1, da), full(1, da), full(L, L),
                  pl.BlockSpec(memory_space=pl.ANY)],
        out_specs=[pl.BlockSpec((L, da), lambda b, c: (row_blk(b, c), 0)),
                   pl.BlockSpec((1, N_HEADS, dh, dh), lambda b, c: (b, 0, 0, 0)),
                   pl.BlockSpec((1, N_HEADS, dh), lambda b, c: (b, 0, 0)),
                   pl.BlockSpec((1, 1, V7X_LANES), lambda b, c: (b, 0, 0)),
                   pl.BlockSpec((1, 8, da), lambda b, c: (b, 0, 0))],
        out_shape=[jax.ShapeDtypeStruct((n_rows, da), BF16),
                   jax.ShapeDtypeStruct((bsz, N_HEADS, dh, dh), F32),
                   jax.ShapeDtypeStruct((bsz, N_HEADS, dh), F32),
                   jax.ShapeDtypeStruct((bsz, 1, V7X_LANES), F32),
                   jax.ShapeDtypeStruct((bsz, 8, da), F32)],
        scratch_shapes=[pltpu.VMEM((L + 8, da), F32)],
        input_output_aliases={12: 0},
        compiler_params=_cparams(2), name="mlstm_prompt",
    )(proj, proj, proj, lw["conv_w"], lw["conv_b"], lw["w_q"], lw["w_k"], lw["w_v"], lw["b_if"],
      lw["g_head"], lw["skip"], tri, jnp.zeros((n_rows, da), BF16))
    a_pre, c_p, n_p, m_p, conv_p = outs
    return a_pre, c_p, n_p, m_p[:, 0, :N_HEADS], conv_p[:, 8 - (CONV_W - 1):, :]


def _k_mlstm_sample_pre(u_ref, if_ref, conv_ref, m_ref, cw_ref, cb_ref, wq_ref, wk_ref, wv_ref, bif_ref,
                        q_out, k_out, v_out, c_out, gates_out, conv_out, *, dh):
    nh = N_HEADS
    u = u_ref[...]
    conv = cb_ref[...] + u * cw_ref[CONV_W - 1:CONV_W, :]
    for j in range(CONV_W - 1):
        conv = conv + conv_ref[j] * cw_ref[j:j + 1, :]
        if j > 0:
            conv_out[j - 1] = conv_ref[j]
    conv_out[CONV_W - 2] = u
    cact = _silu(conv)
    c_out[...] = cact
    for h in range(nh):
        hs = slice(h * dh, (h + 1) * dh)
        ch = cact[:, hs].astype(BF16)
        q_out[:, hs] = _dot(ch, wq_ref[h].astype(BF16))
        k_out[:, hs] = _dot(ch, wk_ref[h].astype(BF16)) * (dh ** -0.5)
        v_out[:, hs] = _dot(u[:, hs].astype(BF16), wv_ref[h].astype(BF16))
    pre = if_ref[...]
    li = pre[:, :V7X_LANES] + bif_ref[:, :V7X_LANES]
    lf = _log_sigmoid(pre[:, V7X_LANES:] + bif_ref[:, V7X_LANES:])
    m_prev = m_ref[...]
    inter = lf + m_prev
    m_t = jnp.maximum(inter, li)
    gates_out[0] = jnp.exp(inter - m_t)
    gates_out[1] = jnp.exp(li - m_t)
    gates_out[2] = jnp.exp(-m_t)
    gates_out[3] = m_t


def _k_mlstm_sample_step(q_ref, k_ref, v_ref, gates_ref, c_ref, n_ref, cact_ref, o_ref_in, gh_ref, sk_ref,
                         a_any, c_any, c_out, n_out, a_out, hh_ref, *, bt, dh):
    del a_any, c_any
    i = pl.program_id(0)
    nh = N_HEADS
    q = q_ref[...]
    k = k_ref[...]
    v = v_ref[...]
    w_inter = gates_ref[0]
    w_new = gates_ref[1]
    e_neg_m = gates_ref[2]
    n_prev = n_ref[...]
    rows = pl.ds(pl.multiple_of(i * bt, bt), bt)
    for h in range(nh):
        hs = slice(h * dh, (h + 1) * dh)
        qh, kh, vh, nh_prev = q[:, hs], k[:, hs], v[:, hs], n_prev[:, hs]
        q_t = qh.T
        k_t = kh.T
        wi = w_inter[:, h:h + 1]
        wn = w_new[:, h:h + 1]
        s = jnp.sum(qh * kh, axis=-1, keepdims=True) * wn
        den = wi * jnp.sum(qh * nh_prev, axis=-1, keepdims=True) + s
        wv = wn * vh
        qc_rows = []
        for bl in range(bt):
            c_prev = c_ref[bl, h]
            qc_rows.append(jnp.sum(q_t[:, bl:bl + 1] * c_prev, axis=0, keepdims=True))
            c_out[bl, h] = wi[bl:bl + 1, :] * c_prev + k_t[:, bl:bl + 1] * wv[bl:bl + 1, :]
        qc = jnp.concatenate(qc_rows, axis=0)
        num = wi * qc + s * vh
        hh = num / jnp.maximum(jnp.abs(den), e_neg_m[:, h:h + 1])
        hh_ref[rows, hs] = hh
        n_out[:, hs] = wi * nh_prev + wn * kh

    @pl.when(i == pl.num_programs(0) - 1)
    def _():
        for h in range(nh):
            hs = slice(h * dh, (h + 1) * dh)
            hn = _rmsnorm_rows(hh_ref[:, hs], gh_ref[:, hs])
            a_out[:, hs] = ((hn + sk_ref[:, hs] * cact_ref[:, hs]) * _sigmoid(o_ref_in[:, hs])).astype(a_out.dtype)


def mlstm_sample(proj, col_u, col_o, col_if, row0, a_pre_all, c_all, layer, c_new_all, n0, m0, conv0, lw, *,
                 bt=8):
    _, bs, nh, dh, _ = c_all.shape
    da = nh * dh
    rb = row0 // bs
    m_pad = jnp.pad(m0, ((0, 0), (0, V7X_LANES - nh)))
    conv_t = jnp.transpose(conv0, (1, 0, 2))
    full = lambda *shape: pl.BlockSpec(shape, lambda i: (0,) * len(shape))
    q, k, v, cact, gates, conv_new = pl.pallas_call(
        functools.partial(_k_mlstm_sample_pre, dh=dh),
        grid=(1,),
        in_specs=[pl.BlockSpec((bs, da), lambda i: (rb, col_u)),
                  pl.BlockSpec((bs, 2 * V7X_LANES), lambda i: (rb, col_if)),
                  full(CONV_W - 1, bs, da), full(bs, V7X_LANES), full(CONV_W, da), full(1, da),
                  full(nh, dh, dh), full(nh, dh, dh), full(nh, dh, dh), full(1, 2 * V7X_LANES)],
        out_specs=[full(bs, da), full(bs, da), full(bs, da), full(bs, da), full(4, bs, V7X_LANES),
                   full(CONV_W - 1, bs, da)],
        out_shape=[jax.ShapeDtypeStruct((bs, da), F32)] * 4
                  + [jax.ShapeDtypeStruct((4, bs, V7X_LANES), F32),
                     jax.ShapeDtypeStruct((CONV_W - 1, bs, da), F32)],
        compiler_params=_cparams(1), name="mlstm_sample_pre",
    )(proj, proj, conv_t, m_pad, lw["conv_w"], lw["conv_b"], lw["w_q"], lw["w_k"], lw["w_v"], lw["b_if"])

    blk = lambda *shape: pl.BlockSpec(shape, lambda i: (i,) + (0,) * (len(shape) - 1))
    cst = lambda *shape: pl.BlockSpec(shape, lambda i: (0,) * len(shape))
    c_blk = pl.BlockSpec((None, bt, nh, dh, dh), lambda i: (layer, i, 0, 0, 0))
    c_new_all, n_new, a_pre_all = pl.pallas_call(
        functools.partial(_k_mlstm_sample_step, bt=bt, dh=dh),
        grid=(bs // bt,),
        in_specs=[blk(bt, da), blk(bt, da), blk(bt, da),
                  pl.BlockSpec((4, bt, V7X_LANES), lambda i: (0, i, 0)),
                  c_blk, blk(bt, da), cst(bs, da),
                  pl.BlockSpec((bs, da), lambda i: (rb, col_o)), cst(1, da), cst(1, da),
                  pl.BlockSpec(memory_space=pl.ANY), pl.BlockSpec(memory_space=pl.ANY)],
        out_specs=[c_blk, blk(bt, da), pl.BlockSpec((bs, da), lambda i: (rb, 0))],
        out_shape=[jax.ShapeDtypeStruct(c_new_all.shape, F32), jax.ShapeDtypeStruct((bs, da), F32),
                   jax.ShapeDtypeStruct(a_pre_all.shape, a_pre_all.dtype)],
        scratch_shapes=[pltpu.VMEM((bs, da), F32)],
        input_output_aliases={10: 2, 11: 0},
        compiler_params=_cparams(1), name="mlstm_sample_step",
    )(q, k, v, gates, c_all, n0.reshape(bs, da), cact, proj, lw["g_head"], lw["skip"], a_pre_all, c_new_all)
    m_new = gates[3][:, :nh]
    return a_pre_all, c_new_all, n_new.reshape(bs, nh, dh), m_new, jnp.transpose(conv_new, (1, 0, 2))


S5_SLAB_GROUPS = V7X_LANES // S5_GROUP
S5_SLAB_STATES = S5_SLAB_GROUPS * S5_STATE


def _s5_params(lp):
    g, p = lp["a_re"].shape
    dt = jnp.exp(lp["log_dt"].astype(F32))[:, None]
    a_re = lp["a_re"].astype(F32)
    a_im = lp["a_im"].astype(F32)
    lam_re = a_re * dt
    lam_im = a_im * dt
    mag = jnp.exp(lam_re)
    ab_re = mag * jnp.cos(lam_im)
    ab_im = mag * jnp.sin(lam_im)
    den = a_re * a_re + a_im * a_im
    nr = ab_re - 1.0
    ni = ab_im
    k_re = (nr * a_re + ni * a_im) / den
    k_im = (ni * a_re - nr * a_im) / den
    b_re = lp["b_re"].astype(F32)
    b_im = lp["b_im"].astype(F32)
    bb_re = k_re[..., None] * b_re - k_im[..., None] * b_im
    bb_im = k_re[..., None] * b_im + k_im[..., None] * b_re
    ns = g // S5_SLAB_GROUPS
    eye = jnp.eye(S5_SLAB_GROUPS, dtype=F32)

    def in_blockdiag(bb):
        bs = bb.reshape(ns, S5_SLAB_GROUPS, p, S5_GROUP)
        w = jnp.einsum("ab,sapc->sacbp", eye, bs)
        return w.reshape(ns, V7X_LANES, S5_SLAB_STATES).astype(BF16)

    def out_blockdiag(cc):
        cs = cc.astype(F32).reshape(ns, S5_SLAB_GROUPS, S5_GROUP, p)
        w = jnp.einsum("ab,sacp->sapbc", eye, cs)
        return w.reshape(ns, S5_SLAB_STATES, V7X_LANES)

    wc = jnp.concatenate([out_blockdiag(lp["c_re"]), -out_blockdiag(lp["c_im"])], axis=1).astype(BF16)
    return dict(lam_re=lam_re.reshape(1, g * p), lam_im=lam_im.reshape(1, g * p),
                ab_re=ab_re.reshape(1, g * p), ab_im=ab_im.reshape(1, g * p),
                wb_re=in_blockdiag(bb_re), wb_im=in_blockdiag(bb_im), wc=wc,
                d_skip=lp["d_skip"].astype(F32).reshape(1, g * S5_GROUP))


def _s5_powers(sp, ks):
    kk = jnp.asarray(ks, F32)[:, None]
    mag = jnp.exp(kk * sp["lam_re"])
    return mag * jnp.cos(kk * sp["lam_im"]), mag * jnp.sin(kk * sp["lam_im"])


def _s5_input_proj(u_bf, wbr_ref, wbi_ref, bur_ref, bui_ref):
    ns = wbr_ref.shape[0]
    for s in range(ns):
        us = u_bf[:, s * V7X_LANES:(s + 1) * V7X_LANES]
        cols = slice(s * S5_SLAB_STATES, (s + 1) * S5_SLAB_STATES)
        bur_ref[:, cols] = _dot(us, wbr_ref[s])
        bui_ref[:, cols] = _dot(us, wbi_ref[s])


def _s5_output_proj(xr_ref, xi_ref, wc_ref, skip):
    ns = wc_ref.shape[0]
    outs = []
    for s in range(ns):
        cols = slice(s * S5_SLAB_STATES, (s + 1) * S5_SLAB_STATES)
        xcat = jnp.concatenate([xr_ref[:, cols].astype(BF16), xi_ref[:, cols].astype(BF16)], axis=1)
        y = _dot(xcat, wc_ref[s]) + skip[:, s * V7X_LANES:(s + 1) * V7X_LANES]
        outs.append(_gelu_tanh(y).astype(BF16))
    return jnp.concatenate(outs, axis=1)


def _k_s5_prompt(u_ref, wbr_ref, wbi_ref, wc_ref, d_ref, abr_ref, abi_ref, tpr_ref, tpi_ref, ajr_ref, aji_ref,
                 perm_ref, permt_ref, gy_init, gy_ref, sre_out, sim_out,
                 bur0, bur1, bui0, bui1, xb0, xb1, *, tc):
    del gy_init
    c_idx = pl.program_id(1)
    nsteps = tc // V7X_SUBLANES
    ns = wbr_ref.shape[0]
    lw = S5_SLAB_STATES
    pair = 2 * V7X_SUBLANES
    bur, bui, xb = (bur0, bur1), (bui0, bui1), (xb0, xb1)

    @pl.when(c_idx == 0)
    def _():
        sre_out[...] = jnp.zeros_like(sre_out)
        sim_out[...] = jnp.zeros_like(sim_out)

    u = u_ref[...]
    u_hi, u_lo = _split_hi_lo(u)
    perm = perm_ref[...]
    up_hi = _dot(perm, u_hi)
    skip = d_ref[...] * (up_hi + _dot(perm, u_lo))
    up_bf = up_hi.astype(BF16)

    sub = lax.broadcasted_iota(jnp.int32, (V7X_SUBLANES, lw), 0)

    def input_proj(s):
        us = up_bf[:, s * V7X_LANES:(s + 1) * V7X_LANES]
        bur[s % 2][...] = _dot(us, wbr_ref[s])
        bui[s % 2][...] = _dot(us, wbi_ref[s])

    def scan(s):
        br, bi, xo = bur[s % 2], bui[s % 2], xb[s % 2]
        lanes = slice(s * lw, (s + 1) * lw)
        ar = jnp.broadcast_to(abr_ref[:, lanes], (V7X_SUBLANES, lw))
        ai = jnp.broadcast_to(abi_ref[:, lanes], (V7X_SUBLANES, lw))
        er = jnp.zeros((V7X_SUBLANES, lw), F32)
        ei = er
        for i in range(nsteps):
            rows = slice(i * V7X_SUBLANES, (i + 1) * V7X_SUBLANES)
            er, ei = (ar * er - ai * ei + br[rows, :], ar * ei + ai * er + bi[rows, :])
            br[rows, :] = er
            bi[rows, :] = ei
        for d, row in ((1, 0), (2, 1), (4, 3)):
            pr = ajr_ref[row:row + 1, lanes]
            pi = aji_ref[row:row + 1, lanes]
            sr = pltpu.roll(er, d, 0)
            si = pltpu.roll(ei, d, 0)
            keep = sub >= d
            er, ei = (er + jnp.where(keep, pr * sr - pi * si, 0.0),
                      ei + jnp.where(keep, pr * si + pi * sr, 0.0))
        c0r = jnp.broadcast_to(sre_out[0, :, lanes], (V7X_SUBLANES, lw))
        c0i = jnp.broadcast_to(sim_out[0, :, lanes], (V7X_SUBLANES, lw))
        ajr = ajr_ref[:, lanes]
        aji = aji_ref[:, lanes]
        fr = ajr * c0r - aji * c0i + er
        fi = ajr * c0i + aji * c0r + ei
        cin_r = jnp.where(sub >= 1, pltpu.roll(fr, 1, 0), c0r)
        cin_i = jnp.where(sub >= 1, pltpu.roll(fi, 1, 0), c0i)
        sre_out[0, :, lanes] = fr[V7X_SUBLANES - 1:V7X_SUBLANES, :]
        sim_out[0, :, lanes] = fi[V7X_SUBLANES - 1:V7X_SUBLANES, :]

        cin2_r = jnp.concatenate([cin_r, cin_r], axis=0)
        cin2_i = jnp.concatenate([cin_i, cin_i], axis=0)
        for k in range(tc // pair):
            rows = slice(k * pair, (k + 1) * pair)
            pr = tpr_ref[rows, lanes]
            pi = tpi_ref[rows, lanes]
            xo[rows, :lw] = (br[rows, :] + (pr * cin2_r - pi * cin2_i)).astype(BF16)
            xo[rows, lw:] = (bi[rows, :] + (pr * cin2_i + pi * cin2_r)).astype(BF16)

    def output_proj(s):
        y = _dot(xb[s % 2][...], wc_ref[s]) + skip[:, s * V7X_LANES:(s + 1) * V7X_LANES]
        return _gelu_tanh(y).astype(BF16)

    outs = []
    input_proj(0)
    for s in range(ns):
        if s + 1 < ns:
            input_proj(s + 1)
        scan(s)
        outs.append(output_proj(s))
    g_perm = jnp.concatenate(outs, axis=1)
    gy_ref[...] = _dot(permt_ref[...], g_perm).astype(gy_ref.dtype)


def s5_prompt(proj, col_u, n_rows, bsz, seq, sp, *, tc=256):
    db = sp["d_skip"].shape[1]
    nch = sp["ab_re"].shape[1]
    nc = seq // tc
    nsteps = tc // V7X_SUBLANES
    tpr, tpi = (jnp.repeat(a, V7X_SUBLANES, axis=0) for a in _s5_powers(sp, np.arange(1, nsteps + 1)))
    ajr, aji = _s5_powers(sp, nsteps * np.arange(1, V7X_SUBLANES + 1))
    perm = np.zeros((tc, tc), np.float32)
    r = np.arange(tc)
    perm[r, (r % V7X_SUBLANES) * nsteps + r // V7X_SUBLANES] = 1.0
    full = lambda a: pl.BlockSpec(a.shape, lambda b, c: (0,) * a.ndim)
    consts = [sp["wb_re"], sp["wb_im"], sp["wc"], sp["d_skip"], sp["ab_re"], sp["ab_im"], tpr, tpi, ajr, aji,
              jnp.asarray(perm, BF16), jnp.asarray(perm.T, BF16)]
    gy, s_re, s_im = pl.pallas_call(
        functools.partial(_k_s5_prompt, tc=tc),
        grid=(bsz, nc),
        in_specs=[pl.BlockSpec((tc, db), lambda b, c: (b * nc + c, col_u))] + [full(a) for a in consts]
                 + [pl.BlockSpec(memory_space=pl.ANY)],
        out_specs=[pl.BlockSpec((tc, db), lambda b, c: (b * nc + c, 0)),
                   pl.BlockSpec((1, 1, nch), lambda b, c: (b, 0, 0)),
                   pl.BlockSpec((1, 1, nch), lambda b, c: (b, 0, 0))],
        out_shape=[jax.ShapeDtypeStruct((n_rows, db), BF16),
                   jax.ShapeDtypeStruct((bsz, 1, nch), F32), jax.ShapeDtypeStruct((bsz, 1, nch), F32)],
        scratch_shapes=[pltpu.VMEM((tc, S5_SLAB_STATES), F32)] * 4 + [pltpu.VMEM((tc, 2 * S5_SLAB_STATES), BF16)] * 2,
        input_output_aliases={1 + len(consts): 0},
        compiler_params=_cparams(2), name="s5_prompt",
    )(proj, *consts, jnp.zeros((n_rows, db), BF16))
    return gy, s_re, s_im


def _k_s5_sample(u_ref, x0r_ref, x0i_ref, wbr_ref, wbi_ref, wc_ref, d_ref, abr_ref, abi_ref, gy_any,
                 gy_ref, xr_out, xi_out):
    del gy_any
    u = u_ref[...]
    _s5_input_proj(u.astype(BF16), wbr_ref, wbi_ref, xr_out, xi_out)
    ar = abr_ref[...]
    ai = abi_ref[...]
    x0r = x0r_ref[...]
    x0i = x0i_ref[...]
    xr_out[...] = xr_out[...] + (ar * x0r - ai * x0i)
    xi_out[...] = xi_out[...] + (ar * x0i + ai * x0r)
    gy_ref[...] = _s5_output_proj(xr_out, xi_out, wc_ref, d_ref[...] * u).astype(gy_ref.dtype)


def s5_sample(proj, col_u, row0, gy_all, x0_re, x0_im, sp):
    bs = x0_re.shape[0]
    db = sp["d_skip"].shape[1]
    nch = sp["ab_re"].shape[1]
    rb = row0 // bs
    full = lambda a: pl.BlockSpec(a.shape, lambda i: (0,) * a.ndim)
    consts = [sp["wb_re"], sp["wb_im"], sp["wc"], sp["d_skip"], sp["ab_re"], sp["ab_im"]]
    x0r = x0_re.reshape(bs, nch)
    x0i = x0_im.reshape(bs, nch)
    gy_all, xr, xi = pl.pallas_call(
        _k_s5_sample,
        grid=(1,),
        in_specs=[pl.BlockSpec((bs, db), lambda i: (rb, col_u)), full(x0r), full(x0i)]
                 + [full(a) for a in consts] + [pl.BlockSpec(memory_space=pl.ANY)],
        out_specs=[pl.BlockSpec((bs, db), lambda i: (rb, 0)),
                   pl.BlockSpec((bs, nch), lambda i: (0, 0)), pl.BlockSpec((bs, nch), lambda i: (0, 0))],
        out_shape=[jax.ShapeDtypeStruct(gy_all.shape, gy_all.dtype),
                   jax.ShapeDtypeStruct((bs, nch), F32), jax.ShapeDtypeStruct((bs, nch), F32)],
        input_output_aliases={9: 0},
        compiler_params=_cparams(1), name="s5_sample",
    )(proj, x0r, x0i, *consts, gy_all)
    return gy_all, xr, xi


def _k_router(x_ref, g_ref, w_ref, b_ref, hn_ref, lg_ref):
    hn = _rmsnorm_rows(x_ref[...], g_ref[...])
    hn_ref[...] = hn
    x_hi, x_lo = _split_hi_lo(hn)
    w_hi, w_lo = _split_hi_lo(w_ref[...])
    lg_ref[...] = _dot(x_hi, w_hi) + (_dot(x_lo, w_hi) + _dot(x_hi, w_lo)) + b_ref[...]


def router(x, g, w_router, b_router, *, tm):
    t, k = x.shape
    ne = w_router.shape[1]
    w_pad = jnp.pad(w_router, ((0, 0), (0, V7X_LANES - ne)))
    b_pad = jnp.pad(b_router.astype(F32), (0, V7X_LANES - ne)).reshape(1, V7X_LANES)
    hn, lg = pl.pallas_call(
        _k_router, grid=(t // tm,),
        in_specs=[pl.BlockSpec((tm, k), lambda i: (i, 0)), pl.BlockSpec((1, k), lambda i: (0, 0)),
                  pl.BlockSpec((k, V7X_LANES), lambda i: (0, 0)), pl.BlockSpec((1, V7X_LANES), lambda i: (0, 0))],
        out_specs=[pl.BlockSpec((tm, k), lambda i: (i, 0)), pl.BlockSpec((tm, V7X_LANES), lambda i: (i, 0))],
        out_shape=[jax.ShapeDtypeStruct((t, k), F32), jax.ShapeDtypeStruct((t, V7X_LANES), F32)],
        compiler_params=_cparams(1), name="router",
    )(x, g.reshape(1, k), w_pad, b_pad)
    return hn, lg[:, :ne]


DMA_ISSUE_UNROLL = 8


def _k_gather_rows(valid_ref, tok_ref, src_hbm, o_ref, buf_ref, sems, *, tr):
    t = pl.program_id(0)
    last = pl.num_programs(0) - 1
    has_rows = valid_ref[t] > 0
    next_has_rows = jnp.logical_and(t < last, valid_ref[jnp.minimum(t + 1, last)] > 0)

    def issue_tile(tile):
        slot = tile % 2
        base = tile * tr

        def issue(r, c):
            tok = tok_ref[base + r]
            pltpu.make_async_copy(src_hbm.at[pl.ds(tok, 1), :], buf_ref.at[slot, pl.ds(r, 1), :],
                                  sems.at[slot]).start()
            return c

        lax.fori_loop(0, tr, issue, 0, unroll=DMA_ISSUE_UNROLL)

    @pl.when(jnp.logical_and(t == 0, has_rows))
    def _():
        issue_tile(t)

    @pl.when(next_has_rows)
    def _():
        issue_tile(t + 1)

    @pl.when(has_rows)
    def _():
        slot = t % 2
        pltpu.make_async_copy(src_hbm.at[pl.ds(0, tr), :], buf_ref.at[slot], sems.at[slot]).wait()
        o_ref[...] = buf_ref[slot].astype(o_ref.dtype)

    @pl.when(jnp.logical_not(has_rows))
    def _():
        o_ref[...] = jnp.zeros_like(o_ref)


def gather_rows(src, tok, tile_valid, *, tr):
    t, k = src.shape
    r_pad = tok.shape[0]
    nt = r_pad // tr
    return pl.pallas_call(
        functools.partial(_k_gather_rows, tr=tr),
        grid_spec=pltpu.PrefetchScalarGridSpec(
            num_scalar_prefetch=2, grid=(nt,),
            in_specs=[pl.BlockSpec(memory_space=pl.ANY)],
            out_specs=pl.BlockSpec((tr, k), lambda i, nu, tk: (i, 0)),
            scratch_shapes=[pltpu.VMEM((2, tr, k), src.dtype), pltpu.SemaphoreType.DMA((2,))]),
        out_shape=jax.ShapeDtypeStruct((r_pad, k), BF16),
        compiler_params=_cparams(1), name="moe_gather",
    )(tile_valid, tok, src)


def _new_group(te_ref, t):
    return jnp.logical_or(t == 0, te_ref[t] != te_ref[jnp.maximum(t - 1, 0)])


MOE_SUB_ROWS = 256


def _expert_rows(x_ref, o_ref, valid, compute):
    for q in range(x_ref.shape[0] // MOE_SUB_ROWS):
        rows = slice(q * MOE_SUB_ROWS, (q + 1) * MOE_SUB_ROWS)

        @pl.when(q * MOE_SUB_ROWS < valid)
        def _():
            o_ref[rows, :] = compute(x_ref[rows, :]).astype(o_ref.dtype)

        @pl.when(q * MOE_SUB_ROWS >= valid)
        def _():
            o_ref[rows, :] = jnp.zeros((MOE_SUB_ROWS, o_ref.shape[1]), o_ref.dtype)


def _k_moe_up(nused_ref, te_ref, tv_ref, x_ref, wg_ref, wu_ref, o_ref, wgb_ref, wub_ref):
    t = pl.program_id(1)
    valid = tv_ref[t]

    @pl.when(jnp.logical_and(t < nused_ref[0], _new_group(te_ref, t)))
    def _():
        wgb_ref[...] = wg_ref[...].astype(BF16)
        wub_ref[...] = wu_ref[...].astype(BF16)

    _expert_rows(x_ref, o_ref, valid, lambda x: _silu(_dot(x, wgb_ref[...])) * _dot(x, wub_ref[...]))


def moe_up(xs, w_gate, w_up, n_used, tile_expert, tile_valid, *, tr, tn):
    r_pad, k = xs.shape
    ne, _, f = w_gate.shape
    nt = r_pad // tr
    row = lambda j, t, nu, te, tv: (jnp.minimum(t, nu[0] - 1), 0)
    wmap = lambda j, t, nu, te, tv: (te[t], 0, j)
    return pl.pallas_call(
        _k_moe_up,
        grid_spec=pltpu.PrefetchScalarGridSpec(
            num_scalar_prefetch=3, grid=(pl.cdiv(f, tn), nt),
            in_specs=[pl.BlockSpec((tr, k), row),
                      pl.BlockSpec((None, k, tn), wmap), pl.BlockSpec((None, k, tn), wmap)],
            out_specs=pl.BlockSpec((tr, tn), lambda j, t, nu, te, tv: (t, j)),
            scratch_shapes=[pltpu.VMEM((k, tn), BF16), pltpu.VMEM((k, tn), BF16)]),
        out_shape=jax.ShapeDtypeStruct((r_pad, f), BF16),
        compiler_params=_cparams(2), name="moe_up",
    )(n_used, tile_expert, tile_valid, xs, w_gate, w_up)


def _k_moe_down(nused_ref, te_ref, tv_ref, x_ref, w_ref, o_ref, wb_ref):
    t = pl.program_id(1)
    valid = tv_ref[t]

    @pl.when(jnp.logical_and(t < nused_ref[0], _new_group(te_ref, t)))
    def _():
        wb_ref[...] = w_ref[...].astype(BF16)

    _expert_rows(x_ref, o_ref, valid, lambda x: _dot(x, wb_ref[...]))


def moe_down(hid, w_down, n_used, tile_expert, tile_valid, *, tr, tn):
    r_pad, f = hid.shape
    d = w_down.shape[2]
    nt = r_pad // tr
    return pl.pallas_call(
        _k_moe_down,
        grid_spec=pltpu.PrefetchScalarGridSpec(
            num_scalar_prefetch=3, grid=(d // tn, nt),
            in_specs=[pl.BlockSpec((tr, f), lambda j, t, nu, te, tv: (jnp.minimum(t, nu[0] - 1), 0)),
                      pl.BlockSpec((None, f, tn), lambda j, t, nu, te, tv: (te[t], 0, j))],
            out_specs=pl.BlockSpec((tr, tn), lambda j, t, nu, te, tv: (t, j)),
            scratch_shapes=[pltpu.VMEM((f, tn), BF16)]),
        out_shape=jax.ShapeDtypeStruct((r_pad, d), F32),
        compiler_params=_cparams(2), name="moe_down",
    )(n_used, tile_expert, tile_valid, hid, w_down)


def _k_moe_combine(pos_ref, ys_hbm, gate_ref, res_ref, o_ref, buf0_ref, buf1_ref, sems, *, tr):
    t = pl.program_id(0)
    n_tiles = pl.num_programs(0)
    n_tok = n_tiles * tr

    def issue_tile(tile):
        slot = tile % 2
        base = tile * tr

        def issue(r, c):
            p0 = pos_ref[base + r]
            p1 = pos_ref[n_tok + base + r]
            pltpu.make_async_copy(ys_hbm.at[pl.ds(p0, 1), :], buf0_ref.at[slot, pl.ds(r, 1), :],
                                  sems.at[slot]).start()
            pltpu.make_async_copy(ys_hbm.at[pl.ds(p1, 1), :], buf1_ref.at[slot, pl.ds(r, 1), :],
                                  sems.at[slot]).start()
            return c

        lax.fori_loop(0, tr, issue, 0, unroll=DMA_ISSUE_UNROLL)

    @pl.when(t == 0)
    def _():
        issue_tile(t)

    @pl.when(t + 1 < n_tiles)
    def _():
        issue_tile(t + 1)

    slot = t % 2
    pltpu.make_async_copy(ys_hbm.at[pl.ds(0, tr), :], buf0_ref.at[slot], sems.at[slot]).wait()
    pltpu.make_async_copy(ys_hbm.at[pl.ds(0, tr), :], buf1_ref.at[slot], sems.at[slot]).wait()
    g = gate_ref[...]
    o_ref[...] = res_ref[...] + (g[:, 0:1] * buf0_ref[slot] + g[:, 1:2] * buf1_ref[slot])


def moe_combine(ys, pos, gates, res, *, tr):
    t, d = res.shape
    g_pad = jnp.pad(gates, ((0, 0), (0, V7X_LANES - gates.shape[1])))
    return pl.pallas_call(
        functools.partial(_k_moe_combine, tr=tr),
        grid_spec=pltpu.PrefetchScalarGridSpec(
            num_scalar_prefetch=1, grid=(t // tr,),
            in_specs=[pl.BlockSpec(memory_space=pl.ANY),
                      pl.BlockSpec((tr, V7X_LANES), lambda i, p: (i, 0)),
                      pl.BlockSpec((tr, d), lambda i, p: (i, 0))],
            out_specs=pl.BlockSpec((tr, d), lambda i, p: (i, 0)),
            scratch_shapes=[pltpu.VMEM((2, tr, d), F32), pltpu.VMEM((2, tr, d), F32),
                            pltpu.SemaphoreType.DMA((2,))]),
        out_shape=jax.ShapeDtypeStruct((t, d), F32),
        compiler_params=_cparams(1), name="moe_combine",
    )(pos, ys, g_pad, res)


def _subtiles(n_used, tile_expert, tile_valid, tr, sub):
    s = tr // sub
    part = jnp.tile(jnp.arange(s, dtype=jnp.int32) * sub, tile_valid.shape[0])
    valid = jnp.clip(jnp.repeat(tile_valid, s) - part, 0, sub).astype(jnp.int32)
    return n_used * s, jnp.repeat(tile_expert, s), valid


def moe_layer(h, g_ffn, w_router, b_router, w_gate, w_up, w_down, *, tm, tr, tr_down, tn_up, tn_down, tr_gather,
              tr_combine):
    t, d = h.shape
    ne = w_gate.shape[0]
    hn, logits = router(h, g_ffn, w_router, b_router, tm=tm)
    top_v, top_e = lax.top_k(logits, TOP_K)
    gates = jax.nn.softmax(top_v, axis=-1)
    flat_e = top_e.reshape(-1)
    onehot = (flat_e[:, None] == jnp.arange(ne, dtype=flat_e.dtype)[None, :]).astype(jnp.int32)
    rank = jnp.sum((jnp.cumsum(onehot, axis=0) - onehot) * onehot, axis=1)
    sizes = jnp.sum(onehot, axis=0)
    tiles_per = (sizes + tr - 1) // tr
    tile_end = jnp.cumsum(tiles_per)
    tile_start = tile_end - tiles_per
    n_used = tile_end[-1:].astype(jnp.int32)
    nt = (t * TOP_K) // tr + ne
    r_pad = nt * tr
    pos = (tile_start[flat_e] * tr + rank).astype(jnp.int32)
    src_tok = jnp.zeros((r_pad,), jnp.int32).at[pos].set(jnp.arange(t * TOP_K, dtype=jnp.int32) // TOP_K)
    tile_ids = jnp.arange(nt, dtype=jnp.int32)
    used_ids = jnp.minimum(tile_ids, n_used[0] - 1)
    tile_expert = jnp.sum((used_ids[:, None] >= tile_end[None, :]).astype(jnp.int32), axis=1).astype(jnp.int32)
    rows_left = sizes[tile_expert] - (used_ids - tile_start[tile_expert]) * tr
    tile_valid = jnp.where(tile_ids < n_used[0], jnp.clip(rows_left, 0, tr), 0).astype(jnp.int32)
    xs = gather_rows(hn, src_tok, _subtiles(n_used, tile_expert, tile_valid, tr, tr_gather)[2], tr=tr_gather)
    hid = moe_up(xs, w_gate, w_up, n_used, tile_expert, tile_valid, tr=tr, tn=tn_up)
    ys = moe_down(hid, w_down, *_subtiles(n_used, tile_expert, tile_valid, tr, tr_down), tr=tr_down, tn=tn_down)
    return moe_combine(ys, pos.reshape(t, TOP_K).T.reshape(-1), gates, h, tr=tr_combine)


def _layer_weights(i, conv_w, conv_b, w_q, w_k, w_v, b_i, b_f, g_head, skip_a):
    nh = b_i.shape[1]
    pad = jnp.zeros((V7X_LANES - nh,), F32)
    b_if = jnp.concatenate([b_i[i].astype(F32), pad, b_f[i].astype(F32), pad]).reshape(1, 2 * V7X_LANES)
    da = conv_w.shape[-1]
    return dict(conv_w=conv_w[i], conv_b=conv_b[i].reshape(1, da), w_q=w_q[i], w_k=w_k[i], w_v=w_v[i],
                b_if=b_if, g_head=g_head[i].reshape(1, da), skip=skip_a[i].reshape(1, da))


def _tile_plan(n_rows, seq):
    tm = next(c for c in (832, 640, 512, 256, 128, 64, 32, 16) if n_rows % c == 0)
    tc = next(c for c in (320, 256, 128, 64, 32, 16, 8) if n_rows % c == 0)
    tm_big = 2 * tm if n_rows % (2 * tm) == 0 else tm
    return dict(tm=tm, tm_big=tm_big, tn=512, tn_down=256, mlstm_chunk=256, s5_chunk=256,
                moe_tr=1024, moe_tr_down=512, moe_tn_up=512, moe_tn_down=512, gather_tr=256, combine_tr=tc,
                ple_tm=tm // 2,
                norm_tm=min(seq, 1024))


def kernel(x_prompt, x_sample, p_prompt, p_sample, state_mlstm_C, state_mlstm_n, state_mlstm_m, state_mlstm_conv,
           state_s5_re, state_s5_im, g_mix, w_in, conv_w, conv_b, w_q, w_k, w_v, b_i, b_f, g_head, skip_a, w_proj_a,
           s5_log_dt, s5_A_re, s5_A_im, s5_B_re, s5_B_im, s5_C_re, s5_C_im, s5_D, w_glu_b, w_out, g_ffn,
           w_ff_gate, w_ff_up, w_ff_down, w_router, b_router, w_moe_gate, w_moe_up, w_moe_down,
           g_ple, w_ple, w_pg, g_final):
    bsz, seq, d = x_prompt.shape
    bs = x_sample.shape[0]
    depth = g_mix.shape[0]
    nh = b_i.shape[1]
    d_a = conv_w.shape[-1]
    d_b = s5_D.shape[-1]
    n_p = bsz * seq
    t = n_p + bs
    tl = _tile_plan(t, seq)
    tm, tmb, tn = tl["tm"], tl["tm_big"], tl["tn"]

    h = jnp.concatenate([x_prompt.reshape(n_p, d), x_sample.reshape(bs, d)], axis=0).astype(F32)
    p_all = jnp.concatenate([p_prompt.reshape(depth, n_p, -1), p_sample.reshape(depth, bs, -1)], axis=1)

    col_ua, col_oa, col_ub = 0, 1, 2 * d_a // d_b
    col_ga = (2 * d_a + d_b) // tn
    col_gb = (2 * d_a + d_b + d) // tn
    col_if = (2 * d_a + d_b + 2 * d) // (2 * V7X_LANES)

    states = [[] for _ in range(11)]
    c_s_all = jnp.zeros(state_mlstm_C.shape, F32)
    for i in range(depth):
        lw = _layer_weights(i, conv_w, conv_b, w_q, w_k, w_v, b_i, b_f, g_head, skip_a)
        sp = _s5_params(dict(log_dt=s5_log_dt[i], a_re=s5_A_re[i], a_im=s5_A_im[i], b_re=s5_B_re[i],
                             b_im=s5_B_im[i], c_re=s5_C_re[i], c_im=s5_C_im[i], d_skip=s5_D[i]))
        proj = in_proj(h, g_mix[i], w_in, i, d_a, d_b, d, nh, tm=tmb, tn=tn)

        a_pre, c_p, n_pp, m_p, conv_p = mlstm_prompt(proj, col_ua, col_oa, col_if, t, bsz, seq, lw,
                                                     L=tl["mlstm_chunk"])
        a_pre, c_s_all, n_s, m_s, conv_s = mlstm_sample(proj, col_ua, col_oa, col_if, n_p, a_pre,
                                                        state_mlstm_C, i, c_s_all, state_mlstm_n[i].astype(F32),
                                                        state_mlstm_m[i].astype(F32),
                                                        state_mlstm_conv[i].astype(F32), lw)
        gy, sre_p, sim_p = s5_prompt(proj, col_ub, t, bsz, seq, sp, tc=tl["s5_chunk"])
        gy, sre_s, sim_s = s5_sample(proj, col_ub, n_p, gy, state_s5_re[i].astype(F32),
                                     state_s5_im[i].astype(F32), sp)
        mix = branch_mix(a_pre, gy, w_proj_a, w_glu_b, i, proj, col_ga, col_gb, tm=tmb, tn=tn)
        h = matmul(mix, w_out, i, h, tm=tmb, tn=tn, name="out_proj")

        j = i // 2
        if i % 2 == 0:
            hid = norm_swiglu_up(h, g_ffn[i], w_ff_gate, w_ff_up, j, tm=tmb, tn=tn)
            h = matmul(hid, w_ff_down, j, h, tm=tmb, tn=tl["tn_down"], name="ffn_down")
        else:
            h = moe_layer(h, g_ffn[i], w_router[j], b_router[j], w_moe_gate[j], w_moe_up[j], w_moe_down[j],
                          tm=tm, tr=tl["moe_tr"], tr_down=tl["moe_tr_down"], tn_up=tl["moe_tn_up"],
                          tn_down=tl["moe_tn_down"],
                          tr_gather=tl["gather_tr"], tr_combine=tl["combine_tr"])
        h = ple_matmul(h, g_ple[i], p_all, w_pg, w_ple, i, tm=tl["ple_tm"], tn=tn)

        g_s, p_s = S5_STATE, sre_p.shape[-1] // S5_STATE
        new = [c_p, n_pp, m_p, conv_p, sre_p.reshape(bsz, p_s, g_s), sim_p.reshape(bsz, p_s, g_s),
               n_s, m_s, conv_s, sre_s.reshape(bs, p_s, g_s), sim_s.reshape(bs, p_s, g_s)]
        for lst, s in zip(states, new):
            lst.append(s)

    y_prompt, y_sample = final_norm_split(h, g_final, n_p, tm=tl["norm_tm"])
    y_prompt = y_prompt.reshape(bsz, seq, d)
    y_sample = y_sample.reshape(bs, 1, d)
    st = [jnp.stack(lst) for lst in states]
    return (y_prompt, y_sample) + tuple(st[:6]) + (c_s_all,) + tuple(st[6:])
```

```python
import functools
import math

import numpy as np
import jax
import jax.numpy as jnp
from jax import lax
from jax.experimental import pallas as pl
from jax.experimental.pallas import tpu as pltpu

F32 = jnp.float32
BF16 = jnp.bfloat16
EPS = 1e-6

V7X_VMEM_BYTES = 64 * 1024 * 1024
V7X_LANES = 128
V7X_SUBLANES = 8
VMEM_LIMIT = 56 * 1024 * 1024

N_HEADS = 4
CONV_W = 4
S5_GROUP = 16
S5_STATE = 64
N_EXPERTS = 8
TOP_K = 2


def _cparams(n_axes, vmem=VMEM_LIMIT):
    return pltpu.CompilerParams(dimension_semantics=("arbitrary",) * n_axes, vmem_limit_bytes=vmem)


def _sigmoid(x):
    return 1.0 / (1.0 + jnp.exp(-x))


def _silu(x):
    return x * _sigmoid(x)


def _gelu_tanh(x):
    return 0.5 * x * (1.0 + jnp.tanh(math.sqrt(2.0 / math.pi) * (x + 0.044715 * (x * x * x))))


def _log_sigmoid(x):
    return jnp.minimum(x, 0.0) - jnp.log(1.0 + jnp.exp(-jnp.abs(x)))


def _dot(a, b):
    return jnp.dot(a, b, preferred_element_type=F32)


def _dot_nt(a, b):
    return lax.dot_general(a, b, (((1,), (1,)), ((), ())), preferred_element_type=F32)


def _dot_tn(a, b):
    return lax.dot_general(a, b, (((0,), (0,)), ((), ())), preferred_element_type=F32)


def _rmsnorm_rows(x, g):
    ms = jnp.mean(x * x, axis=-1, keepdims=True)
    return x * lax.rsqrt(ms + EPS) * g


def _row_tile_buffering(tm, k, dtype):
    two_copies = 2 * tm * k * jnp.dtype(dtype).itemsize
    return pl.Buffered(1) if two_copies > VMEM_LIMIT // 4 else None


def _row_chunks(tm):
    for rc in (256, 208, 128, 104, 64, 32, 16, 8):
        if tm % rc == 0:
            return rc
    return tm


def _norm_to_scratch(x_ref, g_ref, xn_ref, tm):
    rc = _row_chunks(tm)

    def body(r, c):
        rows = pl.ds(pl.multiple_of(r * rc, rc), rc)
        xn_ref[rows, :] = _rmsnorm_rows(x_ref[rows, :], g_ref[...]).astype(BF16)
        return c

    lax.fori_loop(0, tm // rc, body, 0)


def _wspec(w, layer, tn, col=lambda j: j):
    return pl.BlockSpec((None, w.shape[1], tn), lambda i, j: (layer, 0, col(j)))


def _k_in_proj(x_ref, g_ref, wa_ref, wb_ref, wif_ref, o32_ref, o16_ref, xn_ref, *, tm, n_head, n_f32, n_main, nh):
    j = pl.program_id(1)

    @pl.when(j == 0)
    def _():
        _norm_to_scratch(x_ref, g_ref, xn_ref, tm)

    @pl.when(j < n_head)
    def _():
        o32_ref[...] = _dot_nt(xn_ref[...], wa_ref[0].astype(BF16))

    @pl.when(jnp.logical_and(j >= n_head, j < n_f32))
    def _():
        o32_ref[...] = _dot_nt(xn_ref[...], wb_ref[0].astype(BF16))

    @pl.when(jnp.logical_and(j >= n_f32, j < n_main))
    def _():
        o16_ref[...] = _dot_nt(xn_ref[...], wb_ref[0].astype(BF16)).astype(o16_ref.dtype)

    @pl.when(j == n_main)
    def _():
        pre = _dot_nt(xn_ref[...], wif_ref[0].astype(BF16))
        o32_ref[...] = jnp.zeros_like(o32_ref)
        o32_ref[:, 0:nh] = pre[:, 0:nh]
        o32_ref[:, V7X_LANES:V7X_LANES + nh] = pre[:, nh:2 * nh]


def in_proj(x, g, w_in, layer, d_a, d_b, d, nh, *, tm, tn):
    t, k = x.shape
    assert (2 * nh) % V7X_SUBLANES == 0 and (2 * d_a) % tn == 0 and d_b % tn == 0 and (2 * d) % tn == 0
    wt = jnp.swapaxes(w_in, 1, 2)
    n_head = 2 * d_a // tn
    n_f32 = n_head + d_b // tn
    n_main = n_f32 + 2 * d // tn
    if_row = 2 * d_a

    def rows(nrows, start):
        return pl.BlockSpec((pl.Element(1), pl.Element(nrows), pl.Element(k)), lambda i, j: (layer, start(j), 0))

    o32_col = lambda j: jnp.where(j == n_main, n_f32, jnp.minimum(j, n_f32 - 1))
    o16_col = lambda j: jnp.clip(j - n_f32, 0, n_main - n_f32 - 1)
    return pl.pallas_call(
        functools.partial(_k_in_proj, tm=tm, n_head=n_head, n_f32=n_f32, n_main=n_main, nh=nh),
        grid=(t // tm, n_main + 1),
        in_specs=[pl.BlockSpec((tm, k), lambda i, j: (i, 0), pipeline_mode=_row_tile_buffering(tm, k, x.dtype)),
                  pl.BlockSpec((1, k), lambda i, j: (0, 0)),
                  rows(tn, lambda j: jnp.minimum(j, n_head - 1) * tn),
                  rows(tn, lambda j: (jnp.clip(j, n_head, n_main - 1) * (tn // V7X_SUBLANES)
                                      + 2 * nh // V7X_SUBLANES) * V7X_SUBLANES),
                  rows(2 * nh, lambda j: if_row)],
        out_specs=[pl.BlockSpec((tm, tn), lambda i, j: (i, o32_col(j))),
                   pl.BlockSpec((tm, tn), lambda i, j: (i, o16_col(j)))],
        out_shape=[jax.ShapeDtypeStruct((t, (n_f32 + 1) * tn), F32),
                   jax.ShapeDtypeStruct((t, (n_main - n_f32) * tn), BF16)],
        scratch_shapes=[pltpu.VMEM((tm, k), BF16)],
        compiler_params=_cparams(2), name="in_proj",
    )(x, g.reshape(1, k), wt, wt, wt)


def _k_mm_res(x_ref, w_ref, r_ref, o_ref):
    o_ref[...] = r_ref[...] + _dot(x_ref[...], w_ref[...].astype(BF16))


def _k_mm(x_ref, w_ref, o_ref):
    o_ref[...] = _dot(x_ref[...], w_ref[...].astype(BF16)).astype(o_ref.dtype)


def matmul(x, w, layer, res=None, *, tm, tn, out_dtype=F32, name="mm"):
    t, k = x.shape
    n = w.shape[2]
    in_specs = [pl.BlockSpec((tm, k), lambda i, j: (i, 0), pipeline_mode=_row_tile_buffering(tm, k, x.dtype)),
                _wspec(w, layer, tn)]
    args = [x, w]
    body = _k_mm
    if res is not None:
        in_specs.append(pl.BlockSpec((tm, tn), lambda i, j: (i, j)))
        args.append(res)
        body = _k_mm_res
    return pl.pallas_call(
        body, grid=(t // tm, pl.cdiv(n, tn)), in_specs=in_specs,
        out_specs=pl.BlockSpec((tm, tn), lambda i, j: (i, j)),
        out_shape=jax.ShapeDtypeStruct((t, n), out_dtype),
        compiler_params=_cparams(2), name=name,
    )(*args)


def _k_branch_mix(a_ref, gy_ref, wp_ref, wv_ref, wg_ref, ga_ref, gb_ref, o_ref):
    a = a_ref[...]
    gy = gy_ref[...]
    a_out = _dot(a, wp_ref[...].astype(BF16))
    val = _dot(gy, wv_ref[...].astype(BF16))
    gate = _dot(gy, wg_ref[...].astype(BF16))
    b_out = val * _sigmoid(gate)
    gate_a = _sigmoid(ga_ref[...].astype(F32))
    gate_b = _sigmoid(gb_ref[...].astype(F32))
    o_ref[...] = (gate_a * a_out + gate_b * b_out).astype(o_ref.dtype)


def branch_mix(a_pre, gy, w_proj_a, w_glu_b, layer, gates, *, tm, tn):
    t, k = a_pre.shape
    n = w_proj_a.shape[2]
    nj = n // tn
    rows = pl.BlockSpec((tm, k), lambda i, j: (i, 0), pipeline_mode=pl.Buffered(1))
    return pl.pallas_call(
        _k_branch_mix, grid=(t // tm, nj),
        in_specs=[rows, rows,
                  _wspec(w_proj_a, layer, tn), _wspec(w_glu_b, layer, tn),
                  _wspec(w_glu_b, layer, tn, lambda j: j + nj),
                  pl.BlockSpec((tm, tn), lambda i, j: (i, j)),
                  pl.BlockSpec((tm, tn), lambda i, j: (i, j + nj))],
        out_specs=pl.BlockSpec((tm, tn), lambda i, j: (i, j)),
        out_shape=jax.ShapeDtypeStruct((t, n), BF16),
        compiler_params=_cparams(2), name="branch_mix",
    )(a_pre, gy, w_proj_a, w_glu_b, w_glu_b, gates, gates)


def _k_swiglu_up(x_ref, g_ref, wg_ref, wu_ref, o_ref, xn_ref, *, tm):
    @pl.when(pl.program_id(1) == 0)
    def _():
        _norm_to_scratch(x_ref, g_ref, xn_ref, tm)

    xn = xn_ref[...]
    gate = _dot(xn, wg_ref[...].astype(BF16))
    up = _dot(xn, wu_ref[...].astype(BF16))
    o_ref[...] = (_silu(gate) * up).astype(o_ref.dtype)


def norm_swiglu_up(x, g, w_gate, w_up, layer, *, tm, tn, name="ffn_up"):
    t, k = x.shape
    n = w_gate.shape[2]
    return pl.pallas_call(
        functools.partial(_k_swiglu_up, tm=tm),
        grid=(t // tm, pl.cdiv(n, tn)),
        in_specs=[pl.BlockSpec((tm, k), lambda i, j: (i, 0), pipeline_mode=_row_tile_buffering(tm, k, x.dtype)),
                  pl.BlockSpec((1, k), lambda i, j: (0, 0)),
                  _wspec(w_gate, layer, tn), _wspec(w_up, layer, tn)],
        out_specs=pl.BlockSpec((tm, tn), lambda i, j: (i, j)),
        out_shape=jax.ShapeDtypeStruct((t, n), BF16),
        scratch_shapes=[pltpu.VMEM((tm, k), BF16)],
        compiler_params=_cparams(2), name=name,
    )(x, g.reshape(1, k), w_gate, w_up)


def _cast_rows_to(dst_ref, src_ref, rc):
    def body(r, c):
        rows = pl.ds(pl.multiple_of(r * rc, rc), rc)
        dst_ref[rows, :] = src_ref[rows, :].astype(dst_ref.dtype)
        return c

    lax.fori_loop(0, src_ref.shape[0] // rc, body, 0)


def _k_ple(x_ref, g_ref, p_ref, wpg_ref, wple_ref, o_ref, wpg_bf, wple_bf, *, tn):
    @pl.when(pl.program_id(0) == 0)
    def _():
        _cast_rows_to(wpg_bf, wpg_ref, V7X_LANES)
        _cast_rows_to(wple_bf, wple_ref, V7X_LANES)

    x = x_ref[...]
    xn = _rmsnorm_rows(x, g_ref[...]).astype(BF16)
    pb = p_ref[...].astype(BF16)
    for c in range(x.shape[1] // tn):
        cols = slice(c * tn, (c + 1) * tn)
        gate = _dot(xn, wpg_bf[:, cols])
        emb = _dot(pb, wple_bf[:, cols])
        o_ref[:, cols] = x[:, cols] + emb * _sigmoid(gate)


def ple_matmul(x, g, p, w_pg, w_ple, layer, *, tm, tn, name="ple"):
    t, k = x.shape
    kp = p.shape[2]
    resident = lambda w: pl.BlockSpec((None,) + w.shape[1:], lambda i: (layer, 0, 0), pipeline_mode=pl.Buffered(1))
    return pl.pallas_call(
        functools.partial(_k_ple, tn=tn),
        grid=(t // tm,),
        in_specs=[pl.BlockSpec((tm, k), lambda i: (i, 0)),
                  pl.BlockSpec((1, k), lambda i: (0, 0)),
                  pl.BlockSpec((None, tm, kp), lambda i: (layer, i, 0)),
                  resident(w_pg), resident(w_ple)],
        out_specs=pl.BlockSpec((tm, k), lambda i: (i, 0)),
        out_shape=jax.ShapeDtypeStruct((t, k), F32),
        scratch_shapes=[pltpu.VMEM(w_pg.shape[1:], BF16), pltpu.VMEM(w_ple.shape[1:], BF16)],
        compiler_params=_cparams(1), name=name,
    )(x, g.reshape(1, k), p, w_pg, w_ple)


def _k_final_norm(xp_ref, xs_ref, g_ref, op_ref, os_ref, *, n_prompt_tiles):
    i = pl.program_id(0)

    @pl.when(i < n_prompt_tiles)
    def _():
        op_ref[...] = _rmsnorm_rows(xp_ref[...], g_ref[...])

    @pl.when(i == n_prompt_tiles)
    def _():
        os_ref[...] = _rmsnorm_rows(xs_ref[...], g_ref[...])


def final_norm_split(x, g, n_prompt, *, tm):
    t, k = x.shape
    bs = t - n_prompt
    npt = n_prompt // tm
    last = npt - 1
    return pl.pallas_call(
        functools.partial(_k_final_norm, n_prompt_tiles=npt), grid=(npt + 1,),
        in_specs=[pl.BlockSpec((tm, k), lambda i: (jnp.minimum(i, last), 0)),
                  pl.BlockSpec((bs, k), lambda i: (n_prompt // bs, 0)),
                  pl.BlockSpec((1, k), lambda i: (0, 0))],
        out_specs=[pl.BlockSpec((tm, k), lambda i: (jnp.minimum(i, last), 0)),
                   pl.BlockSpec((bs, k), lambda i: (0, 0))],
        out_shape=[jax.ShapeDtypeStruct((n_prompt, k), F32), jax.ShapeDtypeStruct((bs, k), F32)],
        compiler_params=_cparams(1), name="final_norm",
    )(x, x, g.reshape(1, k))


def _split_hi_lo(x):
    hi = x.astype(BF16)
    lo = (x - hi.astype(F32)).astype(BF16)
    return hi, lo


def _k_mlstm_prompt(u_ref, o_ref_in, if_ref, cw_ref, cb_ref, wq_ref, wk_ref, wv_ref, bif_ref, gh_ref, sk_ref,
                    tri_ref, a_init, a_ref, c_out, n_out, m_out, conv_out, upad_ref, *, L, dh):
    del a_init
    c_idx = pl.program_id(1)
    nh = N_HEADS

    @pl.when(c_idx == 0)
    def _():
        c_out[...] = jnp.zeros_like(c_out)
        n_out[...] = jnp.zeros_like(n_out)
        m_out[...] = jnp.zeros_like(m_out)
        upad_ref[pl.ds(0, 8), :] = jnp.zeros((8, nh * dh), F32)

    @pl.when(c_idx > 0)
    def _():
        upad_ref[pl.ds(0, 8), :] = upad_ref[pl.ds(L, 8), :]

    u = u_ref[...]
    upad_ref[pl.ds(8, L), :] = u
    conv = cb_ref[...] + u * cw_ref[CONV_W - 1:CONV_W, :]
    for j in range(CONV_W - 1):
        conv = conv + upad_ref[pl.ds(8 - (CONV_W - 1) + j, L), :] * cw_ref[j:j + 1, :]
    cact = _silu(conv)
    conv_out[0] = upad_ref[pl.ds(L, 8), :]

    pre = if_ref[...]
    li = pre[:, :V7X_LANES] + bif_ref[:, :V7X_LANES]
    lf = _log_sigmoid(pre[:, V7X_LANES:] + bif_ref[:, V7X_LANES:])
    tri = tri_ref[...]
    lf_hi, lf_mid = _split_hi_lo(lf)
    lf_lo = (lf - lf_hi.astype(F32) - lf_mid.astype(F32)).astype(BF16)
    bcum = _dot(tri, lf_hi) + _dot(tri, lf_mid) + _dot(tri, lf_lo)
    li_t = li.T
    b_t = bcum.T
    row_id = lax.broadcasted_iota(jnp.int32, (L, L), 0)
    col_id = lax.broadcasted_iota(jnp.int32, (L, L), 1)
    causal = col_id <= row_id
    lane = lax.broadcasted_iota(jnp.int32, (1, V7X_LANES), 1)
    m_row = m_out[0]
    m_new_row = m_row

    for h in range(nh):
        hs = slice(h * dh, (h + 1) * dh)
        ch = cact[:, hs].astype(BF16)
        uh = u[:, hs].astype(BF16)
        q = _dot(ch, wq_ref[h].astype(BF16))
        k = _dot(ch, wk_ref[h].astype(BF16)) * (dh ** -0.5)
        v = _dot(uh, wv_ref[h].astype(BF16))
        qb, kb, vb = q.astype(BF16), k.astype(BF16), v.astype(BF16)

        b_col = bcum[:, h:h + 1]
        li_col = li[:, h:h + 1]
        r_row = li_t[h:h + 1, :] - b_t[h:h + 1, :]
        m_prev = m_row[:, h:h + 1]
        d = jnp.where(causal, b_col + r_row, -jnp.inf)
        inter = b_col + m_prev
        m_t = jnp.maximum(inter, jnp.max(d, axis=-1, keepdims=True))
        w_inter = jnp.exp(inter - m_t)
        s = _dot_nt(qb, kb) * jnp.exp(d - m_t)
        c_prev = c_out[0, h]
        n_prev = n_out[0, h:h + 1, :]
        num = w_inter * _dot(qb, c_prev.astype(BF16)) + _dot(s.astype(BF16), vb)
        den = w_inter * jnp.sum(q * n_prev, axis=-1, keepdims=True) + jnp.sum(s, axis=-1, keepdims=True)
        hh = num / jnp.maximum(jnp.abs(den), jnp.exp(-m_t))

        b_last = b_col[L - 1:L, :]
        g_col = b_last - b_col + li_col
        m_new = jnp.maximum(b_last + m_prev, jnp.max(g_col, axis=0, keepdims=True))
        decay = jnp.exp(b_last + m_prev - m_new)
        wk_ = jnp.exp(g_col - m_new) * k
        c_out[0, h] = decay * c_prev + _dot_tn(wk_.astype(BF16), vb)
        n_out[0, h:h + 1, :] = decay * n_prev + jnp.sum(wk_, axis=0, keepdims=True)
        m_new_row = jnp.where(lane == h, m_new, m_new_row)

        hn = _rmsnorm_rows(hh, gh_ref[:, hs])
        gated = (hn + sk_ref[:, hs] * cact[:, hs]) * _sigmoid(o_ref_in[:, hs])
        a_ref[:, hs] = gated.astype(a_ref.dtype)

    m_out[0] = m_new_row


def mlstm_prompt(proj, col_u, col_o, col_if, n_rows, bsz, seq, lw, *, L):
    dh = lw["w_q"].shape[-1]
    da = N_HEADS * dh
    nc = seq // L
    tri = jnp.asarray(np.tril(np.ones((L, L), np.float32)), BF16)
    row_blk = lambda b, c: b * nc + c
    full = lambda *shape: pl.BlockSpec(shape, lambda b, c: (0,) * len(shape))
    outs = pl.pallas_call(
        functools.partial(_k_mlstm_prompt, L=L, dh=dh),
        grid=(bsz, nc),
        in_specs=[pl.BlockSpec((L, da), lambda b, c: (row_blk(b, c), col_u)),
                  pl.BlockSpec((L, da), lambda b, c: (row_blk(b, c), col_o)),
                  pl.BlockSpec((L, 2 * V7X_LANES), lambda b, c: (row_blk(b, c), col_if)),
                  full(CONV_W, da), full(1, da), full(N_HEADS, dh, dh), full(N_HEADS, dh, dh),
                  full(N_HEADS, dh, dh), full(1, 2 * V7X_LANES), full(1, da), full(1, da), full(L, L),
                  pl.BlockSpec(memory_space=pl.ANY)],
        out_specs=[pl.BlockSpec((L, da), lambda b, c: (row_blk(b, c), 0)),
                   pl.BlockSpec((1, N_HEADS, dh, dh), lambda b, c: (b, 0, 0, 0)),
                   pl.BlockSpec((1, N_HEADS, dh), lambda b, c: (b, 0, 0)),
                   pl.BlockSpec((1, 1, V7X_LANES), lambda b, c: (b, 0, 0)),
                   pl.BlockSpec((1, 8, da), lambda b, c: (b, 0, 0))],
        out_shape=[jax.ShapeDtypeStruct((n_rows, da), BF16),
                   jax.ShapeDtypeStruct((bsz, N_HEADS, dh, dh), F32),
                   jax.ShapeDtypeStruct((bsz, N_HEADS, dh), F32),
                   jax.ShapeDtypeStruct((bsz, 1, V7X_LANES), F32),
                   jax.ShapeDtypeStruct((bsz, 8, da), F32)],
        scratch_shapes=[pltpu.VMEM((L + 8, da), F32)],
        input_output_aliases={12: 0},
        compiler_params=_cparams(2), name="mlstm_prompt",
    )(proj, proj, proj, lw["conv_w"], lw["conv_b"], lw["w_q"], lw["w_k"], lw["w_v"], lw["b_if"],
      lw["g_head"], lw["skip"], tri, jnp.zeros((n_rows, da), BF16))
    a_pre, c_p, n_p, m_p, conv_p = outs
    return a_pre, c_p, n_p, m_p[:, 0, :N_HEADS], conv_p[:, 8 - (CONV_W - 1):, :]


def _k_mlstm_sample_pre(u_ref, if_ref, conv_ref, m_ref, cw_ref, cb_ref, wq_ref, wk_ref, wv_ref, bif_ref,
                        q_out, k_out, v_out, c_out, gates_out, conv_out, *, dh):
    nh = N_HEADS
    u = u_ref[...]
    conv = cb_ref[...] + u * cw_ref[CONV_W - 1:CONV_W, :]
    for j in range(CONV_W - 1):
        conv = conv + conv_ref[j] * cw_ref[j:j + 1, :]
        if j > 0:
            conv_out[j - 1] = conv_ref[j]
    conv_out[CONV_W - 2] = u
    cact = _silu(conv)
    c_out[...] = cact
    for h in range(nh):
        hs = slice(h * dh, (h + 1) * dh)
        ch = cact[:, hs].astype(BF16)
        q_out[:, hs] = _dot(ch, wq_ref[h].astype(BF16))
        k_out[:, hs] = _dot(ch, wk_ref[h].astype(BF16)) * (dh ** -0.5)
        v_out[:, hs] = _dot(u[:, hs].astype(BF16), wv_ref[h].astype(BF16))
    pre = if_ref[...]
    li = pre[:, :V7X_LANES] + bif_ref[:, :V7X_LANES]
    lf = _log_sigmoid(pre[:, V7X_LANES:] + bif_ref[:, V7X_LANES:])
    m_prev = m_ref[...]
    inter = lf + m_prev
    m_t = jnp.maximum(inter, li)
    gates_out[0] = jnp.exp(inter - m_t)
    gates_out[1] = jnp.exp(li - m_t)
    gates_out[2] = jnp.exp(-m_t)
    gates_out[3] = m_t


def _k_mlstm_sample_step(q_ref, k_ref, v_ref, gates_ref, c_ref, n_ref, cact_ref, o_ref_in, gh_ref, sk_ref,
                         a_any, c_any, c_out, n_out, a_out, hh_ref, *, bt, dh):
    del a_any, c_any
    i = pl.program_id(0)
    nh = N_HEADS
    q = q_ref[...]
    k = k_ref[...]
    v = v_ref[...]
    w_inter = gates_ref[0]
    w_new = gates_ref[1]
    e_neg_m = gates_ref[2]
    n_prev = n_ref[...]
    rows = pl.ds(pl.multiple_of(i * bt, bt), bt)
    for h in range(nh):
        hs = slice(h * dh, (h + 1) * dh)
        qh, kh, vh, nh_prev = q[:, hs], k[:, hs], v[:, hs], n_prev[:, hs]
        q_t = qh.T
        k_t = kh.T
        wi = w_inter[:, h:h + 1]
        wn = w_new[:, h:h + 1]
        s = jnp.sum(qh * kh, axis=-1, keepdims=True) * wn
        den = wi * jnp.sum(qh * nh_prev, axis=-1, keepdims=True) + s
        wv = wn * vh
        qc_rows = []
        for bl in range(bt):
            c_prev = c_ref[bl, h]
            qc_rows.append(jnp.sum(q_t[:, bl:bl + 1] * c_prev, axis=0, keepdims=True))
            c_out[bl, h] = wi[bl:bl + 1, :] * c_prev + k_t[:, bl:bl + 1] * wv[bl:bl + 1, :]
        qc = jnp.concatenate(qc_rows, axis=0)
        num = wi * qc + s * vh
        hh = num / jnp.maximum(jnp.abs(den), e_neg_m[:, h:h + 1])
        hh_ref[rows, hs] = hh
        n_out[:, hs] = wi * nh_prev + wn * kh

    @pl.when(i == pl.num_programs(0) - 1)
    def _():
        for h in range(nh):
            hs = slice(h * dh, (h + 1) * dh)
            hn = _rmsnorm_rows(hh_ref[:, hs], gh_ref[:, hs])
            a_out[:, hs] = ((hn + sk_ref[:, hs] * cact_ref[:, hs]) * _sigmoid(o_ref_in[:, hs])).astype(a_out.dtype)


def mlstm_sample(proj, col_u, col_o, col_if, row0, a_pre_all, c_all, layer, c_new_all, n0, m0, conv0, lw, *,
                 bt=8):
    _, bs, nh, dh, _ = c_all.shape
    da = nh * dh
    rb = row0 // bs
    m_pad = jnp.pad(m0, ((0, 0), (0, V7X_LANES - nh)))
    conv_t = jnp.transpose(conv0, (1, 0, 2))
    full = lambda *shape: pl.BlockSpec(shape, lambda i: (0,) * len(shape))
    q, k, v, cact, gates, conv_new = pl.pallas_call(
        functools.partial(_k_mlstm_sample_pre, dh=dh),
        grid=(1,),
        in_specs=[pl.BlockSpec((bs, da), lambda i: (rb, col_u)),
                  pl.BlockSpec((bs, 2 * V7X_LANES), lambda i: (rb, col_if)),
                  full(CONV_W - 1, bs, da), full(bs, V7X_LANES), full(CONV_W, da), full(1, da),
                  full(nh, dh, dh), full(nh, dh, dh), full(nh, dh, dh), full(1, 2 * V7X_LANES)],
        out_specs=[full(bs, da), full(bs, da), full(bs, da), full(bs, da), full(4, bs, V7X_LANES),
                   full(CONV_W - 1, bs, da)],
        out_shape=[jax.ShapeDtypeStruct((bs, da), F32)] * 4
                  + [jax.ShapeDtypeStruct((4, bs, V7X_LANES), F32),
                     jax.ShapeDtypeStruct((CONV_W - 1, bs, da), F32)],
        compiler_params=_cparams(1), name="mlstm_sample_pre",
    )(proj, proj, conv_t, m_pad, lw["conv_w"], lw["conv_b"], lw["w_q"], lw["w_k"], lw["w_v"], lw["b_if"])

    blk = lambda *shape: pl.BlockSpec(shape, lambda i: (i,) + (0,) * (len(shape) - 1))
    cst = lambda *shape: pl.BlockSpec(shape, lambda i: (0,) * len(shape))
    c_blk = pl.BlockSpec((None, bt, nh, dh, dh), lambda i: (layer, i, 0, 0, 0))
    c_new_all, n_new, a_pre_all = pl.pallas_call(
        functools.partial(_k_mlstm_sample_step, bt=bt, dh=dh),
        grid=(bs // bt,),
        in_specs=[blk(bt, da), blk(bt, da), blk(bt, da),
                  pl.BlockSpec((4, bt, V7X_LANES), lambda i: (0, i, 0)),
                  c_blk, blk(bt, da), cst(bs, da),
                  pl.BlockSpec((bs, da), lambda i: (rb, col_o)), cst(1, da), cst(1, da),
                  pl.BlockSpec(memory_space=pl.ANY), pl.BlockSpec(memory_space=pl.ANY)],
        out_specs=[c_blk, blk(bt, da), pl.BlockSpec((bs, da), lambda i: (rb, 0))],
        out_shape=[jax.ShapeDtypeStruct(c_new_all.shape, F32), jax.ShapeDtypeStruct((bs, da), F32),
                   jax.ShapeDtypeStruct(a_pre_all.shape, a_pre_all.dtype)],
        scratch_shapes=[pltpu.VMEM((bs, da), F32)],
        input_output_aliases={10: 2, 11: 0},
        compiler_params=_cparams(1), name="mlstm_sample_step",
    )(q, k, v, gates, c_all, n0.reshape(bs, da), cact, proj, lw["g_head"], lw["skip"], a_pre_all, c_new_all)
    m_new = gates[3][:, :nh]
    return a_pre_all, c_new_all, n_new.reshape(bs, nh, dh), m_new, jnp.transpose(conv_new, (1, 0, 2))


S5_SLAB_GROUPS = V7X_LANES // S5_GROUP
S5_SLAB_STATES = S5_SLAB_GROUPS * S5_STATE


def _s5_params(lp):
    g, p = lp["a_re"].shape
    dt = jnp.exp(lp["log_dt"].astype(F32))[:, None]
    a_re = lp["a_re"].astype(F32)
    a_im = lp["a_im"].astype(F32)
    lam_re = a_re * dt
    lam_im = a_im * dt
    mag = jnp.exp(lam_re)
    ab_re = mag * jnp.cos(lam_im)
    ab_im = mag * jnp.sin(lam_im)
    den = a_re * a_re + a_im * a_im
    nr = ab_re - 1.0
    ni = ab_im
    k_re = (nr * a_re + ni * a_im) / den
    k_im = (ni * a_re - nr * a_im) / den
    b_re = lp["b_re"].astype(F32)
    b_im = lp["b_im"].astype(F32)
    bb_re = k_re[..., None] * b_re - k_im[..., None] * b_im
    bb_im = k_re[..., None] * b_im + k_im[..., None] * b_re
    ns = g // S5_SLAB_GROUPS
    eye = jnp.eye(S5_SLAB_GROUPS, dtype=F32)

    def in_blockdiag(bb):
        bs = bb.reshape(ns, S5_SLAB_GROUPS, p, S5_GROUP)
        w = jnp.einsum("ab,sapc->sacbp", eye, bs)
        return w.reshape(ns, V7X_LANES, S5_SLAB_STATES).astype(BF16)

    def out_blockdiag(cc):
        cs = cc.astype(F32).reshape(ns, S5_SLAB_GROUPS, S5_GROUP, p)
        w = jnp.einsum("ab,sacp->sapbc", eye, cs)
        return w.reshape(ns, S5_SLAB_STATES, V7X_LANES)

    wc = jnp.concatenate([out_blockdiag(lp["c_re"]), -out_blockdiag(lp["c_im"])], axis=1).astype(BF16)
    return dict(lam_re=lam_re.reshape(1, g * p), lam_im=lam_im.reshape(1, g * p),
                ab_re=ab_re.reshape(1, g * p), ab_im=ab_im.reshape(1, g * p),
                wb_re=in_blockdiag(bb_re), wb_im=in_blockdiag(bb_im), wc=wc,
                d_skip=lp["d_skip"].astype(F32).reshape(1, g * S5_GROUP))


def _s5_powers(sp, ks):
    kk = jnp.asarray(ks, F32)[:, None]
    mag = jnp.exp(kk * sp["lam_re"])
    return mag * jnp.cos(kk * sp["lam_im"]), mag * jnp.sin(kk * sp["lam_im"])


def _s5_input_proj(u_bf, wbr_ref, wbi_ref, bur_ref, bui_ref):
    ns = wbr_ref.shape[0]
    for s in range(ns):
        us = u_bf[:, s * V7X_LANES:(s + 1) * V7X_LANES]
        cols = slice(s * S5_SLAB_STATES, (s + 1) * S5_SLAB_STATES)
        bur_ref[:, cols] = _dot(us, wbr_ref[s])
        bui_ref[:, cols] = _dot(us, wbi_ref[s])


def _s5_output_proj(xr_ref, xi_ref, wc_ref, skip):
    ns = wc_ref.shape[0]
    outs = []
    for s in range(ns):
        cols = slice(s * S5_SLAB_STATES, (s + 1) * S5_SLAB_STATES)
        xcat = jnp.concatenate([xr_ref[:, cols].astype(BF16), xi_ref[:, cols].astype(BF16)], axis=1)
        y = _dot(xcat, wc_ref[s]) + skip[:, s * V7X_LANES:(s + 1) * V7X_LANES]
        outs.append(_gelu_tanh(y).astype(BF16))
    return jnp.concatenate(outs, axis=1)


def _k_s5_prompt(u_ref, wbr_ref, wbi_ref, wc_ref, d_ref, abr_ref, abi_ref, tpr_ref, tpi_ref, ajr_ref, aji_ref,
                 perm_ref, permt_ref, gy_init, gy_ref, sre_out, sim_out,
                 bur0, bur1, bui0, bui1, xb0, xb1, *, tc):
    del gy_init
    c_idx = pl.program_id(1)
    nsteps = tc // V7X_SUBLANES
    ns = wbr_ref.shape[0]
    lw = S5_SLAB_STATES
    pair = 2 * V7X_SUBLANES
    bur, bui, xb = (bur0, bur1), (bui0, bui1), (xb0, xb1)

    @pl.when(c_idx == 0)
    def _():
        sre_out[...] = jnp.zeros_like(sre_out)
        sim_out[...] = jnp.zeros_like(sim_out)

    u = u_ref[...]
    u_hi, u_lo = _split_hi_lo(u)
    perm = perm_ref[...]
    up_hi = _dot(perm, u_hi)
    skip = d_ref[...] * (up_hi + _dot(perm, u_lo))
    up_bf = up_hi.astype(BF16)

    sub = lax.broadcasted_iota(jnp.int32, (V7X_SUBLANES, lw), 0)

    def input_proj(s):
        us = up_bf[:, s * V7X_LANES:(s + 1) * V7X_LANES]
        bur[s % 2][...] = _dot(us, wbr_ref[s])
        bui[s % 2][...] = _dot(us, wbi_ref[s])

    def scan(s):
        br, bi, xo = bur[s % 2], bui[s % 2], xb[s % 2]
        lanes = slice(s * lw, (s + 1) * lw)
        ar = jnp.broadcast_to(abr_ref[:, lanes], (V7X_SUBLANES, lw))
        ai = jnp.broadcast_to(abi_ref[:, lanes], (V7X_SUBLANES, lw))
        er = jnp.zeros((V7X_SUBLANES, lw), F32)
        ei = er
        for i in range(nsteps):
            rows = slice(i * V7X_SUBLANES, (i + 1) * V7X_SUBLANES)
            er, ei = (ar * er - ai * ei + br[rows, :], ar * ei + ai * er + bi[rows, :])
            br[rows, :] = er
            bi[rows, :] = ei
        for d, row in ((1, 0), (2, 1), (4, 3)):
            pr = ajr_ref[row:row + 1, lanes]
            pi = aji_ref[row:row + 1, lanes]
            sr = pltpu.roll(er, d, 0)
            si = pltpu.roll(ei, d, 0)
            keep = sub >= d
            er, ei = (er + jnp.where(keep, pr * sr - pi * si, 0.0),
                      ei + jnp.where(keep, pr * si + pi * sr, 0.0))
        c0r = jnp.broadcast_to(sre_out[0, :, lanes], (V7X_SUBLANES, lw))
        c0i = jnp.broadcast_to(sim_out[0, :, lanes], (V7X_SUBLANES, lw))
        ajr = ajr_ref[:, lanes]
        aji = aji_ref[:, lanes]
        fr = ajr * c0r - aji * c0i + er
        fi = ajr * c0i + aji * c0r + ei
        cin_r = jnp.where(sub >= 1, pltpu.roll(fr, 1, 0), c0r)
        cin_i = jnp.where(sub >= 1, pltpu.roll(fi, 1, 0), c0i)
        sre_out[0, :, lanes] = fr[V7X_SUBLANES - 1:V7X_SUBLANES, :]
        sim_out[0, :, lanes] = fi[V7X_SUBLANES - 1:V7X_SUBLANES, :]

        cin2_r = jnp.concatenate([cin_r, cin_r], axis=0)
        cin2_i = jnp.concatenate([cin_i, cin_i], axis=0)
        for k in range(tc // pair):
            rows = slice(k * pair, (k + 1) * pair)
            pr = tpr_ref[rows, lanes]
            pi = tpi_ref[rows, lanes]
            xo[rows, :lw] = (br[rows, :] + (pr * cin2_r - pi * cin2_i)).astype(BF16)
            xo[rows, lw:] = (bi[rows, :] + (pr * cin2_i + pi * cin2_r)).astype(BF16)

    def output_proj(s):
        y = _dot(xb[s % 2][...], wc_ref[s]) + skip[:, s * V7X_LANES:(s + 1) * V7X_LANES]
        return _gelu_tanh(y).astype(BF16)

    outs = []
    input_proj(0)
    for s in range(ns):
        if s + 1 < ns:
            input_proj(s + 1)
        scan(s)
        outs.append(output_proj(s))
    g_perm = jnp.concatenate(outs, axis=1)
    gy_ref[...] = _dot(permt_ref[...], g_perm).astype(gy_ref.dtype)


def s5_prompt(proj, col_u, n_rows, bsz, seq, sp, *, tc=256):
    db = sp["d_skip"].shape[1]
    nch = sp["ab_re"].shape[1]
    nc = seq // tc
    nsteps = tc // V7X_SUBLANES
    tpr, tpi = (jnp.repeat(a, V7X_SUBLANES, axis=0) for a in _s5_powers(sp, np.arange(1, nsteps + 1)))
    ajr, aji = _s5_powers(sp, nsteps * np.arange(1, V7X_SUBLANES + 1))
    perm = np.zeros((tc, tc), np.float32)
    r = np.arange(tc)
    perm[r, (r % V7X_SUBLANES) * nsteps + r // V7X_SUBLANES] = 1.0
    full = lambda a: pl.BlockSpec(a.shape, lambda b, c: (0,) * a.ndim)
    consts = [sp["wb_re"], sp["wb_im"], sp["wc"], sp["d_skip"], sp["ab_re"], sp["ab_im"], tpr, tpi, ajr, aji,
              jnp.asarray(perm, BF16), jnp.asarray(perm.T, BF16)]
    gy, s_re, s_im = pl.pallas_call(
        functools.partial(_k_s5_prompt, tc=tc),
        grid=(bsz, nc),
        in_specs=[pl.BlockSpec((tc, db), lambda b, c: (b * nc + c, col_u))] + [full(a) for a in consts]
                 + [pl.BlockSpec(memory_space=pl.ANY)],
        out_specs=[pl.BlockSpec((tc, db), lambda b, c: (b * nc + c, 0)),
                   pl.BlockSpec((1, 1, nch), lambda b, c: (b, 0, 0)),
                   pl.BlockSpec((1, 1, nch), lambda b, c: (b, 0, 0))],
        out_shape=[jax.ShapeDtypeStruct((n_rows, db), BF16),
                   jax.ShapeDtypeStruct((bsz, 1, nch), F32), jax.ShapeDtypeStruct((bsz, 1, nch), F32)],
        scratch_shapes=[pltpu.VMEM((tc, S5_SLAB_STATES), F32)] * 4 + [pltpu.VMEM((tc, 2 * S5_SLAB_STATES), BF16)] * 2,
        input_output_aliases={1 + len(consts): 0},
        compiler_params=_cparams(2), name="s5_prompt",
    )(proj, *consts, jnp.zeros((n_rows, db), BF16))
    return gy, s_re, s_im


def _k_s5_sample(u_ref, x0r_ref, x0i_ref, wbr_ref, wbi_ref, wc_ref, d_ref, abr_ref, abi_ref, gy_any,
                 gy_ref, xr_out, xi_out):
    del gy_any
    u = u_ref[...]
    _s5_input_proj(u.astype(BF16), wbr_ref, wbi_ref, xr_out, xi_out)
    ar = abr_ref[...]
    ai = abi_ref[...]
    x0r = x0r_ref[...]
    x0i = x0i_ref[...]
    xr_out[...] = xr_out[...] + (ar * x0r - ai * x0i)
    xi_out[...] = xi_out[...] + (ar * x0i + ai * x0r)
    gy_ref[...] = _s5_output_proj(xr_out, xi_out, wc_ref, d_ref[...] * u).astype(gy_ref.dtype)


def s5_sample(proj, col_u, row0, gy_all, x0_re, x0_im, sp):
    bs = x0_re.shape[0]
    db = sp["d_skip"].shape[1]
    nch = sp["ab_re"].shape[1]
    rb = row0 // bs
    full = lambda a: pl.BlockSpec(a.shape, lambda i: (0,) * a.ndim)
    consts = [sp["wb_re"], sp["wb_im"], sp["wc"], sp["d_skip"], sp["ab_re"], sp["ab_im"]]
    x0r = x0_re.reshape(bs, nch)
    x0i = x0_im.reshape(bs, nch)
    gy_all, xr, xi = pl.pallas_call(
        _k_s5_sample,
        grid=(1,),
        in_specs=[pl.BlockSpec((bs, db), lambda i: (rb, col_u)), full(x0r), full(x0i)]
                 + [full(a) for a in consts] + [pl.BlockSpec(memory_space=pl.ANY)],
        out_specs=[pl.BlockSpec((bs, db), lambda i: (rb, 0)),
                   pl.BlockSpec((bs, nch), lambda i: (0, 0)), pl.BlockSpec((bs, nch), lambda i: (0, 0))],
        out_shape=[jax.ShapeDtypeStruct(gy_all.shape, gy_all.dtype),
                   jax.ShapeDtypeStruct((bs, nch), F32), jax.ShapeDtypeStruct((bs, nch), F32)],
        input_output_aliases={9: 0},
        compiler_params=_cparams(1), name="s5_sample",
    )(proj, x0r, x0i, *consts, gy_all)
    return gy_all, xr, xi


def _k_router(x_ref, g_ref, w_ref, b_ref, hn_ref, lg_ref):
    hn = _rmsnorm_rows(x_ref[...], g_ref[...])
    hn_ref[...] = hn
    x_hi, x_lo = _split_hi_lo(hn)
    w_hi, w_lo = _split_hi_lo(w_ref[...])
    lg_ref[...] = _dot(x_hi, w_hi) + (_dot(x_lo, w_hi) + _dot(x_hi, w_lo)) + b_ref[...]


def router(x, g, w_router, b_router, *, tm):
    t, k = x.shape
    ne = w_router.shape[1]
    w_pad = jnp.pad(w_router, ((0, 0), (0, V7X_LANES - ne)))
    b_pad = jnp.pad(b_router.astype(F32), (0, V7X_LANES - ne)).reshape(1, V7X_LANES)
    hn, lg = pl.pallas_call(
        _k_router, grid=(t // tm,),
        in_specs=[pl.BlockSpec((tm, k), lambda i: (i, 0)), pl.BlockSpec((1, k), lambda i: (0, 0)),
                  pl.BlockSpec((k, V7X_LANES), lambda i: (0, 0)), pl.BlockSpec((1, V7X_LANES), lambda i: (0, 0))],
        out_specs=[pl.BlockSpec((tm, k), lambda i: (i, 0)), pl.BlockSpec((tm, V7X_LANES), lambda i: (i, 0))],
        out_shape=[jax.ShapeDtypeStruct((t, k), F32), jax.ShapeDtypeStruct((t, V7X_LANES), F32)],
        compiler_params=_cparams(1), name="router",
    )(x, g.reshape(1, k), w_pad, b_pad)
    return hn, lg[:, :ne]


DMA_ISSUE_UNROLL = 8


def _k_gather_rows(nused_ref, tok_ref, src_hbm, o_ref, buf_ref, sems, *, tr):
    t = pl.program_id(0)
    n_used = nused_ref[0]

    def issue_tile(tile):
        slot = tile % 2
        base = tile * tr

        def issue(r, c):
            tok = tok_ref[base + r]
            pltpu.make_async_copy(src_hbm.at[pl.ds(tok, 1), :], buf_ref.at[slot, pl.ds(r, 1), :],
                                  sems.at[slot]).start()
            return c

        lax.fori_loop(0, tr, issue, 0, unroll=DMA_ISSUE_UNROLL)

    @pl.when(t == 0)
    def _():
        issue_tile(t)

    @pl.when(t + 1 < n_used)
    def _():
        issue_tile(t + 1)

    @pl.when(t < n_used)
    def _():
        slot = t % 2
        pltpu.make_async_copy(src_hbm.at[pl.ds(0, tr), :], buf_ref.at[slot], sems.at[slot]).wait()
        o_ref[...] = buf_ref[slot].astype(o_ref.dtype)

    @pl.when(t >= n_used)
    def _():
        o_ref[...] = jnp.zeros_like(o_ref)


def gather_rows(src, tok, n_used, *, tr):
    t, k = src.shape
    r_pad = tok.shape[0]
    nt = r_pad // tr
    return pl.pallas_call(
        functools.partial(_k_gather_rows, tr=tr),
        grid_spec=pltpu.PrefetchScalarGridSpec(
            num_scalar_prefetch=2, grid=(nt,),
            in_specs=[pl.BlockSpec(memory_space=pl.ANY)],
            out_specs=pl.BlockSpec((tr, k), lambda i, nu, tk: (i, 0)),
            scratch_shapes=[pltpu.VMEM((2, tr, k), src.dtype), pltpu.SemaphoreType.DMA((2,))]),
        out_shape=jax.ShapeDtypeStruct((r_pad, k), BF16),
        compiler_params=_cparams(1), name="moe_gather",
    )(n_used, tok, src)


def _k_moe_up(nused_ref, te_ref, x_ref, wg_ref, wu_ref, o_ref):
    del te_ref
    t = pl.program_id(1)

    @pl.when(t < nused_ref[0])
    def _():
        x = x_ref[...]
        gate = _dot(x, wg_ref[...].astype(BF16))
        up = _dot(x, wu_ref[...].astype(BF16))
        o_ref[...] = (_silu(gate) * up).astype(o_ref.dtype)

    @pl.when(t >= nused_ref[0])
    def _():
        o_ref[...] = jnp.zeros_like(o_ref)


def moe_up(xs, w_gate, w_up, tile_expert, n_used, *, tr, tn):
    r_pad, k = xs.shape
    ne, _, f = w_gate.shape
    nt = r_pad // tr
    row = lambda j, t, nu, te: (jnp.minimum(t, nu[0] - 1), 0)
    wmap = lambda j, t, nu, te: (te[t], 0, j)
    return pl.pallas_call(
        _k_moe_up,
        grid_spec=pltpu.PrefetchScalarGridSpec(
            num_scalar_prefetch=2, grid=(pl.cdiv(f, tn), nt),
            in_specs=[pl.BlockSpec((tr, k), row),
                      pl.BlockSpec((None, k, tn), wmap), pl.BlockSpec((None, k, tn), wmap)],
            out_specs=pl.BlockSpec((tr, tn), lambda j, t, nu, te: (t, j))),
        out_shape=jax.ShapeDtypeStruct((r_pad, f), BF16),
        compiler_params=_cparams(2), name="moe_up",
    )(n_used, tile_expert, xs, w_gate, w_up)


def _k_moe_down(nused_ref, te_ref, x_ref, w_ref, o_ref):
    del te_ref
    t = pl.program_id(1)

    @pl.when(t < nused_ref[0])
    def _():
        o_ref[...] = _dot(x_ref[...], w_ref[...].astype(BF16))

    @pl.when(t >= nused_ref[0])
    def _():
        o_ref[...] = jnp.zeros_like(o_ref)


def moe_down(hid, w_down, tile_expert, n_used, *, tr, tn):
    r_pad, f = hid.shape
    d = w_down.shape[2]
    nt = r_pad // tr
    return pl.pallas_call(
        _k_moe_down,
        grid_spec=pltpu.PrefetchScalarGridSpec(
            num_scalar_prefetch=2, grid=(d // tn, nt),
            in_specs=[pl.BlockSpec((tr, f), lambda j, t, nu, te: (jnp.minimum(t, nu[0] - 1), 0)),
                      pl.BlockSpec((None, f, tn), lambda j, t, nu, te: (te[t], 0, j))],
            out_specs=pl.BlockSpec((tr, tn), lambda j, t, nu, te: (t, j))),
        out_shape=jax.ShapeDtypeStruct((r_pad, d), F32),
        compiler_params=_cparams(2), name="moe_down",
    )(n_used, tile_expert, hid, w_down)


def _k_moe_combine(pos_ref, ys_hbm, gate_ref, res_ref, o_ref, buf0_ref, buf1_ref, sems, *, tr):
    t = pl.program_id(0)
    n_tiles = pl.num_programs(0)
    n_tok = n_tiles * tr

    def issue_tile(tile):
        slot = tile % 2
        base = tile * tr

        def issue(r, c):
            p0 = pos_ref[base + r]
            p1 = pos_ref[n_tok + base + r]
            pltpu.make_async_copy(ys_hbm.at[pl.ds(p0, 1), :], buf0_ref.at[slot, pl.ds(r, 1), :],
                                  sems.at[slot]).start()
            pltpu.make_async_copy(ys_hbm.at[pl.ds(p1, 1), :], buf1_ref.at[slot, pl.ds(r, 1), :],
                                  sems.at[slot]).start()
            return c

        lax.fori_loop(0, tr, issue, 0, unroll=DMA_ISSUE_UNROLL)

    @pl.when(t == 0)
    def _():
        issue_tile(t)

    @pl.when(t + 1 < n_tiles)
    def _():
        issue_tile(t + 1)

    slot = t % 2
    pltpu.make_async_copy(ys_hbm.at[pl.ds(0, tr), :], buf0_ref.at[slot], sems.at[slot]).wait()
    pltpu.make_async_copy(ys_hbm.at[pl.ds(0, tr), :], buf1_ref.at[slot], sems.at[slot]).wait()
    g = gate_ref[...]
    o_ref[...] = res_ref[...] + (g[:, 0:1] * buf0_ref[slot] + g[:, 1:2] * buf1_ref[slot])


def moe_combine(ys, pos, gates, res, *, tr):
    t, d = res.shape
    g_pad = jnp.pad(gates, ((0, 0), (0, V7X_LANES - gates.shape[1])))
    return pl.pallas_call(
        functools.partial(_k_moe_combine, tr=tr),
        grid_spec=pltpu.PrefetchScalarGridSpec(
            num_scalar_prefetch=1, grid=(t // tr,),
            in_specs=[pl.BlockSpec(memory_space=pl.ANY),
                      pl.BlockSpec((tr, V7X_LANES), lambda i, p: (i, 0)),
                      pl.BlockSpec((tr, d), lambda i, p: (i, 0))],
            out_specs=pl.BlockSpec((tr, d), lambda i, p: (i, 0)),
            scratch_shapes=[pltpu.VMEM((2, tr, d), F32), pltpu.VMEM((2, tr, d), F32),
                            pltpu.SemaphoreType.DMA((2,))]),
        out_shape=jax.ShapeDtypeStruct((t, d), F32),
        compiler_params=_cparams(1), name="moe_combine",
    )(pos, ys, g_pad, res)


def moe_layer(h, g_ffn, w_router, b_router, w_gate, w_up, w_down, *, tm, tr, tn_up, tn_down, tr_gather,
              tr_combine):
    t, d = h.shape
    ne = w_gate.shape[0]
    hn, logits = router(h, g_ffn, w_router, b_router, tm=tm)
    top_v, top_e = lax.top_k(logits, TOP_K)
    gates = jax.nn.softmax(top_v, axis=-1)
    flat_e = top_e.reshape(-1)
    onehot = (flat_e[:, None] == jnp.arange(ne, dtype=flat_e.dtype)[None, :]).astype(jnp.int32)
    rank = jnp.sum((jnp.cumsum(onehot, axis=0) - onehot) * onehot, axis=1)
    sizes = jnp.sum(onehot, axis=0)
    tiles_per = (sizes + tr - 1) // tr
    tile_end = jnp.cumsum(tiles_per)
    tile_start = tile_end - tiles_per
    n_used = tile_end[-1:].astype(jnp.int32)
    nt = (t * TOP_K) // tr + ne
    r_pad = nt * tr
    pos = (tile_start[flat_e] * tr + rank).astype(jnp.int32)
    src_tok = jnp.zeros((r_pad,), jnp.int32).at[pos].set(jnp.arange(t * TOP_K, dtype=jnp.int32) // TOP_K)
    tile_ids = jnp.minimum(jnp.arange(nt, dtype=jnp.int32), n_used[0] - 1)
    tile_expert = jnp.sum((tile_ids[:, None] >= tile_end[None, :]).astype(jnp.int32), axis=1).astype(jnp.int32)
    xs = gather_rows(hn, src_tok, n_used * (tr // tr_gather), tr=tr_gather)
    hid = moe_up(xs, w_gate, w_up, tile_expert, n_used, tr=tr, tn=tn_up)
    ys = moe_down(hid, w_down, tile_expert, n_used, tr=tr, tn=tn_down)
    return moe_combine(ys, pos.reshape(t, TOP_K).T.reshape(-1), gates, h, tr=tr_combine)


def _layer_weights(i, conv_w, conv_b, w_q, w_k, w_v, b_i, b_f, g_head, skip_a):
    nh = b_i.shape[1]
    pad = jnp.zeros((V7X_LANES - nh,), F32)
    b_if = jnp.concatenate([b_i[i].astype(F32), pad, b_f[i].astype(F32), pad]).reshape(1, 2 * V7X_LANES)
    da = conv_w.shape[-1]
    return dict(conv_w=conv_w[i], conv_b=conv_b[i].reshape(1, da), w_q=w_q[i], w_k=w_k[i], w_v=w_v[i],
                b_if=b_if, g_head=g_head[i].reshape(1, da), skip=skip_a[i].reshape(1, da))


def _tile_plan(n_rows, seq):
    tm = next(c for c in (832, 640, 512, 256, 128, 64, 32, 16) if n_rows % c == 0)
    tc = next(c for c in (320, 256, 128, 64, 32, 16, 8) if n_rows % c == 0)
    tm_big = 2 * tm if n_rows % (2 * tm) == 0 else tm
    return dict(tm=tm, tm_big=tm_big, tn=512, tn_down=256, mlstm_chunk=256, s5_chunk=256,
                moe_tr=512, moe_tn_up=512, moe_tn_down=512, gather_tr=256, combine_tr=tc, ple_tm=tm // 2,
                norm_tm=min(seq, 1024))


def kernel(x_prompt, x_sample, p_prompt, p_sample, state_mlstm_C, state_mlstm_n, state_mlstm_m, state_mlstm_conv,
           state_s5_re, state_s5_im, g_mix, w_in, conv_w, conv_b, w_q, w_k, w_v, b_i, b_f, g_head, skip_a, w_proj_a,
           s5_log_dt, s5_A_re, s5_A_im, s5_B_re, s5_B_im, s5_C_re, s5_C_im, s5_D, w_glu_b, w_out, g_ffn,
           w_ff_gate, w_ff_up, w_ff_down, w_router, b_router, w_moe_gate, w_moe_up, w_moe_down,
           g_ple, w_ple, w_pg, g_final):
    bsz, seq, d = x_prompt.shape
    bs = x_sample.shape[0]
    depth = g_mix.shape[0]
    nh = b_i.shape[1]
    d_a = conv_w.shape[-1]
    d_b = s5_D.shape[-1]
    n_p = bsz * seq
    t = n_p + bs
    tl = _tile_plan(t, seq)
    tm, tmb, tn = tl["tm"], tl["tm_big"], tl["tn"]

    h = jnp.concatenate([x_prompt.reshape(n_p, d), x_sample.reshape(bs, d)], axis=0).astype(F32)
    p_all = jnp.concatenate([p_prompt.reshape(depth, n_p, -1), p_sample.reshape(depth, bs, -1)], axis=1)

    col_ua, col_oa, col_ub = 0, 1, 2 * d_a // d_b
    col_if = (2 * d_a + d_b) // (2 * V7X_LANES)

    states = [[] for _ in range(11)]
    c_s_all = jnp.zeros(state_mlstm_C.shape, F32)
    for i in range(depth):
        lw = _layer_weights(i, conv_w, conv_b, w_q, w_k, w_v, b_i, b_f, g_head, skip_a)
        sp = _s5_params(dict(log_dt=s5_log_dt[i], a_re=s5_A_re[i], a_im=s5_A_im[i], b_re=s5_B_re[i],
                             b_im=s5_B_im[i], c_re=s5_C_re[i], c_im=s5_C_im[i], d_skip=s5_D[i]))
        proj, branch_gates = in_proj(h, g_mix[i], w_in, i, d_a, d_b, d, nh, tm=tmb, tn=tn)

        a_pre, c_p, n_pp, m_p, conv_p = mlstm_prompt(proj, col_ua, col_oa, col_if, t, bsz, seq, lw,
                                                     L=tl["mlstm_chunk"])
        a_pre, c_s_all, n_s, m_s, conv_s = mlstm_sample(proj, col_ua, col_oa, col_if, n_p, a_pre,
                                                        state_mlstm_C, i, c_s_all, state_mlstm_n[i].astype(F32),
                                                        state_mlstm_m[i].astype(F32),
                                                        state_mlstm_conv[i].astype(F32), lw)
        gy, sre_p, sim_p = s5_prompt(proj, col_ub, t, bsz, seq, sp, tc=tl["s5_chunk"])
        gy, sre_s, sim_s = s5_sample(proj, col_ub, n_p, gy, state_s5_re[i].astype(F32),
                                     state_s5_im[i].astype(F32), sp)
        mix = branch_mix(a_pre, gy, w_proj_a, w_glu_b, i, branch_gates, tm=tmb, tn=tn)
        h = matmul(mix, w_out, i, h, tm=tmb, tn=tn, name="out_proj")

        j = i // 2
        if i % 2 == 0:
            hid = norm_swiglu_up(h, g_ffn[i], w_ff_gate, w_ff_up, j, tm=tmb, tn=tn)
            h = matmul(hid, w_ff_down, j, h, tm=tmb, tn=tl["tn_down"], name="ffn_down")
        else:
            h = moe_layer(h, g_ffn[i], w_router[j], b_router[j], w_moe_gate[j], w_moe_up[j], w_moe_down[j],
                          tm=tm, tr=tl["moe_tr"], tn_up=tl["moe_tn_up"], tn_down=tl["moe_tn_down"],
                          tr_gather=tl["gather_tr"], tr_combine=tl["combine_tr"])
        h = ple_matmul(h, g_ple[i], p_all, w_pg, w_ple, i, tm=tl["ple_tm"], tn=tn)

        g_s, p_s = S5_STATE, sre_p.shape[-1] // S5_STATE
        new = [c_p, n_pp, m_p, conv_p, sre_p.reshape(bsz, p_s, g_s), sim_p.reshape(bsz, p_s, g_s),
               n_s, m_s, conv_s, sre_s.reshape(bs, p_s, g_s), sim_s.reshape(bs, p_s, g_s)]
        for lst, s in zip(states, new):
            lst.append(s)

    y_prompt, y_sample = final_norm_split(h, g_final, n_p, tm=tl["norm_tm"])
    y_prompt = y_prompt.reshape(bsz, seq, d)
    y_sample = y_sample.reshape(bs, 1, d)
    st = [jnp.stack(lst) for lst in states]
    return (y_prompt, y_sample) + tuple(st[:6]) + (c_s_all,) + tuple(st[6:])
```

```python
import functools
import math

import numpy as np
import jax
import jax.numpy as jnp
from jax import lax
from jax.experimental import pallas as pl
from jax.experimental.pallas import tpu as pltpu

F32 = jnp.float32
BF16 = jnp.bfloat16
EPS = 1e-6

V7X_VMEM_BYTES = 64 * 1024 * 1024
V7X_LANES = 128
V7X_SUBLANES = 8
VMEM_LIMIT = 56 * 1024 * 1024

N_HEADS = 4
CONV_W = 4
S5_GROUP = 16
S5_STATE = 64
N_EXPERTS = 8
TOP_K = 2


def _cparams(n_axes, vmem=VMEM_LIMIT):
    return pltpu.CompilerParams(dimension_semantics=("arbitrary",) * n_axes, vmem_limit_bytes=vmem)


def _sigmoid(x):
    return 1.0 / (1.0 + jnp.exp(-x))


def _silu(x):
    return x * _sigmoid(x)


def _gelu_tanh(x):
    return 0.5 * x * (1.0 + jnp.tanh(math.sqrt(2.0 / math.pi) * (x + 0.044715 * (x * x * x))))


def _log_sigmoid(x):
    return jnp.minimum(x, 0.0) - jnp.log(1.0 + jnp.exp(-jnp.abs(x)))


def _dot(a, b):
    return jnp.dot(a, b, preferred_element_type=F32)


def _dot_nt(a, b):
    return lax.dot_general(a, b, (((1,), (1,)), ((), ())), preferred_element_type=F32)


def _dot_tn(a, b):
    return lax.dot_general(a, b, (((0,), (0,)), ((), ())), preferred_element_type=F32)


def _rmsnorm_rows(x, g):
    ms = jnp.mean(x * x, axis=-1, keepdims=True)
    return x * lax.rsqrt(ms + EPS) * g


def _row_tile_buffering(tm, k, dtype):
    two_copies = 2 * tm * k * jnp.dtype(dtype).itemsize
    return pl.Buffered(1) if two_copies > VMEM_LIMIT // 4 else None


def _row_chunks(tm):
    for rc in (256, 208, 128, 104, 64, 32, 16, 8):
        if tm % rc == 0:
            return rc
    return tm


def _norm_to_scratch(x_ref, g_ref, xn_ref, tm):
    rc = _row_chunks(tm)

    def body(r, c):
        rows = pl.ds(pl.multiple_of(r * rc, rc), rc)
        xn_ref[rows, :] = _rmsnorm_rows(x_ref[rows, :], g_ref[...]).astype(BF16)
        return c

    lax.fori_loop(0, tm // rc, body, 0)


def _wspec(w, layer, tn, col=lambda j: j):
    return pl.BlockSpec((None, w.shape[1], tn), lambda i, j: (layer, 0, col(j)))


def _k_in_proj(x_ref, g_ref, wa_ref, wb_ref, wif_ref, o_ref, xn_ref, *, tm, n_head, n_main, nh):
    j = pl.program_id(1)

    @pl.when(j == 0)
    def _():
        _norm_to_scratch(x_ref, g_ref, xn_ref, tm)

    @pl.when(j < n_head)
    def _():
        o_ref[...] = _dot_nt(xn_ref[...], wa_ref[0].astype(BF16))

    @pl.when(jnp.logical_and(j >= n_head, j < n_main))
    def _():
        o_ref[...] = _dot_nt(xn_ref[...], wb_ref[0].astype(BF16))

    @pl.when(j == n_main)
    def _():
        pre = _dot_nt(xn_ref[...], wif_ref[0].astype(BF16))
        o_ref[...] = jnp.zeros_like(o_ref)
        o_ref[:, 0:nh] = pre[:, 0:nh]
        o_ref[:, V7X_LANES:V7X_LANES + nh] = pre[:, nh:2 * nh]


def in_proj(x, g, w_in, layer, d_a, d_b, d, nh, *, tm, tn):
    t, k = x.shape
    assert (2 * nh) % V7X_SUBLANES == 0 and (2 * d_a) % tn == 0 and (d_b + 2 * d) % tn == 0
    wt = jnp.swapaxes(w_in, 1, 2)
    n_head = 2 * d_a // tn
    n_main = n_head + (d_b + 2 * d) // tn
    if_row = 2 * d_a

    def rows(nrows, start):
        return pl.BlockSpec((pl.Element(1), pl.Element(nrows), pl.Element(k)), lambda i, j: (layer, start(j), 0))

    return pl.pallas_call(
        functools.partial(_k_in_proj, tm=tm, n_head=n_head, n_main=n_main, nh=nh),
        grid=(t // tm, n_main + 1),
        in_specs=[pl.BlockSpec((tm, k), lambda i, j: (i, 0), pipeline_mode=_row_tile_buffering(tm, k, x.dtype)),
                  pl.BlockSpec((1, k), lambda i, j: (0, 0)),
                  rows(tn, lambda j: jnp.minimum(j, n_head - 1) * tn),
                  rows(tn, lambda j: (jnp.clip(j, n_head, n_main - 1) * (tn // V7X_SUBLANES)
                                      + 2 * nh // V7X_SUBLANES) * V7X_SUBLANES),
                  rows(2 * nh, lambda j: if_row)],
        out_specs=pl.BlockSpec((tm, tn), lambda i, j: (i, j)),
        out_shape=jax.ShapeDtypeStruct((t, (n_main + 1) * tn), F32),
        scratch_shapes=[pltpu.VMEM((tm, k), BF16)],
        compiler_params=_cparams(2), name="in_proj",
    )(x, g.reshape(1, k), wt, wt, wt)


def _k_mm_res(x_ref, w_ref, r_ref, o_ref):
    o_ref[...] = r_ref[...] + _dot(x_ref[...], w_ref[...].astype(BF16))


def _k_mm(x_ref, w_ref, o_ref):
    o_ref[...] = _dot(x_ref[...], w_ref[...].astype(BF16)).astype(o_ref.dtype)


def matmul(x, w, layer, res=None, *, tm, tn, out_dtype=F32, name="mm"):
    t, k = x.shape
    n = w.shape[2]
    in_specs = [pl.BlockSpec((tm, k), lambda i, j: (i, 0), pipeline_mode=_row_tile_buffering(tm, k, x.dtype)),
                _wspec(w, layer, tn)]
    args = [x, w]
    body = _k_mm
    if res is not None:
        in_specs.append(pl.BlockSpec((tm, tn), lambda i, j: (i, j)))
        args.append(res)
        body = _k_mm_res
    return pl.pallas_call(
        body, grid=(t // tm, pl.cdiv(n, tn)), in_specs=in_specs,
        out_specs=pl.BlockSpec((tm, tn), lambda i, j: (i, j)),
        out_shape=jax.ShapeDtypeStruct((t, n), out_dtype),
        compiler_params=_cparams(2), name=name,
    )(*args)


def _k_branch_mix(a_ref, gy_ref, wp_ref, wv_ref, wg_ref, ga_ref, gb_ref, o_ref):
    a = a_ref[...]
    gy = gy_ref[...]
    a_out = _dot(a, wp_ref[...].astype(BF16))
    val = _dot(gy, wv_ref[...].astype(BF16))
    gate = _dot(gy, wg_ref[...].astype(BF16))
    b_out = val * _sigmoid(gate)
    o_ref[...] = (_sigmoid(ga_ref[...]) * a_out + _sigmoid(gb_ref[...]) * b_out).astype(o_ref.dtype)


def branch_mix(a_pre, gy, w_proj_a, w_glu_b, layer, proj, col_ga, col_gb, *, tm, tn):
    t, k = a_pre.shape
    n = w_proj_a.shape[2]
    nj = n // tn
    rows = pl.BlockSpec((tm, k), lambda i, j: (i, 0), pipeline_mode=pl.Buffered(1))
    return pl.pallas_call(
        _k_branch_mix, grid=(t // tm, nj),
        in_specs=[rows, rows,
                  _wspec(w_proj_a, layer, tn), _wspec(w_glu_b, layer, tn),
                  _wspec(w_glu_b, layer, tn, lambda j: j + nj),
                  pl.BlockSpec((tm, tn), lambda i, j: (i, col_ga + j)),
                  pl.BlockSpec((tm, tn), lambda i, j: (i, col_gb + j))],
        out_specs=pl.BlockSpec((tm, tn), lambda i, j: (i, j)),
        out_shape=jax.ShapeDtypeStruct((t, n), BF16),
        compiler_params=_cparams(2), name="branch_mix",
    )(a_pre, gy, w_proj_a, w_glu_b, w_glu_b, proj, proj)


def _k_swiglu_up(x_ref, g_ref, wg_ref, wu_ref, o_ref, xn_ref, *, tm):
    @pl.when(pl.program_id(1) == 0)
    def _():
        _norm_to_scratch(x_ref, g_ref, xn_ref, tm)

    xn = xn_ref[...]
    gate = _dot(xn, wg_ref[...].astype(BF16))
    up = _dot(xn, wu_ref[...].astype(BF16))
    o_ref[...] = (_silu(gate) * up).astype(o_ref.dtype)


def norm_swiglu_up(x, g, w_gate, w_up, layer, *, tm, tn, name="ffn_up"):
    t, k = x.shape
    n = w_gate.shape[2]
    return pl.pallas_call(
        functools.partial(_k_swiglu_up, tm=tm),
        grid=(t // tm, pl.cdiv(n, tn)),
        in_specs=[pl.BlockSpec((tm, k), lambda i, j: (i, 0), pipeline_mode=_row_tile_buffering(tm, k, x.dtype)),
                  pl.BlockSpec((1, k), lambda i, j: (0, 0)),
                  _wspec(w_gate, layer, tn), _wspec(w_up, layer, tn)],
        out_specs=pl.BlockSpec((tm, tn), lambda i, j: (i, j)),
        out_shape=jax.ShapeDtypeStruct((t, n), BF16),
        scratch_shapes=[pltpu.VMEM((tm, k), BF16)],
        compiler_params=_cparams(2), name=name,
    )(x, g.reshape(1, k), w_gate, w_up)


def _cast_rows_to(dst_ref, src_ref, rc):
    def body(r, c):
        rows = pl.ds(pl.multiple_of(r * rc, rc), rc)
        dst_ref[rows, :] = src_ref[rows, :].astype(dst_ref.dtype)
        return c

    lax.fori_loop(0, src_ref.shape[0] // rc, body, 0)


def _k_ple(x_ref, g_ref, p_ref, wpg_ref, wple_ref, o_ref, wpg_bf, wple_bf, *, tn):
    @pl.when(pl.program_id(0) == 0)
    def _():
        _cast_rows_to(wpg_bf, wpg_ref, V7X_LANES)
        _cast_rows_to(wple_bf, wple_ref, V7X_LANES)

    x = x_ref[...]
    xn = _rmsnorm_rows(x, g_ref[...]).astype(BF16)
    pb = p_ref[...].astype(BF16)
    for c in range(x.shape[1] // tn):
        cols = slice(c * tn, (c + 1) * tn)
        gate = _dot(xn, wpg_bf[:, cols])
        emb = _dot(pb, wple_bf[:, cols])
        o_ref[:, cols] = x[:, cols] + emb * _sigmoid(gate)


def ple_matmul(x, g, p, w_pg, w_ple, layer, *, tm, tn, name="ple"):
    t, k = x.shape
    kp = p.shape[2]
    resident = lambda w: pl.BlockSpec((None,) + w.shape[1:], lambda i: (layer, 0, 0), pipeline_mode=pl.Buffered(1))
    return pl.pallas_call(
        functools.partial(_k_ple, tn=tn),
        grid=(t // tm,),
        in_specs=[pl.BlockSpec((tm, k), lambda i: (i, 0)),
                  pl.BlockSpec((1, k), lambda i: (0, 0)),
                  pl.BlockSpec((None, tm, kp), lambda i: (layer, i, 0)),
                  resident(w_pg), resident(w_ple)],
        out_specs=pl.BlockSpec((tm, k), lambda i: (i, 0)),
        out_shape=jax.ShapeDtypeStruct((t, k), F32),
        scratch_shapes=[pltpu.VMEM(w_pg.shape[1:], BF16), pltpu.VMEM(w_ple.shape[1:], BF16)],
        compiler_params=_cparams(1), name=name,
    )(x, g.reshape(1, k), p, w_pg, w_ple)


def _k_final_norm(xp_ref, xs_ref, g_ref, op_ref, os_ref, *, n_prompt_tiles):
    i = pl.program_id(0)

    @pl.when(i < n_prompt_tiles)
    def _():
        op_ref[...] = _rmsnorm_rows(xp_ref[...], g_ref[...])

    @pl.when(i == n_prompt_tiles)
    def _():
        os_ref[...] = _rmsnorm_rows(xs_ref[...], g_ref[...])


def final_norm_split(x, g, n_prompt, *, tm):
    t, k = x.shape
    bs = t - n_prompt
    npt = n_prompt // tm
    last = npt - 1
    return pl.pallas_call(
        functools.partial(_k_final_norm, n_prompt_tiles=npt), grid=(npt + 1,),
        in_specs=[pl.BlockSpec((tm, k), lambda i: (jnp.minimum(i, last), 0)),
                  pl.BlockSpec((bs, k), lambda i: (n_prompt // bs, 0)),
                  pl.BlockSpec((1, k), lambda i: (0, 0))],
        out_specs=[pl.BlockSpec((tm, k), lambda i: (jnp.minimum(i, last), 0)),
                   pl.BlockSpec((bs, k), lambda i: (0, 0))],
        out_shape=[jax.ShapeDtypeStruct((n_prompt, k), F32), jax.ShapeDtypeStruct((bs, k), F32)],
        compiler_params=_cparams(1), name="final_norm",
    )(x, x, g.reshape(1, k))


def _split_hi_lo(x):
    hi = x.astype(BF16)
    lo = (x - hi.astype(F32)).astype(BF16)
    return hi, lo


def _k_mlstm_prompt(u_ref, o_ref_in, if_ref, cw_ref, cb_ref, wq_ref, wk_ref, wv_ref, bif_ref, gh_ref, sk_ref,
                    tri_ref, a_init, a_ref, c_out, n_out, m_out, conv_out, upad_ref, *, L, dh):
    del a_init
    c_idx = pl.program_id(1)
    nh = N_HEADS

    @pl.when(c_idx == 0)
    def _():
        c_out[...] = jnp.zeros_like(c_out)
        n_out[...] = jnp.zeros_like(n_out)
        m_out[...] = jnp.zeros_like(m_out)
        upad_ref[pl.ds(0, 8), :] = jnp.zeros((8, nh * dh), F32)

    @pl.when(c_idx > 0)
    def _():
        upad_ref[pl.ds(0, 8), :] = upad_ref[pl.ds(L, 8), :]

    u = u_ref[...]
    upad_ref[pl.ds(8, L), :] = u
    conv = cb_ref[...] + u * cw_ref[CONV_W - 1:CONV_W, :]
    for j in range(CONV_W - 1):
        conv = conv + upad_ref[pl.ds(8 - (CONV_W - 1) + j, L), :] * cw_ref[j:j + 1, :]
    cact = _silu(conv)
    conv_out[0] = upad_ref[pl.ds(L, 8), :]

    pre = if_ref[...]
    li = pre[:, :V7X_LANES] + bif_ref[:, :V7X_LANES]
    lf = _log_sigmoid(pre[:, V7X_LANES:] + bif_ref[:, V7X_LANES:])
    tri = tri_ref[...]
    lf_hi, lf_mid = _split_hi_lo(lf)
    lf_lo = (lf - lf_hi.astype(F32) - lf_mid.astype(F32)).astype(BF16)
    bcum = _dot(tri, lf_hi) + _dot(tri, lf_mid) + _dot(tri, lf_lo)
    li_t = li.T
    b_t = bcum.T
    row_id = lax.broadcasted_iota(jnp.int32, (L, L), 0)
    col_id = lax.broadcasted_iota(jnp.int32, (L, L), 1)
    causal = col_id <= row_id
    lane = lax.broadcasted_iota(jnp.int32, (1, V7X_LANES), 1)
    m_row = m_out[0]
    m_new_row = m_row

    for h in range(nh):
        hs = slice(h * dh, (h + 1) * dh)
        ch = cact[:, hs].astype(BF16)
        uh = u[:, hs].astype(BF16)
        q = _dot(ch, wq_ref[h].astype(BF16))
        k = _dot(ch, wk_ref[h].astype(BF16)) * (dh ** -0.5)
        v = _dot(uh, wv_ref[h].astype(BF16))
        qb, kb, vb = q.astype(BF16), k.astype(BF16), v.astype(BF16)

        b_col = bcum[:, h:h + 1]
        li_col = li[:, h:h + 1]
        r_row = li_t[h:h + 1, :] - b_t[h:h + 1, :]
        m_prev = m_row[:, h:h + 1]
        d = jnp.where(causal, b_col + r_row, -jnp.inf)
        inter = b_col + m_prev
        m_t = jnp.maximum(inter, jnp.max(d, axis=-1, keepdims=True))
        w_inter = jnp.exp(inter - m_t)
        s = _dot_nt(qb, kb) * jnp.exp(d - m_t)
        c_prev = c_out[0, h]
        n_prev = n_out[0, h:h + 1, :]
        num = w_inter * _dot(qb, c_prev.astype(BF16)) + _dot(s.astype(BF16), vb)
        den = w_inter * jnp.sum(q * n_prev, axis=-1, keepdims=True) + jnp.sum(s, axis=-1, keepdims=True)
        hh = num / jnp.maximum(jnp.abs(den), jnp.exp(-m_t))

        b_last = b_col[L - 1:L, :]
        g_col = b_last - b_col + li_col
        m_new = jnp.maximum(b_last + m_prev, jnp.max(g_col, axis=0, keepdims=True))
        decay = jnp.exp(b_last + m_prev - m_new)
        wk_ = jnp.exp(g_col - m_new) * k
        c_out[0, h] = decay * c_prev + _dot_tn(wk_.astype(BF16), vb)
        n_out[0, h:h + 1, :] = decay * n_prev + jnp.sum(wk_, axis=0, keepdims=True)
        m_new_row = jnp.where(lane == h, m_new, m_new_row)

        hn = _rmsnorm_rows(hh, gh_ref[:, hs])
        gated = (hn + sk_ref[:, hs] * cact[:, hs]) * _sigmoid(o_ref_in[:, hs])
        a_ref[:, hs] = gated.astype(a_ref.dtype)

    m_out[0] = m_new_row


def mlstm_prompt(proj, col_u, col_o, col_if, n_rows, bsz, seq, lw, *, L):
    dh = lw["w_q"].shape[-1]
    da = N_HEADS * dh
    nc = seq // L
    tri = jnp.asarray(np.tril(np.ones((L, L), np.float32)), BF16)
    row_blk = lambda b, c: b * nc + c
    full = lambda *shape: pl.BlockSpec(shape, lambda b, c: (0,) * len(shape))
    outs = pl.pallas_call(
        functools.partial(_k_mlstm_prompt, L=L, dh=dh),
        grid=(bsz, nc),
        in_specs=[pl.BlockSpec((L, da), lambda b, c: (row_blk(b, c), col_u)),
                  pl.BlockSpec((L, da), lambda b, c: (row_blk(b, c), col_o)),
                  pl.BlockSpec((L, 2 * V7X_LANES), lambda b, c: (row_blk(b, c), col_if)),
                  full(CONV_W, da), full(1, da), full(N_HEADS, dh, dh), full(N_HEADS, dh, dh),
                  full(N_HEADS, dh, dh), full(1, 2 * V7X_LANES), full(1, da), full(1, da), full(L, L),
                  pl.BlockSpec(memory_space=pl.ANY)],
        out_specs=[pl.BlockSpec((L, da), lambda b, c: (row_blk(b, c), 0)),
                   pl.BlockSpec((1, N_HEADS, dh, dh), lambda b, c: (b, 0, 0, 0)),
                   pl.BlockSpec((1, N_HEADS, dh), lambda b, c: (b, 0, 0)),
                   pl.BlockSpec((1, 1, V7X_LANES), lambda b, c: (b, 0, 0)),
                   pl.BlockSpec((1, 8, da), lambda b, c: (b, 0, 0))],
        out_shape=[jax.ShapeDtypeStruct((n_rows, da), BF16),
                   jax.ShapeDtypeStruct((bsz, N_HEADS, dh, dh), F32),
                   jax.ShapeDtypeStruct((bsz, N_HEADS, dh), F32),
                   jax.ShapeDtypeStruct((bsz, 1, V7X_LANES), F32),
                   jax.ShapeDtypeStruct((bsz, 8, da), F32)],
        scratch_shapes=[pltpu.VMEM((L + 8, da), F32)],
        input_output_aliases={12: 0},
        compiler_params=_cparams(2), name="mlstm_prompt",
    )(proj, proj, proj, lw["conv_w"], lw["conv_b"], lw["w_q"], lw["w_k"], lw["w_v"], lw["b_if"],
      lw["g_head"], lw["skip"], tri, jnp.zeros((n_rows, da), BF16))
    a_pre, c_p, n_p, m_p, conv_p = outs
    return a_pre, c_p, n_p, m_p[:, 0, :N_HEADS], conv_p[:, 8 - (CONV_W - 1):, :]


def _k_mlstm_sample_pre(u_ref, if_ref, conv_ref, m_ref, cw_ref, cb_ref, wq_ref, wk_ref, wv_ref, bif_ref,
                        q_out, k_out, v_out, c_out, gates_out, conv_out, *, dh):
    nh = N_HEADS
    u = u_ref[...]
    conv = cb_ref[...] + u * cw_ref[CONV_W - 1:CONV_W, :]
    for j in range(CONV_W - 1):
        conv = conv + conv_ref[j] * cw_ref[j:j + 1, :]
        if j > 0:
            conv_out[j - 1] = conv_ref[j]
    conv_out[CONV_W - 2] = u
    cact = _silu(conv)
    c_out[...] = cact
    for h in range(nh):
        hs = slice(h * dh, (h + 1) * dh)
        ch = cact[:, hs].astype(BF16)
        q_out[:, hs] = _dot(ch, wq_ref[h].astype(BF16))
        k_out[:, hs] = _dot(ch, wk_ref[h].astype(BF16)) * (dh ** -0.5)
        v_out[:, hs] = _dot(u[:, hs].astype(BF16), wv_ref[h].astype(BF16))
    pre = if_ref[...]
    li = pre[:, :V7X_LANES] + bif_ref[:, :V7X_LANES]
    lf = _log_sigmoid(pre[:, V7X_LANES:] + bif_ref[:, V7X_LANES:])
    m_prev = m_ref[...]
    inter = lf + m_prev
    m_t = jnp.maximum(inter, li)
    gates_out[0] = jnp.exp(inter - m_t)
    gates_out[1] = jnp.exp(li - m_t)
    gates_out[2] = jnp.exp(-m_t)
    gates_out[3] = m_t


def _k_mlstm_sample_step(q_ref, k_ref, v_ref, gates_ref, c_ref, n_ref, cact_ref, o_ref_in, gh_ref, sk_ref,
                         a_any, c_any, c_out, n_out, a_out, hh_ref, *, bt, dh):
    del a_any, c_any
    i = pl.program_id(0)
    nh = N_HEADS
    q = q_ref[...]
    k = k_ref[...]
    v = v_ref[...]
    w_inter = gates_ref[0]
    w_new = gates_ref[1]
    e_neg_m = gates_ref[2]
    n_prev = n_ref[...]
    rows = pl.ds(pl.multiple_of(i * bt, bt), bt)
    for h in range(nh):
        hs = slice(h * dh, (h + 1) * dh)
        qh, kh, vh, nh_prev = q[:, hs], k[:, hs], v[:, hs], n_prev[:, hs]
        q_t = qh.T
        k_t = kh.T
        wi = w_inter[:, h:h + 1]
        wn = w_new[:, h:h + 1]
        s = jnp.sum(qh * kh, axis=-1, keepdims=True) * wn
        den = wi * jnp.sum(qh * nh_prev, axis=-1, keepdims=True) + s
        wv = wn * vh
        qc_rows = []
        for bl in range(bt):
            c_prev = c_ref[bl, h]
            qc_rows.append(jnp.sum(q_t[:, bl:bl + 1] * c_prev, axis=0, keepdims=True))
            c_out[bl, h] = wi[bl:bl + 1, :] * c_prev + k_t[:, bl:bl + 1] * wv[bl:bl + 1, :]
        qc = jnp.concatenate(qc_rows, axis=0)
        num = wi * qc + s * vh
        hh = num / jnp.maximum(jnp.abs(den), e_neg_m[:, h:h + 1])
        hh_ref[rows, hs] = hh
        n_out[:, hs] = wi * nh_prev + wn * kh

    @pl.when(i == pl.num_programs(0) - 1)
    def _():
        for h in range(nh):
            hs = slice(h * dh, (h + 1) * dh)
            hn = _rmsnorm_rows(hh_ref[:, hs], gh_ref[:, hs])
            a_out[:, hs] = ((hn + sk_ref[:, hs] * cact_ref[:, hs]) * _sigmoid(o_ref_in[:, hs])).astype(a_out.dtype)


def mlstm_sample(proj, col_u, col_o, col_if, row0, a_pre_all, c_all, layer, c_new_all, n0, m0, conv0, lw, *,
                 bt=8):
    _, bs, nh, dh, _ = c_all.shape
    da = nh * dh
    rb = row0 // bs
    m_pad = jnp.pad(m0, ((0, 0), (0, V7X_LANES - nh)))
    conv_t = jnp.transpose(conv0, (1, 0, 2))
    full = lambda *shape: pl.BlockSpec(shape, lambda i: (0,) * len(shape))
    q, k, v, cact, gates, conv_new = pl.pallas_call(
        functools.partial(_k_mlstm_sample_pre, dh=dh),
        grid=(1,),
        in_specs=[pl.BlockSpec((bs, da), lambda i: (rb, col_u)),
                  pl.BlockSpec((bs, 2 * V7X_LANES), lambda i: (rb, col_if)),
                  full(CONV_W - 1, bs, da), full(bs, V7X_LANES), full(CONV_W, da), full(1, da),
                  full(nh, dh, dh), full(nh, dh, dh), full(nh, dh, dh), full(1, 2 * V7X_LANES)],
        out_specs=[full(bs, da), full(bs, da), full(bs, da), full(bs, da), full(4, bs, V7X_LANES),
                   full(CONV_W - 1, bs, da)],
        out_shape=[jax.ShapeDtypeStruct((bs, da), F32)] * 4
                  + [jax.ShapeDtypeStruct((4, bs, V7X_LANES), F32),
                     jax.ShapeDtypeStruct((CONV_W - 1, bs, da), F32)],
        compiler_params=_cparams(1), name="mlstm_sample_pre",
    )(proj, proj, conv_t, m_pad, lw["conv_w"], lw["conv_b"], lw["w_q"], lw["w_k"], lw["w_v"], lw["b_if"])

    blk = lambda *shape: pl.BlockSpec(shape, lambda i: (i,) + (0,) * (len(shape) - 1))
    cst = lambda *shape: pl.BlockSpec(shape, lambda i: (0,) * len(shape))
    c_blk = pl.BlockSpec((None, bt, nh, dh, dh), lambda i: (layer, i, 0, 0, 0))
    c_new_all, n_new, a_pre_all = pl.pallas_call(
        functools.partial(_k_mlstm_sample_step, bt=bt, dh=dh),
        grid=(bs // bt,),
        in_specs=[blk(bt, da), blk(bt, da), blk(bt, da),
                  pl.BlockSpec((4, bt, V7X_LANES), lambda i: (0, i, 0)),
                  c_blk, blk(bt, da), cst(bs, da),
                  pl.BlockSpec((bs, da), lambda i: (rb, col_o)), cst(1, da), cst(1, da),
                  pl.BlockSpec(memory_space=pl.ANY), pl.BlockSpec(memory_space=pl.ANY)],
        out_specs=[c_blk, blk(bt, da), pl.BlockSpec((bs, da), lambda i: (rb, 0))],
        out_shape=[jax.ShapeDtypeStruct(c_new_all.shape, F32), jax.ShapeDtypeStruct((bs, da), F32),
                   jax.ShapeDtypeStruct(a_pre_all.shape, a_pre_all.dtype)],
        scratch_shapes=[pltpu.VMEM((bs, da), F32)],
        input_output_aliases={10: 2, 11: 0},
        compiler_params=_cparams(1), name="mlstm_sample_step",
    )(q, k, v, gates, c_all, n0.reshape(bs, da), cact, proj, lw["g_head"], lw["skip"], a_pre_all, c_new_all)
    m_new = gates[3][:, :nh]
    return a_pre_all, c_new_all, n_new.reshape(bs, nh, dh), m_new, jnp.transpose(conv_new, (1, 0, 2))


S5_SLAB_GROUPS = V7X_LANES // S5_GROUP
S5_SLAB_STATES = S5_SLAB_GROUPS * S5_STATE


def _s5_params(lp):
    g, p = lp["a_re"].shape
    dt = jnp.exp(lp["log_dt"].astype(F32))[:, None]
    a_re = lp["a_re"].astype(F32)
    a_im = lp["a_im"].astype(F32)
    lam_re = a_re * dt
    lam_im = a_im * dt
    mag = jnp.exp(lam_re)
    ab_re = mag * jnp.cos(lam_im)
    ab_im = mag * jnp.sin(lam_im)
    den = a_re * a_re + a_im * a_im
    nr = ab_re - 1.0
    ni = ab_im
    k_re = (nr * a_re + ni * a_im) / den
    k_im = (ni * a_re - nr * a_im) / den
    b_re = lp["b_re"].astype(F32)
    b_im = lp["b_im"].astype(F32)
    bb_re = k_re[..., None] * b_re - k_im[..., None] * b_im
    bb_im = k_re[..., None] * b_im + k_im[..., None] * b_re
    ns = g // S5_SLAB_GROUPS
    eye = jnp.eye(S5_SLAB_GROUPS, dtype=F32)

    def in_blockdiag(bb):
        bs = bb.reshape(ns, S5_SLAB_GROUPS, p, S5_GROUP)
        w = jnp.einsum("ab,sapc->sacbp", eye, bs)
        return w.reshape(ns, V7X_LANES, S5_SLAB_STATES).astype(BF16)

    def out_blockdiag(cc):
        cs = cc.astype(F32).reshape(ns, S5_SLAB_GROUPS, S5_GROUP, p)
        w = jnp.einsum("ab,sacp->sapbc", eye, cs)
        return w.reshape(ns, S5_SLAB_STATES, V7X_LANES)

    wc = jnp.concatenate([out_blockdiag(lp["c_re"]), -out_blockdiag(lp["c_im"])], axis=1).astype(BF16)
    return dict(lam_re=lam_re.reshape(1, g * p), lam_im=lam_im.reshape(1, g * p),
                ab_re=ab_re.reshape(1, g * p), ab_im=ab_im.reshape(1, g * p),
                wb_re=in_blockdiag(bb_re), wb_im=in_blockdiag(bb_im), wc=wc,
                d_skip=lp["d_skip"].astype(F32).reshape(1, g * S5_GROUP))


def _s5_powers(sp, ks):
    kk = jnp.asarray(ks, F32)[:, None]
    mag = jnp.exp(kk * sp["lam_re"])
    return mag * jnp.cos(kk * sp["lam_im"]), mag * jnp.sin(kk * sp["lam_im"])


def _s5_input_proj(u_bf, wbr_ref, wbi_ref, bur_ref, bui_ref):
    ns = wbr_ref.shape[0]
    for s in range(ns):
        us = u_bf[:, s * V7X_LANES:(s + 1) * V7X_LANES]
        cols = slice(s * S5_SLAB_STATES, (s + 1) * S5_SLAB_STATES)
        bur_ref[:, cols] = _dot(us, wbr_ref[s])
        bui_ref[:, cols] = _dot(us, wbi_ref[s])


def _s5_output_proj(xr_ref, xi_ref, wc_ref, skip):
    ns = wc_ref.shape[0]
    outs = []
    for s in range(ns):
        cols = slice(s * S5_SLAB_STATES, (s + 1) * S5_SLAB_STATES)
        xcat = jnp.concatenate([xr_ref[:, cols].astype(BF16), xi_ref[:, cols].astype(BF16)], axis=1)
        y = _dot(xcat, wc_ref[s]) + skip[:, s * V7X_LANES:(s + 1) * V7X_LANES]
        outs.append(_gelu_tanh(y).astype(BF16))
    return jnp.concatenate(outs, axis=1)


def _k_s5_prompt(u_ref, wbr_ref, wbi_ref, wc_ref, d_ref, abr_ref, abi_ref, tpr_ref, tpi_ref, ajr_ref, aji_ref,
                 perm_ref, permt_ref, gy_init, gy_ref, sre_out, sim_out,
                 bur0, bur1, bui0, bui1, xb0, xb1, *, tc):
    del gy_init
    c_idx = pl.program_id(1)
    nsteps = tc // V7X_SUBLANES
    ns = wbr_ref.shape[0]
    lw = S5_SLAB_STATES
    pair = 2 * V7X_SUBLANES
    bur, bui, xb = (bur0, bur1), (bui0, bui1), (xb0, xb1)

    @pl.when(c_idx == 0)
    def _():
        sre_out[...] = jnp.zeros_like(sre_out)
        sim_out[...] = jnp.zeros_like(sim_out)

    u = u_ref[...]
    u_hi, u_lo = _split_hi_lo(u)
    perm = perm_ref[...]
    up_hi = _dot(perm, u_hi)
    skip = d_ref[...] * (up_hi + _dot(perm, u_lo))
    up_bf = up_hi.astype(BF16)

    sub = lax.broadcasted_iota(jnp.int32, (V7X_SUBLANES, lw), 0)

    def input_proj(s):
        us = up_bf[:, s * V7X_LANES:(s + 1) * V7X_LANES]
        bur[s % 2][...] = _dot(us, wbr_ref[s])
        bui[s % 2][...] = _dot(us, wbi_ref[s])

    def scan(s):
        br, bi, xo = bur[s % 2], bui[s % 2], xb[s % 2]
        lanes = slice(s * lw, (s + 1) * lw)
        ar = jnp.broadcast_to(abr_ref[:, lanes], (V7X_SUBLANES, lw))
        ai = jnp.broadcast_to(abi_ref[:, lanes], (V7X_SUBLANES, lw))
        er = jnp.zeros((V7X_SUBLANES, lw), F32)
        ei = er
        for i in range(nsteps):
            rows = slice(i * V7X_SUBLANES, (i + 1) * V7X_SUBLANES)
            er, ei = (ar * er - ai * ei + br[rows, :], ar * ei + ai * er + bi[rows, :])
            br[rows, :] = er
            bi[rows, :] = ei
        for d, row in ((1, 0), (2, 1), (4, 3)):
            pr = ajr_ref[row:row + 1, lanes]
            pi = aji_ref[row:row + 1, lanes]
            sr = pltpu.roll(er, d, 0)
            si = pltpu.roll(ei, d, 0)
            keep = sub >= d
            er, ei = (er + jnp.where(keep, pr * sr - pi * si, 0.0),
                      ei + jnp.where(keep, pr * si + pi * sr, 0.0))
        c0r = jnp.broadcast_to(sre_out[0, :, lanes], (V7X_SUBLANES, lw))
        c0i = jnp.broadcast_to(sim_out[0, :, lanes], (V7X_SUBLANES, lw))
        ajr = ajr_ref[:, lanes]
        aji = aji_ref[:, lanes]
        fr = ajr * c0r - aji * c0i + er
        fi = ajr * c0i + aji * c0r + ei
        cin_r = jnp.where(sub >= 1, pltpu.roll(fr, 1, 0), c0r)
        cin_i = jnp.where(sub >= 1, pltpu.roll(fi, 1, 0), c0i)
        sre_out[0, :, lanes] = fr[V7X_SUBLANES - 1:V7X_SUBLANES, :]
        sim_out[0, :, lanes] = fi[V7X_SUBLANES - 1:V7X_SUBLANES, :]

        cin2_r = jnp.concatenate([cin_r, cin_r], axis=0)
        cin2_i = jnp.concatenate([cin_i, cin_i], axis=0)
        for k in range(tc // pair):
            rows = slice(k * pair, (k + 1) * pair)
            pr = tpr_ref[rows, lanes]
            pi = tpi_ref[rows, lanes]
            xo[rows, :lw] = (br[rows, :] + (pr * cin2_r - pi * cin2_i)).astype(BF16)
            xo[rows, lw:] = (bi[rows, :] + (pr * cin2_i + pi * cin2_r)).astype(BF16)

    def output_proj(s):
        y = _dot(xb[s % 2][...], wc_ref[s]) + skip[:, s * V7X_LANES:(s + 1) * V7X_LANES]
        return _gelu_tanh(y).astype(BF16)

    outs = []
    input_proj(0)
    for s in range(ns):
        if s + 1 < ns:
            input_proj(s + 1)
        scan(s)
        outs.append(output_proj(s))
    g_perm = jnp.concatenate(outs, axis=1)
    gy_ref[...] = _dot(permt_ref[...], g_perm).astype(gy_ref.dtype)


def s5_prompt(proj, col_u, n_rows, bsz, seq, sp, *, tc=256):
    db = sp["d_skip"].shape[1]
    nch = sp["ab_re"].shape[1]
    nc = seq // tc
    nsteps = tc // V7X_SUBLANES
    tpr, tpi = (jnp.repeat(a, V7X_SUBLANES, axis=0) for a in _s5_powers(sp, np.arange(1, nsteps + 1)))
    ajr, aji = _s5_powers(sp, nsteps * np.arange(1, V7X_SUBLANES + 1))
    perm = np.zeros((tc, tc), np.float32)
    r = np.arange(tc)
    perm[r, (r % V7X_SUBLANES) * nsteps + r // V7X_SUBLANES] = 1.0
    full = lambda a: pl.BlockSpec(a.shape, lambda b, c: (0,) * a.ndim)
    consts = [sp["wb_re"], sp["wb_im"], sp["wc"], sp["d_skip"], sp["ab_re"], sp["ab_im"], tpr, tpi, ajr, aji,
              jnp.asarray(perm, BF16), jnp.asarray(perm.T, BF16)]
    gy, s_re, s_im = pl.pallas_call(
        functools.partial(_k_s5_prompt, tc=tc),
        grid=(bsz, nc),
        in_specs=[pl.BlockSpec((tc, db), lambda b, c: (b * nc + c, col_u))] + [full(a) for a in consts]
                 + [pl.BlockSpec(memory_space=pl.ANY)],
        out_specs=[pl.BlockSpec((tc, db), lambda b, c: (b * nc + c, 0)),
                   pl.BlockSpec((1, 1, nch), lambda b, c: (b, 0, 0)),
                   pl.BlockSpec((1, 1, nch), lambda b, c: (b, 0, 0))],
        out_shape=[jax.ShapeDtypeStruct((n_rows, db), BF16),
                   jax.ShapeDtypeStruct((bsz, 1, nch), F32), jax.ShapeDtypeStruct((bsz, 1, nch), F32)],
        scratch_shapes=[pltpu.VMEM((tc, S5_SLAB_STATES), F32)] * 4 + [pltpu.VMEM((tc, 2 * S5_SLAB_STATES), BF16)] * 2,
        input_output_aliases={1 + len(consts): 0},
        compiler_params=_cparams(2), name="s5_prompt",
    )(proj, *consts, jnp.zeros((n_rows, db), BF16))
    return gy, s_re, s_im


def _k_s5_sample(u_ref, x0r_ref, x0i_ref, wbr_ref, wbi_ref, wc_ref, d_ref, abr_ref, abi_ref, gy_any,
                 gy_ref, xr_out, xi_out):
    del gy_any
    u = u_ref[...]
    _s5_input_proj(u.astype(BF16), wbr_ref, wbi_ref, xr_out, xi_out)
    ar = abr_ref[...]
    ai = abi_ref[...]
    x0r = x0r_ref[...]
    x0i = x0i_ref[...]
    xr_out[...] = xr_out[...] + (ar * x0r - ai * x0i)
    xi_out[...] = xi_out[...] + (ar * x0i + ai * x0r)
    gy_ref[...] = _s5_output_proj(xr_out, xi_out, wc_ref, d_ref[...] * u).astype(gy_ref.dtype)


def s5_sample(proj, col_u, row0, gy_all, x0_re, x0_im, sp):
    bs = x0_re.shape[0]
    db = sp["d_skip"].shape[1]
    nch = sp["ab_re"].shape[1]
    rb = row0 // bs
    full = lambda a: pl.BlockSpec(a.shape, lambda i: (0,) * a.ndim)
    consts = [sp["wb_re"], sp["wb_im"], sp["wc"], sp["d_skip"], sp["ab_re"], sp["ab_im"]]
    x0r = x0_re.reshape(bs, nch)
    x0i = x0_im.reshape(bs, nch)
    gy_all, xr, xi = pl.pallas_call(
        _k_s5_sample,
        grid=(1,),
        in_specs=[pl.BlockSpec((bs, db), lambda i: (rb, col_u)), full(x0r), full(x0i)]
                 + [full(a) for a in consts] + [pl.BlockSpec(memory_space=pl.ANY)],
        out_specs=[pl.BlockSpec((bs, db), lambda i: (rb, 0)),
                   pl.BlockSpec((bs, nch), lambda i: (0, 0)), pl.BlockSpec((bs, nch), lambda i: (0, 0))],
        out_shape=[jax.ShapeDtypeStruct(gy_all.shape, gy_all.dtype),
                   jax.ShapeDtypeStruct((bs, nch), F32), jax.ShapeDtypeStruct((bs, nch), F32)],
        input_output_aliases={9: 0},
        compiler_params=_cparams(1), name="s5_sample",
    )(proj, x0r, x0i, *consts, gy_all)
    return gy_all, xr, xi


def _k_router(x_ref, g_ref, w_ref, b_ref, hn_ref, lg_ref):
    hn = _rmsnorm_rows(x_ref[...], g_ref[...])
    hn_ref[...] = hn
    x_hi, x_lo = _split_hi_lo(hn)
    w_hi, w_lo = _split_hi_lo(w_ref[...])
    lg_ref[...] = _dot(x_hi, w_hi) + (_dot(x_lo, w_hi) + _dot(x_hi, w_lo)) + b_ref[...]


def router(x, g, w_router, b_router, *, tm):
    t, k = x.shape
    ne = w_router.shape[1]
    w_pad = jnp.pad(w_router, ((0, 0), (0, V7X_LANES - ne)))
    b_pad = jnp.pad(b_router.astype(F32), (0, V7X_LANES - ne)).reshape(1, V7X_LANES)
    hn, lg = pl.pallas_call(
        _k_router, grid=(t // tm,),
        in_specs=[pl.BlockSpec((tm, k), lambda i: (i, 0)), pl.BlockSpec((1, k), lambda i: (0, 0)),
                  pl.BlockSpec((k, V7X_LANES), lambda i: (0, 0)), pl.BlockSpec((1, V7X_LANES), lambda i: (0, 0))],
        out_specs=[pl.BlockSpec((tm, k), lambda i: (i, 0)), pl.BlockSpec((tm, V7X_LANES), lambda i: (i, 0))],
        out_shape=[jax.ShapeDtypeStruct((t, k), F32), jax.ShapeDtypeStruct((t, V7X_LANES), F32)],
        compiler_params=_cparams(1), name="router",
    )(x, g.reshape(1, k), w_pad, b_pad)
    return hn, lg[:, :ne]


DMA_ISSUE_UNROLL = 8


def _k_gather_rows(nused_ref, tok_ref, src_hbm, o_ref, buf_ref, sems, *, tr):
    t = pl.program_id(0)
    n_used = nused_ref[0]

    def issue_tile(tile):
        slot = tile % 2
        base = tile * tr

        def issue(r, c):
            tok = tok_ref[base + r]
            pltpu.make_async_copy(src_hbm.at[pl.ds(tok, 1), :], buf_ref.at[slot, pl.ds(r, 1), :],
                                  sems.at[slot]).start()
            return c

        lax.fori_loop(0, tr, issue, 0, unroll=DMA_ISSUE_UNROLL)

    @pl.when(t == 0)
    def _():
        issue_tile(t)

    @pl.when(t + 1 < n_used)
    def _():
        issue_tile(t + 1)

    @pl.when(t < n_used)
    def _():
        slot = t % 2
        pltpu.make_async_copy(src_hbm.at[pl.ds(0, tr), :], buf_ref.at[slot], sems.at[slot]).wait()
        o_ref[...] = buf_ref[slot].astype(o_ref.dtype)

    @pl.when(t >= n_used)
    def _():
        o_ref[...] = jnp.zeros_like(o_ref)


def gather_rows(src, tok, n_used, *, tr):
    t, k = src.shape
    r_pad = tok.shape[0]
    nt = r_pad // tr
    return pl.pallas_call(
        functools.partial(_k_gather_rows, tr=tr),
        grid_spec=pltpu.PrefetchScalarGridSpec(
            num_scalar_prefetch=2, grid=(nt,),
            in_specs=[pl.BlockSpec(memory_space=pl.ANY)],
            out_specs=pl.BlockSpec((tr, k), lambda i, nu, tk: (i, 0)),
            scratch_shapes=[pltpu.VMEM((2, tr, k), src.dtype), pltpu.SemaphoreType.DMA((2,))]),
        out_shape=jax.ShapeDtypeStruct((r_pad, k), BF16),
        compiler_params=_cparams(1), name="moe_gather",
    )(n_used, tok, src)


def _k_moe_up(nused_ref, te_ref, x_ref, wg_ref, wu_ref, o_ref):
    del te_ref
    t = pl.program_id(1)

    @pl.when(t < nused_ref[0])
    def _():
        x = x_ref[...]
        gate = _dot(x, wg_ref[...].astype(BF16))
        up = _dot(x, wu_ref[...].astype(BF16))
        o_ref[...] = (_silu(gate) * up).astype(o_ref.dtype)

    @pl.when(t >= nused_ref[0])
    def _():
        o_ref[...] = jnp.zeros_like(o_ref)


def moe_up(xs, w_gate, w_up, tile_expert, n_used, *, tr, tn):
    r_pad, k = xs.shape
    ne, _, f = w_gate.shape
    nt = r_pad // tr
    row = lambda j, t, nu, te: (jnp.minimum(t, nu[0] - 1), 0)
    wmap = lambda j, t, nu, te: (te[t], 0, j)
    return pl.pallas_call(
        _k_moe_up,
        grid_spec=pltpu.PrefetchScalarGridSpec(
            num_scalar_prefetch=2, grid=(pl.cdiv(f, tn), nt),
            in_specs=[pl.BlockSpec((tr, k), row),
                      pl.BlockSpec((None, k, tn), wmap), pl.BlockSpec((None, k, tn), wmap)],
            out_specs=pl.BlockSpec((tr, tn), lambda j, t, nu, te: (t, j))),
        out_shape=jax.ShapeDtypeStruct((r_pad, f), BF16),
        compiler_params=_cparams(2), name="moe_up",
    )(n_used, tile_expert, xs, w_gate, w_up)


def _k_moe_down(nused_ref, te_ref, x_ref, w_ref, o_ref):
    del te_ref
    t = pl.program_id(1)

    @pl.when(t < nused_ref[0])
    def _():
        o_ref[...] = _dot(x_ref[...], w_ref[...].astype(BF16))

    @pl.when(t >= nused_ref[0])
    def _():
        o_ref[...] = jnp.zeros_like(o_ref)


def moe_down(hid, w_down, tile_expert, n_used, *, tr, tn):
    r_pad, f = hid.shape
    d = w_down.shape[2]
    nt = r_pad // tr
    return pl.pallas_call(
        _k_moe_down,
        grid_spec=pltpu.PrefetchScalarGridSpec(
            num_scalar_prefetch=2, grid=(d // tn, nt),
            in_specs=[pl.BlockSpec((tr, f), lambda j, t, nu, te: (jnp.minimum(t, nu[0] - 1), 0)),
                      pl.BlockSpec((None, f, tn), lambda j, t, nu, te: (te[t], 0, j))],
            out_specs=pl.BlockSpec((tr, tn), lambda j, t, nu, te: (t, j))),
        out_shape=jax.ShapeDtypeStruct((r_pad, d), F32),
        compiler_params=_cparams(2), name="moe_down",
    )(n_used, tile_expert, hid, w_down)


def _k_moe_combine(pos_ref, ys_hbm, gate_ref, res_ref, o_ref, buf0_ref, buf1_ref, sems, *, tr):
    t = pl.program_id(0)
    n_tiles = pl.num_programs(0)
    n_tok = n_tiles * tr

    def issue_tile(tile):
        slot = tile % 2
        base = tile * tr

        def issue(r, c):
            p0 = pos_ref[base + r]
            p1 = pos_ref[n_tok + base + r]
            pltpu.make_async_copy(ys_hbm.at[pl.ds(p0, 1), :], buf0_ref.at[slot, pl.ds(r, 1), :],
                                  sems.at[slot]).start()
            pltpu.make_async_copy(ys_hbm.at[pl.ds(p1, 1), :], buf1_ref.at[slot, pl.ds(r, 1), :],
                                  sems.at[slot]).start()
            return c

        lax.fori_loop(0, tr, issue, 0, unroll=DMA_ISSUE_UNROLL)

    @pl.when(t == 0)
    def _():
        issue_tile(t)

    @pl.when(t + 1 < n_tiles)
    def _():
        issue_tile(t + 1)

    slot = t % 2
    pltpu.make_async_copy(ys_hbm.at[pl.ds(0, tr), :], buf0_ref.at[slot], sems.at[slot]).wait()
    pltpu.make_async_copy(ys_hbm.at[pl.ds(0, tr), :], buf1_ref.at[slot], sems.at[slot]).wait()
    g = gate_ref[...]
    o_ref[...] = res_ref[...] + (g[:, 0:1] * buf0_ref[slot] + g[:, 1:2] * buf1_ref[slot])


def moe_combine(ys, pos, gates, res, *, tr):
    t, d = res.shape
    g_pad = jnp.pad(gates, ((0, 0), (0, V7X_LANES - gates.shape[1])))
    return pl.pallas_call(
        functools.partial(_k_moe_combine, tr=tr),
        grid_spec=pltpu.PrefetchScalarGridSpec(
            num_scalar_prefetch=1, grid=(t // tr,),
            in_specs=[pl.BlockSpec(memory_space=pl.ANY),
                      pl.BlockSpec((tr, V7X_LANES), lambda i, p: (i, 0)),
                      pl.BlockSpec((tr, d), lambda i, p: (i, 0))],
            out_specs=pl.BlockSpec((tr, d), lambda i, p: (i, 0)),
            scratch_shapes=[pltpu.VMEM((2, tr, d), F32), pltpu.VMEM((2, tr, d), F32),
                            pltpu.SemaphoreType.DMA((2,))]),
        out_shape=jax.ShapeDtypeStruct((t, d), F32),
        compiler_params=_cparams(1), name="moe_combine",
    )(pos, ys, g_pad, res)


def moe_layer(h, g_ffn, w_router, b_router, w_gate, w_up, w_down, *, tm, tr, tn_up, tn_down, tr_gather,
              tr_combine):
    t, d = h.shape
    ne = w_gate.shape[0]
    hn, logits = router(h, g_ffn, w_router, b_router, tm=tm)
    top_v, top_e = lax.top_k(logits, TOP_K)
    gates = jax.nn.softmax(top_v, axis=-1)
    flat_e = top_e.reshape(-1)
    onehot = (flat_e[:, None] == jnp.arange(ne, dtype=flat_e.dtype)[None, :]).astype(jnp.int32)
    rank = jnp.sum((jnp.cumsum(onehot, axis=0) - onehot) * onehot, axis=1)
    sizes = jnp.sum(onehot, axis=0)
    tiles_per = (sizes + tr - 1) // tr
    tile_end = jnp.cumsum(tiles_per)
    tile_start = tile_end - tiles_per
    n_used = tile_end[-1:].astype(jnp.int32)
    nt = (t * TOP_K) // tr + ne
    r_pad = nt * tr
    pos = (tile_start[flat_e] * tr + rank).astype(jnp.int32)
    src_tok = jnp.zeros((r_pad,), jnp.int32).at[pos].set(jnp.arange(t * TOP_K, dtype=jnp.int32) // TOP_K)
    tile_ids = jnp.minimum(jnp.arange(nt, dtype=jnp.int32), n_used[0] - 1)
    tile_expert = jnp.sum((tile_ids[:, None] >= tile_end[None, :]).astype(jnp.int32), axis=1).astype(jnp.int32)
    xs = gather_rows(hn, src_tok, n_used * (tr // tr_gather), tr=tr_gather)
    hid = moe_up(xs, w_gate, w_up, tile_expert, n_used, tr=tr, tn=tn_up)
    ys = moe_down(hid, w_down, tile_expert, n_used, tr=tr, tn=tn_down)
    return moe_combine(ys, pos.reshape(t, TOP_K).T.reshape(-1), gates, h, tr=tr_combine)


def _layer_weights(i, conv_w, conv_b, w_q, w_k, w_v, b_i, b_f, g_head, skip_a):
    nh = b_i.shape[1]
    pad = jnp.zeros((V7X_LANES - nh,), F32)
    b_if = jnp.concatenate([b_i[i].astype(F32), pad, b_f[i].astype(F32), pad]).reshape(1, 2 * V7X_LANES)
    da = conv_w.shape[-1]
    return dict(conv_w=conv_w[i], conv_b=conv_b[i].reshape(1, da), w_q=w_q[i], w_k=w_k[i], w_v=w_v[i],
                b_if=b_if, g_head=g_head[i].reshape(1, da), skip=skip_a[i].reshape(1, da))


def _tile_plan(n_rows, seq):
    tm = next(c for c in (832, 640, 512, 256, 128, 64, 32, 16) if n_rows % c == 0)
    tc = next(c for c in (320, 256, 128, 64, 32, 16, 8) if n_rows % c == 0)
    tm_big = 2 * tm if n_rows % (2 * tm) == 0 else tm
    return dict(tm=tm, tm_big=tm_big, tn=512, tn_down=256, mlstm_chunk=256, s5_chunk=256,
                moe_tr=512, moe_tn_up=512, moe_tn_down=512, gather_tr=256, combine_tr=tc, ple_tm=tm // 2,
                norm_tm=min(seq, 1024))


def kernel(x_prompt, x_sample, p_prompt, p_sample, state_mlstm_C, state_mlstm_n, state_mlstm_m, state_mlstm_conv,
           state_s5_re, state_s5_im, g_mix, w_in, conv_w, conv_b, w_q, w_k, w_v, b_i, b_f, g_head, skip_a, w_proj_a,
           s5_log_dt, s5_A_re, s5_A_im, s5_B_re, s5_B_im, s5_C_re, s5_C_im, s5_D, w_glu_b, w_out, g_ffn,
           w_ff_gate, w_ff_up, w_ff_down, w_router, b_router, w_moe_gate, w_moe_up, w_moe_down,
           g_ple, w_ple, w_pg, g_final):
    bsz, seq, d = x_prompt.shape
    bs = x_sample.shape[0]
    depth = g_mix.shape[0]
    nh = b_i.shape[1]
    d_a = conv_w.shape[-1]
    d_b = s5_D.shape[-1]
    n_p = bsz * seq
    t = n_p + bs
    tl = _tile_plan(t, seq)
    tm, tmb, tn = tl["tm"], tl["tm_big"], tl["tn"]

    h = jnp.concatenate([x_prompt.reshape(n_p, d), x_sample.reshape(bs, d)], axis=0).astype(F32)
    p_all = jnp.concatenate([p_prompt.reshape(depth, n_p, -1), p_sample.reshape(depth, bs, -1)], axis=1)

    col_ua, col_oa, col_ub = 0, 1, 2 * d_a // d_b
    col_ga = (2 * d_a + d_b) // tn
    col_gb = (2 * d_a + d_b + d) // tn
    col_if = (2 * d_a + d_b + 2 * d) // (2 * V7X_LANES)

    states = [[] for _ in range(11)]
    c_s_all = jnp.zeros(state_mlstm_C.shape, F32)
    for i in range(depth):
        lw = _layer_weights(i, conv_w, conv_b, w_q, w_k, w_v, b_i, b_f, g_head, skip_a)
        sp = _s5_params(dict(log_dt=s5_log_dt[i], a_re=s5_A_re[i], a_im=s5_A_im[i], b_re=s5_B_re[i],
                             b_im=s5_B_im[i], c_re=s5_C_re[i], c_im=s5_C_im[i], d_skip=s5_D[i]))
        proj = in_proj(h, g_mix[i], w_in, i, d_a, d_b, d, nh, tm=tmb, tn=tn)

        a_pre, c_p, n_pp, m_p, conv_p = mlstm_prompt(proj, col_ua, col_oa, col_if, t, bsz, seq, lw,
                                                     L=tl["mlstm_chunk"])
        a_pre, c_s_all, n_s, m_s, conv_s = mlstm_sample(proj, col_ua, col_oa, col_if, n_p, a_pre,
                                                        state_mlstm_C, i, c_s_all, state_mlstm_n[i].astype(F32),
                                                        state_mlstm_m[i].astype(F32),
                                                        state_mlstm_conv[i].astype(F32), lw)
        gy, sre_p, sim_p = s5_prompt(proj, col_ub, t, bsz, seq, sp, tc=tl["s5_chunk"])
        gy, sre_s, sim_s = s5_sample(proj, col_ub, n_p, gy, state_s5_re[i].astype(F32),
                                     state_s5_im[i].astype(F32), sp)
        mix = branch_mix(a_pre, gy, w_proj_a, w_glu_b, i, proj, col_ga, col_gb, tm=tmb, tn=tn)
        h = matmul(mix, w_out, i, h, tm=tmb, tn=tn, name="out_proj")

        j = i // 2
        if i % 2 == 0:
            hid = norm_swiglu_up(h, g_ffn[i], w_ff_gate, w_ff_up, j, tm=tmb, tn=tn)
            h = matmul(hid, w_ff_down, j, h, tm=tmb, tn=tl["tn_down"], name="ffn_down")
        else:
            h = moe_layer(h, g_ffn[i], w_router[j], b_router[j], w_moe_gate[j], w_moe_up[j], w_moe_down[j],
                          tm=tm, tr=tl["moe_tr"], tn_up=tl["moe_tn_up"], tn_down=tl["moe_tn_down"],
                          tr_gather=tl["gather_tr"], tr_combine=tl["combine_tr"])
        h = ple_matmul(h, g_ple[i], p_all, w_pg, w_ple, i, tm=tl["ple_tm"], tn=tn)

        g_s, p_s = S5_STATE, sre_p.shape[-1] // S5_STATE
        new = [c_p, n_pp, m_p, conv_p, sre_p.reshape(bsz, p_s, g_s), sim_p.reshape(bsz, p_s, g_s),
               n_s, m_s, conv_s, sre_s.reshape(bs, p_s, g_s), sim_s.reshape(bs, p_s, g_s)]
        for lst, s in zip(states, new):
            lst.append(s)

    y_prompt, y_sample = final_norm_split(h, g_final, n_p, tm=tl["norm_tm"])
    y_prompt = y_prompt.reshape(bsz, seq, d)
    y_sample = y_sample.reshape(bs, 1, d)
    st = [jnp.stack(lst) for lst in states]
    return (y_prompt, y_sample) + tuple(st[:6]) + (c_s_all,) + tuple(st[6:])
```

```python
import functools
import math

import numpy as np
import jax
import jax.numpy as jnp
from jax import lax
from jax.experimental import pallas as pl
from jax.experimental.pallas import tpu as pltpu

F32 = jnp.float32
BF16 = jnp.bfloat16
EPS = 1e-6

V7X_VMEM_BYTES = 64 * 1024 * 1024
V7X_LANES = 128
V7X_SUBLANES = 8
VMEM_LIMIT = V7X_VMEM_BYTES - 8 * 1024 * 1024

N_HEADS = 4
CONV_W = 4
S5_GROUP = 16
S5_STATE = 64
TOP_K = 2


def _cparams(n_axes, vmem=VMEM_LIMIT):
    return pltpu.CompilerParams(dimension_semantics=("arbitrary",) * n_axes, vmem_limit_bytes=vmem)


def _sigmoid(x):
    return 1.0 / (1.0 + jnp.exp(-x))


def _silu(x):
    return x * _sigmoid(x)


def _gelu_tanh(x):
    return 0.5 * x * (1.0 + jnp.tanh(math.sqrt(2.0 / math.pi) * (x + 0.044715 * (x * x * x))))


def _log_sigmoid(x):
    return jnp.minimum(x, 0.0) - jnp.log(1.0 + jnp.exp(-jnp.abs(x)))


def _dot(a, b):
    return jnp.dot(a, b, preferred_element_type=F32)


def _dot_nt(a, b):
    return lax.dot_general(a, b, (((1,), (1,)), ((), ())), preferred_element_type=F32)


def _dot_tn(a, b):
    return lax.dot_general(a, b, (((0,), (0,)), ((), ())), preferred_element_type=F32)


def _rmsnorm_rows(x, g):
    ms = jnp.mean(x * x, axis=-1, keepdims=True)
    return x * lax.rsqrt(ms + EPS) * g


def _row_tile_buffering(tm, k, dtype):
    two_copies = 2 * tm * k * jnp.dtype(dtype).itemsize
    return pl.Buffered(1) if two_copies > VMEM_LIMIT // 4 else None


def _row_chunks(tm):
    for rc in (256, 208, 128, 104, 64, 32, 16, 8):
        if tm % rc == 0:
            return rc
    return tm


def _norm_to_scratch(x_ref, g_ref, xn_ref, tm):
    rc = _row_chunks(tm)

    def body(r, c):
        rows = pl.ds(pl.multiple_of(r * rc, rc), rc)
        xn_ref[rows, :] = _rmsnorm_rows(x_ref[rows, :], g_ref[...]).astype(BF16)
        return c

    lax.fori_loop(0, tm // rc, body, 0)


def _wspec(w, layer, tn, col=lambda j: j):
    return pl.BlockSpec((None, w.shape[1], tn), lambda i, j: (layer, 0, col(j)))


def _k_in_proj(x_ref, g_ref, wa_ref, wb_ref, wif_ref, o_ref, xn_ref, *, tm, n_head, n_main, nh):
    j = pl.program_id(1)

    @pl.when(j == 0)
    def _():
        _norm_to_scratch(x_ref, g_ref, xn_ref, tm)

    @pl.when(j < n_head)
    def _():
        o_ref[...] = _dot_nt(xn_ref[...], wa_ref[0].astype(BF16))

    @pl.when(jnp.logical_and(j >= n_head, j < n_main))
    def _():
        o_ref[...] = _dot_nt(xn_ref[...], wb_ref[0].astype(BF16))

    @pl.when(j == n_main)
    def _():
        pre = _dot_nt(xn_ref[...], wif_ref[0].astype(BF16))
        o_ref[...] = jnp.zeros_like(o_ref)
        o_ref[:, 0:nh] = pre[:, 0:nh]
        o_ref[:, V7X_LANES:V7X_LANES + nh] = pre[:, nh:2 * nh]


def in_proj(x, g, w_in, layer, d_a, d_b, d, nh, *, tm, tn):
    t, k = x.shape
    assert (2 * nh) % V7X_SUBLANES == 0 and (2 * d_a) % tn == 0 and (d_b + 2 * d) % tn == 0
    wt = jnp.swapaxes(w_in, 1, 2)
    n_head = 2 * d_a // tn
    n_main = n_head + (d_b + 2 * d) // tn
    if_row = 2 * d_a

    def rows(nrows, start):
        return pl.BlockSpec((pl.Element(1), pl.Element(nrows), pl.Element(k)), lambda i, j: (layer, start(j), 0))

    return pl.pallas_call(
        functools.partial(_k_in_proj, tm=tm, n_head=n_head, n_main=n_main, nh=nh),
        grid=(t // tm, n_main + 1),
        in_specs=[pl.BlockSpec((tm, k), lambda i, j: (i, 0), pipeline_mode=_row_tile_buffering(tm, k, x.dtype)),
                  pl.BlockSpec((1, k), lambda i, j: (0, 0)),
                  rows(tn, lambda j: jnp.minimum(j, n_head - 1) * tn),
                  rows(tn, lambda j: (jnp.clip(j, n_head, n_main - 1) * (tn // V7X_SUBLANES)
                                      + 2 * nh // V7X_SUBLANES) * V7X_SUBLANES),
                  rows(2 * nh, lambda j: if_row)],
        out_specs=pl.BlockSpec((tm, tn), lambda i, j: (i, j)),
        out_shape=jax.ShapeDtypeStruct((t, (n_main + 1) * tn), F32),
        scratch_shapes=[pltpu.VMEM((tm, k), BF16)],
        compiler_params=_cparams(2), name="in_proj",
    )(x, g.reshape(1, k), wt, wt, wt)


def _k_mm_res(x_ref, w_ref, r_ref, o_ref):
    o_ref[...] = r_ref[...] + _dot(x_ref[...], w_ref[...].astype(BF16))


def _k_mm(x_ref, w_ref, o_ref):
    o_ref[...] = _dot(x_ref[...], w_ref[...].astype(BF16)).astype(o_ref.dtype)


def matmul(x, w, layer, res=None, *, tm, tn, out_dtype=F32, name="mm"):
    t, k = x.shape
    n = w.shape[2]
    in_specs = [pl.BlockSpec((tm, k), lambda i, j: (i, 0), pipeline_mode=_row_tile_buffering(tm, k, x.dtype)),
                _wspec(w, layer, tn)]
    args = [x, w]
    body = _k_mm
    if res is not None:
        in_specs.append(pl.BlockSpec((tm, tn), lambda i, j: (i, j)))
        args.append(res)
        body = _k_mm_res
    return pl.pallas_call(
        body, grid=(t // tm, pl.cdiv(n, tn)), in_specs=in_specs,
        out_specs=pl.BlockSpec((tm, tn), lambda i, j: (i, j)),
        out_shape=jax.ShapeDtypeStruct((t, n), out_dtype),
        compiler_params=_cparams(2), name=name,
    )(*args)


def _k_branch_mix(a_ref, gy_ref, wp_ref, wv_ref, wg_ref, ga_ref, gb_ref, o_ref):
    a = a_ref[...]
    gy = gy_ref[...]
    a_out = _dot(a, wp_ref[...].astype(BF16))
    val = _dot(gy, wv_ref[...].astype(BF16))
    gate = _dot(gy, wg_ref[...].astype(BF16))
    b_out = val * _sigmoid(gate)
    o_ref[...] = (_sigmoid(ga_ref[...]) * a_out + _sigmoid(gb_ref[...]) * b_out).astype(o_ref.dtype)


def branch_mix(a_pre, gy, w_proj_a, w_glu_b, layer, proj, col_ga, col_gb, *, tm, tn):
    t, k = a_pre.shape
    n = w_proj_a.shape[2]
    nj = n // tn
    rows = pl.BlockSpec((tm, k), lambda i, j: (i, 0), pipeline_mode=pl.Buffered(1))
    return pl.pallas_call(
        _k_branch_mix, grid=(t // tm, nj),
        in_specs=[rows, rows,
                  _wspec(w_proj_a, layer, tn), _wspec(w_glu_b, layer, tn),
                  _wspec(w_glu_b, layer, tn, lambda j: j + nj),
                  pl.BlockSpec((tm, tn), lambda i, j: (i, col_ga + j)),
                  pl.BlockSpec((tm, tn), lambda i, j: (i, col_gb + j))],
        out_specs=pl.BlockSpec((tm, tn), lambda i, j: (i, j)),
        out_shape=jax.ShapeDtypeStruct((t, n), BF16),
        compiler_params=_cparams(2), name="branch_mix",
    )(a_pre, gy, w_proj_a, w_glu_b, w_glu_b, proj, proj)


def _k_swiglu_up(x_ref, g_ref, wg_ref, wu_ref, o_ref, xn_ref, *, tm):
    @pl.when(pl.program_id(1) == 0)
    def _():
        _norm_to_scratch(x_ref, g_ref, xn_ref, tm)

    xn = xn_ref[...]
    gate = _dot(xn, wg_ref[...].astype(BF16))
    up = _dot(xn, wu_ref[...].astype(BF16))
    o_ref[...] = (_silu(gate) * up).astype(o_ref.dtype)


def norm_swiglu_up(x, g, w_gate, w_up, layer, *, tm, tn, name="ffn_up"):
    t, k = x.shape
    n = w_gate.shape[2]
    return pl.pallas_call(
        functools.partial(_k_swiglu_up, tm=tm),
        grid=(t // tm, pl.cdiv(n, tn)),
        in_specs=[pl.BlockSpec((tm, k), lambda i, j: (i, 0), pipeline_mode=_row_tile_buffering(tm, k, x.dtype)),
                  pl.BlockSpec((1, k), lambda i, j: (0, 0)),
                  _wspec(w_gate, layer, tn), _wspec(w_up, layer, tn)],
        out_specs=pl.BlockSpec((tm, tn), lambda i, j: (i, j)),
        out_shape=jax.ShapeDtypeStruct((t, n), BF16),
        scratch_shapes=[pltpu.VMEM((tm, k), BF16)],
        compiler_params=_cparams(2), name=name,
    )(x, g.reshape(1, k), w_gate, w_up)


def _cast_rows_to(dst_ref, src_ref, rc):
    def body(r, c):
        rows = pl.ds(pl.multiple_of(r * rc, rc), rc)
        dst_ref[rows, :] = src_ref[rows, :].astype(dst_ref.dtype)
        return c

    lax.fori_loop(0, src_ref.shape[0] // rc, body, 0)


def _k_ple(x_ref, g_ref, p_ref, wpg_ref, wple_ref, o_ref, wpg_bf, wple_bf, *, tn):
    @pl.when(pl.program_id(0) == 0)
    def _():
        _cast_rows_to(wpg_bf, wpg_ref, V7X_LANES)
        _cast_rows_to(wple_bf, wple_ref, V7X_LANES)

    x = x_ref[...]
    xn = _rmsnorm_rows(x, g_ref[...]).astype(BF16)
    pb = p_ref[...].astype(BF16)
    for c in range(x.shape[1] // tn):
        cols = slice(c * tn, (c + 1) * tn)
        gate = _dot(xn, wpg_bf[:, cols])
        emb = _dot(pb, wple_bf[:, cols])
        o_ref[:, cols] = x[:, cols] + emb * _sigmoid(gate)


def ple_matmul(x, g, p, w_pg, w_ple, layer, *, tm, tn, name="ple"):
    t, k = x.shape
    kp = p.shape[2]
    resident = lambda w: pl.BlockSpec((None,) + w.shape[1:], lambda i: (layer, 0, 0), pipeline_mode=pl.Buffered(1))
    return pl.pallas_call(
        functools.partial(_k_ple, tn=tn),
        grid=(t // tm,),
        in_specs=[pl.BlockSpec((tm, k), lambda i: (i, 0)),
                  pl.BlockSpec((1, k), lambda i: (0, 0)),
                  pl.BlockSpec((None, tm, kp), lambda i: (layer, i, 0)),
                  resident(w_pg), resident(w_ple)],
        out_specs=pl.BlockSpec((tm, k), lambda i: (i, 0)),
        out_shape=jax.ShapeDtypeStruct((t, k), F32),
        scratch_shapes=[pltpu.VMEM(w_pg.shape[1:], BF16), pltpu.VMEM(w_ple.shape[1:], BF16)],
        compiler_params=_cparams(1), name=name,
    )(x, g.reshape(1, k), p, w_pg, w_ple)


def _k_final_norm(xp_ref, xs_ref, g_ref, op_ref, os_ref, *, n_prompt_tiles):
    i = pl.program_id(0)

    @pl.when(i < n_prompt_tiles)
    def _():
        op_ref[...] = _rmsnorm_rows(xp_ref[...], g_ref[...])

    @pl.when(i == n_prompt_tiles)
    def _():
        os_ref[...] = _rmsnorm_rows(xs_ref[...], g_ref[...])


def final_norm_split(x, g, n_prompt, *, tm):
    t, k = x.shape
    bs = t - n_prompt
    npt = n_prompt // tm
    last = npt - 1
    return pl.pallas_call(
        functools.partial(_k_final_norm, n_prompt_tiles=npt), grid=(npt + 1,),
        in_specs=[pl.BlockSpec((tm, k), lambda i: (jnp.minimum(i, last), 0)),
                  pl.BlockSpec((bs, k), lambda i: (n_prompt // bs, 0)),
                  pl.BlockSpec((1, k), lambda i: (0, 0))],
        out_specs=[pl.BlockSpec((tm, k), lambda i: (jnp.minimum(i, last), 0)),
                   pl.BlockSpec((bs, k), lambda i: (0, 0))],
        out_shape=[jax.ShapeDtypeStruct((n_prompt, k), F32), jax.ShapeDtypeStruct((bs, k), F32)],
        compiler_params=_cparams(1), name="final_norm",
    )(x, x, g.reshape(1, k))


def _split_hi_lo(x):
    hi = x.astype(BF16)
    lo = (x - hi.astype(F32)).astype(BF16)
    return hi, lo


def _k_mlstm_prompt(u_ref, o_ref_in, if_ref, cw_ref, cb_ref, wq_ref, wk_ref, wv_ref, bif_ref, gh_ref, sk_ref,
                    tri_ref, a_init, a_ref, c_out, n_out, m_out, conv_out, upad_ref, *, L, dh):
    del a_init
    c_idx = pl.program_id(1)
    nh = N_HEADS

    @pl.when(c_idx == 0)
    def _():
        c_out[...] = jnp.zeros_like(c_out)
        n_out[...] = jnp.zeros_like(n_out)
        m_out[...] = jnp.zeros_like(m_out)
        upad_ref[pl.ds(0, 8), :] = jnp.zeros((8, nh * dh), F32)

    @pl.when(c_idx > 0)
    def _():
        upad_ref[pl.ds(0, 8), :] = upad_ref[pl.ds(L, 8), :]

    u = u_ref[...]
    upad_ref[pl.ds(8, L), :] = u
    conv = cb_ref[...] + u * cw_ref[CONV_W - 1:CONV_W, :]
    for j in range(CONV_W - 1):
        conv = conv + upad_ref[pl.ds(8 - (CONV_W - 1) + j, L), :] * cw_ref[j:j + 1, :]
    cact = _silu(conv)
    conv_out[0] = upad_ref[pl.ds(L, 8), :]

    pre = if_ref[...]
    li = pre[:, :V7X_LANES] + bif_ref[:, :V7X_LANES]
    lf = _log_sigmoid(pre[:, V7X_LANES:] + bif_ref[:, V7X_LANES:])
    tri = tri_ref[...]
    lf_hi, lf_mid = _split_hi_lo(lf)
    lf_lo = (lf - lf_hi.astype(F32) - lf_mid.astype(F32)).astype(BF16)
    bcum = _dot(tri, lf_hi) + _dot(tri, lf_mid) + _dot(tri, lf_lo)
    li_t = li.T
    b_t = bcum.T
    row_id = lax.broadcasted_iota(jnp.int32, (L, L), 0)
    col_id = lax.broadcasted_iota(jnp.int32, (L, L), 1)
    causal = col_id <= row_id
    lane = lax.broadcasted_iota(jnp.int32, (1, V7X_LANES), 1)
    m_row = m_out[0]
    m_new_row = m_row

    for h in range(nh):
        hs = slice(h * dh, (h + 1) * dh)
        ch = cact[:, hs].astype(BF16)
        uh = u[:, hs].astype(BF16)
        q = _dot(ch, wq_ref[h].astype(BF16))
        k = _dot(ch, wk_ref[h].astype(BF16)) * (dh ** -0.5)
        v = _dot(uh, wv_ref[h].astype(BF16))
        qb, kb, vb = q.astype(BF16), k.astype(BF16), v.astype(BF16)

        b_col = bcum[:, h:h + 1]
        li_col = li[:, h:h + 1]
        r_row = li_t[h:h + 1, :] - b_t[h:h + 1, :]
        m_prev = m_row[:, h:h + 1]
        d = jnp.where(causal, b_col + r_row, -jnp.inf)
        inter = b_col + m_prev
        m_t = jnp.maximum(inter, jnp.max(d, axis=-1, keepdims=True))
        w_inter = jnp.exp(inter - m_t)
        s = _dot_nt(qb, kb) * jnp.exp(d - m_t)
        c_prev = c_out[0, h]
        n_prev = n_out[0, h:h + 1, :]
        num = w_inter * _dot(qb, c_prev.astype(BF16)) + _dot(s.astype(BF16), vb)
        den = w_inter * jnp.sum(q * n_prev, axis=-1, keepdims=True) + jnp.sum(s, axis=-1, keepdims=True)
        hh = num / jnp.maximum(jnp.abs(den), jnp.exp(-m_t))

        b_last = b_col[L - 1:L, :]
        g_col = b_last - b_col + li_col
        m_new = jnp.maximum(b_last + m_prev, jnp.max(g_col, axis=0, keepdims=True))
        decay = jnp.exp(b_last + m_prev - m_new)
        wk_ = jnp.exp(g_col - m_new) * k
        c_out[0, h] = decay * c_prev + _dot_tn(wk_.astype(BF16), vb)
        n_out[0, h:h + 1, :] = decay * n_prev + jnp.sum(wk_, axis=0, keepdims=True)
        m_new_row = jnp.where(lane == h, m_new, m_new_row)

        hn = _rmsnorm_rows(hh, gh_ref[:, hs])
        gated = (hn + sk_ref[:, hs] * cact[:, hs]) * _sigmoid(o_ref_in[:, hs])
        a_ref[:, hs] = gated.astype(a_ref.dtype)

    m_out[0] = m_new_row


def mlstm_prompt(proj, col_u, col_o, col_if, n_rows, bsz, seq, lw, rows_init=None, *, L):
    dh = lw["w_q"].shape[-1]
    da = N_HEADS * dh
    nc = seq // L
    tri = jnp.asarray(np.tril(np.ones((L, L), np.float32)), BF16)
    row_blk = lambda b, c: b * nc + c
    full = lambda *shape: pl.BlockSpec(shape, lambda b, c: (0,) * len(shape))
    outs = pl.pallas_call(
        functools.partial(_k_mlstm_prompt, L=L, dh=dh),
        grid=(bsz, nc),
        in_specs=[pl.BlockSpec((L, da), lambda b, c: (row_blk(b, c), col_u)),
                  pl.BlockSpec((L, da), lambda b, c: (row_blk(b, c), col_o)),
                  pl.BlockSpec((L, 2 * V7X_LANES), lambda b, c: (row_blk(b, c), col_if)),
                  full(CONV_W, da), full(1, da), full(N_HEADS, dh, dh), full(N_HEADS, dh, dh),
                  full(N_HEADS, dh, dh), full(1, 2 * V7X_LANES), full(1, da), full(1, da), full(L, L),
                  pl.BlockSpec(memory_space=pl.ANY)],
        out_specs=[pl.BlockSpec((L, da), lambda b, c: (row_blk(b, c), 0)),
                   pl.BlockSpec((1, N_HEADS, dh, dh), lambda b, c: (b, 0, 0, 0)),
                   pl.BlockSpec((1, N_HEADS, dh), lambda b, c: (b, 0, 0)),
                   pl.BlockSpec((1, 1, V7X_LANES), lambda b, c: (b, 0, 0)),
                   pl.BlockSpec((1, 8, da), lambda b, c: (b, 0, 0))],
        out_shape=[jax.ShapeDtypeStruct((n_rows, da), BF16),
                   jax.ShapeDtypeStruct((bsz, N_HEADS, dh, dh), F32),
                   jax.ShapeDtypeStruct((bsz, N_HEADS, dh), F32),
                   jax.ShapeDtypeStruct((bsz, 1, V7X_LANES), F32),
                   jax.ShapeDtypeStruct((bsz, 8, da), F32)],
        scratch_shapes=[pltpu.VMEM((L + 8, da), F32)],
        input_output_aliases={12: 0},
        compiler_params=_cparams(2), name="mlstm_prompt",
    )(proj, proj, proj, lw["conv_w"], lw["conv_b"], lw["w_q"], lw["w_k"], lw["w_v"], lw["b_if"],
      lw["g_head"], lw["skip"], tri, jnp.zeros((n_rows, da), BF16) if rows_init is None else rows_init)
    a_pre, c_p, n_p, m_p, conv_p = outs
    return a_pre, c_p, n_p, m_p[:, 0, :N_HEADS], conv_p[:, 8 - (CONV_W - 1):, :]


def _k_mlstm_sample_pre(u_ref, if_ref, conv_ref, m_ref, cw_ref, cb_ref, wq_ref, wk_ref, wv_ref, bif_ref,
                        q_out, k_out, v_out, c_out, gates_out, conv_out, *, dh):
    nh = N_HEADS
    u = u_ref[...]
    conv = cb_ref[...] + u * cw_ref[CONV_W - 1:CONV_W, :]
    for j in range(CONV_W - 1):
        conv = conv + conv_ref[j] * cw_ref[j:j + 1, :]
        if j > 0:
            conv_out[j - 1] = conv_ref[j]
    conv_out[CONV_W - 2] = u
    cact = _silu(conv)
    c_out[...] = cact
    for h in range(nh):
        hs = slice(h * dh, (h + 1) * dh)
        ch = cact[:, hs].astype(BF16)
        q_out[:, hs] = _dot(ch, wq_ref[h].astype(BF16))
        k_out[:, hs] = _dot(ch, wk_ref[h].astype(BF16)) * (dh ** -0.5)
        v_out[:, hs] = _dot(u[:, hs].astype(BF16), wv_ref[h].astype(BF16))
    pre = if_ref[...]
    li = pre[:, :V7X_LANES] + bif_ref[:, :V7X_LANES]
    lf = _log_sigmoid(pre[:, V7X_LANES:] + bif_ref[:, V7X_LANES:])
    m_prev = m_ref[...]
    inter = lf + m_prev
    m_t = jnp.maximum(inter, li)
    gates_out[0] = jnp.exp(inter - m_t)
    gates_out[1] = jnp.exp(li - m_t)
    gates_out[2] = jnp.exp(-m_t)
    gates_out[3] = m_t


def _k_mlstm_sample_step(q_ref, k_ref, v_ref, gates_ref, c_ref, n_ref, cact_ref, o_ref_in, gh_ref, sk_ref,
                         a_any, c_any, c_out, n_out, a_out, hh_ref, *, bt, dh):
    del a_any, c_any
    i = pl.program_id(0)
    nh = N_HEADS
    q = q_ref[...]
    k = k_ref[...]
    v = v_ref[...]
    w_inter = gates_ref[0]
    w_new = gates_ref[1]
    e_neg_m = gates_ref[2]
    n_prev = n_ref[...]
    rows = pl.ds(pl.multiple_of(i * bt, bt), bt)
    for h in range(nh):
        hs = slice(h * dh, (h + 1) * dh)
        qh, kh, vh, nh_prev = q[:, hs], k[:, hs], v[:, hs], n_prev[:, hs]
        q_t = qh.T
        k_t = kh.T
        wi = w_inter[:, h:h + 1]
        wn = w_new[:, h:h + 1]
        s = jnp.sum(qh * kh, axis=-1, keepdims=True) * wn
        den = wi * jnp.sum(qh * nh_prev, axis=-1, keepdims=True) + s
        wv = wn * vh
        qc_rows = []
        for bl in range(bt):
            c_prev = c_ref[bl, h]
            qc_rows.append(jnp.sum(q_t[:, bl:bl + 1] * c_prev, axis=0, keepdims=True))
            c_out[bl, h] = wi[bl:bl + 1, :] * c_prev + k_t[:, bl:bl + 1] * wv[bl:bl + 1, :]
        qc = jnp.concatenate(qc_rows, axis=0)
        num = wi * qc + s * vh
        hh = num / jnp.maximum(jnp.abs(den), e_neg_m[:, h:h + 1])
        hh_ref[rows, hs] = hh
        n_out[:, hs] = wi * nh_prev + wn * kh

    @pl.when(i == pl.num_programs(0) - 1)
    def _():
        for h in range(nh):
            hs = slice(h * dh, (h + 1) * dh)
            hn = _rmsnorm_rows(hh_ref[:, hs], gh_ref[:, hs])
            a_out[:, hs] = ((hn + sk_ref[:, hs] * cact_ref[:, hs]) * _sigmoid(o_ref_in[:, hs])).astype(a_out.dtype)


def mlstm_sample(proj, col_u, col_o, col_if, row0, a_pre_all, c_all, layer, c_new_all, n0, m0, conv0, lw, *,
                 bt=8):
    _, bs, nh, dh, _ = c_all.shape
    da = nh * dh
    rb = row0 // bs
    m_pad = jnp.pad(m0, ((0, 0), (0, V7X_LANES - nh)))
    conv_t = jnp.transpose(conv0, (1, 0, 2))
    full = lambda *shape: pl.BlockSpec(shape, lambda i: (0,) * len(shape))
    q, k, v, cact, gates, conv_new = pl.pallas_call(
        functools.partial(_k_mlstm_sample_pre, dh=dh),
        grid=(1,),
        in_specs=[pl.BlockSpec((bs, da), lambda i: (rb, col_u)),
                  pl.BlockSpec((bs, 2 * V7X_LANES), lambda i: (rb, col_if)),
                  full(CONV_W - 1, bs, da), full(bs, V7X_LANES), full(CONV_W, da), full(1, da),
                  full(nh, dh, dh), full(nh, dh, dh), full(nh, dh, dh), full(1, 2 * V7X_LANES)],
        out_specs=[full(bs, da), full(bs, da), full(bs, da), full(bs, da), full(4, bs, V7X_LANES),
                   full(CONV_W - 1, bs, da)],
        out_shape=[jax.ShapeDtypeStruct((bs, da), F32)] * 4
                  + [jax.ShapeDtypeStruct((4, bs, V7X_LANES), F32),
                     jax.ShapeDtypeStruct((CONV_W - 1, bs, da), F32)],
        compiler_params=_cparams(1), name="mlstm_sample_pre",
    )(proj, proj, conv_t, m_pad, lw["conv_w"], lw["conv_b"], lw["w_q"], lw["w_k"], lw["w_v"], lw["b_if"])

    blk = lambda *shape: pl.BlockSpec(shape, lambda i: (i,) + (0,) * (len(shape) - 1))
    cst = lambda *shape: pl.BlockSpec(shape, lambda i: (0,) * len(shape))
    c_blk = pl.BlockSpec((None, bt, nh, dh, dh), lambda i: (layer, i, 0, 0, 0))
    c_new_all, n_new, a_pre_all = pl.pallas_call(
        functools.partial(_k_mlstm_sample_step, bt=bt, dh=dh),
        grid=(bs // bt,),
        in_specs=[blk(bt, da), blk(bt, da), blk(bt, da),
                  pl.BlockSpec((4, bt, V7X_LANES), lambda i: (0, i, 0)),
                  c_blk, blk(bt, da), cst(bs, da),
                  pl.BlockSpec((bs, da), lambda i: (rb, col_o)), cst(1, da), cst(1, da),
                  pl.BlockSpec(memory_space=pl.ANY), pl.BlockSpec(memory_space=pl.ANY)],
        out_specs=[c_blk, blk(bt, da), pl.BlockSpec((bs, da), lambda i: (rb, 0))],
        out_shape=[jax.ShapeDtypeStruct(c_new_all.shape, F32), jax.ShapeDtypeStruct((bs, da), F32),
                   jax.ShapeDtypeStruct(a_pre_all.shape, a_pre_all.dtype)],
        scratch_shapes=[pltpu.VMEM((bs, da), F32)],
        input_output_aliases={10: 2, 11: 0},
        compiler_params=_cparams(1), name="mlstm_sample_step",
    )(q, k, v, gates, c_all, n0.reshape(bs, da), cact, proj, lw["g_head"], lw["skip"], a_pre_all, c_new_all)
    m_new = gates[3][:, :nh]
    return a_pre_all, c_new_all, n_new.reshape(bs, nh, dh), m_new, jnp.transpose(conv_new, (1, 0, 2))


S5_SLAB_GROUPS = V7X_LANES // S5_GROUP
S5_SLAB_STATES = S5_SLAB_GROUPS * S5_STATE


def _s5_params(lp):
    g, p = lp["a_re"].shape
    dt = jnp.exp(lp["log_dt"].astype(F32))[:, None]
    a_re = lp["a_re"].astype(F32)
    a_im = lp["a_im"].astype(F32)
    lam_re = a_re * dt
    lam_im = a_im * dt
    mag = jnp.exp(lam_re)
    ab_re = mag * jnp.cos(lam_im)
    ab_im = mag * jnp.sin(lam_im)
    den = a_re * a_re + a_im * a_im
    nr = ab_re - 1.0
    ni = ab_im
    k_re = (nr * a_re + ni * a_im) / den
    k_im = (ni * a_re - nr * a_im) / den
    b_re = lp["b_re"].astype(F32)
    b_im = lp["b_im"].astype(F32)
    bb_re = k_re[..., None] * b_re - k_im[..., None] * b_im
    bb_im = k_re[..., None] * b_im + k_im[..., None] * b_re
    ns = g // S5_SLAB_GROUPS
    eye = jnp.eye(S5_SLAB_GROUPS, dtype=F32)

    def in_blockdiag(bb):
        bs = bb.reshape(ns, S5_SLAB_GROUPS, p, S5_GROUP)
        w = jnp.einsum("ab,sapc->sacbp", eye, bs)
        return w.reshape(ns, V7X_LANES, S5_SLAB_STATES).astype(BF16)

    def out_blockdiag(cc):
        cs = cc.astype(F32).reshape(ns, S5_SLAB_GROUPS, S5_GROUP, p)
        w = jnp.einsum("ab,sacp->sapbc", eye, cs)
        return w.reshape(ns, S5_SLAB_STATES, V7X_LANES)

    wc = jnp.concatenate([out_blockdiag(lp["c_re"]), -out_blockdiag(lp["c_im"])], axis=1).astype(BF16)
    return dict(lam_re=lam_re.reshape(1, g * p), lam_im=lam_im.reshape(1, g * p),
                ab_re=ab_re.reshape(1, g * p), ab_im=ab_im.reshape(1, g * p),
                wb_re=in_blockdiag(bb_re), wb_im=in_blockdiag(bb_im), wc=wc,
                d_skip=lp["d_skip"].astype(F32).reshape(1, g * S5_GROUP))


def _s5_powers(sp, ks):
    kk = jnp.asarray(ks, F32)[:, None]
    mag = jnp.exp(kk * sp["lam_re"])
    return mag * jnp.cos(kk * sp["lam_im"]), mag * jnp.sin(kk * sp["lam_im"])


def _s5_input_proj(u_bf, wbr_ref, wbi_ref, bur_ref, bui_ref):
    ns = wbr_ref.shape[0]
    for s in range(ns):
        us = u_bf[:, s * V7X_LANES:(s + 1) * V7X_LANES]
        cols = slice(s * S5_SLAB_STATES, (s + 1) * S5_SLAB_STATES)
        bur_ref[:, cols] = _dot(us, wbr_ref[s])
        bui_ref[:, cols] = _dot(us, wbi_ref[s])


def _s5_output_proj(xr_ref, xi_ref, wc_ref, skip):
    ns = wc_ref.shape[0]
    outs = []
    for s in range(ns):
        cols = slice(s * S5_SLAB_STATES, (s + 1) * S5_SLAB_STATES)
        xcat = jnp.concatenate([xr_ref[:, cols].astype(BF16), xi_ref[:, cols].astype(BF16)], axis=1)
        y = _dot(xcat, wc_ref[s]) + skip[:, s * V7X_LANES:(s + 1) * V7X_LANES]
        outs.append(_gelu_tanh(y).astype(BF16))
    return jnp.concatenate(outs, axis=1)


def _k_s5_prompt(u_ref, wbr_ref, wbi_ref, wc_ref, d_ref, abr_ref, abi_ref, tpr_ref, tpi_ref, ajr_ref, aji_ref,
                 perm_ref, permt_ref, gy_init, gy_ref, sre_out, sim_out,
                 bur0, bur1, bui0, bui1, xb0, xb1, *, tc):
    del gy_init
    c_idx = pl.program_id(1)
    nsteps = tc // V7X_SUBLANES
    ns = wbr_ref.shape[0]
    lw = S5_SLAB_STATES
    pair = 2 * V7X_SUBLANES
    bur, bui, xb = (bur0, bur1), (bui0, bui1), (xb0, xb1)

    @pl.when(c_idx == 0)
    def _():
        sre_out[...] = jnp.zeros_like(sre_out)
        sim_out[...] = jnp.zeros_like(sim_out)

    u = u_ref[...]
    u_hi, u_lo = _split_hi_lo(u)
    perm = perm_ref[...]
    up_hi = _dot(perm, u_hi)
    skip = d_ref[...] * (up_hi + _dot(perm, u_lo))
    up_bf = up_hi.astype(BF16)

    sub = lax.broadcasted_iota(jnp.int32, (V7X_SUBLANES, lw), 0)

    def input_proj(s):
        us = up_bf[:, s * V7X_LANES:(s + 1) * V7X_LANES]
        bur[s % 2][...] = _dot(us, wbr_ref[s])
        bui[s % 2][...] = _dot(us, wbi_ref[s])

    def scan(s):
        br, bi, xo = bur[s % 2], bui[s % 2], xb[s % 2]
        lanes = slice(s * lw, (s + 1) * lw)
        ar = jnp.broadcast_to(abr_ref[:, lanes], (V7X_SUBLANES, lw))
        ai = jnp.broadcast_to(abi_ref[:, lanes], (V7X_SUBLANES, lw))
        er = jnp.zeros((V7X_SUBLANES, lw), F32)
        ei = er
        for i in range(nsteps):
            rows = slice(i * V7X_SUBLANES, (i + 1) * V7X_SUBLANES)
            er, ei = (ar * er - ai * ei + br[rows, :], ar * ei + ai * er + bi[rows, :])
            br[rows, :] = er
            bi[rows, :] = ei
        for d, row in ((1, 0), (2, 1), (4, 3)):
            pr = ajr_ref[row:row + 1, lanes]
            pi = aji_ref[row:row + 1, lanes]
            sr = pltpu.roll(er, d, 0)
            si = pltpu.roll(ei, d, 0)
            keep = sub >= d
            er, ei = (er + jnp.where(keep, pr * sr - pi * si, 0.0),
                      ei + jnp.where(keep, pr * si + pi * sr, 0.0))
        c0r = jnp.broadcast_to(sre_out[0, :, lanes], (V7X_SUBLANES, lw))
        c0i = jnp.broadcast_to(sim_out[0, :, lanes], (V7X_SUBLANES, lw))
        ajr = ajr_ref[:, lanes]
        aji = aji_ref[:, lanes]
        fr = ajr * c0r - aji * c0i + er
        fi = ajr * c0i + aji * c0r + ei
        cin_r = jnp.where(sub >= 1, pltpu.roll(fr, 1, 0), c0r)
        cin_i = jnp.where(sub >= 1, pltpu.roll(fi, 1, 0), c0i)
        sre_out[0, :, lanes] = fr[V7X_SUBLANES - 1:V7X_SUBLANES, :]
        sim_out[0, :, lanes] = fi[V7X_SUBLANES - 1:V7X_SUBLANES, :]

        cin2_r = jnp.concatenate([cin_r, cin_r], axis=0)
        cin2_i = jnp.concatenate([cin_i, cin_i], axis=0)
        for k in range(tc // pair):
            rows = slice(k * pair, (k + 1) * pair)
            pr = tpr_ref[rows, lanes]
            pi = tpi_ref[rows, lanes]
            xo[rows, :lw] = (br[rows, :] + (pr * cin2_r - pi * cin2_i)).astype(BF16)
            xo[rows, lw:] = (bi[rows, :] + (pr * cin2_i + pi * cin2_r)).astype(BF16)

    def output_proj(s):
        y = _dot(xb[s % 2][...], wc_ref[s]) + skip[:, s * V7X_LANES:(s + 1) * V7X_LANES]
        return _gelu_tanh(y).astype(BF16)

    outs = []
    input_proj(0)
    for s in range(ns):
        if s + 1 < ns:
            input_proj(s + 1)
        scan(s)
        outs.append(output_proj(s))
    g_perm = jnp.concatenate(outs, axis=1)
    gy_ref[...] = _dot(permt_ref[...], g_perm).astype(gy_ref.dtype)


def s5_prompt(proj, col_u, n_rows, bsz, seq, sp, rows_init=None, *, tc=256):
    db = sp["d_skip"].shape[1]
    nch = sp["ab_re"].shape[1]
    nc = seq // tc
    nsteps = tc // V7X_SUBLANES
    tpr, tpi = (jnp.repeat(a, V7X_SUBLANES, axis=0) for a in _s5_powers(sp, np.arange(1, nsteps + 1)))
    ajr, aji = _s5_powers(sp, nsteps * np.arange(1, V7X_SUBLANES + 1))
    perm = np.zeros((tc, tc), np.float32)
    r = np.arange(tc)
    perm[r, (r % V7X_SUBLANES) * nsteps + r // V7X_SUBLANES] = 1.0
    full = lambda a: pl.BlockSpec(a.shape, lambda b, c: (0,) * a.ndim)
    consts = [sp["wb_re"], sp["wb_im"], sp["wc"], sp["d_skip"], sp["ab_re"], sp["ab_im"], tpr, tpi, ajr, aji,
              jnp.asarray(perm, BF16), jnp.asarray(perm.T, BF16)]
    gy, s_re, s_im = pl.pallas_call(
        functools.partial(_k_s5_prompt, tc=tc),
        grid=(bsz, nc),
        in_specs=[pl.BlockSpec((tc, db), lambda b, c: (b * nc + c, col_u))] + [full(a) for a in consts]
                 + [pl.BlockSpec(memory_space=pl.ANY)],
        out_specs=[pl.BlockSpec((tc, db), lambda b, c: (b * nc + c, 0)),
                   pl.BlockSpec((1, 1, nch), lambda b, c: (b, 0, 0)),
                   pl.BlockSpec((1, 1, nch), lambda b, c: (b, 0, 0))],
        out_shape=[jax.ShapeDtypeStruct((n_rows, db), BF16),
                   jax.ShapeDtypeStruct((bsz, 1, nch), F32), jax.ShapeDtypeStruct((bsz, 1, nch), F32)],
        scratch_shapes=[pltpu.VMEM((tc, S5_SLAB_STATES), F32)] * 4 + [pltpu.VMEM((tc, 2 * S5_SLAB_STATES), BF16)] * 2,
        input_output_aliases={1 + len(consts): 0},
        compiler_params=_cparams(2), name="s5_prompt",
    )(proj, *consts, jnp.zeros((n_rows, db), BF16) if rows_init is None else rows_init)
    return gy, s_re, s_im


def _k_s5_sample(u_ref, x0r_ref, x0i_ref, wbr_ref, wbi_ref, wc_ref, d_ref, abr_ref, abi_ref, gy_any,
                 gy_ref, xr_out, xi_out):
    del gy_any
    u = u_ref[...]
    _s5_input_proj(u.astype(BF16), wbr_ref, wbi_ref, xr_out, xi_out)
    ar = abr_ref[...]
    ai = abi_ref[...]
    x0r = x0r_ref[...]
    x0i = x0i_ref[...]
    xr_out[...] = xr_out[...] + (ar * x0r - ai * x0i)
    xi_out[...] = xi_out[...] + (ar * x0i + ai * x0r)
    gy_ref[...] = _s5_output_proj(xr_out, xi_out, wc_ref, d_ref[...] * u).astype(gy_ref.dtype)


def s5_sample(proj, col_u, row0, gy_all, x0_re, x0_im, sp):
    bs = x0_re.shape[0]
    db = sp["d_skip"].shape[1]
    nch = sp["ab_re"].shape[1]
    rb = row0 // bs
    full = lambda a: pl.BlockSpec(a.shape, lambda i: (0,) * a.ndim)
    consts = [sp["wb_re"], sp["wb_im"], sp["wc"], sp["d_skip"], sp["ab_re"], sp["ab_im"]]
    x0r = x0_re.reshape(bs, nch)
    x0i = x0_im.reshape(bs, nch)
    gy_all, xr, xi = pl.pallas_call(
        _k_s5_sample,
        grid=(1,),
        in_specs=[pl.BlockSpec((bs, db), lambda i: (rb, col_u)), full(x0r), full(x0i)]
                 + [full(a) for a in consts] + [pl.BlockSpec(memory_space=pl.ANY)],
        out_specs=[pl.BlockSpec((bs, db), lambda i: (rb, 0)),
                   pl.BlockSpec((bs, nch), lambda i: (0, 0)), pl.BlockSpec((bs, nch), lambda i: (0, 0))],
        out_shape=[jax.ShapeDtypeStruct(gy_all.shape, gy_all.dtype),
                   jax.ShapeDtypeStruct((bs, nch), F32), jax.ShapeDtypeStruct((bs, nch), F32)],
        input_output_aliases={9: 0},
        compiler_params=_cparams(1), name="s5_sample",
    )(proj, x0r, x0i, *consts, gy_all)
    return gy_all, xr, xi


def _k_router(x_ref, g_ref, w_ref, b_ref, hn_ref, lg_ref):
    hn = _rmsnorm_rows(x_ref[...], g_ref[...])
    hn_ref[...] = hn
    x_hi, x_lo = _split_hi_lo(hn)
    w_hi, w_lo = _split_hi_lo(w_ref[...])
    lg_ref[...] = _dot(x_hi, w_hi) + (_dot(x_lo, w_hi) + _dot(x_hi, w_lo)) + b_ref[...]


def router(x, g, w_router, b_router, *, tm):
    t, k = x.shape
    ne = w_router.shape[1]
    w_pad = jnp.pad(w_router, ((0, 0), (0, V7X_LANES - ne)))
    b_pad = jnp.pad(b_router.astype(F32), (0, V7X_LANES - ne)).reshape(1, V7X_LANES)
    hn, lg = pl.pallas_call(
        _k_router, grid=(t // tm,),
        in_specs=[pl.BlockSpec((tm, k), lambda i: (i, 0)), pl.BlockSpec((1, k), lambda i: (0, 0)),
                  pl.BlockSpec((k, V7X_LANES), lambda i: (0, 0)), pl.BlockSpec((1, V7X_LANES), lambda i: (0, 0))],
        out_specs=[pl.BlockSpec((tm, k), lambda i: (i, 0)), pl.BlockSpec((tm, V7X_LANES), lambda i: (i, 0))],
        out_shape=[jax.ShapeDtypeStruct((t, k), F32), jax.ShapeDtypeStruct((t, V7X_LANES), F32)],
        compiler_params=_cparams(1), name="router",
    )(x, g.reshape(1, k), w_pad, b_pad)
    return hn, lg[:, :ne]


DMA_ISSUE_UNROLL = 8


def _k_gather_rows(nused_ref, tok_ref, src_hbm, o_ref, buf_ref, sems, *, tr):
    t = pl.program_id(0)
    n_used = nused_ref[0]

    def issue_tile(tile):
        slot = tile % 2
        base = tile * tr

        def issue(r, c):
            tok = tok_ref[base + r]
            pltpu.make_async_copy(src_hbm.at[pl.ds(tok, 1), :], buf_ref.at[slot, pl.ds(r, 1), :],
                                  sems.at[slot]).start()
            return c

        lax.fori_loop(0, tr, issue, 0, unroll=DMA_ISSUE_UNROLL)

    @pl.when(t == 0)
    def _():
        issue_tile(t)

    @pl.when(t + 1 < n_used)
    def _():
        issue_tile(t + 1)

    @pl.when(t < n_used)
    def _():
        slot = t % 2
        pltpu.make_async_copy(src_hbm.at[pl.ds(0, tr), :], buf_ref.at[slot], sems.at[slot]).wait()
        o_ref[...] = buf_ref[slot].astype(o_ref.dtype)

    @pl.when(t >= n_used)
    def _():
        o_ref[...] = jnp.zeros_like(o_ref)


def gather_rows(src, tok, n_used, *, tr):
    t, k = src.shape
    r_pad = tok.shape[0]
    nt = r_pad // tr
    return pl.pallas_call(
        functools.partial(_k_gather_rows, tr=tr),
        grid_spec=pltpu.PrefetchScalarGridSpec(
            num_scalar_prefetch=2, grid=(nt,),
            in_specs=[pl.BlockSpec(memory_space=pl.ANY)],
            out_specs=pl.BlockSpec((tr, k), lambda i, nu, tk: (i, 0)),
            scratch_shapes=[pltpu.VMEM((2, tr, k), src.dtype), pltpu.SemaphoreType.DMA((2,))]),
        out_shape=jax.ShapeDtypeStruct((r_pad, k), BF16),
        compiler_params=_cparams(1), name="moe_gather",
    )(n_used, tok, src)


def _k_moe_up(nused_ref, te_ref, x_ref, wg_ref, wu_ref, o_ref):
    del te_ref
    t = pl.program_id(1)

    @pl.when(t < nused_ref[0])
    def _():
        x = x_ref[...]
        gate = _dot(x, wg_ref[...].astype(BF16))
        up = _dot(x, wu_ref[...].astype(BF16))
        o_ref[...] = (_silu(gate) * up).astype(o_ref.dtype)

    @pl.when(t >= nused_ref[0])
    def _():
        o_ref[...] = jnp.zeros_like(o_ref)


def moe_up(xs, w_gate, w_up, tile_expert, n_used, *, tr, tn):
    r_pad, k = xs.shape
    ne, _, f = w_gate.shape
    nt = r_pad // tr
    row = lambda j, t, nu, te: (jnp.minimum(t, nu[0] - 1), 0)
    wmap = lambda j, t, nu, te: (te[t], 0, j)
    return pl.pallas_call(
        _k_moe_up,
        grid_spec=pltpu.PrefetchScalarGridSpec(
            num_scalar_prefetch=2, grid=(pl.cdiv(f, tn), nt),
            in_specs=[pl.BlockSpec((tr, k), row),
                      pl.BlockSpec((None, k, tn), wmap), pl.BlockSpec((None, k, tn), wmap)],
            out_specs=pl.BlockSpec((tr, tn), lambda j, t, nu, te: (t, j))),
        out_shape=jax.ShapeDtypeStruct((r_pad, f), BF16),
        compiler_params=_cparams(2), name="moe_up",
    )(n_used, tile_expert, xs, w_gate, w_up)


def _k_moe_down(nused_ref, te_ref, x_ref, w_ref, o_ref):
    del te_ref
    t = pl.program_id(1)

    @pl.when(t < nused_ref[0])
    def _():
        o_ref[...] = _dot(x_ref[...], w_ref[...].astype(BF16))

    @pl.when(t >= nused_ref[0])
    def _():
        o_ref[...] = jnp.zeros_like(o_ref)


def moe_down(hid, w_down, tile_expert, n_used, *, tr, tn):
    r_pad, f = hid.shape
    d = w_down.shape[2]
    nt = r_pad // tr
    return pl.pallas_call(
        _k_moe_down,
        grid_spec=pltpu.PrefetchScalarGridSpec(
            num_scalar_prefetch=2, grid=(d // tn, nt),
            in_specs=[pl.BlockSpec((tr, f), lambda j, t, nu, te: (jnp.minimum(t, nu[0] - 1), 0)),
                      pl.BlockSpec((None, f, tn), lambda j, t, nu, te: (te[t], 0, j))],
            out_specs=pl.BlockSpec((tr, tn), lambda j, t, nu, te: (t, j))),
        out_shape=jax.ShapeDtypeStruct((r_pad, d), F32),
        compiler_params=_cparams(2), name="moe_down",
    )(n_used, tile_expert, hid, w_down)


def _k_moe_combine(pos_ref, ys_hbm, gate_ref, res_ref, o_ref, buf0_ref, buf1_ref, sems, *, tr):
    t = pl.program_id(0)
    n_tiles = pl.num_programs(0)
    n_tok = n_tiles * tr

    def issue_tile(tile):
        slot = tile % 2
        base = tile * tr

        def issue(r, c):
            p0 = pos_ref[base + r]
            p1 = pos_ref[n_tok + base + r]
            pltpu.make_async_copy(ys_hbm.at[pl.ds(p0, 1), :], buf0_ref.at[slot, pl.ds(r, 1), :],
                                  sems.at[slot]).start()
            pltpu.make_async_copy(ys_hbm.at[pl.ds(p1, 1), :], buf1_ref.at[slot, pl.ds(r, 1), :],
                                  sems.at[slot]).start()
            return c

        lax.fori_loop(0, tr, issue, 0, unroll=DMA_ISSUE_UNROLL)

    @pl.when(t == 0)
    def _():
        issue_tile(t)

    @pl.when(t + 1 < n_tiles)
    def _():
        issue_tile(t + 1)

    slot = t % 2
    pltpu.make_async_copy(ys_hbm.at[pl.ds(0, tr), :], buf0_ref.at[slot], sems.at[slot]).wait()
    pltpu.make_async_copy(ys_hbm.at[pl.ds(0, tr), :], buf1_ref.at[slot], sems.at[slot]).wait()
    g = gate_ref[...]
    o_ref[...] = res_ref[...] + (g[:, 0:1] * buf0_ref[slot] + g[:, 1:2] * buf1_ref[slot])


def moe_combine(ys, pos, gates, res, *, tr):
    t, d = res.shape
    g_pad = jnp.pad(gates, ((0, 0), (0, V7X_LANES - gates.shape[1])))
    return pl.pallas_call(
        functools.partial(_k_moe_combine, tr=tr),
        grid_spec=pltpu.PrefetchScalarGridSpec(
            num_scalar_prefetch=1, grid=(t // tr,),
            in_specs=[pl.BlockSpec(memory_space=pl.ANY),
                      pl.BlockSpec((tr, V7X_LANES), lambda i, p: (i, 0)),
                      pl.BlockSpec((tr, d), lambda i, p: (i, 0))],
            out_specs=pl.BlockSpec((tr, d), lambda i, p: (i, 0)),
            scratch_shapes=[pltpu.VMEM((2, tr, d), F32), pltpu.VMEM((2, tr, d), F32),
                            pltpu.SemaphoreType.DMA((2,))]),
        out_shape=jax.ShapeDtypeStruct((t, d), F32),
        compiler_params=_cparams(1), name="moe_combine",
    )(pos, ys, g_pad, res)


def moe_layer(h, g_ffn, w_router, b_router, w_gate, w_up, w_down, *, tm, tr, tn_up, tn_down, tr_gather,
              tr_combine):
    t, d = h.shape
    ne = w_gate.shape[0]
    hn, logits = router(h, g_ffn, w_router, b_router, tm=tm)
    top_v, top_e = lax.top_k(logits, TOP_K)
    gates = jax.nn.softmax(top_v, axis=-1)
    flat_e = top_e.reshape(-1)
    onehot = (flat_e[:, None] == jnp.arange(ne, dtype=flat_e.dtype)[None, :]).astype(jnp.int32)
    rank = jnp.sum((jnp.cumsum(onehot, axis=0) - onehot) * onehot, axis=1)
    sizes = jnp.sum(onehot, axis=0)
    tiles_per = (sizes + tr - 1) // tr
    tile_end = jnp.cumsum(tiles_per)
    tile_start = tile_end - tiles_per
    n_used = tile_end[-1:].astype(jnp.int32)
    nt = (t * TOP_K) // tr + ne
    r_pad = nt * tr
    pos = (tile_start[flat_e] * tr + rank).astype(jnp.int32)
    src_tok = jnp.zeros((r_pad,), jnp.int32).at[pos].set(jnp.arange(t * TOP_K, dtype=jnp.int32) // TOP_K)
    tile_ids = jnp.minimum(jnp.arange(nt, dtype=jnp.int32), n_used[0] - 1)
    tile_expert = jnp.sum((tile_ids[:, None] >= tile_end[None, :]).astype(jnp.int32), axis=1).astype(jnp.int32)
    xs = gather_rows(hn, src_tok, n_used * (tr // tr_gather), tr=tr_gather)
    hid = moe_up(xs, w_gate, w_up, tile_expert, n_used, tr=tr, tn=tn_up)
    ys = moe_down(hid, w_down, tile_expert, n_used, tr=tr, tn=tn_down)
    return moe_combine(ys, pos.reshape(t, TOP_K).T.reshape(-1), gates, h, tr=tr_combine)


def _layer_weights(i, conv_w, conv_b, w_q, w_k, w_v, b_i, b_f, g_head, skip_a):
    nh = b_i.shape[1]
    pad = jnp.zeros((V7X_LANES - nh,), F32)
    b_if = jnp.concatenate([b_i[i].astype(F32), pad, b_f[i].astype(F32), pad]).reshape(1, 2 * V7X_LANES)
    da = conv_w.shape[-1]
    return dict(conv_w=conv_w[i], conv_b=conv_b[i].reshape(1, da), w_q=w_q[i], w_k=w_k[i], w_v=w_v[i],
                b_if=b_if, g_head=g_head[i].reshape(1, da), skip=skip_a[i].reshape(1, da))


def _tile_plan(n_rows, seq):
    tm = next(c for c in (832, 640, 512, 256, 128, 64, 32, 16) if n_rows % c == 0)
    tc = next(c for c in (640, 320, 256, 128, 64, 32, 16, 8) if n_rows % c == 0)
    tm_big = 2 * tm if n_rows % (2 * tm) == 0 else tm
    return dict(tm=tm, tm_big=tm_big, tn=512, tn_down=256, mlstm_chunk=256, s5_chunk=256,
                moe_tr=512, moe_tn_up=512, moe_tn_down=512, gather_tr=512, combine_tr=tc, ple_tm=tm // 2,
                norm_tm=min(seq, 1024))


def kernel(x_prompt, x_sample, p_prompt, p_sample, state_mlstm_C, state_mlstm_n, state_mlstm_m, state_mlstm_conv,
           state_s5_re, state_s5_im, g_mix, w_in, conv_w, conv_b, w_q, w_k, w_v, b_i, b_f, g_head, skip_a, w_proj_a,
           s5_log_dt, s5_A_re, s5_A_im, s5_B_re, s5_B_im, s5_C_re, s5_C_im, s5_D, w_glu_b, w_out, g_ffn,
           w_ff_gate, w_ff_up, w_ff_down, w_router, b_router, w_moe_gate, w_moe_up, w_moe_down,
           g_ple, w_ple, w_pg, g_final):
    bsz, seq, d = x_prompt.shape
    bs = x_sample.shape[0]
    depth = g_mix.shape[0]
    nh = b_i.shape[1]
    d_a = conv_w.shape[-1]
    d_b = s5_D.shape[-1]
    n_p = bsz * seq
    t = n_p + bs
    tl = _tile_plan(t, seq)
    tm, tmb, tn = tl["tm"], tl["tm_big"], tl["tn"]

    h = jnp.concatenate([x_prompt.reshape(n_p, d), x_sample.reshape(bs, d)], axis=0).astype(F32)
    p_all = jnp.concatenate([p_prompt.reshape(depth, n_p, -1), p_sample.reshape(depth, bs, -1)], axis=1)

    col_ua, col_oa, col_ub = 0, 1, 2 * d_a // d_b
    col_ga = (2 * d_a + d_b) // tn
    col_gb = (2 * d_a + d_b + d) // tn
    col_if = (2 * d_a + d_b + 2 * d) // (2 * V7X_LANES)

    states = [[] for _ in range(11)]
    c_s_all = jnp.zeros(state_mlstm_C.shape, F32)
    a_pre = gy = None
    for i in range(depth):
        lw = _layer_weights(i, conv_w, conv_b, w_q, w_k, w_v, b_i, b_f, g_head, skip_a)
        sp = _s5_params(dict(log_dt=s5_log_dt[i], a_re=s5_A_re[i], a_im=s5_A_im[i], b_re=s5_B_re[i],
                             b_im=s5_B_im[i], c_re=s5_C_re[i], c_im=s5_C_im[i], d_skip=s5_D[i]))
        proj = in_proj(h, g_mix[i], w_in, i, d_a, d_b, d, nh, tm=tmb, tn=tn)

        a_pre, c_p, n_pp, m_p, conv_p = mlstm_prompt(proj, col_ua, col_oa, col_if, t, bsz, seq, lw, a_pre,
                                                     L=tl["mlstm_chunk"])
        a_pre, c_s_all, n_s, m_s, conv_s = mlstm_sample(proj, col_ua, col_oa, col_if, n_p, a_pre,
                                                        state_mlstm_C, i, c_s_all, state_mlstm_n[i].astype(F32),
                                                        state_mlstm_m[i].astype(F32),
                                                        state_mlstm_conv[i].astype(F32), lw)
        gy, sre_p, sim_p = s5_prompt(proj, col_ub, t, bsz, seq, sp, gy, tc=tl["s5_chunk"])
        gy, sre_s, sim_s = s5_sample(proj, col_ub, n_p, gy, state_s5_re[i].astype(F32),
                                     state_s5_im[i].astype(F32), sp)
        mix = branch_mix(a_pre, gy, w_proj_a, w_glu_b, i, proj, col_ga, col_gb, tm=tmb, tn=tn)
        h = matmul(mix, w_out, i, h, tm=tmb, tn=tn, name="out_proj")

        j = i // 2
        if i % 2 == 0:
            hid = norm_swiglu_up(h, g_ffn[i], w_ff_gate, w_ff_up, j, tm=tmb, tn=tn)
            h = matmul(hid, w_ff_down, j, h, tm=tmb, tn=tl["tn_down"], name="ffn_down")
        else:
            h = moe_layer(h, g_ffn[i], w_router[j], b_router[j], w_moe_gate[j], w_moe_up[j], w_moe_down[j],
                          tm=tm, tr=tl["moe_tr"], tn_up=tl["moe_tn_up"], tn_down=tl["moe_tn_down"],
                          tr_gather=tl["gather_tr"], tr_combine=tl["combine_tr"])
        h = ple_matmul(h, g_ple[i], p_all, w_pg, w_ple, i, tm=tl["ple_tm"], tn=tn)

        g_s, p_s = S5_STATE, sre_p.shape[-1] // S5_STATE
        new = [c_p, n_pp, m_p, conv_p, sre_p.reshape(bsz, p_s, g_s), sim_p.reshape(bsz, p_s, g_s),
               n_s, m_s, conv_s, sre_s.reshape(bs, p_s, g_s), sim_s.reshape(bs, p_s, g_s)]
        for lst, s in zip(states, new):
            lst.append(s)

    y_prompt, y_sample = final_norm_split(h, g_final, n_p, tm=tl["norm_tm"])
    y_prompt = y_prompt.reshape(bsz, seq, d)
    y_sample = y_sample.reshape(bs, 1, d)
    st = [jnp.stack(lst) for lst in states]
    return (y_prompt, y_sample) + tuple(st[:6]) + (c_s_all,) + tuple(st[6:])
```

```python
import functools
import math

import numpy as np
import jax
import jax.numpy as jnp
from jax import lax
from jax.experimental import pallas as pl
from jax.experimental.pallas import tpu as pltpu

F32 = jnp.float32
BF16 = jnp.bfloat16
EPS = 1e-6

V7X_VMEM_BYTES = 64 * 1024 * 1024
V7X_LANES = 128
V7X_SUBLANES = 8
VMEM_LIMIT = V7X_VMEM_BYTES - 8 * 1024 * 1024

N_HEADS = 4
CONV_W = 4
S5_GROUP = 16
S5_STATE = 64
TOP_K = 2


def _cparams(n_axes, vmem=VMEM_LIMIT):
    return pltpu.CompilerParams(dimension_semantics=("arbitrary",) * n_axes, vmem_limit_bytes=vmem)


def _sigmoid(x):
    return 1.0 / (1.0 + jnp.exp(-x))


def _silu(x):
    return x * _sigmoid(x)


def _gelu_tanh(x):
    return 0.5 * x * (1.0 + jnp.tanh(math.sqrt(2.0 / math.pi) * (x + 0.044715 * (x * x * x))))


def _log_sigmoid(x):
    return jnp.minimum(x, 0.0) - jnp.log(1.0 + jnp.exp(-jnp.abs(x)))


def _dot(a, b):
    return jnp.dot(a, b, preferred_element_type=F32)


def _dot_nt(a, b):
    return lax.dot_general(a, b, (((1,), (1,)), ((), ())), preferred_element_type=F32)


def _dot_tn(a, b):
    return lax.dot_general(a, b, (((0,), (0,)), ((), ())), preferred_element_type=F32)


def _rmsnorm_rows(x, g):
    ms = jnp.mean(x * x, axis=-1, keepdims=True)
    return x * lax.rsqrt(ms + EPS) * g


def _row_tile_buffering(tm, k, dtype):
    two_copies = 2 * tm * k * jnp.dtype(dtype).itemsize
    return pl.Buffered(1) if two_copies > VMEM_LIMIT // 4 else None


def _row_chunks(tm):
    for rc in (256, 208, 128, 104, 64, 32, 16, 8):
        if tm % rc == 0:
            return rc
    return tm


def _norm_to_scratch(x_ref, g_ref, xn_ref, tm):
    rc = _row_chunks(tm)

    def body(r, c):
        rows = pl.ds(pl.multiple_of(r * rc, rc), rc)
        xn_ref[rows, :] = _rmsnorm_rows(x_ref[rows, :], g_ref[...]).astype(BF16)
        return c

    lax.fori_loop(0, tm // rc, body, 0)


def _wspec(w, layer, tn, col=lambda j: j):
    return pl.BlockSpec((None, w.shape[1], tn), lambda i, j: (layer, 0, col(j)))


def _k_in_proj(x_ref, g_ref, wa_ref, wb_ref, wif_ref, o_ref, xn_ref, *, tm, n_head, n_main, nh):
    j = pl.program_id(1)

    @pl.when(j == 0)
    def _():
        _norm_to_scratch(x_ref, g_ref, xn_ref, tm)

    @pl.when(j < n_head)
    def _():
        o_ref[...] = _dot_nt(xn_ref[...], wa_ref[0].astype(BF16))

    @pl.when(jnp.logical_and(j >= n_head, j < n_main))
    def _():
        o_ref[...] = _dot_nt(xn_ref[...], wb_ref[0].astype(BF16))

    @pl.when(j == n_main)
    def _():
        pre = _dot_nt(xn_ref[...], wif_ref[0].astype(BF16))
        o_ref[...] = jnp.zeros_like(o_ref)
        o_ref[:, 0:nh] = pre[:, 0:nh]
        o_ref[:, V7X_LANES:V7X_LANES + nh] = pre[:, nh:2 * nh]


def in_proj(x, g, w_in, layer, d_a, d_b, d, nh, *, tm, tn):
    t, k = x.shape
    assert (2 * nh) % V7X_SUBLANES == 0 and (2 * d_a) % tn == 0 and (d_b + 2 * d) % tn == 0
    wt = jnp.swapaxes(w_in, 1, 2)
    n_head = 2 * d_a // tn
    n_main = n_head + (d_b + 2 * d) // tn
    if_row = 2 * d_a

    def rows(nrows, start):
        return pl.BlockSpec((pl.Element(1), pl.Element(nrows), pl.Element(k)), lambda i, j: (layer, start(j), 0))

    return pl.pallas_call(
        functools.partial(_k_in_proj, tm=tm, n_head=n_head, n_main=n_main, nh=nh),
        grid=(t // tm, n_main + 1),
        in_specs=[pl.BlockSpec((tm, k), lambda i, j: (i, 0), pipeline_mode=_row_tile_buffering(tm, k, x.dtype)),
                  pl.BlockSpec((1, k), lambda i, j: (0, 0)),
                  rows(tn, lambda j: jnp.minimum(j, n_head - 1) * tn),
                  rows(tn, lambda j: (jnp.clip(j, n_head, n_main - 1) * (tn // V7X_SUBLANES)
                                      + 2 * nh // V7X_SUBLANES) * V7X_SUBLANES),
                  rows(2 * nh, lambda j: if_row)],
        out_specs=pl.BlockSpec((tm, tn), lambda i, j: (i, j)),
        out_shape=jax.ShapeDtypeStruct((t, (n_main + 1) * tn), F32),
        scratch_shapes=[pltpu.VMEM((tm, k), BF16)],
        compiler_params=_cparams(2), name="in_proj",
    )(x, g.reshape(1, k), wt, wt, wt)


def _k_mm_res(x_ref, w_ref, r_ref, o_ref):
    o_ref[...] = r_ref[...] + _dot(x_ref[...], w_ref[...].astype(BF16))


def _k_mm(x_ref, w_ref, o_ref):
    o_ref[...] = _dot(x_ref[...], w_ref[...].astype(BF16)).astype(o_ref.dtype)


def matmul(x, w, layer, res=None, *, tm, tn, out_dtype=F32, name="mm"):
    t, k = x.shape
    n = w.shape[2]
    in_specs = [pl.BlockSpec((tm, k), lambda i, j: (i, 0), pipeline_mode=_row_tile_buffering(tm, k, x.dtype)),
                _wspec(w, layer, tn)]
    args = [x, w]
    body = _k_mm
    if res is not None:
        in_specs.append(pl.BlockSpec((tm, tn), lambda i, j: (i, j)))
        args.append(res)
        body = _k_mm_res
    return pl.pallas_call(
        body, grid=(t // tm, pl.cdiv(n, tn)), in_specs=in_specs,
        out_specs=pl.BlockSpec((tm, tn), lambda i, j: (i, j)),
        out_shape=jax.ShapeDtypeStruct((t, n), out_dtype),
        compiler_params=_cparams(2), name=name,
    )(*args)


def _k_branch_mix(a_ref, gy_ref, wp_ref, wv_ref, wg_ref, ga_ref, gb_ref, o_ref):
    a = a_ref[...]
    gy = gy_ref[...]
    a_out = _dot(a, wp_ref[...].astype(BF16))
    val = _dot(gy, wv_ref[...].astype(BF16))
    gate = _dot(gy, wg_ref[...].astype(BF16))
    b_out = val * _sigmoid(gate)
    o_ref[...] = (_sigmoid(ga_ref[...]) * a_out + _sigmoid(gb_ref[...]) * b_out).astype(o_ref.dtype)


def branch_mix(a_pre, gy, w_proj_a, w_glu_b, layer, proj, col_ga, col_gb, *, tm, tn):
    t, k = a_pre.shape
    n = w_proj_a.shape[2]
    nj = n // tn
    rows = pl.BlockSpec((tm, k), lambda i, j: (i, 0), pipeline_mode=pl.Buffered(1))
    return pl.pallas_call(
        _k_branch_mix, grid=(t // tm, nj),
        in_specs=[rows, rows,
                  _wspec(w_proj_a, layer, tn), _wspec(w_glu_b, layer, tn),
                  _wspec(w_glu_b, layer, tn, lambda j: j + nj),
                  pl.BlockSpec((tm, tn), lambda i, j: (i, col_ga + j)),
                  pl.BlockSpec((tm, tn), lambda i, j: (i, col_gb + j))],
        out_specs=pl.BlockSpec((tm, tn), lambda i, j: (i, j)),
        out_shape=jax.ShapeDtypeStruct((t, n), BF16),
        compiler_params=_cparams(2), name="branch_mix",
    )(a_pre, gy, w_proj_a, w_glu_b, w_glu_b, proj, proj)


def _k_swiglu_up(x_ref, g_ref, wg_ref, wu_ref, o_ref, xn_ref, *, tm):
    @pl.when(pl.program_id(1) == 0)
    def _():
        _norm_to_scratch(x_ref, g_ref, xn_ref, tm)

    xn = xn_ref[...]
    gate = _dot(xn, wg_ref[...].astype(BF16))
    up = _dot(xn, wu_ref[...].astype(BF16))
    o_ref[...] = (_silu(gate) * up).astype(o_ref.dtype)


def norm_swiglu_up(x, g, w_gate, w_up, layer, *, tm, tn, name="ffn_up"):
    t, k = x.shape
    n = w_gate.shape[2]
    return pl.pallas_call(
        functools.partial(_k_swiglu_up, tm=tm),
        grid=(t // tm, pl.cdiv(n, tn)),
        in_specs=[pl.BlockSpec((tm, k), lambda i, j: (i, 0), pipeline_mode=_row_tile_buffering(tm, k, x.dtype)),
                  pl.BlockSpec((1, k), lambda i, j: (0, 0)),
                  _wspec(w_gate, layer, tn), _wspec(w_up, layer, tn)],
        out_specs=pl.BlockSpec((tm, tn), lambda i, j: (i, j)),
        out_shape=jax.ShapeDtypeStruct((t, n), BF16),
        scratch_shapes=[pltpu.VMEM((tm, k), BF16)],
        compiler_params=_cparams(2), name=name,
    )(x, g.reshape(1, k), w_gate, w_up)


def _cast_rows_to(dst_ref, src_ref, rc):
    def body(r, c):
        rows = pl.ds(pl.multiple_of(r * rc, rc), rc)
        dst_ref[rows, :] = src_ref[rows, :].astype(dst_ref.dtype)
        return c

    lax.fori_loop(0, src_ref.shape[0] // rc, body, 0)


def _k_ple(x_ref, g_ref, p_ref, wpg_ref, wple_ref, o_ref, wpg_bf, wple_bf, *, tn):
    @pl.when(pl.program_id(0) == 0)
    def _():
        _cast_rows_to(wpg_bf, wpg_ref, V7X_LANES)
        _cast_rows_to(wple_bf, wple_ref, V7X_LANES)

    x = x_ref[...]
    xn = _rmsnorm_rows(x, g_ref[...]).astype(BF16)
    pb = p_ref[...].astype(BF16)
    for c in range(x.shape[1] // tn):
        cols = slice(c * tn, (c + 1) * tn)
        gate = _dot(xn, wpg_bf[:, cols])
        emb = _dot(pb, wple_bf[:, cols])
        o_ref[:, cols] = x[:, cols] + emb * _sigmoid(gate)


def ple_matmul(x, g, p, w_pg, w_ple, layer, *, tm, tn, name="ple"):
    t, k = x.shape
    kp = p.shape[2]
    resident = lambda w: pl.BlockSpec((None,) + w.shape[1:], lambda i: (layer, 0, 0), pipeline_mode=pl.Buffered(1))
    return pl.pallas_call(
        functools.partial(_k_ple, tn=tn),
        grid=(t // tm,),
        in_specs=[pl.BlockSpec((tm, k), lambda i: (i, 0)),
                  pl.BlockSpec((1, k), lambda i: (0, 0)),
                  pl.BlockSpec((None, tm, kp), lambda i: (layer, i, 0)),
                  resident(w_pg), resident(w_ple)],
        out_specs=pl.BlockSpec((tm, k), lambda i: (i, 0)),
        out_shape=jax.ShapeDtypeStruct((t, k), F32),
        scratch_shapes=[pltpu.VMEM(w_pg.shape[1:], BF16), pltpu.VMEM(w_ple.shape[1:], BF16)],
        compiler_params=_cparams(1), name=name,
    )(x, g.reshape(1, k), p, w_pg, w_ple)


def _k_final_norm(xp_ref, xs_ref, g_ref, op_ref, os_ref, *, n_prompt_tiles):
    i = pl.program_id(0)

    @pl.when(i < n_prompt_tiles)
    def _():
        op_ref[...] = _rmsnorm_rows(xp_ref[...], g_ref[...])

    @pl.when(i == n_prompt_tiles)
    def _():
        os_ref[...] = _rmsnorm_rows(xs_ref[...], g_ref[...])


def final_norm_split(x, g, n_prompt, *, tm):
    t, k = x.shape
    bs = t - n_prompt
    npt = n_prompt // tm
    last = npt - 1
    return pl.pallas_call(
        functools.partial(_k_final_norm, n_prompt_tiles=npt), grid=(npt + 1,),
        in_specs=[pl.BlockSpec((tm, k), lambda i: (jnp.minimum(i, last), 0)),
                  pl.BlockSpec((bs, k), lambda i: (n_prompt // bs, 0)),
                  pl.BlockSpec((1, k), lambda i: (0, 0))],
        out_specs=[pl.BlockSpec((tm, k), lambda i: (jnp.minimum(i, last), 0)),
                   pl.BlockSpec((bs, k), lambda i: (0, 0))],
        out_shape=[jax.ShapeDtypeStruct((n_prompt, k), F32), jax.ShapeDtypeStruct((bs, k), F32)],
        compiler_params=_cparams(1), name="final_norm",
    )(x, x, g.reshape(1, k))


def _split_hi_lo(x):
    hi = x.astype(BF16)
    lo = (x - hi.astype(F32)).astype(BF16)
    return hi, lo


def _k_mlstm_prompt(u_ref, o_ref_in, if_ref, cw_ref, cb_ref, wq_ref, wk_ref, wv_ref, bif_ref, gh_ref, sk_ref,
                    tri_ref, a_init, a_ref, c_out, n_out, m_out, conv_out, upad_ref, *, L, dh):
    del a_init
    c_idx = pl.program_id(1)
    nh = N_HEADS

    @pl.when(c_idx == 0)
    def _():
        c_out[...] = jnp.zeros_like(c_out)
        n_out[...] = jnp.zeros_like(n_out)
        m_out[...] = jnp.zeros_like(m_out)
        upad_ref[pl.ds(0, 8), :] = jnp.zeros((8, nh * dh), F32)

    @pl.when(c_idx > 0)
    def _():
        upad_ref[pl.ds(0, 8), :] = upad_ref[pl.ds(L, 8), :]

    u = u_ref[...]
    upad_ref[pl.ds(8, L), :] = u
    conv = cb_ref[...] + u * cw_ref[CONV_W - 1:CONV_W, :]
    for j in range(CONV_W - 1):
        conv = conv + upad_ref[pl.ds(8 - (CONV_W - 1) + j, L), :] * cw_ref[j:j + 1, :]
    cact = _silu(conv)
    conv_out[0] = upad_ref[pl.ds(L, 8), :]

    pre = if_ref[...]
    li = pre[:, :V7X_LANES] + bif_ref[:, :V7X_LANES]
    lf = _log_sigmoid(pre[:, V7X_LANES:] + bif_ref[:, V7X_LANES:])
    tri = tri_ref[...]
    lf_hi, lf_mid = _split_hi_lo(lf)
    lf_lo = (lf - lf_hi.astype(F32) - lf_mid.astype(F32)).astype(BF16)
    bcum = _dot(tri, lf_hi) + _dot(tri, lf_mid) + _dot(tri, lf_lo)
    li_t = li.T
    b_t = bcum.T
    row_id = lax.broadcasted_iota(jnp.int32, (L, L), 0)
    col_id = lax.broadcasted_iota(jnp.int32, (L, L), 1)
    causal = col_id <= row_id
    lane = lax.broadcasted_iota(jnp.int32, (1, V7X_LANES), 1)
    m_row = m_out[0]
    m_new_row = m_row

    for h in range(nh):
        hs = slice(h * dh, (h + 1) * dh)
        ch = cact[:, hs].astype(BF16)
        uh = u[:, hs].astype(BF16)
        q = _dot(ch, wq_ref[h].astype(BF16))
        k = _dot(ch, wk_ref[h].astype(BF16)) * (dh ** -0.5)
        v = _dot(uh, wv_ref[h].astype(BF16))
        qb, kb, vb = q.astype(BF16), k.astype(BF16), v.astype(BF16)

        b_col = bcum[:, h:h + 1]
        li_col = li[:, h:h + 1]
        r_row = li_t[h:h + 1, :] - b_t[h:h + 1, :]
        m_prev = m_row[:, h:h + 1]
        d = jnp.where(causal, b_col + r_row, -jnp.inf)
        inter = b_col + m_prev
        m_t = jnp.maximum(inter, jnp.max(d, axis=-1, keepdims=True))
        w_inter = jnp.exp(inter - m_t)
        s = _dot_nt(qb, kb) * jnp.exp(d - m_t)
        c_prev = c_out[0, h]
        n_prev = n_out[0, h:h + 1, :]
        num = w_inter * _dot(qb, c_prev.astype(BF16)) + _dot(s.astype(BF16), vb)
        den = w_inter * jnp.sum(q * n_prev, axis=-1, keepdims=True) + jnp.sum(s, axis=-1, keepdims=True)
        hh = num / jnp.maximum(jnp.abs(den), jnp.exp(-m_t))

        b_last = b_col[L - 1:L, :]
        g_col = b_last - b_col + li_col
        m_new = jnp.maximum(b_last + m_prev, jnp.max(g_col, axis=0, keepdims=True))
        decay = jnp.exp(b_last + m_prev - m_new)
        wk_ = jnp.exp(g_col - m_new) * k
        c_out[0, h] = decay * c_prev + _dot_tn(wk_.astype(BF16), vb)
        n_out[0, h:h + 1, :] = decay * n_prev + jnp.sum(wk_, axis=0, keepdims=True)
        m_new_row = jnp.where(lane == h, m_new, m_new_row)

        hn = _rmsnorm_rows(hh, gh_ref[:, hs])
        gated = (hn + sk_ref[:, hs] * cact[:, hs]) * _sigmoid(o_ref_in[:, hs])
        a_ref[:, hs] = gated.astype(a_ref.dtype)

    m_out[0] = m_new_row


def mlstm_prompt(proj, col_u, col_o, col_if, n_rows, bsz, seq, lw, rows_init=None, *, L):
    dh = lw["w_q"].shape[-1]
    da = N_HEADS * dh
    nc = seq // L
    tri = jnp.asarray(np.tril(np.ones((L, L), np.float32)), BF16)
    row_blk = lambda b, c: b * nc + c
    full = lambda *shape: pl.BlockSpec(shape, lambda b, c: (0,) * len(shape))
    outs = pl.pallas_call(
        functools.partial(_k_mlstm_prompt, L=L, dh=dh),
        grid=(bsz, nc),
        in_specs=[pl.BlockSpec((L, da), lambda b, c: (row_blk(b, c), col_u)),
                  pl.BlockSpec((L, da), lambda b, c: (row_blk(b, c), col_o)),
                  pl.BlockSpec((L, 2 * V7X_LANES), lambda b, c: (row_blk(b, c), col_if)),
                  full(CONV_W, da), full(1, da), full(N_HEADS, dh, dh), full(N_HEADS, dh, dh),
                  full(N_HEADS, dh, dh), full(1, 2 * V7X_LANES), full(1, da), full(1, da), full(L, L),
                  pl.BlockSpec(memory_space=pl.ANY)],
        out_specs=[pl.BlockSpec((L, da), lambda b, c: (row_blk(b, c), 0)),
                   pl.BlockSpec((1, N_HEADS, dh, dh), lambda b, c: (b, 0, 0, 0)),
                   pl.BlockSpec((1, N_HEADS, dh), lambda b, c: (b, 0, 0)),
                   pl.BlockSpec((1, 1, V7X_LANES), lambda b, c: (b, 0, 0)),
                   pl.BlockSpec((1, 8, da), lambda b, c: (b, 0, 0))],
        out_shape=[jax.ShapeDtypeStruct((n_rows, da), BF16),
                   jax.ShapeDtypeStruct((bsz, N_HEADS, dh, dh), F32),
                   jax.ShapeDtypeStruct((bsz, N_HEADS, dh), F32),
                   jax.ShapeDtypeStruct((bsz, 1, V7X_LANES), F32),
                   jax.ShapeDtypeStruct((bsz, 8, da), F32)],
        scratch_shapes=[pltpu.VMEM((L + 8, da), F32)],
        input_output_aliases={12: 0},
        compiler_params=_cparams(2), name="mlstm_prompt",
    )(proj, proj, proj, lw["conv_w"], lw["conv_b"], lw["w_q"], lw["w_k"], lw["w_v"], lw["b_if"],
      lw["g_head"], lw["skip"], tri, jnp.zeros((n_rows, da), BF16) if rows_init is None else rows_init)
    a_pre, c_p, n_p, m_p, conv_p = outs
    return a_pre, c_p, n_p, m_p[:, 0, :N_HEADS], conv_p[:, 8 - (CONV_W - 1):, :]


def _k_mlstm_sample_pre(u_ref, if_ref, conv_ref, m_ref, cw_ref, cb_ref, wq_ref, wk_ref, wv_ref, bif_ref,
                        q_out, k_out, v_out, c_out, gates_out, conv_out, *, dh):
    nh = N_HEADS
    u = u_ref[...]
    conv = cb_ref[...] + u * cw_ref[CONV_W - 1:CONV_W, :]
    for j in range(CONV_W - 1):
        conv = conv + conv_ref[j] * cw_ref[j:j + 1, :]
        if j > 0:
            conv_out[j - 1] = conv_ref[j]
    conv_out[CONV_W - 2] = u
    cact = _silu(conv)
    c_out[...] = cact
    for h in range(nh):
        hs = slice(h * dh, (h + 1) * dh)
        ch = cact[:, hs].astype(BF16)
        q_out[:, hs] = _dot(ch, wq_ref[h].astype(BF16))
        k_out[:, hs] = _dot(ch, wk_ref[h].astype(BF16)) * (dh ** -0.5)
        v_out[:, hs] = _dot(u[:, hs].astype(BF16), wv_ref[h].astype(BF16))
    pre = if_ref[...]
    li = pre[:, :V7X_LANES] + bif_ref[:, :V7X_LANES]
    lf = _log_sigmoid(pre[:, V7X_LANES:] + bif_ref[:, V7X_LANES:])
    m_prev = m_ref[...]
    inter = lf + m_prev
    m_t = jnp.maximum(inter, li)
    gates_out[0] = jnp.exp(inter - m_t)
    gates_out[1] = jnp.exp(li - m_t)
    gates_out[2] = jnp.exp(-m_t)
    gates_out[3] = m_t


def _k_mlstm_sample_step(q_ref, k_ref, v_ref, gates_ref, c_ref, n_ref, cact_ref, o_ref_in, gh_ref, sk_ref,
                         a_any, c_any, c_out, n_out, a_out, hh_ref, *, bt, dh):
    del a_any, c_any
    i = pl.program_id(0)
    nh = N_HEADS
    q = q_ref[...]
    k = k_ref[...]
    v = v_ref[...]
    w_inter = gates_ref[0]
    w_new = gates_ref[1]
    e_neg_m = gates_ref[2]
    n_prev = n_ref[...]
    rows = pl.ds(pl.multiple_of(i * bt, bt), bt)
    for h in range(nh):
        hs = slice(h * dh, (h + 1) * dh)
        qh, kh, vh, nh_prev = q[:, hs], k[:, hs], v[:, hs], n_prev[:, hs]
        q_t = qh.T
        k_t = kh.T
        wi = w_inter[:, h:h + 1]
        wn = w_new[:, h:h + 1]
        s = jnp.sum(qh * kh, axis=-1, keepdims=True) * wn
        den = wi * jnp.sum(qh * nh_prev, axis=-1, keepdims=True) + s
        wv = wn * vh
        qc_rows = []
        for bl in range(bt):
            c_prev = c_ref[bl, h]
            qc_rows.append(jnp.sum(q_t[:, bl:bl + 1] * c_prev, axis=0, keepdims=True))
            c_out[bl, h] = wi[bl:bl + 1, :] * c_prev + k_t[:, bl:bl + 1] * wv[bl:bl + 1, :]
        qc = jnp.concatenate(qc_rows, axis=0)
        num = wi * qc + s * vh
        hh = num / jnp.maximum(jnp.abs(den), e_neg_m[:, h:h + 1])
        hh_ref[rows, hs] = hh
        n_out[:, hs] = wi * nh_prev + wn * kh

    @pl.when(i == pl.num_programs(0) - 1)
    def _():
        for h in range(nh):
            hs = slice(h * dh, (h + 1) * dh)
            hn = _rmsnorm_rows(hh_ref[:, hs], gh_ref[:, hs])
            a_out[:, hs] = ((hn + sk_ref[:, hs] * cact_ref[:, hs]) * _sigmoid(o_ref_in[:, hs])).astype(a_out.dtype)


def mlstm_sample(proj, col_u, col_o, col_if, row0, a_pre_all, c_all, layer, c_new_all, n0, m0, conv0, lw, *,
                 bt=8):
    _, bs, nh, dh, _ = c_all.shape
    da = nh * dh
    rb = row0 // bs
    m_pad = jnp.pad(m0, ((0, 0), (0, V7X_LANES - nh)))
    conv_t = jnp.transpose(conv0, (1, 0, 2))
    full = lambda *shape: pl.BlockSpec(shape, lambda i: (0,) * len(shape))
    q, k, v, cact, gates, conv_new = pl.pallas_call(
        functools.partial(_k_mlstm_sample_pre, dh=dh),
        grid=(1,),
        in_specs=[pl.BlockSpec((bs, da), lambda i: (rb, col_u)),
                  pl.BlockSpec((bs, 2 * V7X_LANES), lambda i: (rb, col_if)),
                  full(CONV_W - 1, bs, da), full(bs, V7X_LANES), full(CONV_W, da), full(1, da),
                  full(nh, dh, dh), full(nh, dh, dh), full(nh, dh, dh), full(1, 2 * V7X_LANES)],
        out_specs=[full(bs, da), full(bs, da), full(bs, da), full(bs, da), full(4, bs, V7X_LANES),
                   full(CONV_W - 1, bs, da)],
        out_shape=[jax.ShapeDtypeStruct((bs, da), F32)] * 4
                  + [jax.ShapeDtypeStruct((4, bs, V7X_LANES), F32),
                     jax.ShapeDtypeStruct((CONV_W - 1, bs, da), F32)],
        compiler_params=_cparams(1), name="mlstm_sample_pre",
    )(proj, proj, conv_t, m_pad, lw["conv_w"], lw["conv_b"], lw["w_q"], lw["w_k"], lw["w_v"], lw["b_if"])

    blk = lambda *shape: pl.BlockSpec(shape, lambda i: (i,) + (0,) * (len(shape) - 1))
    cst = lambda *shape: pl.BlockSpec(shape, lambda i: (0,) * len(shape))
    c_blk = pl.BlockSpec((None, bt, nh, dh, dh), lambda i: (layer, i, 0, 0, 0))
    c_new_all, n_new, a_pre_all = pl.pallas_call(
        functools.partial(_k_mlstm_sample_step, bt=bt, dh=dh),
        grid=(bs // bt,),
        in_specs=[blk(bt, da), blk(bt, da), blk(bt, da),
                  pl.BlockSpec((4, bt, V7X_LANES), lambda i: (0, i, 0)),
                  c_blk, blk(bt, da), cst(bs, da),
                  pl.BlockSpec((bs, da), lambda i: (rb, col_o)), cst(1, da), cst(1, da),
                  pl.BlockSpec(memory_space=pl.ANY), pl.BlockSpec(memory_space=pl.ANY)],
        out_specs=[c_blk, blk(bt, da), pl.BlockSpec((bs, da), lambda i: (rb, 0))],
        out_shape=[jax.ShapeDtypeStruct(c_new_all.shape, F32), jax.ShapeDtypeStruct((bs, da), F32),
                   jax.ShapeDtypeStruct(a_pre_all.shape, a_pre_all.dtype)],
        scratch_shapes=[pltpu.VMEM((bs, da), F32)],
        input_output_aliases={10: 2, 11: 0},
        compiler_params=_cparams(1), name="mlstm_sample_step",
    )(q, k, v, gates, c_all, n0.reshape(bs, da), cact, proj, lw["g_head"], lw["skip"], a_pre_all, c_new_all)
    m_new = gates[3][:, :nh]
    return a_pre_all, c_new_all, n_new.reshape(bs, nh, dh), m_new, jnp.transpose(conv_new, (1, 0, 2))


S5_SLAB_GROUPS = V7X_LANES // S5_GROUP
S5_SLAB_STATES = S5_SLAB_GROUPS * S5_STATE


def _s5_params(lp):
    g, p = lp["a_re"].shape
    dt = jnp.exp(lp["log_dt"].astype(F32))[:, None]
    a_re = lp["a_re"].astype(F32)
    a_im = lp["a_im"].astype(F32)
    lam_re = a_re * dt
    lam_im = a_im * dt
    mag = jnp.exp(lam_re)
    ab_re = mag * jnp.cos(lam_im)
    ab_im = mag * jnp.sin(lam_im)
    den = a_re * a_re + a_im * a_im
    nr = ab_re - 1.0
    ni = ab_im
    k_re = (nr * a_re + ni * a_im) / den
    k_im = (ni * a_re - nr * a_im) / den
    b_re = lp["b_re"].astype(F32)
    b_im = lp["b_im"].astype(F32)
    bb_re = k_re[..., None] * b_re - k_im[..., None] * b_im
    bb_im = k_re[..., None] * b_im + k_im[..., None] * b_re
    ns = g // S5_SLAB_GROUPS
    eye = jnp.eye(S5_SLAB_GROUPS, dtype=F32)

    def in_blockdiag(bb):
        bs = bb.reshape(ns, S5_SLAB_GROUPS, p, S5_GROUP)
        w = jnp.einsum("ab,sapc->sacbp", eye, bs)
        return w.reshape(ns, V7X_LANES, S5_SLAB_STATES).astype(BF16)

    def out_blockdiag(cc):
        cs = cc.astype(F32).reshape(ns, S5_SLAB_GROUPS, S5_GROUP, p)
        w = jnp.einsum("ab,sacp->sapbc", eye, cs)
        return w.reshape(ns, S5_SLAB_STATES, V7X_LANES)

    wc = jnp.concatenate([out_blockdiag(lp["c_re"]), -out_blockdiag(lp["c_im"])], axis=1).astype(BF16)
    return dict(lam_re=lam_re.reshape(1, g * p), lam_im=lam_im.reshape(1, g * p),
                ab_re=ab_re.reshape(1, g * p), ab_im=ab_im.reshape(1, g * p),
                wb_re=in_blockdiag(bb_re), wb_im=in_blockdiag(bb_im), wc=wc,
                d_skip=lp["d_skip"].astype(F32).reshape(1, g * S5_GROUP))


def _s5_powers(sp, ks):
    kk = jnp.asarray(ks, F32)[:, None]
    mag = jnp.exp(kk * sp["lam_re"])
    return mag * jnp.cos(kk * sp["lam_im"]), mag * jnp.sin(kk * sp["lam_im"])


def _s5_input_proj(u_bf, wbr_ref, wbi_ref, bur_ref, bui_ref):
    ns = wbr_ref.shape[0]
    for s in range(ns):
        us = u_bf[:, s * V7X_LANES:(s + 1) * V7X_LANES]
        cols = slice(s * S5_SLAB_STATES, (s + 1) * S5_SLAB_STATES)
        bur_ref[:, cols] = _dot(us, wbr_ref[s])
        bui_ref[:, cols] = _dot(us, wbi_ref[s])


def _s5_output_proj(xr_ref, xi_ref, wc_ref, skip):
    ns = wc_ref.shape[0]
    outs = []
    for s in range(ns):
        cols = slice(s * S5_SLAB_STATES, (s + 1) * S5_SLAB_STATES)
        xcat = jnp.concatenate([xr_ref[:, cols].astype(BF16), xi_ref[:, cols].astype(BF16)], axis=1)
        y = _dot(xcat, wc_ref[s]) + skip[:, s * V7X_LANES:(s + 1) * V7X_LANES]
        outs.append(_gelu_tanh(y).astype(BF16))
    return jnp.concatenate(outs, axis=1)


def _k_s5_prompt(u_ref, wbr_ref, wbi_ref, wc_ref, d_ref, abr_ref, abi_ref, tpr_ref, tpi_ref, ajr_ref, aji_ref,
                 perm_ref, permt_ref, gy_init, gy_ref, sre_out, sim_out,
                 bur0, bur1, bui0, bui1, xb0, xb1, *, tc):
    del gy_init
    c_idx = pl.program_id(1)
    nsteps = tc // V7X_SUBLANES
    ns = wbr_ref.shape[0]
    lw = S5_SLAB_STATES
    pair = 2 * V7X_SUBLANES
    bur, bui, xb = (bur0, bur1), (bui0, bui1), (xb0, xb1)

    @pl.when(c_idx == 0)
    def _():
        sre_out[...] = jnp.zeros_like(sre_out)
        sim_out[...] = jnp.zeros_like(sim_out)

    u = u_ref[...]
    u_hi, u_lo = _split_hi_lo(u)
    perm = perm_ref[...]
    up_hi = _dot(perm, u_hi)
    skip = d_ref[...] * (up_hi + _dot(perm, u_lo))
    up_bf = up_hi.astype(BF16)

    sub = lax.broadcasted_iota(jnp.int32, (V7X_SUBLANES, lw), 0)

    def input_proj(s):
        us = up_bf[:, s * V7X_LANES:(s + 1) * V7X_LANES]
        bur[s % 2][...] = _dot(us, wbr_ref[s])
        bui[s % 2][...] = _dot(us, wbi_ref[s])

    def scan(s):
        br, bi, xo = bur[s % 2], bui[s % 2], xb[s % 2]
        lanes = slice(s * lw, (s + 1) * lw)
        ar = jnp.broadcast_to(abr_ref[:, lanes], (V7X_SUBLANES, lw))
        ai = jnp.broadcast_to(abi_ref[:, lanes], (V7X_SUBLANES, lw))
        er = jnp.zeros((V7X_SUBLANES, lw), F32)
        ei = er
        for i in range(nsteps):
            rows = slice(i * V7X_SUBLANES, (i + 1) * V7X_SUBLANES)
            er, ei = (ar * er - ai * ei + br[rows, :], ar * ei + ai * er + bi[rows, :])
            br[rows, :] = er
            bi[rows, :] = ei
        for d, row in ((1, 0), (2, 1), (4, 3)):
            pr = ajr_ref[row:row + 1, lanes]
            pi = aji_ref[row:row + 1, lanes]
            sr = pltpu.roll(er, d, 0)
            si = pltpu.roll(ei, d, 0)
            keep = sub >= d
            er, ei = (er + jnp.where(keep, pr * sr - pi * si, 0.0),
                      ei + jnp.where(keep, pr * si + pi * sr, 0.0))
        c0r = jnp.broadcast_to(sre_out[0, :, lanes], (V7X_SUBLANES, lw))
        c0i = jnp.broadcast_to(sim_out[0, :, lanes], (V7X_SUBLANES, lw))
        ajr = ajr_ref[:, lanes]
        aji = aji_ref[:, lanes]
        fr = ajr * c0r - aji * c0i + er
        fi = ajr * c0i + aji * c0r + ei
        cin_r = jnp.where(sub >= 1, pltpu.roll(fr, 1, 0), c0r)
        cin_i = jnp.where(sub >= 1, pltpu.roll(fi, 1, 0), c0i)
        sre_out[0, :, lanes] = fr[V7X_SUBLANES - 1:V7X_SUBLANES, :]
        sim_out[0, :, lanes] = fi[V7X_SUBLANES - 1:V7X_SUBLANES, :]

        cin2_r = jnp.concatenate([cin_r, cin_r], axis=0)
        cin2_i = jnp.concatenate([cin_i, cin_i], axis=0)
        for k in range(tc // pair):
            rows = slice(k * pair, (k + 1) * pair)
            pr = tpr_ref[rows, lanes]
            pi = tpi_ref[rows, lanes]
            xo[rows, :lw] = (br[rows, :] + (pr * cin2_r - pi * cin2_i)).astype(BF16)
            xo[rows, lw:] = (bi[rows, :] + (pr * cin2_i + pi * cin2_r)).astype(BF16)

    def output_proj(s):
        y = _dot(xb[s % 2][...], wc_ref[s]) + skip[:, s * V7X_LANES:(s + 1) * V7X_LANES]
        return _gelu_tanh(y).astype(BF16)

    outs = []
    input_proj(0)
    for s in range(ns):
        if s + 1 < ns:
            input_proj(s + 1)
        scan(s)
        outs.append(output_proj(s))
    g_perm = jnp.concatenate(outs, axis=1)
    gy_ref[...] = _dot(permt_ref[...], g_perm).astype(gy_ref.dtype)


def s5_prompt(proj, col_u, n_rows, bsz, seq, sp, rows_init=None, *, tc=256):
    db = sp["d_skip"].shape[1]
    nch = sp["ab_re"].shape[1]
    nc = seq // tc
    nsteps = tc // V7X_SUBLANES
    tpr, tpi = (jnp.repeat(a, V7X_SUBLANES, axis=0) for a in _s5_powers(sp, np.arange(1, nsteps + 1)))
    ajr, aji = _s5_powers(sp, nsteps * np.arange(1, V7X_SUBLANES + 1))
    perm = np.zeros((tc, tc), np.float32)
    r = np.arange(tc)
    perm[r, (r % V7X_SUBLANES) * nsteps + r // V7X_SUBLANES] = 1.0
    full = lambda a: pl.BlockSpec(a.shape, lambda b, c: (0,) * a.ndim)
    consts = [sp["wb_re"], sp["wb_im"], sp["wc"], sp["d_skip"], sp["ab_re"], sp["ab_im"], tpr, tpi, ajr, aji,
              jnp.asarray(perm, BF16), jnp.asarray(perm.T, BF16)]
    gy, s_re, s_im = pl.pallas_call(
        functools.partial(_k_s5_prompt, tc=tc),
        grid=(bsz, nc),
        in_specs=[pl.BlockSpec((tc, db), lambda b, c: (b * nc + c, col_u))] + [full(a) for a in consts]
                 + [pl.BlockSpec(memory_space=pl.ANY)],
        out_specs=[pl.BlockSpec((tc, db), lambda b, c: (b * nc + c, 0)),
                   pl.BlockSpec((1, 1, nch), lambda b, c: (b, 0, 0)),
                   pl.BlockSpec((1, 1, nch), lambda b, c: (b, 0, 0))],
        out_shape=[jax.ShapeDtypeStruct((n_rows, db), BF16),
                   jax.ShapeDtypeStruct((bsz, 1, nch), F32), jax.ShapeDtypeStruct((bsz, 1, nch), F32)],
        scratch_shapes=[pltpu.VMEM((tc, S5_SLAB_STATES), F32)] * 4 + [pltpu.VMEM((tc, 2 * S5_SLAB_STATES), BF16)] * 2,
        input_output_aliases={1 + len(consts): 0},
        compiler_params=_cparams(2), name="s5_prompt",
    )(proj, *consts, jnp.zeros((n_rows, db), BF16) if rows_init is None else rows_init)
    return gy, s_re, s_im


def _k_s5_sample(u_ref, x0r_ref, x0i_ref, wbr_ref, wbi_ref, wc_ref, d_ref, abr_ref, abi_ref, gy_any,
                 gy_ref, xr_out, xi_out):
    del gy_any
    u = u_ref[...]
    _s5_input_proj(u.astype(BF16), wbr_ref, wbi_ref, xr_out, xi_out)
    ar = abr_ref[...]
    ai = abi_ref[...]
    x0r = x0r_ref[...]
    x0i = x0i_ref[...]
    xr_out[...] = xr_out[...] + (ar * x0r - ai * x0i)
    xi_out[...] = xi_out[...] + (ar * x0i + ai * x0r)
    gy_ref[...] = _s5_output_proj(xr_out, xi_out, wc_ref, d_ref[...] * u).astype(gy_ref.dtype)


def s5_sample(proj, col_u, row0, gy_all, x0_re, x0_im, sp):
    bs = x0_re.shape[0]
    db = sp["d_skip"].shape[1]
    nch = sp["ab_re"].shape[1]
    rb = row0 // bs
    full = lambda a: pl.BlockSpec(a.shape, lambda i: (0,) * a.ndim)
    consts = [sp["wb_re"], sp["wb_im"], sp["wc"], sp["d_skip"], sp["ab_re"], sp["ab_im"]]
    x0r = x0_re.reshape(bs, nch)
    x0i = x0_im.reshape(bs, nch)
    gy_all, xr, xi = pl.pallas_call(
        _k_s5_sample,
        grid=(1,),
        in_specs=[pl.BlockSpec((bs, db), lambda i: (rb, col_u)), full(x0r), full(x0i)]
                 + [full(a) for a in consts] + [pl.BlockSpec(memory_space=pl.ANY)],
        out_specs=[pl.BlockSpec((bs, db), lambda i: (rb, 0)),
                   pl.BlockSpec((bs, nch), lambda i: (0, 0)), pl.BlockSpec((bs, nch), lambda i: (0, 0))],
        out_shape=[jax.ShapeDtypeStruct(gy_all.shape, gy_all.dtype),
                   jax.ShapeDtypeStruct((bs, nch), F32), jax.ShapeDtypeStruct((bs, nch), F32)],
        input_output_aliases={9: 0},
        compiler_params=_cparams(1), name="s5_sample",
    )(proj, x0r, x0i, *consts, gy_all)
    return gy_all, xr, xi


def _k_router(x_ref, g_ref, w_ref, b_ref, hn_ref, lg_ref):
    hn = _rmsnorm_rows(x_ref[...], g_ref[...])
    hn_ref[...] = hn
    x_hi, x_lo = _split_hi_lo(hn)
    w_hi, w_lo = _split_hi_lo(w_ref[...])
    lg_ref[...] = _dot(x_hi, w_hi) + (_dot(x_lo, w_hi) + _dot(x_hi, w_lo)) + b_ref[...]


def router(x, g, w_router, b_router, *, tm):
    t, k = x.shape
    ne = w_router.shape[1]
    w_pad = jnp.pad(w_router, ((0, 0), (0, V7X_LANES - ne)))
    b_pad = jnp.pad(b_router.astype(F32), (0, V7X_LANES - ne)).reshape(1, V7X_LANES)
    hn, lg = pl.pallas_call(
        _k_router, grid=(t // tm,),
        in_specs=[pl.BlockSpec((tm, k), lambda i: (i, 0)), pl.BlockSpec((1, k), lambda i: (0, 0)),
                  pl.BlockSpec((k, V7X_LANES), lambda i: (0, 0)), pl.BlockSpec((1, V7X_LANES), lambda i: (0, 0))],
        out_specs=[pl.BlockSpec((tm, k), lambda i: (i, 0)), pl.BlockSpec((tm, V7X_LANES), lambda i: (i, 0))],
        out_shape=[jax.ShapeDtypeStruct((t, k), F32), jax.ShapeDtypeStruct((t, V7X_LANES), F32)],
        compiler_params=_cparams(1), name="router",
    )(x, g.reshape(1, k), w_pad, b_pad)
    return hn, lg[:, :ne]


DMA_ISSUE_UNROLL = 8


def _k_gather_rows(nused_ref, tok_ref, src_hbm, o_ref, buf_ref, sems, *, tr):
    t = pl.program_id(0)
    n_used = nused_ref[0]

    def issue_tile(tile):
        slot = tile % 2
        base = tile * tr

        def issue(r, c):
            tok = tok_ref[base + r]
            pltpu.make_async_copy(src_hbm.at[pl.ds(tok, 1), :], buf_ref.at[slot, pl.ds(r, 1), :],
                                  sems.at[slot]).start()
            return c

        lax.fori_loop(0, tr, issue, 0, unroll=DMA_ISSUE_UNROLL)

    @pl.when(t == 0)
    def _():
        issue_tile(t)

    @pl.when(t + 1 < n_used)
    def _():
        issue_tile(t + 1)

    @pl.when(t < n_used)
    def _():
        slot = t % 2
        pltpu.make_async_copy(src_hbm.at[pl.ds(0, tr), :], buf_ref.at[slot], sems.at[slot]).wait()
        o_ref[...] = buf_ref[slot].astype(o_ref.dtype)

    @pl.when(t >= n_used)
    def _():
        o_ref[...] = jnp.zeros_like(o_ref)


def gather_rows(src, tok, n_used, *, tr):
    t, k = src.shape
    r_pad = tok.shape[0]
    nt = r_pad // tr
    return pl.pallas_call(
        functools.partial(_k_gather_rows, tr=tr),
        grid_spec=pltpu.PrefetchScalarGridSpec(
            num_scalar_prefetch=2, grid=(nt,),
            in_specs=[pl.BlockSpec(memory_space=pl.ANY)],
            out_specs=pl.BlockSpec((tr, k), lambda i, nu, tk: (i, 0)),
            scratch_shapes=[pltpu.VMEM((2, tr, k), src.dtype), pltpu.SemaphoreType.DMA((2,))]),
        out_shape=jax.ShapeDtypeStruct((r_pad, k), BF16),
        compiler_params=_cparams(1), name="moe_gather",
    )(n_used, tok, src)


def _k_moe_up(nused_ref, te_ref, x_ref, wg_ref, wu_ref, o_ref):
    del te_ref
    t = pl.program_id(1)

    @pl.when(t < nused_ref[0])
    def _():
        x = x_ref[...]
        gate = _dot(x, wg_ref[...].astype(BF16))
        up = _dot(x, wu_ref[...].astype(BF16))
        o_ref[...] = (_silu(gate) * up).astype(o_ref.dtype)

    @pl.when(t >= nused_ref[0])
    def _():
        o_ref[...] = jnp.zeros_like(o_ref)


def moe_up(xs, w_gate, w_up, tile_expert, n_used, *, tr, tn):
    r_pad, k = xs.shape
    ne, _, f = w_gate.shape
    nt = r_pad // tr
    row = lambda j, t, nu, te: (jnp.minimum(t, nu[0] - 1), 0)
    wmap = lambda j, t, nu, te: (te[t], 0, j)
    return pl.pallas_call(
        _k_moe_up,
        grid_spec=pltpu.PrefetchScalarGridSpec(
            num_scalar_prefetch=2, grid=(pl.cdiv(f, tn), nt),
            in_specs=[pl.BlockSpec((tr, k), row),
                      pl.BlockSpec((None, k, tn), wmap), pl.BlockSpec((None, k, tn), wmap)],
            out_specs=pl.BlockSpec((tr, tn), lambda j, t, nu, te: (t, j))),
        out_shape=jax.ShapeDtypeStruct((r_pad, f), BF16),
        compiler_params=_cparams(2), name="moe_up",
    )(n_used, tile_expert, xs, w_gate, w_up)


def _k_moe_down(nused_ref, te_ref, x_ref, w_ref, o_ref):
    del te_ref
    t = pl.program_id(1)

    @pl.when(t < nused_ref[0])
    def _():
        o_ref[...] = _dot(x_ref[...], w_ref[...].astype(BF16))

    @pl.when(t >= nused_ref[0])
    def _():
        o_ref[...] = jnp.zeros_like(o_ref)


def moe_down(hid, w_down, tile_expert, n_used, *, tr, tn):
    r_pad, f = hid.shape
    d = w_down.shape[2]
    nt = r_pad // tr
    return pl.pallas_call(
        _k_moe_down,
        grid_spec=pltpu.PrefetchScalarGridSpec(
            num_scalar_prefetch=2, grid=(d // tn, nt),
            in_specs=[pl.BlockSpec((tr, f), lambda j, t, nu, te: (jnp.minimum(t, nu[0] - 1), 0)),
                      pl.BlockSpec((None, f, tn), lambda j, t, nu, te: (te[t], 0, j))],
            out_specs=pl.BlockSpec((tr, tn), lambda j, t, nu, te: (t, j))),
        out_shape=jax.ShapeDtypeStruct((r_pad, d), F32),
        compiler_params=_cparams(2), name="moe_down",
    )(n_used, tile_expert, hid, w_down)


def _k_moe_combine(pos_ref, ys_hbm, gate_ref, res_ref, o_ref, buf0_ref, buf1_ref, sems, *, tr):
    t = pl.program_id(0)
    n_tiles = pl.num_programs(0)
    n_tok = n_tiles * tr

    def issue_tile(tile):
        slot = tile % 2
        base = tile * tr

        def issue(r, c):
            p0 = pos_ref[base + r]
            p1 = pos_ref[n_tok + base + r]
            pltpu.make_async_copy(ys_hbm.at[pl.ds(p0, 1), :], buf0_ref.at[slot, pl.ds(r, 1), :],
                                  sems.at[slot]).start()
            pltpu.make_async_copy(ys_hbm.at[pl.ds(p1, 1), :], buf1_ref.at[slot, pl.ds(r, 1), :],
                                  sems.at[slot]).start()
            return c

        lax.fori_loop(0, tr, issue, 0, unroll=DMA_ISSUE_UNROLL)

    @pl.when(t == 0)
    def _():
        issue_tile(t)

    @pl.when(t + 1 < n_tiles)
    def _():
        issue_tile(t + 1)

    slot = t % 2
    pltpu.make_async_copy(ys_hbm.at[pl.ds(0, tr), :], buf0_ref.at[slot], sems.at[slot]).wait()
    pltpu.make_async_copy(ys_hbm.at[pl.ds(0, tr), :], buf1_ref.at[slot], sems.at[slot]).wait()
    g = gate_ref[...]
    o_ref[...] = res_ref[...] + (g[:, 0:1] * buf0_ref[slot] + g[:, 1:2] * buf1_ref[slot])


def moe_combine(ys, pos, gates, res, *, tr):
    t, d = res.shape
    g_pad = jnp.pad(gates, ((0, 0), (0, V7X_LANES - gates.shape[1])))
    return pl.pallas_call(
        functools.partial(_k_moe_combine, tr=tr),
        grid_spec=pltpu.PrefetchScalarGridSpec(
            num_scalar_prefetch=1, grid=(t // tr,),
            in_specs=[pl.BlockSpec(memory_space=pl.ANY),
                      pl.BlockSpec((tr, V7X_LANES), lambda i, p: (i, 0)),
                      pl.BlockSpec((tr, d), lambda i, p: (i, 0))],
            out_specs=pl.BlockSpec((tr, d), lambda i, p: (i, 0)),
            scratch_shapes=[pltpu.VMEM((2, tr, d), F32), pltpu.VMEM((2, tr, d), F32),
                            pltpu.SemaphoreType.DMA((2,))]),
        out_shape=jax.ShapeDtypeStruct((t, d), F32),
        compiler_params=_cparams(1), name="moe_combine",
    )(pos, ys, g_pad, res)


def moe_layer(h, g_ffn, w_router, b_router, w_gate, w_up, w_down, *, tm, tr, tn_up, tn_down, tr_gather,
              tr_combine):
    t, d = h.shape
    ne = w_gate.shape[0]
    hn, logits = router(h, g_ffn, w_router, b_router, tm=tm)
    top_v, top_e = lax.top_k(logits, TOP_K)
    gates = jax.nn.softmax(top_v, axis=-1)
    flat_e = top_e.reshape(-1)
    onehot = (flat_e[:, None] == jnp.arange(ne, dtype=flat_e.dtype)[None, :]).astype(jnp.int32)
    rank = jnp.sum((jnp.cumsum(onehot, axis=0) - onehot) * onehot, axis=1)
    sizes = jnp.sum(onehot, axis=0)
    tiles_per = (sizes + tr - 1) // tr
    tile_end = jnp.cumsum(tiles_per)
    tile_start = tile_end - tiles_per
    n_used = tile_end[-1:].astype(jnp.int32)
    nt = (t * TOP_K) // tr + ne
    r_pad = nt * tr
    pos = (tile_start[flat_e] * tr + rank).astype(jnp.int32)
    src_tok = jnp.zeros((r_pad,), jnp.int32).at[pos].set(jnp.arange(t * TOP_K, dtype=jnp.int32) // TOP_K)
    tile_ids = jnp.minimum(jnp.arange(nt, dtype=jnp.int32), n_used[0] - 1)
    tile_expert = jnp.sum((tile_ids[:, None] >= tile_end[None, :]).astype(jnp.int32), axis=1).astype(jnp.int32)
    xs = gather_rows(hn, src_tok, n_used * (tr // tr_gather), tr=tr_gather)
    hid = moe_up(xs, w_gate, w_up, tile_expert, n_used, tr=tr, tn=tn_up)
    ys = moe_down(hid, w_down, tile_expert, n_used, tr=tr, tn=tn_down)
    return moe_combine(ys, pos.reshape(t, TOP_K).T.reshape(-1), gates, h, tr=tr_combine)


def _layer_weights(i, conv_w, conv_b, w_q, w_k, w_v, b_i, b_f, g_head, skip_a):
    nh = b_i.shape[1]
    pad = jnp.zeros((V7X_LANES - nh,), F32)
    b_if = jnp.concatenate([b_i[i].astype(F32), pad, b_f[i].astype(F32), pad]).reshape(1, 2 * V7X_LANES)
    da = conv_w.shape[-1]
    return dict(conv_w=conv_w[i], conv_b=conv_b[i].reshape(1, da), w_q=w_q[i], w_k=w_k[i], w_v=w_v[i],
                b_if=b_if, g_head=g_head[i].reshape(1, da), skip=skip_a[i].reshape(1, da))


def _tile_plan(n_rows, seq):
    tm = next(c for c in (832, 640, 512, 256, 128, 64, 32, 16) if n_rows % c == 0)
    tc = next(c for c in (320, 256, 128, 64, 32, 16, 8) if n_rows % c == 0)
    tm_big = 2 * tm if n_rows % (2 * tm) == 0 else tm
    return dict(tm=tm, tm_big=tm_big, tn=512, tn_down=256, mlstm_chunk=256, s5_chunk=256,
                moe_tr=512, moe_tn_up=512, moe_tn_down=512, gather_tr=512, combine_tr=tc, ple_tm=tm // 2,
                norm_tm=min(seq, 1024))


def kernel(x_prompt, x_sample, p_prompt, p_sample, state_mlstm_C, state_mlstm_n, state_mlstm_m, state_mlstm_conv,
           state_s5_re, state_s5_im, g_mix, w_in, conv_w, conv_b, w_q, w_k, w_v, b_i, b_f, g_head, skip_a, w_proj_a,
           s5_log_dt, s5_A_re, s5_A_im, s5_B_re, s5_B_im, s5_C_re, s5_C_im, s5_D, w_glu_b, w_out, g_ffn,
           w_ff_gate, w_ff_up, w_ff_down, w_router, b_router, w_moe_gate, w_moe_up, w_moe_down,
           g_ple, w_ple, w_pg, g_final):
    bsz, seq, d = x_prompt.shape
    bs = x_sample.shape[0]
    depth = g_mix.shape[0]
    nh = b_i.shape[1]
    d_a = conv_w.shape[-1]
    d_b = s5_D.shape[-1]
    n_p = bsz * seq
    t = n_p + bs
    tl = _tile_plan(t, seq)
    tm, tmb, tn = tl["tm"], tl["tm_big"], tl["tn"]

    h = jnp.concatenate([x_prompt.reshape(n_p, d), x_sample.reshape(bs, d)], axis=0).astype(F32)
    p_all = jnp.concatenate([p_prompt.reshape(depth, n_p, -1), p_sample.reshape(depth, bs, -1)], axis=1)

    col_ua, col_oa, col_ub = 0, 1, 2 * d_a // d_b
    col_ga = (2 * d_a + d_b) // tn
    col_gb = (2 * d_a + d_b + d) // tn
    col_if = (2 * d_a + d_b + 2 * d) // (2 * V7X_LANES)

    states = [[] for _ in range(11)]
    c_s_all = jnp.zeros(state_mlstm_C.shape, F32)
    a_pre = gy = None
    for i in range(depth):
        lw = _layer_weights(i, conv_w, conv_b, w_q, w_k, w_v, b_i, b_f, g_head, skip_a)
        sp = _s5_params(dict(log_dt=s5_log_dt[i], a_re=s5_A_re[i], a_im=s5_A_im[i], b_re=s5_B_re[i],
                             b_im=s5_B_im[i], c_re=s5_C_re[i], c_im=s5_C_im[i], d_skip=s5_D[i]))
        proj = in_proj(h, g_mix[i], w_in, i, d_a, d_b, d, nh, tm=tmb, tn=tn)

        a_pre, c_p, n_pp, m_p, conv_p = mlstm_prompt(proj, col_ua, col_oa, col_if, t, bsz, seq, lw, a_pre,
                                                     L=tl["mlstm_chunk"])
        a_pre, c_s_all, n_s, m_s, conv_s = mlstm_sample(proj, col_ua, col_oa, col_if, n_p, a_pre,
                                                        state_mlstm_C, i, c_s_all, state_mlstm_n[i].astype(F32),
                                                        state_mlstm_m[i].astype(F32),
                                                        state_mlstm_conv[i].astype(F32), lw)
        gy, sre_p, sim_p = s5_prompt(proj, col_ub, t, bsz, seq, sp, gy, tc=tl["s5_chunk"])
        gy, sre_s, sim_s = s5_sample(proj, col_ub, n_p, gy, state_s5_re[i].astype(F32),
                                     state_s5_im[i].astype(F32), sp)
        mix = branch_mix(a_pre, gy, w_proj_a, w_glu_b, i, proj, col_ga, col_gb, tm=tmb, tn=tn)
        h = matmul(mix, w_out, i, h, tm=tmb, tn=tn, name="out_proj")

        j = i // 2
        if i % 2 == 0:
            hid = norm_swiglu_up(h, g_ffn[i], w_ff_gate, w_ff_up, j, tm=tmb, tn=tn)
            h = matmul(hid, w_ff_down, j, h, tm=tmb, tn=tl["tn_down"], name="ffn_down")
        else:
            h = moe_layer(h, g_ffn[i], w_router[j], b_router[j], w_moe_gate[j], w_moe_up[j], w_moe_down[j],
                          tm=tm, tr=tl["moe_tr"], tn_up=tl["moe_tn_up"], tn_down=tl["moe_tn_down"],
                          tr_gather=tl["gather_tr"], tr_combine=tl["combine_tr"])
        h = ple_matmul(h, g_ple[i], p_all, w_pg, w_ple, i, tm=tl["ple_tm"], tn=tn)

        g_s, p_s = S5_STATE, sre_p.shape[-1] // S5_STATE
        new = [c_p, n_pp, m_p, conv_p, sre_p.reshape(bsz, p_s, g_s), sim_p.reshape(bsz, p_s, g_s),
               n_s, m_s, conv_s, sre_s.reshape(bs, p_s, g_s), sim_s.reshape(bs, p_s, g_s)]
        for lst, s in zip(states, new):
            lst.append(s)

    y_prompt, y_sample = final_norm_split(h, g_final, n_p, tm=tl["norm_tm"])
    y_prompt = y_prompt.reshape(bsz, seq, d)
    y_sample = y_sample.reshape(bs, 1, d)
    st = [jnp.stack(lst) for lst in states]
    return (y_prompt, y_sample) + tuple(st[:6]) + (c_s_all,) + tuple(st[6:])
```

```python
import functools
import math

import numpy as np
import jax
import jax.numpy as jnp
from jax import lax
from jax.experimental import pallas as pl
from jax.experimental.pallas import tpu as pltpu

F32 = jnp.float32
BF16 = jnp.bfloat16
EPS = 1e-6

V7X_VMEM_BYTES = 64 * 1024 * 1024
V7X_LANES = 128
V7X_SUBLANES = 8
VMEM_LIMIT = V7X_VMEM_BYTES - 8 * 1024 * 1024

N_HEADS = 4
CONV_W = 4
S5_GROUP = 16
S5_STATE = 64
TOP_K = 2


def _cparams(n_axes, vmem=VMEM_LIMIT):
    return pltpu.CompilerParams(dimension_semantics=("arbitrary",) * n_axes, vmem_limit_bytes=vmem)


def _sigmoid(x):
    return 1.0 / (1.0 + jnp.exp(-x))


def _silu(x):
    return x * _sigmoid(x)


def _gelu_tanh(x):
    return 0.5 * x * (1.0 + jnp.tanh(math.sqrt(2.0 / math.pi) * (x + 0.044715 * (x * x * x))))


def _log_sigmoid(x):
    return jnp.minimum(x, 0.0) - jnp.log(1.0 + jnp.exp(-jnp.abs(x)))


def _dot(a, b):
    return jnp.dot(a, b, preferred_element_type=F32)


def _dot_nt(a, b):
    return lax.dot_general(a, b, (((1,), (1,)), ((), ())), preferred_element_type=F32)


def _dot_tn(a, b):
    return lax.dot_general(a, b, (((0,), (0,)), ((), ())), preferred_element_type=F32)


def _rmsnorm_rows(x, g):
    ms = jnp.mean(x * x, axis=-1, keepdims=True)
    return x * lax.rsqrt(ms + EPS) * g


def _row_tile_buffering(tm, k, dtype):
    two_copies = 2 * tm * k * jnp.dtype(dtype).itemsize
    return pl.Buffered(1) if two_copies > VMEM_LIMIT // 4 else None


def _row_chunks(tm):
    for rc in (256, 208, 128, 104, 64, 32, 16, 8):
        if tm % rc == 0:
            return rc
    return tm


def _norm_to_scratch(x_ref, g_ref, xn_ref, tm):
    rc = _row_chunks(tm)

    def body(r, c):
        rows = pl.ds(pl.multiple_of(r * rc, rc), rc)
        xn_ref[rows, :] = _rmsnorm_rows(x_ref[rows, :], g_ref[...]).astype(BF16)
        return c

    lax.fori_loop(0, tm // rc, body, 0)


def _wspec(w, layer, tn, col=lambda j: j):
    return pl.BlockSpec((None, w.shape[1], tn), lambda i, j: (layer, 0, col(j)))


def _k_in_proj(x_ref, g_ref, wa_ref, wb_ref, wif_ref, o_ref, xn_ref, *, tm, n_head, n_main, nh):
    j = pl.program_id(1)

    @pl.when(j == 0)
    def _():
        _norm_to_scratch(x_ref, g_ref, xn_ref, tm)

    @pl.when(j < n_head)
    def _():
        o_ref[...] = _dot_nt(xn_ref[...], wa_ref[0].astype(BF16))

    @pl.when(jnp.logical_and(j >= n_head, j < n_main))
    def _():
        o_ref[...] = _dot_nt(xn_ref[...], wb_ref[0].astype(BF16))

    @pl.when(j == n_main)
    def _():
        pre = _dot_nt(xn_ref[...], wif_ref[0].astype(BF16))
        o_ref[...] = jnp.zeros_like(o_ref)
        o_ref[:, 0:nh] = pre[:, 0:nh]
        o_ref[:, V7X_LANES:V7X_LANES + nh] = pre[:, nh:2 * nh]


def in_proj(x, g, w_in, layer, d_a, d_b, d, nh, *, tm, tn):
    t, k = x.shape
    assert (2 * nh) % V7X_SUBLANES == 0 and (2 * d_a) % tn == 0 and (d_b + 2 * d) % tn == 0
    wt = jnp.swapaxes(w_in, 1, 2)
    n_head = 2 * d_a // tn
    n_main = n_head + (d_b + 2 * d) // tn
    if_row = 2 * d_a

    def rows(nrows, start):
        return pl.BlockSpec((pl.Element(1), pl.Element(nrows), pl.Element(k)), lambda i, j: (layer, start(j), 0))

    return pl.pallas_call(
        functools.partial(_k_in_proj, tm=tm, n_head=n_head, n_main=n_main, nh=nh),
        grid=(t // tm, n_main + 1),
        in_specs=[pl.BlockSpec((tm, k), lambda i, j: (i, 0), pipeline_mode=_row_tile_buffering(tm, k, x.dtype)),
                  pl.BlockSpec((1, k), lambda i, j: (0, 0)),
                  rows(tn, lambda j: jnp.minimum(j, n_head - 1) * tn),
                  rows(tn, lambda j: (jnp.clip(j, n_head, n_main - 1) * (tn // V7X_SUBLANES)
                                      + 2 * nh // V7X_SUBLANES) * V7X_SUBLANES),
                  rows(2 * nh, lambda j: if_row)],
        out_specs=pl.BlockSpec((tm, tn), lambda i, j: (i, j)),
        out_shape=jax.ShapeDtypeStruct((t, (n_main + 1) * tn), F32),
        scratch_shapes=[pltpu.VMEM((tm, k), BF16)],
        compiler_params=_cparams(2), name="in_proj",
    )(x, g.reshape(1, k), wt, wt, wt)


def _k_mm_res(x_ref, w_ref, r_ref, o_ref):
    o_ref[...] = r_ref[...] + _dot(x_ref[...], w_ref[...].astype(BF16))


def _k_mm(x_ref, w_ref, o_ref):
    o_ref[...] = _dot(x_ref[...], w_ref[...].astype(BF16)).astype(o_ref.dtype)


def matmul(x, w, layer, res=None, *, tm, tn, out_dtype=F32, name="mm"):
    t, k = x.shape
    n = w.shape[2]
    in_specs = [pl.BlockSpec((tm, k), lambda i, j: (i, 0), pipeline_mode=_row_tile_buffering(tm, k, x.dtype)),
                _wspec(w, layer, tn)]
    args = [x, w]
    body = _k_mm
    if res is not None:
        in_specs.append(pl.BlockSpec((tm, tn), lambda i, j: (i, j)))
        args.append(res)
        body = _k_mm_res
    return pl.pallas_call(
        body, grid=(t // tm, pl.cdiv(n, tn)), in_specs=in_specs,
        out_specs=pl.BlockSpec((tm, tn), lambda i, j: (i, j)),
        out_shape=jax.ShapeDtypeStruct((t, n), out_dtype),
        compiler_params=_cparams(2), name=name,
    )(*args)


def _k_branch_mix(a_ref, gy_ref, wp_ref, wv_ref, wg_ref, ga_ref, gb_ref, o_ref):
    a = a_ref[...]
    gy = gy_ref[...]
    a_out = _dot(a, wp_ref[...].astype(BF16))
    val = _dot(gy, wv_ref[...].astype(BF16))
    gate = _dot(gy, wg_ref[...].astype(BF16))
    b_out = val * _sigmoid(gate)
    o_ref[...] = (_sigmoid(ga_ref[...]) * a_out + _sigmoid(gb_ref[...]) * b_out).astype(o_ref.dtype)


def branch_mix(a_pre, gy, w_proj_a, w_glu_b, layer, proj, col_ga, col_gb, *, tm, tn):
    t, k = a_pre.shape
    n = w_proj_a.shape[2]
    nj = n // tn
    rows = pl.BlockSpec((tm, k), lambda i, j: (i, 0), pipeline_mode=pl.Buffered(1))
    return pl.pallas_call(
        _k_branch_mix, grid=(t // tm, nj),
        in_specs=[rows, rows,
                  _wspec(w_proj_a, layer, tn), _wspec(w_glu_b, layer, tn),
                  _wspec(w_glu_b, layer, tn, lambda j: j + nj),
                  pl.BlockSpec((tm, tn), lambda i, j: (i, col_ga + j)),
                  pl.BlockSpec((tm, tn), lambda i, j: (i, col_gb + j))],
        out_specs=pl.BlockSpec((tm, tn), lambda i, j: (i, j)),
        out_shape=jax.ShapeDtypeStruct((t, n), BF16),
        compiler_params=_cparams(2), name="branch_mix",
    )(a_pre, gy, w_proj_a, w_glu_b, w_glu_b, proj, proj)


def _k_swiglu_up(x_ref, g_ref, wg_ref, wu_ref, o_ref, xn_ref, *, tm):
    @pl.when(pl.program_id(1) == 0)
    def _():
        _norm_to_scratch(x_ref, g_ref, xn_ref, tm)

    xn = xn_ref[...]
    gate = _dot(xn, wg_ref[...].astype(BF16))
    up = _dot(xn, wu_ref[...].astype(BF16))
    o_ref[...] = (_silu(gate) * up).astype(o_ref.dtype)


def norm_swiglu_up(x, g, w_gate, w_up, layer, *, tm, tn, name="ffn_up"):
    t, k = x.shape
    n = w_gate.shape[2]
    return pl.pallas_call(
        functools.partial(_k_swiglu_up, tm=tm),
        grid=(t // tm, pl.cdiv(n, tn)),
        in_specs=[pl.BlockSpec((tm, k), lambda i, j: (i, 0), pipeline_mode=_row_tile_buffering(tm, k, x.dtype)),
                  pl.BlockSpec((1, k), lambda i, j: (0, 0)),
                  _wspec(w_gate, layer, tn), _wspec(w_up, layer, tn)],
        out_specs=pl.BlockSpec((tm, tn), lambda i, j: (i, j)),
        out_shape=jax.ShapeDtypeStruct((t, n), BF16),
        scratch_shapes=[pltpu.VMEM((tm, k), BF16)],
        compiler_params=_cparams(2), name=name,
    )(x, g.reshape(1, k), w_gate, w_up)


def _cast_rows_to(dst_ref, src_ref, rc):
    def body(r, c):
        rows = pl.ds(pl.multiple_of(r * rc, rc), rc)
        dst_ref[rows, :] = src_ref[rows, :].astype(dst_ref.dtype)
        return c

    lax.fori_loop(0, src_ref.shape[0] // rc, body, 0)


def _k_ple(x_ref, g_ref, p_ref, wpg_ref, wple_ref, o_ref, wpg_bf, wple_bf, *, tn):
    @pl.when(pl.program_id(0) == 0)
    def _():
        _cast_rows_to(wpg_bf, wpg_ref, V7X_LANES)
        _cast_rows_to(wple_bf, wple_ref, V7X_LANES)

    x = x_ref[...]
    xn = _rmsnorm_rows(x, g_ref[...]).astype(BF16)
    pb = p_ref[...].astype(BF16)
    for c in range(x.shape[1] // tn):
        cols = slice(c * tn, (c + 1) * tn)
        gate = _dot(xn, wpg_bf[:, cols])
        emb = _dot(pb, wple_bf[:, cols])
        o_ref[:, cols] = x[:, cols] + emb * _sigmoid(gate)


def ple_matmul(x, g, p, w_pg, w_ple, layer, *, tm, tn, name="ple"):
    t, k = x.shape
    kp = p.shape[2]
    resident = lambda w: pl.BlockSpec((None,) + w.shape[1:], lambda i: (layer, 0, 0), pipeline_mode=pl.Buffered(1))
    return pl.pallas_call(
        functools.partial(_k_ple, tn=tn),
        grid=(t // tm,),
        in_specs=[pl.BlockSpec((tm, k), lambda i: (i, 0)),
                  pl.BlockSpec((1, k), lambda i: (0, 0)),
                  pl.BlockSpec((None, tm, kp), lambda i: (layer, i, 0)),
                  resident(w_pg), resident(w_ple)],
        out_specs=pl.BlockSpec((tm, k), lambda i: (i, 0)),
        out_shape=jax.ShapeDtypeStruct((t, k), F32),
        scratch_shapes=[pltpu.VMEM(w_pg.shape[1:], BF16), pltpu.VMEM(w_ple.shape[1:], BF16)],
        compiler_params=_cparams(1), name=name,
    )(x, g.reshape(1, k), p, w_pg, w_ple)


def _k_final_norm(xp_ref, xs_ref, g_ref, op_ref, os_ref, *, n_prompt_tiles):
    i = pl.program_id(0)

    @pl.when(i < n_prompt_tiles)
    def _():
        op_ref[...] = _rmsnorm_rows(xp_ref[...], g_ref[...])

    @pl.when(i == n_prompt_tiles)
    def _():
        os_ref[...] = _rmsnorm_rows(xs_ref[...], g_ref[...])


def final_norm_split(x, g, n_prompt, *, tm):
    t, k = x.shape
    bs = t - n_prompt
    npt = n_prompt // tm
    last = npt - 1
    return pl.pallas_call(
        functools.partial(_k_final_norm, n_prompt_tiles=npt), grid=(npt + 1,),
        in_specs=[pl.BlockSpec((tm, k), lambda i: (jnp.minimum(i, last), 0)),
                  pl.BlockSpec((bs, k), lambda i: (n_prompt // bs, 0)),
                  pl.BlockSpec((1, k), lambda i: (0, 0))],
        out_specs=[pl.BlockSpec((tm, k), lambda i: (jnp.minimum(i, last), 0)),
                   pl.BlockSpec((bs, k), lambda i: (0, 0))],
        out_shape=[jax.ShapeDtypeStruct((n_prompt, k), F32), jax.ShapeDtypeStruct((bs, k), F32)],
        compiler_params=_cparams(1), name="final_norm",
    )(x, x, g.reshape(1, k))


def _split_hi_lo(x):
    hi = x.astype(BF16)
    lo = (x - hi.astype(F32)).astype(BF16)
    return hi, lo


def _k_mlstm_prompt(u_ref, o_ref_in, if_ref, cw_ref, cb_ref, wq_ref, wk_ref, wv_ref, bif_ref, gh_ref, sk_ref,
                    tri_ref, a_init, a_ref, c_out, n_out, m_out, conv_out, upad_ref, *, L, dh):
    del a_init
    c_idx = pl.program_id(1)
    nh = N_HEADS

    @pl.when(c_idx == 0)
    def _():
        c_out[...] = jnp.zeros_like(c_out)
        n_out[...] = jnp.zeros_like(n_out)
        m_out[...] = jnp.zeros_like(m_out)
        upad_ref[pl.ds(0, 8), :] = jnp.zeros((8, nh * dh), F32)

    @pl.when(c_idx > 0)
    def _():
        upad_ref[pl.ds(0, 8), :] = upad_ref[pl.ds(L, 8), :]

    u = u_ref[...]
    upad_ref[pl.ds(8, L), :] = u
    conv = cb_ref[...] + u * cw_ref[CONV_W - 1:CONV_W, :]
    for j in range(CONV_W - 1):
        conv = conv + upad_ref[pl.ds(8 - (CONV_W - 1) + j, L), :] * cw_ref[j:j + 1, :]
    cact = _silu(conv)
    conv_out[0] = upad_ref[pl.ds(L, 8), :]

    pre = if_ref[...]
    li = pre[:, :V7X_LANES] + bif_ref[:, :V7X_LANES]
    lf = _log_sigmoid(pre[:, V7X_LANES:] + bif_ref[:, V7X_LANES:])
    tri = tri_ref[...]
    lf_hi, lf_mid = _split_hi_lo(lf)
    lf_lo = (lf - lf_hi.astype(F32) - lf_mid.astype(F32)).astype(BF16)
    bcum = _dot(tri, lf_hi) + _dot(tri, lf_mid) + _dot(tri, lf_lo)
    li_t = li.T
    b_t = bcum.T
    row_id = lax.broadcasted_iota(jnp.int32, (L, L), 0)
    col_id = lax.broadcasted_iota(jnp.int32, (L, L), 1)
    causal = col_id <= row_id
    lane = lax.broadcasted_iota(jnp.int32, (1, V7X_LANES), 1)
    m_row = m_out[0]
    m_new_row = m_row

    for h in range(nh):
        hs = slice(h * dh, (h + 1) * dh)
        ch = cact[:, hs].astype(BF16)
        uh = u[:, hs].astype(BF16)
        q = _dot(ch, wq_ref[h].astype(BF16))
        k = _dot(ch, wk_ref[h].astype(BF16)) * (dh ** -0.5)
        v = _dot(uh, wv_ref[h].astype(BF16))
        qb, kb, vb = q.astype(BF16), k.astype(BF16), v.astype(BF16)

        b_col = bcum[:, h:h + 1]
        li_col = li[:, h:h + 1]
        r_row = li_t[h:h + 1, :] - b_t[h:h + 1, :]
        m_prev = m_row[:, h:h + 1]
        d = jnp.where(causal, b_col + r_row, -jnp.inf)
        inter = b_col + m_prev
        m_t = jnp.maximum(inter, jnp.max(d, axis=-1, keepdims=True))
        w_inter = jnp.exp(inter - m_t)
        s = _dot_nt(qb, kb) * jnp.exp(d - m_t)
        c_prev = c_out[0, h]
        n_prev = n_out[0, h:h + 1, :]
        num = w_inter * _dot(qb, c_prev.astype(BF16)) + _dot(s.astype(BF16), vb)
        den = w_inter * jnp.sum(q * n_prev, axis=-1, keepdims=True) + jnp.sum(s, axis=-1, keepdims=True)
        hh = num / jnp.maximum(jnp.abs(den), jnp.exp(-m_t))

        b_last = b_col[L - 1:L, :]
        g_col = b_last - b_col + li_col
        m_new = jnp.maximum(b_last + m_prev, jnp.max(g_col, axis=0, keepdims=True))
        decay = jnp.exp(b_last + m_prev - m_new)
        wk_ = jnp.exp(g_col - m_new) * k
        c_out[0, h] = decay * c_prev + _dot_tn(wk_.astype(BF16), vb)
        n_out[0, h:h + 1, :] = decay * n_prev + jnp.sum(wk_, axis=0, keepdims=True)
        m_new_row = jnp.where(lane == h, m_new, m_new_row)

        hn = _rmsnorm_rows(hh, gh_ref[:, hs])
        gated = (hn + sk_ref[:, hs] * cact[:, hs]) * _sigmoid(o_ref_in[:, hs])
        a_ref[:, hs] = gated.astype(a_ref.dtype)

    m_out[0] = m_new_row


def mlstm_prompt(proj, col_u, col_o, col_if, n_rows, bsz, seq, lw, rows_init=None, *, L):
    dh = lw["w_q"].shape[-1]
    da = N_HEADS * dh
    nc = seq // L
    tri = jnp.asarray(np.tril(np.ones((L, L), np.float32)), BF16)
    row_blk = lambda b, c: b * nc + c
    full = lambda *shape: pl.BlockSpec(shape, lambda b, c: (0,) * len(shape))
    outs = pl.pallas_call(
        functools.partial(_k_mlstm_prompt, L=L, dh=dh),
        grid=(bsz, nc),
        in_specs=[pl.BlockSpec((L, da), lambda b, c: (row_blk(b, c), col_u)),
                  pl.BlockSpec((L, da), lambda b, c: (row_blk(b, c), col_o)),
                  pl.BlockSpec((L, 2 * V7X_LANES), lambda b, c: (row_blk(b, c), col_if)),
                  full(CONV_W, da), full(1, da), full(N_HEADS, dh, dh), full(N_HEADS, dh, dh),
                  full(N_HEADS, dh, dh), full(1, 2 * V7X_LANES), full(1, da), full(1, da), full(L, L),
                  pl.BlockSpec(memory_space=pl.ANY)],
        out_specs=[pl.BlockSpec((L, da), lambda b, c: (row_blk(b, c), 0)),
                   pl.BlockSpec((1, N_HEADS, dh, dh), lambda b, c: (b, 0, 0, 0)),
                   pl.BlockSpec((1, N_HEADS, dh), lambda b, c: (b, 0, 0)),
                   pl.BlockSpec((1, 1, V7X_LANES), lambda b, c: (b, 0, 0)),
                   pl.BlockSpec((1, 8, da), lambda b, c: (b, 0, 0))],
        out_shape=[jax.ShapeDtypeStruct((n_rows, da), BF16),
                   jax.ShapeDtypeStruct((bsz, N_HEADS, dh, dh), F32),
                   jax.ShapeDtypeStruct((bsz, N_HEADS, dh), F32),
                   jax.ShapeDtypeStruct((bsz, 1, V7X_LANES), F32),
                   jax.ShapeDtypeStruct((bsz, 8, da), F32)],
        scratch_shapes=[pltpu.VMEM((L + 8, da), F32)],
        input_output_aliases={12: 0},
        compiler_params=_cparams(2), name="mlstm_prompt",
    )(proj, proj, proj, lw["conv_w"], lw["conv_b"], lw["w_q"], lw["w_k"], lw["w_v"], lw["b_if"],
      lw["g_head"], lw["skip"], tri, jnp.zeros((n_rows, da), BF16) if rows_init is None else rows_init)
    a_pre, c_p, n_p, m_p, conv_p = outs
    return a_pre, c_p, n_p, m_p[:, 0, :N_HEADS], conv_p[:, 8 - (CONV_W - 1):, :]


def _k_mlstm_sample_pre(u_ref, if_ref, conv_ref, m_ref, cw_ref, cb_ref, wq_ref, wk_ref, wv_ref, bif_ref,
                        q_out, k_out, v_out, c_out, gates_out, conv_out, *, dh):
    nh = N_HEADS
    u = u_ref[...]
    conv = cb_ref[...] + u * cw_ref[CONV_W - 1:CONV_W, :]
    for j in range(CONV_W - 1):
        conv = conv + conv_ref[j] * cw_ref[j:j + 1, :]
        if j > 0:
            conv_out[j - 1] = conv_ref[j]
    conv_out[CONV_W - 2] = u
    cact = _silu(conv)
    c_out[...] = cact
    for h in range(nh):
        hs = slice(h * dh, (h + 1) * dh)
        ch = cact[:, hs].astype(BF16)
        q_out[:, hs] = _dot(ch, wq_ref[h].astype(BF16))
        k_out[:, hs] = _dot(ch, wk_ref[h].astype(BF16)) * (dh ** -0.5)
        v_out[:, hs] = _dot(u[:, hs].astype(BF16), wv_ref[h].astype(BF16))
    pre = if_ref[...]
    li = pre[:, :V7X_LANES] + bif_ref[:, :V7X_LANES]
    lf = _log_sigmoid(pre[:, V7X_LANES:] + bif_ref[:, V7X_LANES:])
    m_prev = m_ref[...]
    inter = lf + m_prev
    m_t = jnp.maximum(inter, li)
    gates_out[0] = jnp.exp(inter - m_t)
    gates_out[1] = jnp.exp(li - m_t)
    gates_out[2] = jnp.exp(-m_t)
    gates_out[3] = m_t


def _k_mlstm_sample_step(q_ref, k_ref, v_ref, gates_ref, c_ref, n_ref, cact_ref, o_ref_in, gh_ref, sk_ref,
                         a_any, c_any, c_out, n_out, a_out, hh_ref, *, bt, dh):
    del a_any, c_any
    i = pl.program_id(0)
    nh = N_HEADS
    q = q_ref[...]
    k = k_ref[...]
    v = v_ref[...]
    w_inter = gates_ref[0]
    w_new = gates_ref[1]
    e_neg_m = gates_ref[2]
    n_prev = n_ref[...]
    rows = pl.ds(pl.multiple_of(i * bt, bt), bt)
    for h in range(nh):
        hs = slice(h * dh, (h + 1) * dh)
        qh, kh, vh, nh_prev = q[:, hs], k[:, hs], v[:, hs], n_prev[:, hs]
        q_t = qh.T
        k_t = kh.T
        wi = w_inter[:, h:h + 1]
        wn = w_new[:, h:h + 1]
        s = jnp.sum(qh * kh, axis=-1, keepdims=True) * wn
        den = wi * jnp.sum(qh * nh_prev, axis=-1, keepdims=True) + s
        wv = wn * vh
        qc_rows = []
        for bl in range(bt):
            c_prev = c_ref[bl, h]
            qc_rows.append(jnp.sum(q_t[:, bl:bl + 1] * c_prev, axis=0, keepdims=True))
            c_out[bl, h] = wi[bl:bl + 1, :] * c_prev + k_t[:, bl:bl + 1] * wv[bl:bl + 1, :]
        qc = jnp.concatenate(qc_rows, axis=0)
        num = wi * qc + s * vh
        hh = num / jnp.maximum(jnp.abs(den), e_neg_m[:, h:h + 1])
        hh_ref[rows, hs] = hh
        n_out[:, hs] = wi * nh_prev + wn * kh

    @pl.when(i == pl.num_programs(0) - 1)
    def _():
        for h in range(nh):
            hs = slice(h * dh, (h + 1) * dh)
            hn = _rmsnorm_rows(hh_ref[:, hs], gh_ref[:, hs])
            a_out[:, hs] = ((hn + sk_ref[:, hs] * cact_ref[:, hs]) * _sigmoid(o_ref_in[:, hs])).astype(a_out.dtype)


def mlstm_sample(proj, col_u, col_o, col_if, row0, a_pre_all, c_all, layer, c_new_all, n0, m0, conv0, lw, *,
                 bt=8):
    _, bs, nh, dh, _ = c_all.shape
    da = nh * dh
    rb = row0 // bs
    m_pad = jnp.pad(m0, ((0, 0), (0, V7X_LANES - nh)))
    conv_t = jnp.transpose(conv0, (1, 0, 2))
    full = lambda *shape: pl.BlockSpec(shape, lambda i: (0,) * len(shape))
    q, k, v, cact, gates, conv_new = pl.pallas_call(
        functools.partial(_k_mlstm_sample_pre, dh=dh),
        grid=(1,),
        in_specs=[pl.BlockSpec((bs, da), lambda i: (rb, col_u)),
                  pl.BlockSpec((bs, 2 * V7X_LANES), lambda i: (rb, col_if)),
                  full(CONV_W - 1, bs, da), full(bs, V7X_LANES), full(CONV_W, da), full(1, da),
                  full(nh, dh, dh), full(nh, dh, dh), full(nh, dh, dh), full(1, 2 * V7X_LANES)],
        out_specs=[full(bs, da), full(bs, da), full(bs, da), full(bs, da), full(4, bs, V7X_LANES),
                   full(CONV_W - 1, bs, da)],
        out_shape=[jax.ShapeDtypeStruct((bs, da), F32)] * 4
                  + [jax.ShapeDtypeStruct((4, bs, V7X_LANES), F32),
                     jax.ShapeDtypeStruct((CONV_W - 1, bs, da), F32)],
        compiler_params=_cparams(1), name="mlstm_sample_pre",
    )(proj, proj, conv_t, m_pad, lw["conv_w"], lw["conv_b"], lw["w_q"], lw["w_k"], lw["w_v"], lw["b_if"])

    blk = lambda *shape: pl.BlockSpec(shape, lambda i: (i,) + (0,) * (len(shape) - 1))
    cst = lambda *shape: pl.BlockSpec(shape, lambda i: (0,) * len(shape))
    c_blk = pl.BlockSpec((None, bt, nh, dh, dh), lambda i: (layer, i, 0, 0, 0))
    c_new_all, n_new, a_pre_all = pl.pallas_call(
        functools.partial(_k_mlstm_sample_step, bt=bt, dh=dh),
        grid=(bs // bt,),
        in_specs=[blk(bt, da), blk(bt, da), blk(bt, da),
                  pl.BlockSpec((4, bt, V7X_LANES), lambda i: (0, i, 0)),
                  c_blk, blk(bt, da), cst(bs, da),
                  pl.BlockSpec((bs, da), lambda i: (rb, col_o)), cst(1, da), cst(1, da),
                  pl.BlockSpec(memory_space=pl.ANY), pl.BlockSpec(memory_space=pl.ANY)],
        out_specs=[c_blk, blk(bt, da), pl.BlockSpec((bs, da), lambda i: (rb, 0))],
        out_shape=[jax.ShapeDtypeStruct(c_new_all.shape, F32), jax.ShapeDtypeStruct((bs, da), F32),
                   jax.ShapeDtypeStruct(a_pre_all.shape, a_pre_all.dtype)],
        scratch_shapes=[pltpu.VMEM((bs, da), F32)],
        input_output_aliases={10: 2, 11: 0},
        compiler_params=_cparams(1), name="mlstm_sample_step",
    )(q, k, v, gates, c_all, n0.reshape(bs, da), cact, proj, lw["g_head"], lw["skip"], a_pre_all, c_new_all)
    m_new = gates[3][:, :nh]
    return a_pre_all, c_new_all, n_new.reshape(bs, nh, dh), m_new, jnp.transpose(conv_new, (1, 0, 2))


S5_SLAB_GROUPS = V7X_LANES // S5_GROUP
S5_SLAB_STATES = S5_SLAB_GROUPS * S5_STATE


def _s5_params(lp):
    g, p = lp["a_re"].shape
    dt = jnp.exp(lp["log_dt"].astype(F32))[:, None]
    a_re = lp["a_re"].astype(F32)
    a_im = lp["a_im"].astype(F32)
    lam_re = a_re * dt
    lam_im = a_im * dt
    mag = jnp.exp(lam_re)
    ab_re = mag * jnp.cos(lam_im)
    ab_im = mag * jnp.sin(lam_im)
    den = a_re * a_re + a_im * a_im
    nr = ab_re - 1.0
    ni = ab_im
    k_re = (nr * a_re + ni * a_im) / den
    k_im = (ni * a_re - nr * a_im) / den
    b_re = lp["b_re"].astype(F32)
    b_im = lp["b_im"].astype(F32)
    bb_re = k_re[..., None] * b_re - k_im[..., None] * b_im
    bb_im = k_re[..., None] * b_im + k_im[..., None] * b_re
    ns = g // S5_SLAB_GROUPS
    eye = jnp.eye(S5_SLAB_GROUPS, dtype=F32)

    def in_blockdiag(bb):
        bs = bb.reshape(ns, S5_SLAB_GROUPS, p, S5_GROUP)
        w = jnp.einsum("ab,sapc->sacbp", eye, bs)
        return w.reshape(ns, V7X_LANES, S5_SLAB_STATES).astype(BF16)

    def out_blockdiag(cc):
        cs = cc.astype(F32).reshape(ns, S5_SLAB_GROUPS, S5_GROUP, p)
        w = jnp.einsum("ab,sacp->sapbc", eye, cs)
        return w.reshape(ns, S5_SLAB_STATES, V7X_LANES)

    wc = jnp.concatenate([out_blockdiag(lp["c_re"]), -out_blockdiag(lp["c_im"])], axis=1).astype(BF16)
    return dict(lam_re=lam_re.reshape(1, g * p), lam_im=lam_im.reshape(1, g * p),
                ab_re=ab_re.reshape(1, g * p), ab_im=ab_im.reshape(1, g * p),
                wb_re=in_blockdiag(bb_re), wb_im=in_blockdiag(bb_im), wc=wc,
                d_skip=lp["d_skip"].astype(F32).reshape(1, g * S5_GROUP))


def _s5_powers(sp, ks):
    kk = jnp.asarray(ks, F32)[:, None]
    mag = jnp.exp(kk * sp["lam_re"])
    return mag * jnp.cos(kk * sp["lam_im"]), mag * jnp.sin(kk * sp["lam_im"])


def _s5_input_proj(u_bf, wbr_ref, wbi_ref, bur_ref, bui_ref):
    ns = wbr_ref.shape[0]
    for s in range(ns):
        us = u_bf[:, s * V7X_LANES:(s + 1) * V7X_LANES]
        cols = slice(s * S5_SLAB_STATES, (s + 1) * S5_SLAB_STATES)
        bur_ref[:, cols] = _dot(us, wbr_ref[s])
        bui_ref[:, cols] = _dot(us, wbi_ref[s])


def _s5_output_proj(xr_ref, xi_ref, wc_ref, skip):
    ns = wc_ref.shape[0]
    outs = []
    for s in range(ns):
        cols = slice(s * S5_SLAB_STATES, (s + 1) * S5_SLAB_STATES)
        xcat = jnp.concatenate([xr_ref[:, cols].astype(BF16), xi_ref[:, cols].astype(BF16)], axis=1)
        y = _dot(xcat, wc_ref[s]) + skip[:, s * V7X_LANES:(s + 1) * V7X_LANES]
        outs.append(_gelu_tanh(y).astype(BF16))
    return jnp.concatenate(outs, axis=1)


def _k_s5_prompt(u_ref, wbr_ref, wbi_ref, wc_ref, d_ref, abr_ref, abi_ref, tpr_ref, tpi_ref, ajr_ref, aji_ref,
                 perm_ref, permt_ref, gy_init, gy_ref, sre_out, sim_out,
                 bur0, bur1, bui0, bui1, xb0, xb1, *, tc):
    del gy_init
    c_idx = pl.program_id(1)
    nsteps = tc // V7X_SUBLANES
    ns = wbr_ref.shape[0]
    lw = S5_SLAB_STATES
    pair = 2 * V7X_SUBLANES
    bur, bui, xb = (bur0, bur1), (bui0, bui1), (xb0, xb1)

    @pl.when(c_idx == 0)
    def _():
        sre_out[...] = jnp.zeros_like(sre_out)
        sim_out[...] = jnp.zeros_like(sim_out)

    u = u_ref[...]
    u_hi, u_lo = _split_hi_lo(u)
    perm = perm_ref[...]
    up_hi = _dot(perm, u_hi)
    skip = d_ref[...] * (up_hi + _dot(perm, u_lo))
    up_bf = up_hi.astype(BF16)

    sub = lax.broadcasted_iota(jnp.int32, (V7X_SUBLANES, lw), 0)

    def input_proj(s):
        us = up_bf[:, s * V7X_LANES:(s + 1) * V7X_LANES]
        bur[s % 2][...] = _dot(us, wbr_ref[s])
        bui[s % 2][...] = _dot(us, wbi_ref[s])

    def scan(s):
        br, bi, xo = bur[s % 2], bui[s % 2], xb[s % 2]
        lanes = slice(s * lw, (s + 1) * lw)
        ar = jnp.broadcast_to(abr_ref[:, lanes], (V7X_SUBLANES, lw))
        ai = jnp.broadcast_to(abi_ref[:, lanes], (V7X_SUBLANES, lw))
        er = jnp.zeros((V7X_SUBLANES, lw), F32)
        ei = er
        for i in range(nsteps):
            rows = slice(i * V7X_SUBLANES, (i + 1) * V7X_SUBLANES)
            er, ei = (ar * er - ai * ei + br[rows, :], ar * ei + ai * er + bi[rows, :])
            br[rows, :] = er
            bi[rows, :] = ei
        for d, row in ((1, 0), (2, 1), (4, 3)):
            pr = ajr_ref[row:row + 1, lanes]
            pi = aji_ref[row:row + 1, lanes]
            sr = pltpu.roll(er, d, 0)
            si = pltpu.roll(ei, d, 0)
            keep = sub >= d
            er, ei = (er + jnp.where(keep, pr * sr - pi * si, 0.0),
                      ei + jnp.where(keep, pr * si + pi * sr, 0.0))
        c0r = jnp.broadcast_to(sre_out[0, :, lanes], (V7X_SUBLANES, lw))
        c0i = jnp.broadcast_to(sim_out[0, :, lanes], (V7X_SUBLANES, lw))
        ajr = ajr_ref[:, lanes]
        aji = aji_ref[:, lanes]
        fr = ajr * c0r - aji * c0i + er
        fi = ajr * c0i + aji * c0r + ei
        cin_r = jnp.where(sub >= 1, pltpu.roll(fr, 1, 0), c0r)
        cin_i = jnp.where(sub >= 1, pltpu.roll(fi, 1, 0), c0i)
        sre_out[0, :, lanes] = fr[V7X_SUBLANES - 1:V7X_SUBLANES, :]
        sim_out[0, :, lanes] = fi[V7X_SUBLANES - 1:V7X_SUBLANES, :]

        cin2_r = jnp.concatenate([cin_r, cin_r], axis=0)
        cin2_i = jnp.concatenate([cin_i, cin_i], axis=0)
        for k in range(tc // pair):
            rows = slice(k * pair, (k + 1) * pair)
            pr = tpr_ref[rows, lanes]
            pi = tpi_ref[rows, lanes]
            xo[rows, :lw] = (br[rows, :] + (pr * cin2_r - pi * cin2_i)).astype(BF16)
            xo[rows, lw:] = (bi[rows, :] + (pr * cin2_i + pi * cin2_r)).astype(BF16)

    def output_proj(s):
        y = _dot(xb[s % 2][...], wc_ref[s]) + skip[:, s * V7X_LANES:(s + 1) * V7X_LANES]
        return _gelu_tanh(y).astype(BF16)

    outs = []
    input_proj(0)
    for s in range(ns):
        if s + 1 < ns:
            input_proj(s + 1)
        scan(s)
        outs.append(output_proj(s))
    g_perm = jnp.concatenate(outs, axis=1)
    gy_ref[...] = _dot(permt_ref[...], g_perm).astype(gy_ref.dtype)


def s5_prompt(proj, col_u, n_rows, bsz, seq, sp, rows_init=None, *, tc=256):
    db = sp["d_skip"].shape[1]
    nch = sp["ab_re"].shape[1]
    nc = seq // tc
    nsteps = tc // V7X_SUBLANES
    tpr, tpi = (jnp.repeat(a, V7X_SUBLANES, axis=0) for a in _s5_powers(sp, np.arange(1, nsteps + 1)))
    ajr, aji = _s5_powers(sp, nsteps * np.arange(1, V7X_SUBLANES + 1))
    perm = np.zeros((tc, tc), np.float32)
    r = np.arange(tc)
    perm[r, (r % V7X_SUBLANES) * nsteps + r // V7X_SUBLANES] = 1.0
    full = lambda a: pl.BlockSpec(a.shape, lambda b, c: (0,) * a.ndim)
    consts = [sp["wb_re"], sp["wb_im"], sp["wc"], sp["d_skip"], sp["ab_re"], sp["ab_im"], tpr, tpi, ajr, aji,
              jnp.asarray(perm, BF16), jnp.asarray(perm.T, BF16)]
    gy, s_re, s_im = pl.pallas_call(
        functools.partial(_k_s5_prompt, tc=tc),
        grid=(bsz, nc),
        in_specs=[pl.BlockSpec((tc, db), lambda b, c: (b * nc + c, col_u))] + [full(a) for a in consts]
                 + [pl.BlockSpec(memory_space=pl.ANY)],
        out_specs=[pl.BlockSpec((tc, db), lambda b, c: (b * nc + c, 0)),
                   pl.BlockSpec((1, 1, nch), lambda b, c: (b, 0, 0)),
                   pl.BlockSpec((1, 1, nch), lambda b, c: (b, 0, 0))],
        out_shape=[jax.ShapeDtypeStruct((n_rows, db), BF16),
                   jax.ShapeDtypeStruct((bsz, 1, nch), F32), jax.ShapeDtypeStruct((bsz, 1, nch), F32)],
        scratch_shapes=[pltpu.VMEM((tc, S5_SLAB_STATES), F32)] * 4 + [pltpu.VMEM((tc, 2 * S5_SLAB_STATES), BF16)] * 2,
        input_output_aliases={1 + len(consts): 0},
        compiler_params=_cparams(2), name="s5_prompt",
    )(proj, *consts, jnp.zeros((n_rows, db), BF16) if rows_init is None else rows_init)
    return gy, s_re, s_im


def _k_s5_sample(u_ref, x0r_ref, x0i_ref, wbr_ref, wbi_ref, wc_ref, d_ref, abr_ref, abi_ref, gy_any,
                 gy_ref, xr_out, xi_out):
    del gy_any
    u = u_ref[...]
    _s5_input_proj(u.astype(BF16), wbr_ref, wbi_ref, xr_out, xi_out)
    ar = abr_ref[...]
    ai = abi_ref[...]
    x0r = x0r_ref[...]
    x0i = x0i_ref[...]
    xr_out[...] = xr_out[...] + (ar * x0r - ai * x0i)
    xi_out[...] = xi_out[...] + (ar * x0i + ai * x0r)
    gy_ref[...] = _s5_output_proj(xr_out, xi_out, wc_ref, d_ref[...] * u).astype(gy_ref.dtype)


def s5_sample(proj, col_u, row0, gy_all, x0_re, x0_im, sp):
    bs = x0_re.shape[0]
    db = sp["d_skip"].shape[1]
    nch = sp["ab_re"].shape[1]
    rb = row0 // bs
    full = lambda a: pl.BlockSpec(a.shape, lambda i: (0,) * a.ndim)
    consts = [sp["wb_re"], sp["wb_im"], sp["wc"], sp["d_skip"], sp["ab_re"], sp["ab_im"]]
    x0r = x0_re.reshape(bs, nch)
    x0i = x0_im.reshape(bs, nch)
    gy_all, xr, xi = pl.pallas_call(
        _k_s5_sample,
        grid=(1,),
        in_specs=[pl.BlockSpec((bs, db), lambda i: (rb, col_u)), full(x0r), full(x0i)]
                 + [full(a) for a in consts] + [pl.BlockSpec(memory_space=pl.ANY)],
        out_specs=[pl.BlockSpec((bs, db), lambda i: (rb, 0)),
                   pl.BlockSpec((bs, nch), lambda i: (0, 0)), pl.BlockSpec((bs, nch), lambda i: (0, 0))],
        out_shape=[jax.ShapeDtypeStruct(gy_all.shape, gy_all.dtype),
                   jax.ShapeDtypeStruct((bs, nch), F32), jax.ShapeDtypeStruct((bs, nch), F32)],
        input_output_aliases={9: 0},
        compiler_params=_cparams(1), name="s5_sample",
    )(proj, x0r, x0i, *consts, gy_all)
    return gy_all, xr, xi


def _k_router(x_ref, g_ref, w_ref, b_ref, hn_ref, lg_ref):
    hn = _rmsnorm_rows(x_ref[...], g_ref[...])
    hn_ref[...] = hn
    x_hi, x_lo = _split_hi_lo(hn)
    w_hi, w_lo = _split_hi_lo(w_ref[...])
    lg_ref[...] = _dot(x_hi, w_hi) + (_dot(x_lo, w_hi) + _dot(x_hi, w_lo)) + b_ref[...]


def router(x, g, w_router, b_router, *, tm):
    t, k = x.shape
    ne = w_router.shape[1]
    w_pad = jnp.pad(w_router, ((0, 0), (0, V7X_LANES - ne)))
    b_pad = jnp.pad(b_router.astype(F32), (0, V7X_LANES - ne)).reshape(1, V7X_LANES)
    hn, lg = pl.pallas_call(
        _k_router, grid=(t // tm,),
        in_specs=[pl.BlockSpec((tm, k), lambda i: (i, 0)), pl.BlockSpec((1, k), lambda i: (0, 0)),
                  pl.BlockSpec((k, V7X_LANES), lambda i: (0, 0)), pl.BlockSpec((1, V7X_LANES), lambda i: (0, 0))],
        out_specs=[pl.BlockSpec((tm, k), lambda i: (i, 0)), pl.BlockSpec((tm, V7X_LANES), lambda i: (i, 0))],
        out_shape=[jax.ShapeDtypeStruct((t, k), F32), jax.ShapeDtypeStruct((t, V7X_LANES), F32)],
        compiler_params=_cparams(1), name="router",
    )(x, g.reshape(1, k), w_pad, b_pad)
    return hn, lg[:, :ne]


DMA_ISSUE_UNROLL = 8


def _k_gather_rows(nused_ref, tok_ref, src_hbm, o_ref, buf_ref, sems, *, tr):
    t = pl.program_id(0)
    n_used = nused_ref[0]

    def issue_tile(tile):
        slot = tile % 2
        base = tile * tr

        def issue(r2, c):
            for prio in range(2):
                r = 2 * r2 + prio
                tok = tok_ref[base + r]
                pltpu.make_async_copy(src_hbm.at[pl.ds(tok, 1), :], buf_ref.at[slot, pl.ds(r, 1), :],
                                      sems.at[slot]).start(priority=prio)
            return c

        lax.fori_loop(0, tr // 2, issue, 0, unroll=DMA_ISSUE_UNROLL // 2)

    @pl.when(t == 0)
    def _():
        issue_tile(t)

    @pl.when(t + 1 < n_used)
    def _():
        issue_tile(t + 1)

    @pl.when(t < n_used)
    def _():
        slot = t % 2
        pltpu.make_async_copy(src_hbm.at[pl.ds(0, tr), :], buf_ref.at[slot], sems.at[slot]).wait()
        o_ref[...] = buf_ref[slot].astype(o_ref.dtype)

    @pl.when(t >= n_used)
    def _():
        o_ref[...] = jnp.zeros_like(o_ref)


def gather_rows(src, tok, n_used, *, tr):
    t, k = src.shape
    r_pad = tok.shape[0]
    nt = r_pad // tr
    return pl.pallas_call(
        functools.partial(_k_gather_rows, tr=tr),
        grid_spec=pltpu.PrefetchScalarGridSpec(
            num_scalar_prefetch=2, grid=(nt,),
            in_specs=[pl.BlockSpec(memory_space=pl.ANY)],
            out_specs=pl.BlockSpec((tr, k), lambda i, nu, tk: (i, 0)),
            scratch_shapes=[pltpu.VMEM((2, tr, k), src.dtype), pltpu.SemaphoreType.DMA((2,))]),
        out_shape=jax.ShapeDtypeStruct((r_pad, k), BF16),
        compiler_params=_cparams(1), name="moe_gather",
    )(n_used, tok, src)


def _k_moe_up(nused_ref, te_ref, x_ref, wg_ref, wu_ref, o_ref):
    del te_ref
    t = pl.program_id(1)

    @pl.when(t < nused_ref[0])
    def _():
        x = x_ref[...]
        gate = _dot(x, wg_ref[...].astype(BF16))
        up = _dot(x, wu_ref[...].astype(BF16))
        o_ref[...] = (_silu(gate) * up).astype(o_ref.dtype)

    @pl.when(t >= nused_ref[0])
    def _():
        o_ref[...] = jnp.zeros_like(o_ref)


def moe_up(xs, w_gate, w_up, tile_expert, n_used, *, tr, tn):
    r_pad, k = xs.shape
    ne, _, f = w_gate.shape
    nt = r_pad // tr
    row = lambda j, t, nu, te: (jnp.minimum(t, nu[0] - 1), 0)
    wmap = lambda j, t, nu, te: (te[t], 0, j)
    return pl.pallas_call(
        _k_moe_up,
        grid_spec=pltpu.PrefetchScalarGridSpec(
            num_scalar_prefetch=2, grid=(pl.cdiv(f, tn), nt),
            in_specs=[pl.BlockSpec((tr, k), row),
                      pl.BlockSpec((None, k, tn), wmap), pl.BlockSpec((None, k, tn), wmap)],
            out_specs=pl.BlockSpec((tr, tn), lambda j, t, nu, te: (t, j))),
        out_shape=jax.ShapeDtypeStruct((r_pad, f), BF16),
        compiler_params=_cparams(2), name="moe_up",
    )(n_used, tile_expert, xs, w_gate, w_up)


def _k_moe_down(nused_ref, te_ref, x_ref, w_ref, o_ref):
    del te_ref
    t = pl.program_id(1)

    @pl.when(t < nused_ref[0])
    def _():
        o_ref[...] = _dot(x_ref[...], w_ref[...].astype(BF16))

    @pl.when(t >= nused_ref[0])
    def _():
        o_ref[...] = jnp.zeros_like(o_ref)


def moe_down(hid, w_down, tile_expert, n_used, *, tr, tn):
    r_pad, f = hid.shape
    d = w_down.shape[2]
    nt = r_pad // tr
    return pl.pallas_call(
        _k_moe_down,
        grid_spec=pltpu.PrefetchScalarGridSpec(
            num_scalar_prefetch=2, grid=(d // tn, nt),
            in_specs=[pl.BlockSpec((tr, f), lambda j, t, nu, te: (jnp.minimum(t, nu[0] - 1), 0)),
                      pl.BlockSpec((None, f, tn), lambda j, t, nu, te: (te[t], 0, j))],
            out_specs=pl.BlockSpec((tr, tn), lambda j, t, nu, te: (t, j))),
        out_shape=jax.ShapeDtypeStruct((r_pad, d), F32),
        compiler_params=_cparams(2), name="moe_down",
    )(n_used, tile_expert, hid, w_down)


def _k_moe_combine(pos_ref, ys_hbm, gate_ref, res_ref, o_ref, buf0_ref, buf1_ref, sems, *, tr):
    t = pl.program_id(0)
    n_tiles = pl.num_programs(0)
    n_tok = n_tiles * tr

    def issue_tile(tile):
        slot = tile % 2
        base = tile * tr

        def issue(r, c):
            p0 = pos_ref[base + r]
            p1 = pos_ref[n_tok + base + r]
            pltpu.make_async_copy(ys_hbm.at[pl.ds(p0, 1), :], buf0_ref.at[slot, pl.ds(r, 1), :],
                                  sems.at[slot]).start(priority=0)
            pltpu.make_async_copy(ys_hbm.at[pl.ds(p1, 1), :], buf1_ref.at[slot, pl.ds(r, 1), :],
                                  sems.at[slot]).start(priority=1)
            return c

        lax.fori_loop(0, tr, issue, 0, unroll=DMA_ISSUE_UNROLL)

    @pl.when(t == 0)
    def _():
        issue_tile(t)

    @pl.when(t + 1 < n_tiles)
    def _():
        issue_tile(t + 1)

    slot = t % 2
    pltpu.make_async_copy(ys_hbm.at[pl.ds(0, tr), :], buf0_ref.at[slot], sems.at[slot]).wait()
    pltpu.make_async_copy(ys_hbm.at[pl.ds(0, tr), :], buf1_ref.at[slot], sems.at[slot]).wait()
    g = gate_ref[...]
    o_ref[...] = res_ref[...] + (g[:, 0:1] * buf0_ref[slot] + g[:, 1:2] * buf1_ref[slot])


def moe_combine(ys, pos, gates, res, *, tr):
    t, d = res.shape
    g_pad = jnp.pad(gates, ((0, 0), (0, V7X_LANES - gates.shape[1])))
    return pl.pallas_call(
        functools.partial(_k_moe_combine, tr=tr),
        grid_spec=pltpu.PrefetchScalarGridSpec(
            num_scalar_prefetch=1, grid=(t // tr,),
            in_specs=[pl.BlockSpec(memory_space=pl.ANY),
                      pl.BlockSpec((tr, V7X_LANES), lambda i, p: (i, 0)),
                      pl.BlockSpec((tr, d), lambda i, p: (i, 0))],
            out_specs=pl.BlockSpec((tr, d), lambda i, p: (i, 0)),
            scratch_shapes=[pltpu.VMEM((2, tr, d), F32), pltpu.VMEM((2, tr, d), F32),
                            pltpu.SemaphoreType.DMA((2,))]),
        out_shape=jax.ShapeDtypeStruct((t, d), F32),
        compiler_params=_cparams(1), name="moe_combine",
    )(pos, ys, g_pad, res)


def moe_layer(h, g_ffn, w_router, b_router, w_gate, w_up, w_down, *, tm, tr, tn_up, tn_down, tr_gather,
              tr_combine):
    t, d = h.shape
    ne = w_gate.shape[0]
    hn, logits = router(h, g_ffn, w_router, b_router, tm=tm)
    top_v, top_e = lax.top_k(logits, TOP_K)
    gates = jax.nn.softmax(top_v, axis=-1)
    flat_e = top_e.reshape(-1)
    onehot = (flat_e[:, None] == jnp.arange(ne, dtype=flat_e.dtype)[None, :]).astype(jnp.int32)
    rank = jnp.sum((jnp.cumsum(onehot, axis=0) - onehot) * onehot, axis=1)
    sizes = jnp.sum(onehot, axis=0)
    tiles_per = (sizes + tr - 1) // tr
    tile_end = jnp.cumsum(tiles_per)
    tile_start = tile_end - tiles_per
    n_used = tile_end[-1:].astype(jnp.int32)
    nt = (t * TOP_K) // tr + ne
    r_pad = nt * tr
    pos = (tile_start[flat_e] * tr + rank).astype(jnp.int32)
    src_tok = jnp.zeros((r_pad,), jnp.int32).at[pos].set(jnp.arange(t * TOP_K, dtype=jnp.int32) // TOP_K)
    tile_ids = jnp.minimum(jnp.arange(nt, dtype=jnp.int32), n_used[0] - 1)
    tile_expert = jnp.sum((tile_ids[:, None] >= tile_end[None, :]).astype(jnp.int32), axis=1).astype(jnp.int32)
    xs = gather_rows(hn, src_tok, n_used * (tr // tr_gather), tr=tr_gather)
    hid = moe_up(xs, w_gate, w_up, tile_expert, n_used, tr=tr, tn=tn_up)
    ys = moe_down(hid, w_down, tile_expert, n_used, tr=tr, tn=tn_down)
    return moe_combine(ys, pos.reshape(t, TOP_K).T.reshape(-1), gates, h, tr=tr_combine)


def _layer_weights(i, conv_w, conv_b, w_q, w_k, w_v, b_i, b_f, g_head, skip_a):
    nh = b_i.shape[1]
    pad = jnp.zeros((V7X_LANES - nh,), F32)
    b_if = jnp.concatenate([b_i[i].astype(F32), pad, b_f[i].astype(F32), pad]).reshape(1, 2 * V7X_LANES)
    da = conv_w.shape[-1]
    return dict(conv_w=conv_w[i], conv_b=conv_b[i].reshape(1, da), w_q=w_q[i], w_k=w_k[i], w_v=w_v[i],
                b_if=b_if, g_head=g_head[i].reshape(1, da), skip=skip_a[i].reshape(1, da))


def _tile_plan(n_rows, seq):
    tm = next(c for c in (832, 640, 512, 256, 128, 64, 32, 16) if n_rows % c == 0)
    tc = next(c for c in (640, 320, 256, 128, 64, 32, 16, 8) if n_rows % c == 0)
    tm_big = 2 * tm if n_rows % (2 * tm) == 0 else tm
    return dict(tm=tm, tm_big=tm_big, tn=512, tn_down=256, mlstm_chunk=256, s5_chunk=256,
                moe_tr=512, moe_tn_up=512, moe_tn_down=512, gather_tr=512, combine_tr=tc, ple_tm=tm // 2,
                norm_tm=min(seq, 1024))


def kernel(x_prompt, x_sample, p_prompt, p_sample, state_mlstm_C, state_mlstm_n, state_mlstm_m, state_mlstm_conv,
           state_s5_re, state_s5_im, g_mix, w_in, conv_w, conv_b, w_q, w_k, w_v, b_i, b_f, g_head, skip_a, w_proj_a,
           s5_log_dt, s5_A_re, s5_A_im, s5_B_re, s5_B_im, s5_C_re, s5_C_im, s5_D, w_glu_b, w_out, g_ffn,
           w_ff_gate, w_ff_up, w_ff_down, w_router, b_router, w_moe_gate, w_moe_up, w_moe_down,
           g_ple, w_ple, w_pg, g_final):
    bsz, seq, d = x_prompt.shape
    bs = x_sample.shape[0]
    depth = g_mix.shape[0]
    nh = b_i.shape[1]
    d_a = conv_w.shape[-1]
    d_b = s5_D.shape[-1]
    n_p = bsz * seq
    t = n_p + bs
    tl = _tile_plan(t, seq)
    tm, tmb, tn = tl["tm"], tl["tm_big"], tl["tn"]

    h = jnp.concatenate([x_prompt.reshape(n_p, d), x_sample.reshape(bs, d)], axis=0).astype(F32)
    p_all = jnp.concatenate([p_prompt.reshape(depth, n_p, -1), p_sample.reshape(depth, bs, -1)], axis=1)

    col_ua, col_oa, col_ub = 0, 1, 2 * d_a // d_b
    col_ga = (2 * d_a + d_b) // tn
    col_gb = (2 * d_a + d_b + d) // tn
    col_if = (2 * d_a + d_b + 2 * d) // (2 * V7X_LANES)

    states = [[] for _ in range(11)]
    c_s_all = jnp.zeros(state_mlstm_C.shape, F32)
    a_pre = gy = None
    for i in range(depth):
        lw = _layer_weights(i, conv_w, conv_b, w_q, w_k, w_v, b_i, b_f, g_head, skip_a)
        sp = _s5_params(dict(log_dt=s5_log_dt[i], a_re=s5_A_re[i], a_im=s5_A_im[i], b_re=s5_B_re[i],
                             b_im=s5_B_im[i], c_re=s5_C_re[i], c_im=s5_C_im[i], d_skip=s5_D[i]))
        proj = in_proj(h, g_mix[i], w_in, i, d_a, d_b, d, nh, tm=tmb, tn=tn)

        a_pre, c_p, n_pp, m_p, conv_p = mlstm_prompt(proj, col_ua, col_oa, col_if, t, bsz, seq, lw, a_pre,
                                                     L=tl["mlstm_chunk"])
        a_pre, c_s_all, n_s, m_s, conv_s = mlstm_sample(proj, col_ua, col_oa, col_if, n_p, a_pre,
                                                        state_mlstm_C, i, c_s_all, state_mlstm_n[i].astype(F32),
                                                        state_mlstm_m[i].astype(F32),
                                                        state_mlstm_conv[i].astype(F32), lw)
        gy, sre_p, sim_p = s5_prompt(proj, col_ub, t, bsz, seq, sp, gy, tc=tl["s5_chunk"])
        gy, sre_s, sim_s = s5_sample(proj, col_ub, n_p, gy, state_s5_re[i].astype(F32),
                                     state_s5_im[i].astype(F32), sp)
        mix = branch_mix(a_pre, gy, w_proj_a, w_glu_b, i, proj, col_ga, col_gb, tm=tmb, tn=tn)
        h = matmul(mix, w_out, i, h, tm=tmb, tn=tn, name="out_proj")

        j = i // 2
        if i % 2 == 0:
            hid = norm_swiglu_up(h, g_ffn[i], w_ff_gate, w_ff_up, j, tm=tmb, tn=tn)
            h = matmul(hid, w_ff_down, j, h, tm=tmb, tn=tl["tn_down"], name="ffn_down")
        else:
            h = moe_layer(h, g_ffn[i], w_router[j], b_router[j], w_moe_gate[j], w_moe_up[j], w_moe_down[j],
                          tm=tm, tr=tl["moe_tr"], tn_up=tl["moe_tn_up"], tn_down=tl["moe_tn_down"],
                          tr_gather=tl["gather_tr"], tr_combine=tl["combine_tr"])
        h = ple_matmul(h, g_ple[i], p_all, w_pg, w_ple, i, tm=tl["ple_tm"], tn=tn)

        g_s, p_s = S5_STATE, sre_p.shape[-1] // S5_STATE
        new = [c_p, n_pp, m_p, conv_p, sre_p.reshape(bsz, p_s, g_s), sim_p.reshape(bsz, p_s, g_s),
               n_s, m_s, conv_s, sre_s.reshape(bs, p_s, g_s), sim_s.reshape(bs, p_s, g_s)]
        for lst, s in zip(states, new):
            lst.append(s)

    y_prompt, y_sample = final_norm_split(h, g_final, n_p, tm=tl["norm_tm"])
    y_prompt = y_prompt.reshape(bsz, seq, d)
    y_sample = y_sample.reshape(bs, 1, d)
    st = [jnp.stack(lst) for lst in states]
    return (y_prompt, y_sample) + tuple(st[:6]) + (c_s_all,) + tuple(st[6:])
```
